```python
import jax, jax.numpy as jnp
from jax import lax
import numpy as np

D_MODEL = 4096
BATCH = 4
SEQ = 2048
DEPTH = 1

MEM_LEN = 256
HEAD_DIM = 128
DIL_GROUPS = ((128, 1), (512, 4), (2048, 16))
ATTN_WIDTH = 3 * D_MODEL // 8
CONV_WIDTH = 3 * D_MODEL // 8
XATTN_WIDTH = D_MODEL // 4
MIX_WIDTH = ATTN_WIDTH + CONV_WIDTH + XATTN_WIDTH
N_ATTN_HEADS = ATTN_WIDTH // HEAD_DIM
HEADS_PER_DIL = N_ATTN_HEADS // len(DIL_GROUPS)
N_XATTN_HEADS = 4
XATTN_HEAD_DIM = XATTN_WIDTH // N_XATTN_HEADS
CONV_K = 3
IN_COLS = 4 * ATTN_WIDTH + 4 * CONV_WIDTH + 2 * XATTN_WIDTH
BLOCK = 128
ROPE_THETA = 10000.0
EPS = 1e-6
NEG_INF = -1e30

kernel_name = "hybrid_dilated_conv_memory_layer"


def rms_norm(x, g):
    x32 = x.astype(jnp.float32)
    y = x32 * lax.rsqrt(jnp.mean(x32 * x32, axis=-1, keepdims=True) + EPS)
    return (y * g.astype(jnp.float32)).astype(x.dtype)


def rope(x, pos):
    half = x.shape[-1] // 2
    inv = 1.0 / (ROPE_THETA ** (jnp.arange(half, dtype=jnp.float32) / half))
    ang = pos.astype(jnp.float32)[:, None] * inv[None, :]
    cos = jnp.cos(ang)[:, None, :]
    sin = jnp.sin(ang)[:, None, :]
    x32 = x.astype(jnp.float32)
    x1, x2 = x32[..., :half], x32[..., half:]
    return jnp.concatenate([x1 * cos - x2 * sin, x2 * cos + x1 * sin], axis=-1).astype(x.dtype)


def dilated_window_attention(q, k, v, window, dilation):
    b, s, h, dh = q.shape
    steps = window // dilation
    L = s // dilation
    nb = -(-L // BLOCK)
    Lp = nb * BLOCK
    bd = b * dilation

    def to_sub(t):
        t = t.reshape(b, L, dilation, h, dh).transpose(0, 2, 1, 3, 4).reshape(bd, L, h, dh)
        return jnp.pad(t, ((0, 0), (0, Lp - L), (0, 0), (0, 0)))

    def band(t):
        tp = jnp.pad(t, ((0, 0), (BLOCK, 0), (0, 0), (0, 0))).reshape(bd, nb + 1, BLOCK, h, dh)
        return jnp.concatenate([tp[:, :-1], tp[:, 1:]], axis=2)

    qb = to_sub(q).reshape(bd, nb, BLOCK, h, dh)
    kb = band(to_sub(k))
    vb = band(to_sub(v))
    scores = jnp.einsum('znqhd,znkhd->znhqk', qb, kb,
                        preferred_element_type=jnp.float32) * (dh ** -0.5)
    qi = jnp.arange(BLOCK)[:, None]
    kk = jnp.arange(2 * BLOCK)[None, :]
    dist = qi + BLOCK - kk
    key_pos = (jnp.arange(nb)[:, None, None] - 1) * BLOCK + kk[None]
    valid = (dist >= 0) & (dist <= steps) & (key_pos >= 0)
    scores = jnp.where(valid[None, :, None], scores, NEG_INF)
    lse = jax.nn.logsumexp(scores, axis=-1)
    p = jnp.exp(scores - lse[..., None])
    out = jnp.einsum('znhqk,znkhd->znqhd', p.astype(v.dtype), vb)
    out = out.reshape(b, dilation, Lp, h, dh)[:, :, :L].transpose(0, 2, 1, 3, 4).reshape(b, s, h, dh)
    lse = lse.transpose(0, 1, 3, 2).reshape(b, dilation, Lp, h)[:, :, :L]
    lse = lse.transpose(0, 2, 1, 3).reshape(b, s, h)
    return out, lse


def short_conv(u, w):
    s = u.shape[1]
    up = jnp.pad(u, ((0, 0), (CONV_K - 1, 0), (0, 0)))
    y = up[:, 0:s] * w[0]
    for j in range(1, CONV_K):
        y = y + up[:, j:j + s] * w[j]
    return y


def memory_cross_attention(q, mk, mv):
    scores = jnp.einsum('bshd,bmhd->bhsm', q, mk,
                        preferred_element_type=jnp.float32) * (q.shape[-1] ** -0.5)
    p = jax.nn.softmax(scores, axis=-1)
    return jnp.einsum('bhsm,bmhd->bshd', p.astype(mv.dtype), mv)


def setup_inputs(seed: int = 0) -> dict:
    key = jax.random.key(seed)
    ks = jax.random.split(key, 10)
    f32 = jnp.float32
    x = jax.random.normal(ks[0], (BATCH, SEQ, D_MODEL), f32)
    mem = jax.random.normal(ks[1], (BATCH, MEM_LEN, D_MODEL), f32)
    pre_norm = 1.0 + 0.05 * jax.random.normal(ks[2], (DEPTH, D_MODEL), f32)
    w_in = jax.random.normal(ks[3], (DEPTH, D_MODEL, IN_COLS), f32) * D_MODEL ** -0.5
    conv_w = jax.random.normal(ks[4], (DEPTH, CONV_K, CONV_WIDTH), f32) * CONV_K ** -0.5
    mem_norm = 1.0 + 0.05 * jax.random.normal(ks[5], (DEPTH, D_MODEL), f32)
    w_mem_kv = jax.random.normal(ks[6], (DEPTH, D_MODEL, 2 * XATTN_WIDTH), f32) * D_MODEL ** -0.5
    w_out = jax.random.normal(ks[7], (DEPTH, MIX_WIDTH, D_MODEL), f32) * MIX_WIDTH ** -0.5
    post_norm = 1.0 + 0.05 * jax.random.normal(ks[8], (DEPTH, D_MODEL), f32)
    return {"x": x, "mem": mem, "pre_norm": pre_norm, "w_in": w_in, "conv_w": conv_w,
            "mem_norm": mem_norm, "w_mem_kv": w_mem_kv, "w_out": w_out,
            "post_norm": post_norm}


def reference(x, mem, pre_norm, w_in, conv_w, mem_norm, w_mem_kv, w_out, post_norm):
    b, s, _ = x.shape
    m_len = mem.shape[1]
    pos = jnp.arange(s)
    mem_pos = jnp.arange(m_len)
    sizes = (ATTN_WIDTH,) * 4 + (CONV_WIDTH,) * 4 + (XATTN_WIDTH,) * 2
    split_pts = [int(c) for c in np.cumsum(sizes)[:-1]]
    for layer in range(DEPTH):
        h = rms_norm(x, pre_norm[layer])
        proj = h @ w_in[layer]
        q_a, k_a, v_a, z_a, u_c, b_c, c_c, z_c, q_x, z_x = jnp.split(proj, split_pts, axis=-1)

        q_a = rope(q_a.reshape(b, s, N_ATTN_HEADS, HEAD_DIM), pos)
        k_a = rope(k_a.reshape(b, s, N_ATTN_HEADS, HEAD_DIM), pos)
        v_a = v_a.reshape(b, s, N_ATTN_HEADS, HEAD_DIM)
        outs, lses = [], []
        for g, (window, dil) in enumerate(DIL_GROUPS):
            sl = slice(g * HEADS_PER_DIL, (g + 1) * HEADS_PER_DIL)
            o, l = dilated_window_attention(q_a[:, :, sl], k_a[:, :, sl], v_a[:, :, sl], window, dil)
            outs.append(o)
            lses.append(l)
        o = jnp.stack(outs, axis=2)
        alpha = jax.nn.softmax(jnp.stack(lses, axis=2), axis=2)
        y_attn = (o * alpha[..., None].astype(o.dtype)).reshape(b, s, ATTN_WIDTH)
        y_attn = y_attn * jax.nn.silu(z_a)

        y_conv = b_c * short_conv(c_c * u_c, conv_w[layer]) * jax.nn.silu(z_c)

        mkv = rms_norm(mem, mem_norm[layer]) @ w_mem_kv[layer]
        mk, mv = jnp.split(mkv, 2, axis=-1)
        mk = rope(mk.reshape(b, m_len, N_XATTN_HEADS, XATTN_HEAD_DIM), mem_pos)
        mv = mv.reshape(b, m_len, N_XATTN_HEADS, XATTN_HEAD_DIM)
        qx = rope(q_x.reshape(b, s, N_XATTN_HEADS, XATTN_HEAD_DIM), pos + m_len)
        y_x = memory_cross_attention(qx, mk, mv).reshape(b, s, XATTN_WIDTH) * jax.nn.silu(z_x)

        y = jnp.concatenate([y_attn, y_conv, y_x], axis=-1) @ w_out[layer]
        x = x + rms_norm(y, post_norm[layer])
    return x
```

```python
import functools

import jax
import jax.numpy as jnp
import numpy as np
from jax import lax
from jax.experimental import pallas as pl
from jax.experimental.pallas import tpu as pltpu

D_MODEL = 4096
MEM_LEN = 256
HEAD_DIM = 128
DIL_GROUPS = ((128, 1), (512, 4), (2048, 16))
ATTN_WIDTH = 3 * D_MODEL // 8
CONV_WIDTH = 3 * D_MODEL // 8
XATTN_WIDTH = D_MODEL // 4
N_XATTN_HEADS = 4
XATTN_HEAD_DIM = XATTN_WIDTH // N_XATTN_HEADS
HEADS_PER_DIL = 4
GROUP_WIDTH = HEADS_PER_DIL * HEAD_DIM
CONV_K = 3
IN_COLS = 4 * ATTN_WIDTH + 4 * CONV_WIDTH + 2 * XATTN_WIDTH
BLOCK = 128
ROPE_THETA = 10000.0
EPS = 1e-6
NEG_INF = -1e30

OFF_QA = 0
OFF_KA = OFF_QA + ATTN_WIDTH
OFF_VA = OFF_KA + ATTN_WIDTH
OFF_ZA = OFF_VA + ATTN_WIDTH
OFF_UC = OFF_ZA + ATTN_WIDTH
OFF_BC = OFF_UC + CONV_WIDTH
OFF_CC = OFF_BC + CONV_WIDTH
OFF_ZC = OFF_CC + CONV_WIDTH
OFF_QX = OFF_ZC + CONV_WIDTH
OFF_ZX = OFF_QX + XATTN_WIDTH

V7X_VMEM_LIMIT_BYTES = 56 * 1024 * 1024

BF16 = jnp.bfloat16
F32 = jnp.float32


def _params(semantics):
    return pltpu.CompilerParams(dimension_semantics=semantics,
                                vmem_limit_bytes=V7X_VMEM_LIMIT_BYTES)


def _rmsnorm_kernel(x_ref, g_ref, o_ref):
    x = x_ref[...]
    ms = jnp.mean(x * x, axis=-1, keepdims=True)
    o_ref[...] = (x * lax.rsqrt(ms + EPS) * g_ref[...]).astype(o_ref.dtype)


def _rmsnorm(x2d, gain, tm):
    rows, d = x2d.shape
    return pl.pallas_call(
        _rmsnorm_kernel,
        grid=(rows // tm,),
        in_specs=[pl.BlockSpec((tm, d), lambda i: (i, 0)),
                  pl.BlockSpec((1, d), lambda i: (0, 0))],
        out_specs=pl.BlockSpec((tm, d), lambda i: (i, 0)),
        out_shape=jax.ShapeDtypeStruct((rows, d), BF16),
        compiler_params=_params(("parallel",)),
        name="rmsnorm",
    )(x2d, gain.reshape(1, d))


def _silu(z):
    return z * (1.0 / (1.0 + jnp.exp(-z)))


def _rope_half_lane(acc, cos, sin_signed, scale):
    outs = []
    for h in range(acc.shape[1] // HEAD_DIM):
        xh = acc[:, h * HEAD_DIM:(h + 1) * HEAD_DIM]
        r = xh * cos + pltpu.roll(xh, HEAD_DIM // 2, axis=1) * sin_signed
        outs.append(r * scale if scale != 1.0 else r)
    return jnp.concatenate(outs, axis=1)


def _rope_two_tiles(acc, cos, sin, scale):
    outs = []
    for h in range(acc.shape[1] // XATTN_HEAD_DIM):
        x1 = acc[:, h * XATTN_HEAD_DIM:h * XATTN_HEAD_DIM + 128]
        x2 = acc[:, h * XATTN_HEAD_DIM + 128:(h + 1) * XATTN_HEAD_DIM]
        outs.append((x1 * cos - x2 * sin) * scale)
        outs.append((x2 * cos + x1 * sin) * scale)
    return jnp.concatenate(outs, axis=1)


def _in_proj_kernel(h_ref, w_ref, cosa_ref, sina_ref, cosx_ref, sinx_ref, o_ref, wbf_ref, *, tn):
    j = pl.program_id(0)
    i = pl.program_id(1)

    @pl.when(i == 0)
    def _():
        wbf_ref[...] = w_ref[...].astype(BF16)

    acc = jnp.dot(h_ref[...], wbf_ref[...], preferred_element_type=F32)

    def sec(off):
        return off // tn

    @pl.when(j < sec(OFF_KA))
    def _():
        o_ref[...] = _rope_half_lane(acc, cosa_ref[...], sina_ref[...],
                                     HEAD_DIM ** -0.5).astype(o_ref.dtype)

    @pl.when((j >= sec(OFF_KA)) & (j < sec(OFF_VA)))
    def _():
        o_ref[...] = _rope_half_lane(acc, cosa_ref[...], sina_ref[...], 1.0).astype(o_ref.dtype)

    is_gate = (((j >= sec(OFF_ZA)) & (j < sec(OFF_UC)))
               | ((j >= sec(OFF_ZC)) & (j < sec(OFF_QX)))
               | (j >= sec(OFF_ZX)))

    @pl.when(is_gate)
    def _():
        o_ref[...] = _silu(acc).astype(o_ref.dtype)

    @pl.when((j >= sec(OFF_QX)) & (j < sec(OFF_ZX)))
    def _():
        o_ref[...] = _rope_two_tiles(acc, cosx_ref[...], sinx_ref[...],
                                     XATTN_HEAD_DIM ** -0.5).astype(o_ref.dtype)

    is_plain = (((j >= sec(OFF_VA)) & (j < sec(OFF_ZA)))
                | ((j >= sec(OFF_UC)) & (j < sec(OFF_ZC))))

    @pl.when(is_plain)
    def _():
        o_ref[...] = acc.astype(o_ref.dtype)


def _in_proj(h, w, cosa, sina, cosx, sinx, seq, tm, tn):
    rows, d = h.shape
    ncols = w.shape[1]
    tiles_per_seq = seq // tm
    tab = pl.BlockSpec((tm, 128), lambda j, i: (i % tiles_per_seq, 0))
    return pl.pallas_call(
        functools.partial(_in_proj_kernel, tn=tn),
        grid=(ncols // tn, rows // tm),
        in_specs=[pl.BlockSpec((tm, d), lambda j, i: (i, 0)),
                  pl.BlockSpec((d, tn), lambda j, i: (0, j)),
                  tab, tab, tab, tab],
        out_specs=pl.BlockSpec((tm, tn), lambda j, i: (i, j)),
        out_shape=jax.ShapeDtypeStruct((rows, ncols), BF16),
        scratch_shapes=[pltpu.VMEM((d, tn), BF16)],
        compiler_params=_params(("arbitrary", "arbitrary")),
        name="in_proj",
    )(h, w, cosa, sina, cosx, sinx)


def _mem_proj_kernel(h_ref, w_ref, cos_ref, sin_ref, o_ref, *, tn):
    j = pl.program_id(0)
    acc = jnp.dot(h_ref[...], w_ref[...].astype(BF16), preferred_element_type=F32)

    @pl.when(j < XATTN_WIDTH // tn)
    def _():
        o_ref[...] = _rope_two_tiles(acc, cos_ref[...], sin_ref[...], 1.0).astype(o_ref.dtype)

    @pl.when(j >= XATTN_WIDTH // tn)
    def _():
        o_ref[...] = acc.astype(o_ref.dtype)


def _mem_proj(hm, w, cos, sin, tn):
    rows, d = hm.shape
    ncols = w.shape[1]
    tab = pl.BlockSpec((rows, 128), lambda j: (0, 0))
    return pl.pallas_call(
        functools.partial(_mem_proj_kernel, tn=tn),
        grid=(ncols // tn,),
        in_specs=[pl.BlockSpec((rows, d), lambda j: (0, 0)),
                  pl.BlockSpec((d, tn), lambda j: (0, j)),
                  tab, tab],
        out_specs=pl.BlockSpec((rows, tn), lambda j: (0, j)),
        out_shape=jax.ShapeDtypeStruct((rows, ncols), BF16),
        compiler_params=_params(("parallel",)),
        name="mem_proj",
    )(hm, w, cos, sin)


def _band_attn_kernel(q_ref, k_ref, v_ref, o_ref, lse_ref, *, sub_len):
    n = pl.program_id(1)
    if sub_len == BLOCK:
        start = 0
        nkeys = BLOCK
    else:
        start = pl.multiple_of(jnp.maximum(n - 1, 0) * BLOCK, BLOCK)
        nkeys = 2 * BLOCK
    qpos = n * BLOCK + lax.broadcasted_iota(jnp.int32, (BLOCK, nkeys), 0)
    kpos = start + lax.broadcasted_iota(jnp.int32, (BLOCK, nkeys), 1)
    dist = qpos - kpos
    valid = (dist >= 0) & (dist <= BLOCK)
    lane = lax.broadcasted_iota(jnp.int32, (BLOCK, 128), 1)
    lse_tile = jnp.zeros((BLOCK, 128), F32)
    for h in range(HEADS_PER_DIL):
        cols = slice(h * HEAD_DIM, (h + 1) * HEAD_DIM)
        q = q_ref[0, :, cols]
        k = k_ref[0, pl.ds(start, nkeys), cols]
        v = v_ref[0, pl.ds(start, nkeys), cols]
        s = lax.dot_general(q, k, (((1,), (1,)), ((), ())), preferred_element_type=F32)
        s = jnp.where(valid, s, NEG_INF)
        m = jnp.max(s, axis=-1, keepdims=True)
        p = jnp.exp(s - m)
        l = jnp.sum(p, axis=-1, keepdims=True)
        o = jnp.dot(p.astype(BF16), v, preferred_element_type=F32)
        o_ref[0, :, cols] = (o * (1.0 / l)).astype(o_ref.dtype)
        lse_tile = jnp.where(lane == h, m + jnp.log(l), lse_tile)
    lse_ref[0] = lse_tile


def _band_attn(q, k, v):
    z, sub_len, w = q.shape
    nb = sub_len // BLOCK
    full = pl.BlockSpec((1, sub_len, w), lambda zi, n: (zi, 0, 0))
    blk = pl.BlockSpec((1, BLOCK, w), lambda zi, n: (zi, n, 0))
    return pl.pallas_call(
        functools.partial(_band_attn_kernel, sub_len=sub_len),
        grid=(z, nb),
        in_specs=[blk, full, full],
        out_specs=[blk, pl.BlockSpec((1, BLOCK, 128), lambda zi, n: (zi, n, 0))],
        out_shape=[jax.ShapeDtypeStruct((z, sub_len, w), BF16),
                   jax.ShapeDtypeStruct((z, sub_len, 128), F32)],
        compiler_params=_params(("parallel", "parallel")),
        name="band_attn",
    )(q, k, v)


def _merge_kernel(o0_ref, o1_ref, o2_ref, l0_ref, l1_ref, l2_ref, z_ref, y_ref):
    o_refs = (o0_ref, o1_ref, o2_ref)
    lse = [r[...] for r in (l0_ref, l1_ref, l2_ref)]
    mx = jnp.maximum(jnp.maximum(lse[0], lse[1]), lse[2])
    e = [jnp.exp(l - mx) for l in lse]
    inv = 1.0 / (e[0] + e[1] + e[2])
    for g in range(len(DIL_GROUPS)):
        alpha = e[g] * inv
        for h in range(HEADS_PER_DIL):
            cols = slice(h * HEAD_DIM, (h + 1) * HEAD_DIM)
            ycols = slice(g * GROUP_WIDTH + h * HEAD_DIM, g * GROUP_WIDTH + (h + 1) * HEAD_DIM)
            a = alpha[:, h:h + 1]
            y_ref[:, ycols] = (o_refs[g][:, cols].astype(F32) * a
                               * z_ref[:, ycols].astype(F32)).astype(y_ref.dtype)


def _merge(o_list, lse_list, proj, tm):
    rows = proj.shape[0]
    ospec = pl.BlockSpec((tm, GROUP_WIDTH), lambda i: (i, 0))
    lspec = pl.BlockSpec((tm, 128), lambda i: (i, 0))
    return pl.pallas_call(
        _merge_kernel,
        grid=(rows // tm,),
        in_specs=[ospec, ospec, ospec, lspec, lspec, lspec,
                  pl.BlockSpec((tm, ATTN_WIDTH), lambda i: (i, OFF_ZA // ATTN_WIDTH))],
        out_specs=pl.BlockSpec((tm, ATTN_WIDTH), lambda i: (i, 0)),
        out_shape=jax.ShapeDtypeStruct((rows, ATTN_WIDTH), BF16),
        compiler_params=_params(("parallel",)),
        name="merge",
    )(*o_list, *lse_list, proj)


def _conv_kernel(u_ref, b_ref, c_ref, z_ref, w_ref, y_ref):
    cu = c_ref[0].astype(F32) * u_ref[0].astype(F32)
    row = lax.broadcasted_iota(jnp.int32, cu.shape, 0)
    w = w_ref[...]
    acc = cu * w[CONV_K - 1:CONV_K, :]
    for lag in range(1, CONV_K):
        shifted = jnp.where(row >= lag, pltpu.roll(cu, lag, axis=0), 0.0)
        acc = acc + shifted * w[CONV_K - 1 - lag:CONV_K - lag, :]
    y_ref[0] = (b_ref[0].astype(F32) * acc * z_ref[0].astype(F32)).astype(y_ref.dtype)


def _conv(proj3, conv_w, tc):
    b, s, _ = proj3.shape

    def spec(off):
        return pl.BlockSpec((1, s, tc), lambda bi, j: (bi, 0, off // tc + j))

    return pl.pallas_call(
        _conv_kernel,
        grid=(b, CONV_WIDTH // tc),
        in_specs=[spec(OFF_UC), spec(OFF_BC), spec(OFF_CC), spec(OFF_ZC),
                  pl.BlockSpec((CONV_K, tc), lambda bi, j: (0, j))],
        out_specs=pl.BlockSpec((1, s, tc), lambda bi, j: (bi, 0, j)),
        out_shape=jax.ShapeDtypeStruct((b, s, CONV_WIDTH), BF16),
        compiler_params=_params(("parallel", "parallel")),
        name="gated_conv",
    )(proj3, proj3, proj3, proj3, conv_w)


def _xattn_kernel(q_ref, z_ref, mk_ref, mv_ref, y_ref):
    for h in range(N_XATTN_HEADS):
        cols = slice(h * XATTN_HEAD_DIM, (h + 1) * XATTN_HEAD_DIM)
        s = lax.dot_general(q_ref[0, :, cols], mk_ref[0, :, cols],
                            (((1,), (1,)), ((), ())), preferred_element_type=F32)
        m = jnp.max(s, axis=-1, keepdims=True)
        p = jnp.exp(s - m)
        l = jnp.sum(p, axis=-1, keepdims=True)
        o = jnp.dot(p.astype(BF16), mv_ref[0, :, cols], preferred_element_type=F32)
        y_ref[0, :, cols] = (o * (1.0 / l) * z_ref[0, :, cols].astype(F32)).astype(y_ref.dtype)


def _xattn(proj3, mkv3, tm):
    b, s, _ = proj3.shape
    m_len = mkv3.shape[1]
    return pl.pallas_call(
        _xattn_kernel,
        grid=(b, s // tm),
        in_specs=[pl.BlockSpec((1, tm, XATTN_WIDTH), lambda bi, i: (bi, i, OFF_QX // XATTN_WIDTH)),
                  pl.BlockSpec((1, tm, XATTN_WIDTH), lambda bi, i: (bi, i, OFF_ZX // XATTN_WIDTH)),
                  pl.BlockSpec((1, m_len, XATTN_WIDTH), lambda bi, i: (bi, 0, 0)),
                  pl.BlockSpec((1, m_len, XATTN_WIDTH), lambda bi, i: (bi, 0, 1))],
        out_specs=pl.BlockSpec((1, tm, XATTN_WIDTH), lambda bi, i: (bi, i, 0)),
        out_shape=jax.ShapeDtypeStruct((b, s, XATTN_WIDTH), BF16),
        compiler_params=_params(("parallel", "parallel")),
        name="mem_xattn",
    )(proj3, proj3, mkv3, mkv3)


def _out_proj_kernel(ya_ref, yc_ref, yx_ref, wa_ref, wc_ref, wx_ref, x_ref, g_ref, o_ref, y2_ref,
                     *, tn, nj):
    j = pl.program_id(1)
    y2 = jnp.dot(ya_ref[...], wa_ref[...], preferred_element_type=F32)
    y2 = y2 + jnp.dot(yc_ref[...], wc_ref[...], preferred_element_type=F32)
    y2 = y2 + jnp.dot(yx_ref[...], wx_ref[...], preferred_element_type=F32)
    y2_ref[j] = y2

    @pl.when(j == nj - 1)
    def _():
        ssq = jnp.zeros((y2.shape[0], 1), F32)
        for jj in range(nj):
            c = y2_ref[jj]
            ssq = ssq + jnp.sum(c * c, axis=-1, keepdims=True)
        scale = lax.rsqrt(ssq * (1.0 / (nj * tn)) + EPS)
        for jj in range(nj):
            cols = slice(jj * tn, (jj + 1) * tn)
            o_ref[:, cols] = x_ref[:, cols] + y2_ref[jj] * scale * g_ref[:, cols]


def _out_proj(ya, yc, yx, w_bf, x2d, gain, tm, tn):
    rows, d = x2d.shape
    nj = d // tn
    return pl.pallas_call(
        functools.partial(_out_proj_kernel, tn=tn, nj=nj),
        grid=(rows // tm, nj),
        in_specs=[pl.BlockSpec((tm, ATTN_WIDTH), lambda i, j: (i, 0)),
                  pl.BlockSpec((tm, CONV_WIDTH), lambda i, j: (i, 0)),
                  pl.BlockSpec((tm, XATTN_WIDTH), lambda i, j: (i, 0)),
                  pl.BlockSpec((ATTN_WIDTH, tn), lambda i, j: (0, j)),
                  pl.BlockSpec((CONV_WIDTH, tn), lambda i, j: (1, j)),
                  pl.BlockSpec((XATTN_WIDTH, tn),
                               lambda i, j: ((ATTN_WIDTH + CONV_WIDTH) // XATTN_WIDTH, j)),
                  pl.BlockSpec((tm, d), lambda i, j: (i, 0)),
                  pl.BlockSpec((1, d), lambda i, j: (0, 0))],
        out_specs=pl.BlockSpec((tm, d), lambda i, j: (i, 0)),
        out_shape=jax.ShapeDtypeStruct((rows, d), F32),
        scratch_shapes=[pltpu.VMEM((nj, tm, tn), F32)],
        compiler_params=_params(("parallel", "arbitrary")),
        name="out_proj",
    )(ya, yc, yx, w_bf, w_bf, w_bf, x2d, gain.reshape(1, d))


def _rope_tables(pos, half):
    inv = 1.0 / (ROPE_THETA ** (jnp.arange(half, dtype=F32) / half))
    ang = pos.astype(F32)[:, None] * inv[None, :]
    return jnp.cos(ang), jnp.sin(ang)


def _to_sub(t, b, dilation):
    rows, w = t.shape
    s = rows // b
    sub = s // dilation
    return t.reshape(b, sub, dilation, w).transpose(0, 2, 1, 3).reshape(b * dilation, sub, w)


def _from_sub(t, b, dilation):
    z, sub, w = t.shape
    return t.reshape(b, dilation, sub, w).transpose(0, 2, 1, 3).reshape(b * sub * dilation, w)


def kernel(x, mem, pre_norm, w_in, conv_w, mem_norm, w_mem_kv, w_out, post_norm):
    b, s, d = x.shape
    m_len = mem.shape[1]
    depth = w_in.shape[0]
    pos = jnp.arange(s)
    cos_a, sin_a = _rope_tables(pos, HEAD_DIM // 2)
    cosa = jnp.concatenate([cos_a, cos_a], axis=-1)
    sina = jnp.concatenate([-sin_a, sin_a], axis=-1)
    cosx, sinx = _rope_tables(pos + m_len, XATTN_HEAD_DIM // 2)
    cos_m, sin_m = _rope_tables(jnp.arange(m_len), XATTN_HEAD_DIM // 2)
    cosm = jnp.tile(cos_m, (b, 1))
    sinm = jnp.tile(sin_m, (b, 1))

    for layer in range(depth):
        x2d = x.reshape(b * s, d)
        h = _rmsnorm(x2d, pre_norm[layer], tm=256)
        proj = _in_proj(h, w_in[layer], cosa, sina, cosx, sinx, seq=s, tm=1024, tn=512)

        hm = _rmsnorm(mem.reshape(b * m_len, d), mem_norm[layer], tm=256)
        mkv = _mem_proj(hm, w_mem_kv[layer], cosm, sinm, tn=512)

        o_list, lse_list = [], []
        for g, (window, dil) in enumerate(DIL_GROUPS):
            assert window // dil == BLOCK
            qg = _to_sub(proj[:, OFF_QA + g * GROUP_WIDTH:OFF_QA + (g + 1) * GROUP_WIDTH], b, dil)
            kg = _to_sub(proj[:, OFF_KA + g * GROUP_WIDTH:OFF_KA + (g + 1) * GROUP_WIDTH], b, dil)
            vg = _to_sub(proj[:, OFF_VA + g * GROUP_WIDTH:OFF_VA + (g + 1) * GROUP_WIDTH], b, dil)
            og, lg = _band_attn(qg, kg, vg)
            o_list.append(_from_sub(og, b, dil))
            lse_list.append(_from_sub(lg, b, dil))
        y_attn = _merge(o_list, lse_list, proj, tm=512)

        proj3 = proj.reshape(b, s, IN_COLS)
        y_conv = _conv(proj3, conv_w[layer], tc=256).reshape(b * s, CONV_WIDTH)
        y_x = _xattn(proj3, mkv.reshape(b, m_len, 2 * XATTN_WIDTH), tm=512)
        y_x = y_x.reshape(b * s, XATTN_WIDTH)

        out = _out_proj(y_attn, y_conv, y_x, w_out[layer].astype(BF16), x2d, post_norm[layer],
                        tm=256, tn=512)
        x = out.reshape(b, s, d)
    return x
```

```python
import functools

import jax
import jax.numpy as jnp
import numpy as np
from jax import lax
from jax.experimental import pallas as pl
from jax.experimental.pallas import tpu as pltpu

D_MODEL = 4096
MEM_LEN = 256
HEAD_DIM = 128
DIL_GROUPS = ((128, 1), (512, 4), (2048, 16))
ATTN_WIDTH = 3 * D_MODEL // 8
CONV_WIDTH = 3 * D_MODEL // 8
XATTN_WIDTH = D_MODEL // 4
N_XATTN_HEADS = 4
XATTN_HEAD_DIM = XATTN_WIDTH // N_XATTN_HEADS
HEADS_PER_DIL = 4
GROUP_WIDTH = HEADS_PER_DIL * HEAD_DIM
CONV_K = 3
IN_COLS = 4 * ATTN_WIDTH + 4 * CONV_WIDTH + 2 * XATTN_WIDTH
BLOCK = 128
ROPE_THETA = 10000.0
EPS = 1e-6
NEG_INF = -1e30

OFF_QA = 0
OFF_KA = OFF_QA + ATTN_WIDTH
OFF_VA = OFF_KA + ATTN_WIDTH
OFF_ZA = OFF_VA + ATTN_WIDTH
OFF_UC = OFF_ZA + ATTN_WIDTH
OFF_BC = OFF_UC + CONV_WIDTH
OFF_CC = OFF_BC + CONV_WIDTH
OFF_ZC = OFF_CC + CONV_WIDTH
OFF_QX = OFF_ZC + CONV_WIDTH
OFF_ZX = OFF_QX + XATTN_WIDTH

V7X_VMEM_LIMIT_BYTES = 56 * 1024 * 1024

BF16 = jnp.bfloat16
F32 = jnp.float32


def _params(semantics):
    return pltpu.CompilerParams(dimension_semantics=semantics,
                                vmem_limit_bytes=V7X_VMEM_LIMIT_BYTES)


def _rmsnorm_kernel(x_ref, g_ref, o_ref):
    x = x_ref[...]
    ms = jnp.mean(x * x, axis=-1, keepdims=True)
    o_ref[...] = (x * lax.rsqrt(ms + EPS) * g_ref[...]).astype(o_ref.dtype)


def _rmsnorm(x2d, gain, tm):
    rows, d = x2d.shape
    return pl.pallas_call(
        _rmsnorm_kernel,
        grid=(rows // tm,),
        in_specs=[pl.BlockSpec((tm, d), lambda i: (i, 0)),
                  pl.BlockSpec((1, d), lambda i: (0, 0))],
        out_specs=pl.BlockSpec((tm, d), lambda i: (i, 0)),
        out_shape=jax.ShapeDtypeStruct((rows, d), BF16),
        compiler_params=_params(("parallel",)),
        name="rmsnorm",
    )(x2d, gain.reshape(1, d))


def _silu(z):
    return z * (1.0 / (1.0 + jnp.exp(-z)))


def _rope_half_lane(acc, cos, sin_signed, scale):
    outs = []
    for h in range(acc.shape[1] // HEAD_DIM):
        xh = acc[:, h * HEAD_DIM:(h + 1) * HEAD_DIM]
        r = xh * cos + pltpu.roll(xh, HEAD_DIM // 2, axis=1) * sin_signed
        outs.append(r * scale if scale != 1.0 else r)
    return jnp.concatenate(outs, axis=1)


def _rope_two_tiles(acc, cos, sin, scale):
    outs = []
    for h in range(acc.shape[1] // XATTN_HEAD_DIM):
        x1 = acc[:, h * XATTN_HEAD_DIM:h * XATTN_HEAD_DIM + 128]
        x2 = acc[:, h * XATTN_HEAD_DIM + 128:(h + 1) * XATTN_HEAD_DIM]
        outs.append((x1 * cos - x2 * sin) * scale)
        outs.append((x2 * cos + x1 * sin) * scale)
    return jnp.concatenate(outs, axis=1)


def _in_proj_kernel(h_ref, w_ref, cosa_ref, sina_ref, cosx_ref, sinx_ref, o_ref, wbf_ref, *, tn):
    j = pl.program_id(0)
    i = pl.program_id(1)

    @pl.when(i == 0)
    def _():
        wbf_ref[...] = w_ref[...].astype(BF16)

    acc = jnp.dot(h_ref[...], wbf_ref[...], preferred_element_type=F32)

    def sec(off):
        return off // tn

    @pl.when(j < sec(OFF_KA))
    def _():
        o_ref[...] = _rope_half_lane(acc, cosa_ref[...], sina_ref[...],
                                     HEAD_DIM ** -0.5).astype(o_ref.dtype)

    @pl.when((j >= sec(OFF_KA)) & (j < sec(OFF_VA)))
    def _():
        o_ref[...] = _rope_half_lane(acc, cosa_ref[...], sina_ref[...], 1.0).astype(o_ref.dtype)

    is_gate = (((j >= sec(OFF_ZA)) & (j < sec(OFF_UC)))
               | ((j >= sec(OFF_ZC)) & (j < sec(OFF_QX)))
               | (j >= sec(OFF_ZX)))

    @pl.when(is_gate)
    def _():
        o_ref[...] = _silu(acc).astype(o_ref.dtype)

    @pl.when((j >= sec(OFF_QX)) & (j < sec(OFF_ZX)))
    def _():
        o_ref[...] = _rope_two_tiles(acc, cosx_ref[...], sinx_ref[...],
                                     XATTN_HEAD_DIM ** -0.5).astype(o_ref.dtype)

    is_plain = (((j >= sec(OFF_VA)) & (j < sec(OFF_ZA)))
                | ((j >= sec(OFF_UC)) & (j < sec(OFF_ZC))))

    @pl.when(is_plain)
    def _():
        o_ref[...] = acc.astype(o_ref.dtype)


def _in_proj(h, w, cosa, sina, cosx, sinx, seq, tm, tn):
    rows, d = h.shape
    ncols = w.shape[1]
    tiles_per_seq = seq // tm
    tab = pl.BlockSpec((tm, 128), lambda j, i: (i % tiles_per_seq, 0))
    return pl.pallas_call(
        functools.partial(_in_proj_kernel, tn=tn),
        grid=(ncols // tn, rows // tm),
        in_specs=[pl.BlockSpec((tm, d), lambda j, i: (i, 0)),
                  pl.BlockSpec((d, tn), lambda j, i: (0, j)),
                  tab, tab, tab, tab],
        out_specs=pl.BlockSpec((tm, tn), lambda j, i: (i, j)),
        out_shape=jax.ShapeDtypeStruct((rows, ncols), BF16),
        scratch_shapes=[pltpu.VMEM((d, tn), BF16)],
        compiler_params=_params(("arbitrary", "arbitrary")),
        name="in_proj",
    )(h, w, cosa, sina, cosx, sinx)


def _mem_proj_kernel(h_ref, w_ref, cos_ref, sin_ref, o_ref, *, tn):
    j = pl.program_id(0)
    acc = jnp.dot(h_ref[...], w_ref[...].astype(BF16), preferred_element_type=F32)

    @pl.when(j < XATTN_WIDTH // tn)
    def _():
        o_ref[...] = _rope_two_tiles(acc, cos_ref[...], sin_ref[...], 1.0).astype(o_ref.dtype)

    @pl.when(j >= XATTN_WIDTH // tn)
    def _():
        o_ref[...] = acc.astype(o_ref.dtype)


def _mem_proj(hm, w, cos, sin, tn):
    rows, d = hm.shape
    ncols = w.shape[1]
    tab = pl.BlockSpec((rows, 128), lambda j: (0, 0))
    return pl.pallas_call(
        functools.partial(_mem_proj_kernel, tn=tn),
        grid=(ncols // tn,),
        in_specs=[pl.BlockSpec((rows, d), lambda j: (0, 0)),
                  pl.BlockSpec((d, tn), lambda j: (0, j)),
                  tab, tab],
        out_specs=pl.BlockSpec((rows, tn), lambda j: (0, j)),
        out_shape=jax.ShapeDtypeStruct((rows, ncols), BF16),
        compiler_params=_params(("parallel",)),
        name="mem_proj",
    )(hm, w, cos, sin)


def _band_attn_kernel(q_ref, k_ref, v_ref, o_ref, lse_ref, *, sub_len):
    n = pl.program_id(1)
    if sub_len == BLOCK:
        start = 0
        nkeys = BLOCK
    else:
        start = pl.multiple_of(jnp.maximum(n - 1, 0) * BLOCK, BLOCK)
        nkeys = 2 * BLOCK
    qpos = n * BLOCK + lax.broadcasted_iota(jnp.int32, (BLOCK, nkeys), 0)
    kpos = start + lax.broadcasted_iota(jnp.int32, (BLOCK, nkeys), 1)
    dist = qpos - kpos
    valid = (dist >= 0) & (dist <= BLOCK)
    lane = lax.broadcasted_iota(jnp.int32, (BLOCK, 128), 1)
    lse_tile = jnp.zeros((BLOCK, 128), F32)
    for h in range(HEADS_PER_DIL):
        cols = slice(h * HEAD_DIM, (h + 1) * HEAD_DIM)
        q = q_ref[0, :, cols]
        k = k_ref[0, pl.ds(start, nkeys), cols]
        v = v_ref[0, pl.ds(start, nkeys), cols]
        s = lax.dot_general(q, k, (((1,), (1,)), ((), ())), preferred_element_type=F32)
        s = jnp.where(valid, s, NEG_INF)
        m = jnp.max(s, axis=-1, keepdims=True)
        p = jnp.exp(s - m)
        l = jnp.sum(p, axis=-1, keepdims=True)
        o = jnp.dot(p.astype(BF16), v, preferred_element_type=F32)
        o_ref[0, :, cols] = (o * (1.0 / l)).astype(o_ref.dtype)
        lse_tile = jnp.where(lane == h, m + jnp.log(l), lse_tile)
    lse_ref[0] = lse_tile


def _band_attn(q, k, v):
    z, sub_len, w = q.shape
    nb = sub_len // BLOCK
    full = pl.BlockSpec((1, sub_len, w), lambda zi, n: (zi, 0, 0))
    blk = pl.BlockSpec((1, BLOCK, w), lambda zi, n: (zi, n, 0))
    return pl.pallas_call(
        functools.partial(_band_attn_kernel, sub_len=sub_len),
        grid=(z, nb),
        in_specs=[blk, full, full],
        out_specs=[blk, pl.BlockSpec((1, BLOCK, 128), lambda zi, n: (zi, n, 0))],
        out_shape=[jax.ShapeDtypeStruct((z, sub_len, w), BF16),
                   jax.ShapeDtypeStruct((z, sub_len, 128), F32)],
        compiler_params=_params(("parallel", "parallel")),
        name="band_attn",
    )(q, k, v)


def _merge_kernel(o0_ref, o1_ref, o2_ref, l0_ref, l1_ref, l2_ref, z_ref, y_ref):
    o_refs = (o0_ref, o1_ref, o2_ref)
    lse = [r[...] for r in (l0_ref, l1_ref, l2_ref)]
    mx = jnp.maximum(jnp.maximum(lse[0], lse[1]), lse[2])
    e = [jnp.exp(l - mx) for l in lse]
    inv = 1.0 / (e[0] + e[1] + e[2])
    for g in range(len(DIL_GROUPS)):
        alpha = e[g] * inv
        for h in range(HEADS_PER_DIL):
            cols = slice(h * HEAD_DIM, (h + 1) * HEAD_DIM)
            ycols = slice(g * GROUP_WIDTH + h * HEAD_DIM, g * GROUP_WIDTH + (h + 1) * HEAD_DIM)
            a = alpha[:, h:h + 1]
            y_ref[:, ycols] = (o_refs[g][:, cols].astype(F32) * a
                               * z_ref[:, ycols].astype(F32)).astype(y_ref.dtype)


def _merge(o_list, lse_list, proj, tm):
    rows = proj.shape[0]
    ospec = pl.BlockSpec((tm, GROUP_WIDTH), lambda i: (i, 0))
    lspec = pl.BlockSpec((tm, 128), lambda i: (i, 0))
    return pl.pallas_call(
        _merge_kernel,
        grid=(rows // tm,),
        in_specs=[ospec, ospec, ospec, lspec, lspec, lspec,
                  pl.BlockSpec((tm, ATTN_WIDTH), lambda i: (i, OFF_ZA // ATTN_WIDTH))],
        out_specs=pl.BlockSpec((tm, ATTN_WIDTH), lambda i: (i, 0)),
        out_shape=jax.ShapeDtypeStruct((rows, ATTN_WIDTH), BF16),
        compiler_params=_params(("parallel",)),
        name="merge",
    )(*o_list, *lse_list, proj)


def _conv_kernel(u_ref, b_ref, c_ref, z_ref, w_ref, y_ref):
    cu = c_ref[0].astype(F32) * u_ref[0].astype(F32)
    row = lax.broadcasted_iota(jnp.int32, cu.shape, 0)
    w = w_ref[...]
    acc = cu * w[CONV_K - 1:CONV_K, :]
    for lag in range(1, CONV_K):
        shifted = jnp.where(row >= lag, pltpu.roll(cu, lag, axis=0), 0.0)
        acc = acc + shifted * w[CONV_K - 1 - lag:CONV_K - lag, :]
    y_ref[0] = (b_ref[0].astype(F32) * acc * z_ref[0].astype(F32)).astype(y_ref.dtype)


def _conv(proj3, conv_w, tc):
    b, s, _ = proj3.shape

    def spec(off):
        return pl.BlockSpec((1, s, tc), lambda bi, j: (bi, 0, off // tc + j))

    return pl.pallas_call(
        _conv_kernel,
        grid=(b, CONV_WIDTH // tc),
        in_specs=[spec(OFF_UC), spec(OFF_BC), spec(OFF_CC), spec(OFF_ZC),
                  pl.BlockSpec((CONV_K, tc), lambda bi, j: (0, j))],
        out_specs=pl.BlockSpec((1, s, tc), lambda bi, j: (bi, 0, j)),
        out_shape=jax.ShapeDtypeStruct((b, s, CONV_WIDTH), BF16),
        compiler_params=_params(("parallel", "parallel")),
        name="gated_conv",
    )(proj3, proj3, proj3, proj3, conv_w)


def _xattn_kernel(q_ref, z_ref, mk_ref, mv_ref, y_ref):
    for h in range(N_XATTN_HEADS):
        cols = slice(h * XATTN_HEAD_DIM, (h + 1) * XATTN_HEAD_DIM)
        s = lax.dot_general(q_ref[0, :, cols], mk_ref[0, :, cols],
                            (((1,), (1,)), ((), ())), preferred_element_type=F32)
        m = jnp.max(s, axis=-1, keepdims=True)
        p = jnp.exp(s - m)
        l = jnp.sum(p, axis=-1, keepdims=True)
        o = jnp.dot(p.astype(BF16), mv_ref[0, :, cols], preferred_element_type=F32)
        y_ref[0, :, cols] = (o * (1.0 / l) * z_ref[0, :, cols].astype(F32)).astype(y_ref.dtype)


def _xattn(proj3, mkv3, tm):
    b, s, _ = proj3.shape
    m_len = mkv3.shape[1]
    return pl.pallas_call(
        _xattn_kernel,
        grid=(b, s // tm),
        in_specs=[pl.BlockSpec((1, tm, XATTN_WIDTH), lambda bi, i: (bi, i, OFF_QX // XATTN_WIDTH)),
                  pl.BlockSpec((1, tm, XATTN_WIDTH), lambda bi, i: (bi, i, OFF_ZX // XATTN_WIDTH)),
                  pl.BlockSpec((1, m_len, XATTN_WIDTH), lambda bi, i: (bi, 0, 0)),
                  pl.BlockSpec((1, m_len, XATTN_WIDTH), lambda bi, i: (bi, 0, 1))],
        out_specs=pl.BlockSpec((1, tm, XATTN_WIDTH), lambda bi, i: (bi, i, 0)),
        out_shape=jax.ShapeDtypeStruct((b, s, XATTN_WIDTH), BF16),
        compiler_params=_params(("parallel", "parallel")),
        name="mem_xattn",
    )(proj3, proj3, mkv3, mkv3)


def _out_proj_kernel(ya_ref, yc_ref, yx_ref, wa_ref, wc_ref, wx_ref, x_ref, g_ref, o_ref,
                     y2_ref, ssq_ref, scale_ref, *, ni, d):
    i = pl.program_id(0)
    j = pl.program_id(1)

    @pl.when(j == 0)
    def _():
        @pl.when(i > 0)
        def _():
            scale_ref[...] = lax.rsqrt(ssq_ref[...] * (1.0 / d) + EPS)
        ssq_ref[...] = jnp.zeros_like(ssq_ref)

    def finish_previous_tile():
        o_ref[...] = x_ref[...] + y2_ref[j] * scale_ref[...] * g_ref[...]

    def multiply_this_tile():
        y2 = jnp.dot(ya_ref[...], wa_ref[...], preferred_element_type=F32)
        y2 = y2 + jnp.dot(yc_ref[...], wc_ref[...], preferred_element_type=F32)
        y2 = y2 + jnp.dot(yx_ref[...], wx_ref[...], preferred_element_type=F32)
        y2_ref[j] = y2
        ssq_ref[...] += jnp.sum(y2 * y2, axis=-1, keepdims=True)

    @pl.when(i == 0)
    def _():
        o_ref[...] = x_ref[...]
        multiply_this_tile()

    @pl.when((i > 0) & (i < ni))
    def _():
        finish_previous_tile()
        multiply_this_tile()

    @pl.when(i == ni)
    def _():
        finish_previous_tile()


def _out_proj(ya, yc, yx, w_bf, x2d, gain, tm, tn):
    rows, d = x2d.shape
    nj = d // tn
    ni = rows // tm

    def y_spec(width):
        return pl.BlockSpec((tm, width), lambda i, j: (jnp.minimum(i, ni - 1), 0))

    def w_spec(width, row_off):
        return pl.BlockSpec((width, tn),
                            lambda i, j: (row_off // width, jnp.where(i == ni, nj - 1, j)))

    io_spec = pl.BlockSpec((tm, tn), lambda i, j: (jnp.maximum(i - 1, 0), j))
    return pl.pallas_call(
        functools.partial(_out_proj_kernel, ni=ni, d=d),
        grid=(ni + 1, nj),
        in_specs=[y_spec(ATTN_WIDTH), y_spec(CONV_WIDTH), y_spec(XATTN_WIDTH),
                  w_spec(ATTN_WIDTH, 0), w_spec(CONV_WIDTH, ATTN_WIDTH),
                  w_spec(XATTN_WIDTH, ATTN_WIDTH + CONV_WIDTH),
                  io_spec,
                  pl.BlockSpec((1, tn), lambda i, j: (0, j))],
        out_specs=io_spec,
        out_shape=jax.ShapeDtypeStruct((rows, d), F32),
        scratch_shapes=[pltpu.VMEM((nj, tm, tn), F32),
                        pltpu.VMEM((tm, 1), F32),
                        pltpu.VMEM((tm, 1), F32)],
        compiler_params=_params(("arbitrary", "arbitrary")),
        name="out_proj",
    )(ya, yc, yx, w_bf, w_bf, w_bf, x2d, gain.reshape(1, d))


def _rope_tables(pos, half):
    inv = 1.0 / (ROPE_THETA ** (jnp.arange(half, dtype=F32) / half))
    ang = pos.astype(F32)[:, None] * inv[None, :]
    return jnp.cos(ang), jnp.sin(ang)


def _to_sub(t, b, dilation):
    rows, w = t.shape
    s = rows // b
    sub = s // dilation
    return t.reshape(b, sub, dilation, w).transpose(0, 2, 1, 3).reshape(b * dilation, sub, w)


def _from_sub(t, b, dilation):
    z, sub, w = t.shape
    return t.reshape(b, dilation, sub, w).transpose(0, 2, 1, 3).reshape(b * sub * dilation, w)


def kernel(x, mem, pre_norm, w_in, conv_w, mem_norm, w_mem_kv, w_out, post_norm):
    b, s, d = x.shape
    m_len = mem.shape[1]
    depth = w_in.shape[0]
    pos = jnp.arange(s)
    cos_a, sin_a = _rope_tables(pos, HEAD_DIM // 2)
    cosa = jnp.concatenate([cos_a, cos_a], axis=-1)
    sina = jnp.concatenate([-sin_a, sin_a], axis=-1)
    cosx, sinx = _rope_tables(pos + m_len, XATTN_HEAD_DIM // 2)
    cos_m, sin_m = _rope_tables(jnp.arange(m_len), XATTN_HEAD_DIM // 2)
    cosm = jnp.tile(cos_m, (b, 1))
    sinm = jnp.tile(sin_m, (b, 1))

    for layer in range(depth):
        x2d = x.reshape(b * s, d)
        h = _rmsnorm(x2d, pre_norm[layer], tm=256)
        proj = _in_proj(h, w_in[layer], cosa, sina, cosx, sinx, seq=s, tm=1024, tn=512)

        hm = _rmsnorm(mem.reshape(b * m_len, d), mem_norm[layer], tm=256)
        mkv = _mem_proj(hm, w_mem_kv[layer], cosm, sinm, tn=512)

        o_list, lse_list = [], []
        for g, (window, dil) in enumerate(DIL_GROUPS):
            assert window // dil == BLOCK
            qg = _to_sub(proj[:, OFF_QA + g * GROUP_WIDTH:OFF_QA + (g + 1) * GROUP_WIDTH], b, dil)
            kg = _to_sub(proj[:, OFF_KA + g * GROUP_WIDTH:OFF_KA + (g + 1) * GROUP_WIDTH], b, dil)
            vg = _to_sub(proj[:, OFF_VA + g * GROUP_WIDTH:OFF_VA + (g + 1) * GROUP_WIDTH], b, dil)
            og, lg = _band_attn(qg, kg, vg)
            o_list.append(_from_sub(og, b, dil))
            lse_list.append(_from_sub(lg, b, dil))
        y_attn = _merge(o_list, lse_list, proj, tm=512)

        proj3 = proj.reshape(b, s, IN_COLS)
        y_conv = _conv(proj3, conv_w[layer], tc=256).reshape(b * s, CONV_WIDTH)
        y_x = _xattn(proj3, mkv.reshape(b, m_len, 2 * XATTN_WIDTH), tm=512)
        y_x = y_x.reshape(b * s, XATTN_WIDTH)

        out = _out_proj(y_attn, y_conv, y_x, w_out[layer].astype(BF16), x2d, post_norm[layer],
                        tm=1024, tn=512)
        x = out.reshape(b, s, d)
    return x
```

```python
import functools

import jax
import jax.numpy as jnp
import numpy as np
from jax import lax
from jax.experimental import pallas as pl
from jax.experimental.pallas import tpu as pltpu

D_MODEL = 4096
MEM_LEN = 256
HEAD_DIM = 128
DIL_GROUPS = ((128, 1), (512, 4), (2048, 16))
ATTN_WIDTH = 3 * D_MODEL // 8
CONV_WIDTH = 3 * D_MODEL // 8
XATTN_WIDTH = D_MODEL // 4
N_XATTN_HEADS = 4
XATTN_HEAD_DIM = XATTN_WIDTH // N_XATTN_HEADS
HEADS_PER_DIL = 4
GROUP_WIDTH = HEADS_PER_DIL * HEAD_DIM
CONV_K = 3
IN_COLS = 4 * ATTN_WIDTH + 4 * CONV_WIDTH + 2 * XATTN_WIDTH
BLOCK = 128
ROPE_THETA = 10000.0
EPS = 1e-6
NEG_INF = -1e30

OFF_QA = 0
OFF_KA = OFF_QA + ATTN_WIDTH
OFF_VA = OFF_KA + ATTN_WIDTH
OFF_ZA = OFF_VA + ATTN_WIDTH
OFF_UC = OFF_ZA + ATTN_WIDTH
OFF_BC = OFF_UC + CONV_WIDTH
OFF_CC = OFF_BC + CONV_WIDTH
OFF_ZC = OFF_CC + CONV_WIDTH
OFF_QX = OFF_ZC + CONV_WIDTH
OFF_ZX = OFF_QX + XATTN_WIDTH

V7X_VMEM_LIMIT_BYTES = 56 * 1024 * 1024

BF16 = jnp.bfloat16
F32 = jnp.float32


def _params(semantics):
    return pltpu.CompilerParams(dimension_semantics=semantics,
                                vmem_limit_bytes=V7X_VMEM_LIMIT_BYTES)


def _rmsnorm_kernel(x_ref, g_ref, o_ref):
    x = x_ref[...]
    ms = jnp.mean(x * x, axis=-1, keepdims=True)
    o_ref[...] = (x * lax.rsqrt(ms + EPS) * g_ref[...]).astype(o_ref.dtype)


def _rmsnorm(x2d, gain, tm):
    rows, d = x2d.shape
    return pl.pallas_call(
        _rmsnorm_kernel,
        grid=(rows // tm,),
        in_specs=[pl.BlockSpec((tm, d), lambda i: (i, 0)),
                  pl.BlockSpec((1, d), lambda i: (0, 0))],
        out_specs=pl.BlockSpec((tm, d), lambda i: (i, 0)),
        out_shape=jax.ShapeDtypeStruct((rows, d), BF16),
        compiler_params=_params(("parallel",)),
        name="rmsnorm",
    )(x2d, gain.reshape(1, d))


def _silu(z):
    return z * (1.0 / (1.0 + jnp.exp(-z)))


def _rope_half_lane(acc, cos, sin_signed, scale):
    outs = []
    for h in range(acc.shape[1] // HEAD_DIM):
        xh = acc[:, h * HEAD_DIM:(h + 1) * HEAD_DIM]
        r = xh * cos + pltpu.roll(xh, HEAD_DIM // 2, axis=1) * sin_signed
        outs.append(r * scale if scale != 1.0 else r)
    return jnp.concatenate(outs, axis=1)


def _rope_two_tiles(acc, cos, sin, scale):
    outs = []
    for h in range(acc.shape[1] // XATTN_HEAD_DIM):
        x1 = acc[:, h * XATTN_HEAD_DIM:h * XATTN_HEAD_DIM + 128]
        x2 = acc[:, h * XATTN_HEAD_DIM + 128:(h + 1) * XATTN_HEAD_DIM]
        outs.append((x1 * cos - x2 * sin) * scale)
        outs.append((x2 * cos + x1 * sin) * scale)
    return jnp.concatenate(outs, axis=1)


def _in_proj_kernel(h_ref, w_ref, cosa_ref, sina_ref, cosx_ref, sinx_ref, o_ref, wbf_ref, *, tn):
    j = pl.program_id(0)
    i = pl.program_id(1)

    @pl.when(i == 0)
    def _():
        wbf_ref[...] = w_ref[...].astype(BF16)

    acc = jnp.dot(h_ref[...], wbf_ref[...], preferred_element_type=F32)

    def sec(off):
        return off // tn

    @pl.when(j < sec(OFF_KA))
    def _():
        o_ref[...] = _rope_half_lane(acc, cosa_ref[...], sina_ref[...],
                                     HEAD_DIM ** -0.5).astype(o_ref.dtype)

    @pl.when((j >= sec(OFF_KA)) & (j < sec(OFF_VA)))
    def _():
        o_ref[...] = _rope_half_lane(acc, cosa_ref[...], sina_ref[...], 1.0).astype(o_ref.dtype)

    is_gate = (((j >= sec(OFF_ZA)) & (j < sec(OFF_UC)))
               | ((j >= sec(OFF_ZC)) & (j < sec(OFF_QX)))
               | (j >= sec(OFF_ZX)))

    @pl.when(is_gate)
    def _():
        o_ref[...] = _silu(acc).astype(o_ref.dtype)

    @pl.when((j >= sec(OFF_QX)) & (j < sec(OFF_ZX)))
    def _():
        o_ref[...] = _rope_two_tiles(acc, cosx_ref[...], sinx_ref[...],
                                     XATTN_HEAD_DIM ** -0.5).astype(o_ref.dtype)

    is_plain = (((j >= sec(OFF_VA)) & (j < sec(OFF_ZA)))
                | ((j >= sec(OFF_UC)) & (j < sec(OFF_ZC))))

    @pl.when(is_plain)
    def _():
        o_ref[...] = acc.astype(o_ref.dtype)


def _in_proj(h, w, cosa, sina, cosx, sinx, seq, tm, tn):
    rows, d = h.shape
    ncols = w.shape[1]
    tiles_per_seq = seq // tm
    tab = pl.BlockSpec((tm, 128), lambda j, i: (i % tiles_per_seq, 0))
    return pl.pallas_call(
        functools.partial(_in_proj_kernel, tn=tn),
        grid=(ncols // tn, rows // tm),
        in_specs=[pl.BlockSpec((tm, d), lambda j, i: (i, 0)),
                  pl.BlockSpec((d, tn), lambda j, i: (0, j)),
                  tab, tab, tab, tab],
        out_specs=pl.BlockSpec((tm, tn), lambda j, i: (i, j)),
        out_shape=jax.ShapeDtypeStruct((rows, ncols), BF16),
        scratch_shapes=[pltpu.VMEM((d, tn), BF16)],
        compiler_params=_params(("arbitrary", "arbitrary")),
        name="in_proj",
    )(h, w, cosa, sina, cosx, sinx)


def _mem_proj_kernel(h_ref, w_ref, cos_ref, sin_ref, o_ref, *, tn):
    j = pl.program_id(0)
    acc = jnp.dot(h_ref[...], w_ref[...].astype(BF16), preferred_element_type=F32)

    @pl.when(j < XATTN_WIDTH // tn)
    def _():
        o_ref[...] = _rope_two_tiles(acc, cos_ref[...], sin_ref[...], 1.0).astype(o_ref.dtype)

    @pl.when(j >= XATTN_WIDTH // tn)
    def _():
        o_ref[...] = acc.astype(o_ref.dtype)


def _mem_proj(hm, w, cos, sin, tn):
    rows, d = hm.shape
    ncols = w.shape[1]
    tab = pl.BlockSpec((rows, 128), lambda j: (0, 0))
    return pl.pallas_call(
        functools.partial(_mem_proj_kernel, tn=tn),
        grid=(ncols // tn,),
        in_specs=[pl.BlockSpec((rows, d), lambda j: (0, 0)),
                  pl.BlockSpec((d, tn), lambda j: (0, j)),
                  tab, tab],
        out_specs=pl.BlockSpec((rows, tn), lambda j: (0, j)),
        out_shape=jax.ShapeDtypeStruct((rows, ncols), BF16),
        compiler_params=_params(("parallel",)),
        name="mem_proj",
    )(hm, w, cos, sin)


def _dilated_attn_kernel(q0, q1, q2, k0, k1, k2, v0, v1, v2, z0, z1, z2, y0, y1, y2,
                         stage_ref, sub_ref, osub_ref, lsub_ref, onat_ref, lnat_ref, *, seq):
    q_refs, k_refs, v_refs = (q0, q1, q2), (k0, k1, k2), (v0, v1, v2)
    z_refs, y_refs = (z0, z1, z2), (y0, y1, y2)
    nblk = seq // BLOCK
    qi = lax.broadcasted_iota(jnp.int32, (BLOCK, 2 * BLOCK), 0)
    kk = lax.broadcasted_iota(jnp.int32, (BLOCK, 2 * BLOCK), 1)
    dist = qi + BLOCK - kk
    band = (dist >= 0) & (dist <= BLOCK)
    causal = (lax.broadcasted_iota(jnp.int32, (BLOCK, BLOCK), 1)
              <= lax.broadcasted_iota(jnp.int32, (BLOCK, BLOCK), 0))

    for g, (window, dil) in enumerate(DIL_GROUPS):
        sub_len = seq // dil
        nb = sub_len // BLOCK
        if dil == 1:
            def rows_of(which, start, size, g=g):
                return (q_refs, k_refs, v_refs)[which][g][0, start:start + size, :]
        else:
            slab = 3 * (g - 1)
            for which, refs in enumerate((q_refs, k_refs, v_refs)):
                stage_ref[slab + which] = refs[g][0].astype(F32)
                for r in range(dil):
                    sub_ref[slab + which, r * sub_len:(r + 1) * sub_len, :] = (
                        stage_ref[slab + which, pl.ds(r, sub_len, stride=dil), :].astype(BF16))

            def rows_of(which, start, size, slab=slab):
                return sub_ref[slab + which, start:start + size, :]

        for c in range(nblk):
            rows = slice(c * BLOCK, (c + 1) * BLOCK)
            q = rows_of(0, c * BLOCK, BLOCK)
            if c % nb == 0:
                k = rows_of(1, c * BLOCK, BLOCK)
                v = rows_of(2, c * BLOCK, BLOCK)
                mask = causal
            else:
                k = rows_of(1, (c - 1) * BLOCK, 2 * BLOCK)
                v = rows_of(2, (c - 1) * BLOCK, 2 * BLOCK)
                mask = band
            s = lax.dot_general(q, k, (((1,), (1,)), ((), ())), preferred_element_type=F32)
            s = jnp.where(mask, s, NEG_INF)
            m = jnp.max(s, axis=-1, keepdims=True)
            p = jnp.exp(s - m)
            l = jnp.sum(p, axis=-1, keepdims=True)
            o = jnp.dot(p.astype(BF16), v, preferred_element_type=F32) * (1.0 / l)
            lse = jnp.broadcast_to(m + jnp.log(l), (BLOCK, HEAD_DIM))
            if dil == 1:
                onat_ref[g, rows, :] = o
                lnat_ref[g, rows, :] = lse
            else:
                osub_ref[g - 1, rows, :] = o
                lsub_ref[g - 1, rows, :] = lse

        if dil > 1:
            for r in range(dil):
                src = slice(r * sub_len, (r + 1) * sub_len)
                onat_ref[g, pl.ds(r, sub_len, stride=dil), :] = osub_ref[g - 1, src, :]
                lnat_ref[g, pl.ds(r, sub_len, stride=dil), :] = lsub_ref[g - 1, src, :]

    chunk = 2 * BLOCK
    for t in range(seq // chunk):
        rows = slice(t * chunk, (t + 1) * chunk)
        lse = [lnat_ref[g, rows, :] for g in range(len(DIL_GROUPS))]
        mx = jnp.maximum(jnp.maximum(lse[0], lse[1]), lse[2])
        e = [jnp.exp(x - mx) for x in lse]
        inv = 1.0 / (e[0] + e[1] + e[2])
        for g in range(len(DIL_GROUPS)):
            y_refs[g][0, rows, :] = (onat_ref[g, rows, :] * (e[g] * inv)
                                     * z_refs[g][0, rows, :].astype(F32)).astype(y_refs[g].dtype)


def _dilated_attn(proj3):
    b, s, _ = proj3.shape
    n_groups = len(DIL_GROUPS)
    n_regrouped = n_groups - 1

    def head_spec(off, g):
        first = (off + g * GROUP_WIDTH) // HEAD_DIM
        return pl.BlockSpec((1, s, HEAD_DIM), lambda bi, hg: (bi, 0, first + hg))

    in_specs = [head_spec(off, g) for off in (OFF_QA, OFF_KA, OFF_VA, OFF_ZA)
                for g in range(n_groups)]
    out_spec = pl.BlockSpec((1, s, HEAD_DIM), lambda bi, hg: (bi, 0, hg))
    return pl.pallas_call(
        functools.partial(_dilated_attn_kernel, seq=s),
        grid=(b, HEADS_PER_DIL),
        in_specs=in_specs,
        out_specs=[out_spec] * n_groups,
        out_shape=[jax.ShapeDtypeStruct((b, s, GROUP_WIDTH), BF16)] * n_groups,
        scratch_shapes=[pltpu.VMEM((3 * n_regrouped, s, HEAD_DIM), F32),
                        pltpu.VMEM((3 * n_regrouped, s, HEAD_DIM), BF16),
                        pltpu.VMEM((n_regrouped, s, HEAD_DIM), F32),
                        pltpu.VMEM((n_regrouped, s, HEAD_DIM), F32),
                        pltpu.VMEM((n_groups, s, HEAD_DIM), F32),
                        pltpu.VMEM((n_groups, s, HEAD_DIM), F32)],
        compiler_params=_params(("parallel", "parallel")),
        name="dilated_attn",
    )(*([proj3] * (4 * n_groups)))


def _conv_kernel(u_ref, b_ref, c_ref, z_ref, w_ref, y_ref):
    cu = c_ref[0].astype(F32) * u_ref[0].astype(F32)
    row = lax.broadcasted_iota(jnp.int32, cu.shape, 0)
    w = w_ref[...]
    acc = cu * w[CONV_K - 1:CONV_K, :]
    for lag in range(1, CONV_K):
        shifted = jnp.where(row >= lag, pltpu.roll(cu, lag, axis=0), 0.0)
        acc = acc + shifted * w[CONV_K - 1 - lag:CONV_K - lag, :]
    y_ref[0] = (b_ref[0].astype(F32) * acc * z_ref[0].astype(F32)).astype(y_ref.dtype)


def _conv(proj3, conv_w, tc):
    b, s, _ = proj3.shape

    def spec(off):
        return pl.BlockSpec((1, s, tc), lambda bi, j: (bi, 0, off // tc + j))

    return pl.pallas_call(
        _conv_kernel,
        grid=(b, CONV_WIDTH // tc),
        in_specs=[spec(OFF_UC), spec(OFF_BC), spec(OFF_CC), spec(OFF_ZC),
                  pl.BlockSpec((CONV_K, tc), lambda bi, j: (0, j))],
        out_specs=pl.BlockSpec((1, s, tc), lambda bi, j: (bi, 0, j)),
        out_shape=jax.ShapeDtypeStruct((b, s, CONV_WIDTH), BF16),
        compiler_params=_params(("parallel", "parallel")),
        name="gated_conv",
    )(proj3, proj3, proj3, proj3, conv_w)


def _xattn_kernel(q_ref, z_ref, mk_ref, mv_ref, y_ref):
    for h in range(N_XATTN_HEADS):
        cols = slice(h * XATTN_HEAD_DIM, (h + 1) * XATTN_HEAD_DIM)
        s = lax.dot_general(q_ref[0, :, cols], mk_ref[0, :, cols],
                            (((1,), (1,)), ((), ())), preferred_element_type=F32)
        m = jnp.max(s, axis=-1, keepdims=True)
        p = jnp.exp(s - m)
        l = jnp.sum(p, axis=-1, keepdims=True)
        o = jnp.dot(p.astype(BF16), mv_ref[0, :, cols], preferred_element_type=F32)
        y_ref[0, :, cols] = (o * (1.0 / l) * z_ref[0, :, cols].astype(F32)).astype(y_ref.dtype)


def _xattn(proj3, mkv3, tm):
    b, s, _ = proj3.shape
    m_len = mkv3.shape[1]
    return pl.pallas_call(
        _xattn_kernel,
        grid=(b, s // tm),
        in_specs=[pl.BlockSpec((1, tm, XATTN_WIDTH), lambda bi, i: (bi, i, OFF_QX // XATTN_WIDTH)),
                  pl.BlockSpec((1, tm, XATTN_WIDTH), lambda bi, i: (bi, i, OFF_ZX // XATTN_WIDTH)),
                  pl.BlockSpec((1, m_len, XATTN_WIDTH), lambda bi, i: (bi, 0, 0)),
                  pl.BlockSpec((1, m_len, XATTN_WIDTH), lambda bi, i: (bi, 0, 1))],
        out_specs=pl.BlockSpec((1, tm, XATTN_WIDTH), lambda bi, i: (bi, i, 0)),
        out_shape=jax.ShapeDtypeStruct((b, s, XATTN_WIDTH), BF16),
        compiler_params=_params(("parallel", "parallel")),
        name="mem_xattn",
    )(proj3, proj3, mkv3, mkv3)


def _out_proj_kernel(*refs, ni, d, n_mix):
    y_refs, w_refs = refs[:n_mix], refs[n_mix:2 * n_mix]
    x_ref, g_ref, o_ref, y2_ref, ssq_ref, scale_ref = refs[2 * n_mix:]
    i = pl.program_id(0)
    j = pl.program_id(1)

    @pl.when(j == 0)
    def _():
        @pl.when(i > 0)
        def _():
            scale_ref[...] = lax.rsqrt(ssq_ref[...] * (1.0 / d) + EPS)
        ssq_ref[...] = jnp.zeros_like(ssq_ref)

    def finish_previous_tile():
        o_ref[...] = x_ref[...] + y2_ref[j] * scale_ref[...] * g_ref[...]

    def multiply_this_tile():
        y2 = jnp.dot(y_refs[0][...], w_refs[0][...], preferred_element_type=F32)
        for y_ref, w_ref in zip(y_refs[1:], w_refs[1:]):
            y2 = y2 + jnp.dot(y_ref[...], w_ref[...], preferred_element_type=F32)
        y2_ref[j] = y2
        ssq_ref[...] += jnp.sum(y2 * y2, axis=-1, keepdims=True)

    @pl.when(i == 0)
    def _():
        multiply_this_tile()

    @pl.when((i > 0) & (i < ni))
    def _():
        finish_previous_tile()
        multiply_this_tile()

    @pl.when(i == ni)
    def _():
        finish_previous_tile()


def _out_proj(ys, w_bf, x2d, gain, tm, tn):
    rows, d = x2d.shape
    nj = d // tn
    ni = rows // tm

    def y_spec(width):
        return pl.BlockSpec((tm, width), lambda i, j: (jnp.minimum(i, ni - 1), 0))

    def w_spec(width, row_off):
        assert row_off % width == 0
        return pl.BlockSpec((width, tn),
                            lambda i, j: (row_off // width, jnp.where(i == ni, nj - 1, j)))

    io_spec = pl.BlockSpec((tm, tn),
                           lambda i, j: (jnp.maximum(i - 1, 0), jnp.where(i == 0, 0, j)))
    widths = [y.shape[1] for y in ys]
    offsets = [sum(widths[:k]) for k in range(len(widths))]
    return pl.pallas_call(
        functools.partial(_out_proj_kernel, ni=ni, d=d, n_mix=len(ys)),
        grid=(ni + 1, nj),
        in_specs=([y_spec(w) for w in widths]
                  + [w_spec(w, off) for w, off in zip(widths, offsets)]
                  + [io_spec, pl.BlockSpec((1, tn), lambda i, j: (0, j))]),
        out_specs=io_spec,
        out_shape=jax.ShapeDtypeStruct((rows, d), F32),
        scratch_shapes=[pltpu.VMEM((nj, tm, tn), F32),
                        pltpu.VMEM((tm, 1), F32),
                        pltpu.VMEM((tm, 1), F32)],
        compiler_params=_params(("arbitrary", "arbitrary")),
        name="out_proj",
    )(*ys, *([w_bf] * len(ys)), x2d, gain.reshape(1, d))


def _rope_tables(pos, half):
    inv = 1.0 / (ROPE_THETA ** (jnp.arange(half, dtype=F32) / half))
    ang = pos.astype(F32)[:, None] * inv[None, :]
    return jnp.cos(ang), jnp.sin(ang)


def kernel(x, mem, pre_norm, w_in, conv_w, mem_norm, w_mem_kv, w_out, post_norm):
    b, s, d = x.shape
    m_len = mem.shape[1]
    depth = w_in.shape[0]
    pos = jnp.arange(s)
    cos_a, sin_a = _rope_tables(pos, HEAD_DIM // 2)
    cosa = jnp.concatenate([cos_a, cos_a], axis=-1)
    sina = jnp.concatenate([-sin_a, sin_a], axis=-1)
    cosx, sinx = _rope_tables(pos + m_len, XATTN_HEAD_DIM // 2)
    cos_m, sin_m = _rope_tables(jnp.arange(m_len), XATTN_HEAD_DIM // 2)
    cosm = jnp.tile(cos_m, (b, 1))
    sinm = jnp.tile(sin_m, (b, 1))

    for layer in range(depth):
        x2d = x.reshape(b * s, d)
        h = _rmsnorm(x2d, pre_norm[layer], tm=256)
        proj = _in_proj(h, w_in[layer], cosa, sina, cosx, sinx, seq=s, tm=1024, tn=512)

        hm = _rmsnorm(mem.reshape(b * m_len, d), mem_norm[layer], tm=256)
        mkv = _mem_proj(hm, w_mem_kv[layer], cosm, sinm, tn=512)

        assert all(window // dil == BLOCK for window, dil in DIL_GROUPS)
        proj3 = proj.reshape(b, s, IN_COLS)
        y_groups = [y.reshape(b * s, GROUP_WIDTH) for y in _dilated_attn(proj3)]
        y_conv = _conv(proj3, conv_w[layer], tc=256).reshape(b * s, CONV_WIDTH)
        y_x = _xattn(proj3, mkv.reshape(b, m_len, 2 * XATTN_WIDTH), tm=512)
        y_x = y_x.reshape(b * s, XATTN_WIDTH)

        out = _out_proj([*y_groups, y_conv, y_x], w_out[layer].astype(BF16), x2d,
                        post_norm[layer], tm=1024, tn=512)
        x = out.reshape(b, s, d)
    return x
```

```python
import functools

import jax
import jax.numpy as jnp
import numpy as np
from jax import lax
from jax.experimental import pallas as pl
from jax.experimental.pallas import tpu as pltpu

D_MODEL = 4096
MEM_LEN = 256
HEAD_DIM = 128
DIL_GROUPS = ((128, 1), (512, 4), (2048, 16))
ATTN_WIDTH = 3 * D_MODEL // 8
CONV_WIDTH = 3 * D_MODEL // 8
XATTN_WIDTH = D_MODEL // 4
N_XATTN_HEADS = 4
XATTN_HEAD_DIM = XATTN_WIDTH // N_XATTN_HEADS
HEADS_PER_DIL = 4
GROUP_WIDTH = HEADS_PER_DIL * HEAD_DIM
CONV_K = 3
IN_COLS = 4 * ATTN_WIDTH + 4 * CONV_WIDTH + 2 * XATTN_WIDTH
BLOCK = 128
ROPE_THETA = 10000.0
EPS = 1e-6
NEG_INF = -1e30

OFF_QA = 0
OFF_KA = OFF_QA + ATTN_WIDTH
OFF_VA = OFF_KA + ATTN_WIDTH
OFF_ZA = OFF_VA + ATTN_WIDTH
OFF_UC = OFF_ZA + ATTN_WIDTH
OFF_BC = OFF_UC + CONV_WIDTH
OFF_CC = OFF_BC + CONV_WIDTH
OFF_ZC = OFF_CC + CONV_WIDTH
OFF_QX = OFF_ZC + CONV_WIDTH
OFF_ZX = OFF_QX + XATTN_WIDTH

V7X_VMEM_LIMIT_BYTES = 56 * 1024 * 1024
LANES = 128

BF16 = jnp.bfloat16
F32 = jnp.float32


def _params(semantics):
    return pltpu.CompilerParams(dimension_semantics=semantics,
                                vmem_limit_bytes=V7X_VMEM_LIMIT_BYTES)


def _rmsnorm_kernel(x_ref, g_ref, o_ref):
    x = x_ref[...]
    ms = jnp.mean(x * x, axis=-1, keepdims=True)
    o_ref[...] = (x * lax.rsqrt(ms + EPS) * g_ref[...]).astype(o_ref.dtype)


def _rmsnorm(x2d, gain, tm):
    rows, d = x2d.shape
    return pl.pallas_call(
        _rmsnorm_kernel,
        grid=(rows // tm,),
        in_specs=[pl.BlockSpec((tm, d), lambda i: (i, 0)),
                  pl.BlockSpec((1, d), lambda i: (0, 0))],
        out_specs=pl.BlockSpec((tm, d), lambda i: (i, 0)),
        out_shape=jax.ShapeDtypeStruct((rows, d), BF16),
        compiler_params=_params(("parallel",)),
        name="rmsnorm",
    )(x2d, gain.reshape(1, d))


def _silu(z):
    return z * (1.0 / (1.0 + jnp.exp(-z)))


def _rope_half_lane(acc, cos, sin_signed):
    outs = []
    for h in range(acc.shape[1] // HEAD_DIM):
        xh = acc[:, h * HEAD_DIM:(h + 1) * HEAD_DIM]
        outs.append(xh * cos + pltpu.roll(xh, HEAD_DIM // 2, axis=1) * sin_signed)
    return jnp.concatenate(outs, axis=1)


def _rope_two_tiles(acc, cos, sin):
    outs = []
    for h in range(acc.shape[1] // XATTN_HEAD_DIM):
        x1 = acc[:, h * XATTN_HEAD_DIM:h * XATTN_HEAD_DIM + LANES]
        x2 = acc[:, h * XATTN_HEAD_DIM + LANES:(h + 1) * XATTN_HEAD_DIM]
        outs.append(x1 * cos - x2 * sin)
        outs.append(x2 * cos + x1 * sin)
    return jnp.concatenate(outs, axis=1)


def _in_proj_kernel(*refs, ni, n_steps, epilogue, n_tables):
    h_ref, w_ref = refs[:2]
    table_refs = refs[2:2 + n_tables]
    o_ref, wbf_ref, acc_ref = refs[2 + n_tables:]
    t = pl.program_id(0)

    @pl.when((t % ni == 0) & (t < n_steps))
    def _():
        wbf_ref[...] = w_ref[...].astype(BF16)

    def multiply(slot):
        acc_ref[slot] = jnp.dot(h_ref[...], wbf_ref[...], preferred_element_type=F32)

    def finish(slot):
        o_ref[...] = epilogue(acc_ref[slot], *[r[0] for r in table_refs]).astype(o_ref.dtype)

    @pl.when(t == 0)
    def _():
        multiply(0)

    for parity in range(2):
        @pl.when((t > 0) & (t < n_steps) & (t % 2 == parity))
        def _(parity=parity):
            finish(1 - parity)
            multiply(parity)

    @pl.when(t == n_steps)
    def _():
        finish((n_steps - 1) % 2)


def _in_proj(h, w, col_tile, n_col_tiles, epilogue, tables, table_kind, seq, tm, tn, name):
    rows, d = h.shape
    ni = rows // tm
    n_steps = n_col_tiles * ni
    tiles_per_seq = seq // tm

    def cur(t):
        c = jnp.minimum(t, n_steps - 1)
        return c // ni, c % ni

    def prev(t):
        p = jnp.maximum(t - 1, 0)
        return p // ni, p % ni

    table_spec = pl.BlockSpec(
        (1, tm, LANES), lambda t: (table_kind(prev(t)[0]), prev(t)[1] % tiles_per_seq, 0))
    return pl.pallas_call(
        functools.partial(_in_proj_kernel, ni=ni, n_steps=n_steps, epilogue=epilogue,
                          n_tables=len(tables)),
        grid=(n_steps + 1,),
        in_specs=[pl.BlockSpec((tm, d), lambda t: (cur(t)[1], 0)),
                  pl.BlockSpec((d, tn), lambda t: (0, col_tile(cur(t)[0])))]
                 + [table_spec] * len(tables),
        out_specs=pl.BlockSpec((tm, tn), lambda t: (prev(t)[1], prev(t)[0])),
        out_shape=jax.ShapeDtypeStruct((rows, n_col_tiles * tn), BF16),
        scratch_shapes=[pltpu.VMEM((d, tn), BF16), pltpu.VMEM((2, tm, tn), F32)],
        compiler_params=_params(("arbitrary",)),
        name=name,
    )(h, w, *tables)


def _mem_proj_kernel(h_ref, w_ref, cos_ref, sin_ref, o_ref, *, tn):
    j = pl.program_id(0)
    acc = jnp.dot(h_ref[...], w_ref[...].astype(BF16), preferred_element_type=F32)

    @pl.when(j < XATTN_WIDTH // tn)
    def _():
        o_ref[...] = _rope_two_tiles(acc, cos_ref[...], sin_ref[...]).astype(o_ref.dtype)

    @pl.when(j >= XATTN_WIDTH // tn)
    def _():
        o_ref[...] = acc.astype(o_ref.dtype)


def _mem_proj(hm, w, cos, sin, tn):
    rows, d = hm.shape
    ncols = w.shape[1]
    tab = pl.BlockSpec((rows, LANES), lambda j: (0, 0))
    return pl.pallas_call(
        functools.partial(_mem_proj_kernel, tn=tn),
        grid=(ncols // tn,),
        in_specs=[pl.BlockSpec((rows, d), lambda j: (0, 0)),
                  pl.BlockSpec((d, tn), lambda j: (0, j)),
                  tab, tab],
        out_specs=pl.BlockSpec((rows, tn), lambda j: (0, j)),
        out_shape=jax.ShapeDtypeStruct((rows, ncols), BF16),
        compiler_params=_params(("parallel",)),
        name="mem_proj",
    )(hm, w, cos, sin)


def _dilated_attn_kernel(q0, q1, q2, k0, k1, k2, v0, v1, v2, z0, z1, z2, y0, y1, y2,
                         stage_ref, sub_ref, osub_ref, lsub_ref, onat_ref, lnat_ref, *, seq):
    q_refs, k_refs, v_refs = (q0, q1, q2), (k0, k1, k2), (v0, v1, v2)
    z_refs, y_refs = (z0, z1, z2), (y0, y1, y2)
    nblk = seq // BLOCK
    qi = lax.broadcasted_iota(jnp.int32, (BLOCK, 2 * BLOCK), 0)
    kk = lax.broadcasted_iota(jnp.int32, (BLOCK, 2 * BLOCK), 1)
    dist = qi + BLOCK - kk
    band = (dist >= 0) & (dist <= BLOCK)
    causal = (lax.broadcasted_iota(jnp.int32, (BLOCK, BLOCK), 1)
              <= lax.broadcasted_iota(jnp.int32, (BLOCK, BLOCK), 0))

    for g, (window, dil) in enumerate(DIL_GROUPS):
        sub_len = seq // dil
        nb = sub_len // BLOCK
        if dil == 1:
            def rows_of(which, start, size, g=g):
                return (q_refs, k_refs, v_refs)[which][g][0, start:start + size, :]
        else:
            slab = 3 * (g - 1)
            for which, refs in enumerate((q_refs, k_refs, v_refs)):
                stage_ref[slab + which] = refs[g][0].astype(F32)
                for r in range(dil):
                    sub_ref[slab + which, r * sub_len:(r + 1) * sub_len, :] = (
                        stage_ref[slab + which, pl.ds(r, sub_len, stride=dil), :].astype(BF16))

            def rows_of(which, start, size, slab=slab):
                return sub_ref[slab + which, start:start + size, :]

        for c in range(nblk):
            rows = slice(c * BLOCK, (c + 1) * BLOCK)
            q = rows_of(0, c * BLOCK, BLOCK)
            if c % nb == 0:
                k = rows_of(1, c * BLOCK, BLOCK)
                v = rows_of(2, c * BLOCK, BLOCK)
                mask = causal
            else:
                k = rows_of(1, (c - 1) * BLOCK, 2 * BLOCK)
                v = rows_of(2, (c - 1) * BLOCK, 2 * BLOCK)
                mask = band
            s = lax.dot_general(q, k, (((1,), (1,)), ((), ())), preferred_element_type=F32)
            s = jnp.where(mask, s, NEG_INF)
            m = jnp.max(s, axis=-1, keepdims=True)
            p = jnp.exp(s - m)
            l = jnp.sum(p, axis=-1, keepdims=True)
            o = jnp.dot(p.astype(BF16), v, preferred_element_type=F32) * (1.0 / l)
            lse = jnp.broadcast_to(m + jnp.log(l), (BLOCK, HEAD_DIM))
            if dil == 1:
                onat_ref[g, rows, :] = o
                lnat_ref[g, rows, :] = lse
            else:
                osub_ref[g - 1, rows, :] = o
                lsub_ref[g - 1, rows, :] = lse

        if dil > 1:
            for r in range(dil):
                src = slice(r * sub_len, (r + 1) * sub_len)
                onat_ref[g, pl.ds(r, sub_len, stride=dil), :] = osub_ref[g - 1, src, :]
                lnat_ref[g, pl.ds(r, sub_len, stride=dil), :] = lsub_ref[g - 1, src, :]

    chunk = 2 * BLOCK
    for t in range(seq // chunk):
        rows = slice(t * chunk, (t + 1) * chunk)
        lse = [lnat_ref[g, rows, :] for g in range(len(DIL_GROUPS))]
        mx = jnp.maximum(jnp.maximum(lse[0], lse[1]), lse[2])
        e = [jnp.exp(x - mx) for x in lse]
        inv = 1.0 / (e[0] + e[1] + e[2])
        for g in range(len(DIL_GROUPS)):
            y_refs[g][0, rows, :] = (onat_ref[g, rows, :] * (e[g] * inv)
                                     * z_refs[g][0, rows, :].astype(F32)).astype(y_refs[g].dtype)


def _dilated_attn(q_src, k_src, v_src, z_src):
    sources = (q_src, k_src, v_src, z_src)
    b, s, _ = q_src[0].shape
    n_groups = len(DIL_GROUPS)
    n_regrouped = n_groups - 1

    def head_spec(off, g):
        first = (off + g * GROUP_WIDTH) // HEAD_DIM
        return pl.BlockSpec((1, s, HEAD_DIM), lambda bi, hg: (bi, 0, first + hg))

    in_specs = [head_spec(off, g) for _, off in sources for g in range(n_groups)]
    operands = [arr for arr, _ in sources for g in range(n_groups)]
    out_spec = pl.BlockSpec((1, s, HEAD_DIM), lambda bi, hg: (bi, 0, hg))
    return pl.pallas_call(
        functools.partial(_dilated_attn_kernel, seq=s),
        grid=(b, HEADS_PER_DIL),
        in_specs=in_specs,
        out_specs=[out_spec] * n_groups,
        out_shape=[jax.ShapeDtypeStruct((b, s, GROUP_WIDTH), BF16)] * n_groups,
        scratch_shapes=[pltpu.VMEM((3 * n_regrouped, s, HEAD_DIM), F32),
                        pltpu.VMEM((3 * n_regrouped, s, HEAD_DIM), BF16),
                        pltpu.VMEM((n_regrouped, s, HEAD_DIM), F32),
                        pltpu.VMEM((n_regrouped, s, HEAD_DIM), F32),
                        pltpu.VMEM((n_groups, s, HEAD_DIM), F32),
                        pltpu.VMEM((n_groups, s, HEAD_DIM), F32)],
        compiler_params=_params(("parallel", "parallel")),
        name="dilated_attn",
    )(*operands)


def _conv_kernel(u_ref, b_ref, c_ref, z_ref, w_ref, y_ref):
    cu = c_ref[0].astype(F32) * u_ref[0].astype(F32)
    row = lax.broadcasted_iota(jnp.int32, cu.shape, 0)
    w = w_ref[...]
    acc = cu * w[CONV_K - 1:CONV_K, :]
    for lag in range(1, CONV_K):
        shifted = jnp.where(row >= lag, pltpu.roll(cu, lag, axis=0), 0.0)
        acc = acc + shifted * w[CONV_K - 1 - lag:CONV_K - lag, :]
    y_ref[0] = (b_ref[0].astype(F32) * acc * z_ref[0].astype(F32)).astype(y_ref.dtype)


def _conv(u_src, b_src, c_src, z_src, conv_w, tc):
    sources = (u_src, b_src, c_src, z_src)
    b, s, _ = u_src[0].shape

    def spec(off):
        assert off % tc == 0
        return pl.BlockSpec((1, s, tc), lambda bi, j: (bi, 0, off // tc + j))

    return pl.pallas_call(
        _conv_kernel,
        grid=(b, CONV_WIDTH // tc),
        in_specs=[spec(off) for _, off in sources]
                 + [pl.BlockSpec((CONV_K, tc), lambda bi, j: (0, j))],
        out_specs=pl.BlockSpec((1, s, tc), lambda bi, j: (bi, 0, j)),
        out_shape=jax.ShapeDtypeStruct((b, s, CONV_WIDTH), BF16),
        compiler_params=_params(("parallel", "parallel")),
        name="gated_conv",
    )(*[arr for arr, _ in sources], conv_w)


def _xattn_kernel(q_ref, z_ref, mk_ref, mv_ref, y_ref):
    for h in range(N_XATTN_HEADS):
        cols = slice(h * XATTN_HEAD_DIM, (h + 1) * XATTN_HEAD_DIM)
        s = lax.dot_general(q_ref[0, :, cols], mk_ref[0, :, cols],
                            (((1,), (1,)), ((), ())), preferred_element_type=F32)
        m = jnp.max(s, axis=-1, keepdims=True)
        p = jnp.exp(s - m)
        l = jnp.sum(p, axis=-1, keepdims=True)
        o = jnp.dot(p.astype(BF16), mv_ref[0, :, cols], preferred_element_type=F32)
        y_ref[0, :, cols] = (o * (1.0 / l) * z_ref[0, :, cols].astype(F32)).astype(y_ref.dtype)


def _xattn(q_src, z_src, mkv3, tm):
    b, s, _ = q_src[0].shape
    m_len = mkv3.shape[1]
    q_off, z_off = q_src[1], z_src[1]
    assert q_off % XATTN_WIDTH == 0 and z_off % XATTN_WIDTH == 0
    return pl.pallas_call(
        _xattn_kernel,
        grid=(b, s // tm),
        in_specs=[pl.BlockSpec((1, tm, XATTN_WIDTH), lambda bi, i: (bi, i, q_off // XATTN_WIDTH)),
                  pl.BlockSpec((1, tm, XATTN_WIDTH), lambda bi, i: (bi, i, z_off // XATTN_WIDTH)),
                  pl.BlockSpec((1, m_len, XATTN_WIDTH), lambda bi, i: (bi, 0, 0)),
                  pl.BlockSpec((1, m_len, XATTN_WIDTH), lambda bi, i: (bi, 0, 1))],
        out_specs=pl.BlockSpec((1, tm, XATTN_WIDTH), lambda bi, i: (bi, i, 0)),
        out_shape=jax.ShapeDtypeStruct((b, s, XATTN_WIDTH), BF16),
        compiler_params=_params(("parallel", "parallel")),
        name="mem_xattn",
    )(q_src[0], z_src[0], mkv3, mkv3)


def _out_proj_kernel(*refs, ni, d, n_mix):
    y_refs, w_refs = refs[:n_mix], refs[n_mix:2 * n_mix]
    x_ref, g_ref, o_ref, y2_ref, ssq_ref, scale_ref = refs[2 * n_mix:]
    i = pl.program_id(0)
    j = pl.program_id(1)

    @pl.when(j == 0)
    def _():
        @pl.when(i > 0)
        def _():
            scale_ref[...] = lax.rsqrt(ssq_ref[...] * (1.0 / d) + EPS)
        ssq_ref[...] = jnp.zeros_like(ssq_ref)

    def finish_previous_tile():
        o_ref[...] = x_ref[...] + y2_ref[j] * scale_ref[...] * g_ref[...]

    def multiply_this_tile():
        y2 = jnp.dot(y_refs[0][...], w_refs[0][...], preferred_element_type=F32)
        for y_ref, w_ref in zip(y_refs[1:], w_refs[1:]):
            y2 = y2 + jnp.dot(y_ref[...], w_ref[...], preferred_element_type=F32)
        y2_ref[j] = y2
        ssq_ref[...] += jnp.sum(y2 * y2, axis=-1, keepdims=True)

    @pl.when(i == 0)
    def _():
        multiply_this_tile()

    @pl.when((i > 0) & (i < ni))
    def _():
        finish_previous_tile()
        multiply_this_tile()

    @pl.when(i == ni)
    def _():
        finish_previous_tile()


def _out_proj(ys, w_bf, x2d, gain, tm, tn):
    rows, d = x2d.shape
    nj = d // tn
    ni = rows // tm

    def y_spec(width):
        return pl.BlockSpec((tm, width), lambda i, j: (jnp.minimum(i, ni - 1), 0))

    def w_spec(width, row_off):
        assert row_off % width == 0
        return pl.BlockSpec((width, tn),
                            lambda i, j: (row_off // width, jnp.where(i == ni, nj - 1, j)))

    io_spec = pl.BlockSpec((tm, tn),
                           lambda i, j: (jnp.maximum(i - 1, 0), jnp.where(i == 0, 0, j)))
    widths = [y.shape[1] for y in ys]
    offsets = [sum(widths[:k]) for k in range(len(widths))]
    return pl.pallas_call(
        functools.partial(_out_proj_kernel, ni=ni, d=d, n_mix=len(ys)),
        grid=(ni + 1, nj),
        in_specs=([y_spec(w) for w in widths]
                  + [w_spec(w, off) for w, off in zip(widths, offsets)]
                  + [io_spec, pl.BlockSpec((1, tn), lambda i, j: (0, j))]),
        out_specs=io_spec,
        out_shape=jax.ShapeDtypeStruct((rows, d), F32),
        scratch_shapes=[pltpu.VMEM((nj, tm, tn), F32),
                        pltpu.VMEM((tm, 1), F32),
                        pltpu.VMEM((tm, 1), F32)],
        compiler_params=_params(("arbitrary", "arbitrary")),
        name="out_proj",
    )(*ys, *([w_bf] * len(ys)), x2d, gain.reshape(1, d))


def _rope_tables(pos, half):
    inv = 1.0 / (ROPE_THETA ** (jnp.arange(half, dtype=F32) / half))
    ang = pos.astype(F32)[:, None] * inv[None, :]
    return jnp.cos(ang), jnp.sin(ang)


def kernel(x, mem, pre_norm, w_in, conv_w, mem_norm, w_mem_kv, w_out, post_norm):
    b, s, d = x.shape
    m_len = mem.shape[1]
    depth = w_in.shape[0]
    assert all(window // dil == BLOCK for window, dil in DIL_GROUPS)
    tm, tn = 1024, 512
    pos = jnp.arange(s)
    cos_a, sin_a = _rope_tables(pos, HEAD_DIM // 2)
    cosa = jnp.concatenate([cos_a, cos_a], axis=-1)
    sina = jnp.concatenate([-sin_a, sin_a], axis=-1)
    a_scale = HEAD_DIM ** -0.5
    cos_qk = jnp.stack([cosa * a_scale, cosa])
    sin_qk = jnp.stack([sina * a_scale, sina])
    cosx, sinx = _rope_tables(pos + m_len, XATTN_HEAD_DIM // 2)
    x_scale = XATTN_HEAD_DIM ** -0.5
    cos_qx, sin_qx = (cosx * x_scale)[None], (sinx * x_scale)[None]
    cos_m, sin_m = _rope_tables(jnp.arange(m_len), XATTN_HEAD_DIM // 2)
    cosm = jnp.tile(cos_m, (b, 1))
    sinm = jnp.tile(sin_m, (b, 1))

    def tile_of(off):
        assert off % tn == 0
        return off // tn

    attn_tiles, conv_tiles, x_tiles = ATTN_WIDTH // tn, CONV_WIDTH // tn, XATTN_WIDTH // tn

    def view(t):
        return t.reshape(b, s, t.shape[-1])

    for layer in range(depth):
        x2d = x.reshape(b * s, d)
        h = _rmsnorm(x2d, pre_norm[layer], tm=512)
        w = w_in[layer]
        common = dict(seq=s, tm=tm, tn=tn)

        qk = view(_in_proj(h, w, lambda j: tile_of(OFF_QA) + j, 2 * attn_tiles, _rope_half_lane,
                           [cos_qk, sin_qk], lambda j: j // attn_tiles, name="in_proj_qk",
                           **common))
        plain = view(_in_proj(
            h, w, lambda j: jnp.where(j < attn_tiles, tile_of(OFF_VA) + j,
                                      tile_of(OFF_UC) - attn_tiles + j),
            attn_tiles + 3 * conv_tiles, lambda acc: acc, [], None, name="in_proj_plain",
            **common))
        gates = view(_in_proj(
            h, w, lambda j: jnp.where(j < attn_tiles, tile_of(OFF_ZA) + j,
                                      jnp.where(j < attn_tiles + conv_tiles,
                                                tile_of(OFF_ZC) - attn_tiles + j,
                                                tile_of(OFF_ZX) - attn_tiles - conv_tiles + j)),
            attn_tiles + conv_tiles + x_tiles, _silu, [], None, name="in_proj_gates", **common))
        qx = view(_in_proj(h, w, lambda j: tile_of(OFF_QX) + j, x_tiles, _rope_two_tiles,
                           [cos_qx, sin_qx], lambda j: 0, name="in_proj_qx", **common))

        hm = _rmsnorm(mem.reshape(b * m_len, d), mem_norm[layer], tm=256)
        mkv = _mem_proj(hm, w_mem_kv[layer], cosm, sinm, tn=512)

        y_groups = _dilated_attn((qk, 0), (qk, ATTN_WIDTH), (plain, 0), (gates, 0))
        y_groups = [y.reshape(b * s, GROUP_WIDTH) for y in y_groups]
        y_conv = _conv((plain, ATTN_WIDTH), (plain, ATTN_WIDTH + CONV_WIDTH),
                       (plain, ATTN_WIDTH + 2 * CONV_WIDTH), (gates, ATTN_WIDTH),
                       conv_w[layer], tc=256).reshape(b * s, CONV_WIDTH)
        y_x = _xattn((qx, 0), (gates, ATTN_WIDTH + CONV_WIDTH),
                     mkv.reshape(b, m_len, 2 * XATTN_WIDTH), tm=512)
        y_x = y_x.reshape(b * s, XATTN_WIDTH)

        out = _out_proj([*y_groups, y_conv, y_x], w_out[layer].astype(BF16), x2d,
                        post_norm[layer], tm=1024, tn=512)
        x = out.reshape(b, s, d)
    return x
```

```python
import functools

import jax
import jax.numpy as jnp
import numpy as np
from jax import lax
from jax.experimental import pallas as pl
from jax.experimental.pallas import tpu as pltpu

D_MODEL = 4096
MEM_LEN = 256
HEAD_DIM = 128
DIL_GROUPS = ((128, 1), (512, 4), (2048, 16))
ATTN_WIDTH = 3 * D_MODEL // 8
CONV_WIDTH = 3 * D_MODEL // 8
XATTN_WIDTH = D_MODEL // 4
N_XATTN_HEADS = 4
XATTN_HEAD_DIM = XATTN_WIDTH // N_XATTN_HEADS
HEADS_PER_DIL = 4
GROUP_WIDTH = HEADS_PER_DIL * HEAD_DIM
CONV_K = 3
IN_COLS = 4 * ATTN_WIDTH + 4 * CONV_WIDTH + 2 * XATTN_WIDTH
BLOCK = 128
ROPE_THETA = 10000.0
EPS = 1e-6
NEG_INF = -1e30

OFF_QA = 0
OFF_KA = OFF_QA + ATTN_WIDTH
OFF_VA = OFF_KA + ATTN_WIDTH
OFF_ZA = OFF_VA + ATTN_WIDTH
OFF_UC = OFF_ZA + ATTN_WIDTH
OFF_BC = OFF_UC + CONV_WIDTH
OFF_CC = OFF_BC + CONV_WIDTH
OFF_ZC = OFF_CC + CONV_WIDTH
OFF_QX = OFF_ZC + CONV_WIDTH
OFF_ZX = OFF_QX + XATTN_WIDTH

V7X_VMEM_LIMIT_BYTES = 56 * 1024 * 1024
LANES = 128

BF16 = jnp.bfloat16
F32 = jnp.float32


def _params(semantics):
    return pltpu.CompilerParams(dimension_semantics=semantics,
                                vmem_limit_bytes=V7X_VMEM_LIMIT_BYTES)


def _rmsnorm_kernel(x_ref, g_ref, o_ref):
    x = x_ref[...]
    ms = jnp.mean(x * x, axis=-1, keepdims=True)
    o_ref[...] = (x * lax.rsqrt(ms + EPS) * g_ref[...]).astype(o_ref.dtype)


def _rmsnorm(x2d, gain, tm):
    rows, d = x2d.shape
    return pl.pallas_call(
        _rmsnorm_kernel,
        grid=(rows // tm,),
        in_specs=[pl.BlockSpec((tm, d), lambda i: (i, 0)),
                  pl.BlockSpec((1, d), lambda i: (0, 0))],
        out_specs=pl.BlockSpec((tm, d), lambda i: (i, 0)),
        out_shape=jax.ShapeDtypeStruct((rows, d), BF16),
        compiler_params=_params(("parallel",)),
        name="rmsnorm",
    )(x2d, gain.reshape(1, d))


def _silu(z):
    return z * (1.0 / (1.0 + jnp.exp(-z)))


def _rope_half_lane(acc, cos, sin_signed):
    outs = []
    for h in range(acc.shape[1] // HEAD_DIM):
        xh = acc[:, h * HEAD_DIM:(h + 1) * HEAD_DIM]
        outs.append(xh * cos + pltpu.roll(xh, HEAD_DIM // 2, axis=1) * sin_signed)
    return jnp.concatenate(outs, axis=1)


def _rope_two_tiles(acc, cos, sin):
    outs = []
    for h in range(acc.shape[1] // XATTN_HEAD_DIM):
        x1 = acc[:, h * XATTN_HEAD_DIM:h * XATTN_HEAD_DIM + LANES]
        x2 = acc[:, h * XATTN_HEAD_DIM + LANES:(h + 1) * XATTN_HEAD_DIM]
        outs.append(x1 * cos - x2 * sin)
        outs.append(x2 * cos + x1 * sin)
    return jnp.concatenate(outs, axis=1)


def _in_proj_kernel(*refs, ni, n_steps, epilogue, n_tables):
    h_ref, w_ref = refs[:2]
    table_refs = refs[2:2 + n_tables]
    o_ref, wbf_ref, acc_ref = refs[2 + n_tables:]
    t = pl.program_id(0)

    @pl.when((t % ni == 0) & (t < n_steps))
    def _():
        wbf_ref[...] = w_ref[...].astype(BF16)

    def multiply(slot):
        acc_ref[slot] = jnp.dot(h_ref[...], wbf_ref[...], preferred_element_type=F32)

    def finish(slot):
        o_ref[...] = epilogue(acc_ref[slot], *[r[0] for r in table_refs]).astype(o_ref.dtype)

    @pl.when(t == 0)
    def _():
        multiply(0)

    for parity in range(2):
        @pl.when((t > 0) & (t < n_steps) & (t % 2 == parity))
        def _(parity=parity):
            finish(1 - parity)
            multiply(parity)

    @pl.when(t == n_steps)
    def _():
        finish((n_steps - 1) % 2)


def _in_proj(h, w, col_tile, n_col_tiles, epilogue, tables, table_kind, seq, tm, tn, name):
    rows, d = h.shape
    ni = rows // tm
    n_steps = n_col_tiles * ni
    tiles_per_seq = seq // tm

    def cur(t):
        c = jnp.minimum(t, n_steps - 1)
        return c // ni, c % ni

    def prev(t):
        p = jnp.maximum(t - 1, 0)
        return p // ni, p % ni

    table_spec = pl.BlockSpec(
        (1, tm, LANES), lambda t: (table_kind(prev(t)[0]), prev(t)[1] % tiles_per_seq, 0))
    return pl.pallas_call(
        functools.partial(_in_proj_kernel, ni=ni, n_steps=n_steps, epilogue=epilogue,
                          n_tables=len(tables)),
        grid=(n_steps + 1,),
        in_specs=[pl.BlockSpec((tm, d), lambda t: (cur(t)[1], 0)),
                  pl.BlockSpec((d, tn), lambda t: (0, col_tile(cur(t)[0])))]
                 + [table_spec] * len(tables),
        out_specs=pl.BlockSpec((tm, tn), lambda t: (prev(t)[1], prev(t)[0])),
        out_shape=jax.ShapeDtypeStruct((rows, n_col_tiles * tn), BF16),
        scratch_shapes=[pltpu.VMEM((d, tn), BF16), pltpu.VMEM((2, tm, tn), F32)],
        compiler_params=_params(("arbitrary",)),
        name=name,
    )(h, w, *tables)


def _mem_proj_kernel(h_ref, w_ref, cos_ref, sin_ref, o_ref, *, tn):
    j = pl.program_id(0)
    acc = jnp.dot(h_ref[...], w_ref[...].astype(BF16), preferred_element_type=F32)

    @pl.when(j < XATTN_WIDTH // tn)
    def _():
        o_ref[...] = _rope_two_tiles(acc, cos_ref[...], sin_ref[...]).astype(o_ref.dtype)

    @pl.when(j >= XATTN_WIDTH // tn)
    def _():
        o_ref[...] = acc.astype(o_ref.dtype)


def _mem_proj(hm, w, cos, sin, tn):
    rows, d = hm.shape
    ncols = w.shape[1]
    tab = pl.BlockSpec((rows, LANES), lambda j: (0, 0))
    return pl.pallas_call(
        functools.partial(_mem_proj_kernel, tn=tn),
        grid=(ncols // tn,),
        in_specs=[pl.BlockSpec((rows, d), lambda j: (0, 0)),
                  pl.BlockSpec((d, tn), lambda j: (0, j)),
                  tab, tab],
        out_specs=pl.BlockSpec((rows, tn), lambda j: (0, j)),
        out_shape=jax.ShapeDtypeStruct((rows, ncols), BF16),
        compiler_params=_params(("parallel",)),
        name="mem_proj",
    )(hm, w, cos, sin)


def _dilated_attn_kernel(q0, q1, q2, k0, k1, k2, v0, v1, v2, z0, z1, z2, y0, y1, y2,
                         stage_ref, sub_ref, osub_ref, lsub_ref, onat_ref, lnat_ref, *, seq):
    q_refs, k_refs, v_refs = (q0, q1, q2), (k0, k1, k2), (v0, v1, v2)
    z_refs, y_refs = (z0, z1, z2), (y0, y1, y2)
    nblk = seq // BLOCK
    qi = lax.broadcasted_iota(jnp.int32, (BLOCK, 2 * BLOCK), 0)
    kk = lax.broadcasted_iota(jnp.int32, (BLOCK, 2 * BLOCK), 1)
    dist = qi + BLOCK - kk
    band = (dist >= 0) & (dist <= BLOCK)
    causal = (lax.broadcasted_iota(jnp.int32, (BLOCK, BLOCK), 1)
              <= lax.broadcasted_iota(jnp.int32, (BLOCK, BLOCK), 0))

    for g, (window, dil) in enumerate(DIL_GROUPS):
        sub_len = seq // dil
        nb = sub_len // BLOCK
        if dil == 1:
            def rows_of(which, start, size, g=g):
                return (q_refs, k_refs, v_refs)[which][g][0, start:start + size, :]
        else:
            slab = 3 * (g - 1)
            for which, refs in enumerate((q_refs, k_refs, v_refs)):
                stage_ref[slab + which] = refs[g][0].astype(F32)
                for r in range(dil):
                    sub_ref[slab + which, r * sub_len:(r + 1) * sub_len, :] = (
                        stage_ref[slab + which, pl.ds(r, sub_len, stride=dil), :].astype(BF16))

            def rows_of(which, start, size, slab=slab):
                return sub_ref[slab + which, start:start + size, :]

        for c in range(nblk):
            rows = slice(c * BLOCK, (c + 1) * BLOCK)
            q = rows_of(0, c * BLOCK, BLOCK)
            if c % nb == 0:
                k = rows_of(1, c * BLOCK, BLOCK)
                v = rows_of(2, c * BLOCK, BLOCK)
                mask = causal
            else:
                k = rows_of(1, (c - 1) * BLOCK, 2 * BLOCK)
                v = rows_of(2, (c - 1) * BLOCK, 2 * BLOCK)
                mask = band
            s = lax.dot_general(q, k, (((1,), (1,)), ((), ())), preferred_element_type=F32)
            s = jnp.where(mask, s, NEG_INF)
            m = jnp.max(s, axis=-1, keepdims=True)
            p = jnp.exp(s - m)
            l = jnp.sum(p, axis=-1, keepdims=True)
            o = jnp.dot(p.astype(BF16), v, preferred_element_type=F32) * (1.0 / l)
            lse = jnp.broadcast_to(m + jnp.log(l), (BLOCK, HEAD_DIM))
            if dil == 1:
                onat_ref[g, rows, :] = o
                lnat_ref[g, rows, :] = lse
            else:
                osub_ref[g - 1, rows, :] = o
                lsub_ref[g - 1, rows, :] = lse

        if dil > 1:
            for r in range(dil):
                src = slice(r * sub_len, (r + 1) * sub_len)
                onat_ref[g, pl.ds(r, sub_len, stride=dil), :] = osub_ref[g - 1, src, :]
                lnat_ref[g, pl.ds(r, sub_len, stride=dil), :] = lsub_ref[g - 1, src, :]

    chunk = 2 * BLOCK
    for t in range(seq // chunk):
        rows = slice(t * chunk, (t + 1) * chunk)
        lse = [lnat_ref[g, rows, :] for g in range(len(DIL_GROUPS))]
        mx = jnp.maximum(jnp.maximum(lse[0], lse[1]), lse[2])
        e = [jnp.exp(x - mx) for x in lse]
        inv = 1.0 / (e[0] + e[1] + e[2])
        for g in range(len(DIL_GROUPS)):
            y_refs[g][0, rows, :] = (onat_ref[g, rows, :] * (e[g] * inv)
                                     * z_refs[g][0, rows, :].astype(F32)).astype(y_refs[g].dtype)


def _dilated_attn(q_src, k_src, v_src, z_src):
    sources = (q_src, k_src, v_src, z_src)
    b, s, _ = q_src[0].shape
    n_groups = len(DIL_GROUPS)
    n_regrouped = n_groups - 1

    def head_spec(off, g):
        first = (off + g * GROUP_WIDTH) // HEAD_DIM
        return pl.BlockSpec((1, s, HEAD_DIM), lambda bi, hg: (bi, 0, first + hg))

    in_specs = [head_spec(off, g) for _, off in sources for g in range(n_groups)]
    operands = [arr for arr, _ in sources for g in range(n_groups)]
    out_spec = pl.BlockSpec((1, s, HEAD_DIM), lambda bi, hg: (bi, 0, hg))
    return pl.pallas_call(
        functools.partial(_dilated_attn_kernel, seq=s),
        grid=(b, HEADS_PER_DIL),
        in_specs=in_specs,
        out_specs=[out_spec] * n_groups,
        out_shape=[jax.ShapeDtypeStruct((b, s, GROUP_WIDTH), BF16)] * n_groups,
        scratch_shapes=[pltpu.VMEM((3 * n_regrouped, s, HEAD_DIM), F32),
                        pltpu.VMEM((3 * n_regrouped, s, HEAD_DIM), BF16),
                        pltpu.VMEM((n_regrouped, s, HEAD_DIM), F32),
                        pltpu.VMEM((n_regrouped, s, HEAD_DIM), F32),
                        pltpu.VMEM((n_groups, s, HEAD_DIM), F32),
                        pltpu.VMEM((n_groups, s, HEAD_DIM), F32)],
        compiler_params=_params(("parallel", "parallel")),
        name="dilated_attn",
    )(*operands)


def _conv_kernel(u_ref, b_ref, c_ref, z_ref, w_ref, y_ref):
    cu = c_ref[0].astype(F32) * u_ref[0].astype(F32)
    row = lax.broadcasted_iota(jnp.int32, cu.shape, 0)
    w = w_ref[...]
    acc = cu * w[CONV_K - 1:CONV_K, :]
    for lag in range(1, CONV_K):
        shifted = jnp.where(row >= lag, pltpu.roll(cu, lag, axis=0), 0.0)
        acc = acc + shifted * w[CONV_K - 1 - lag:CONV_K - lag, :]
    y_ref[0] = (b_ref[0].astype(F32) * acc * z_ref[0].astype(F32)).astype(y_ref.dtype)


def _conv(u_src, b_src, c_src, z_src, conv_w, tc):
    sources = (u_src, b_src, c_src, z_src)
    b, s, _ = u_src[0].shape

    def spec(off):
        assert off % tc == 0
        return pl.BlockSpec((1, s, tc), lambda bi, j: (bi, 0, off // tc + j))

    return pl.pallas_call(
        _conv_kernel,
        grid=(b, CONV_WIDTH // tc),
        in_specs=[spec(off) for _, off in sources]
                 + [pl.BlockSpec((CONV_K, tc), lambda bi, j: (0, j))],
        out_specs=pl.BlockSpec((1, s, tc), lambda bi, j: (bi, 0, j)),
        out_shape=jax.ShapeDtypeStruct((b, s, CONV_WIDTH), BF16),
        compiler_params=_params(("parallel", "parallel")),
        name="gated_conv",
    )(*[arr for arr, _ in sources], conv_w)


def _xattn_kernel(q_ref, z_ref, mk_ref, mv_ref, y_ref):
    for h in range(N_XATTN_HEADS):
        cols = slice(h * XATTN_HEAD_DIM, (h + 1) * XATTN_HEAD_DIM)
        s = lax.dot_general(q_ref[0, :, cols], mk_ref[0, :, cols],
                            (((1,), (1,)), ((), ())), preferred_element_type=F32)
        m = jnp.max(s, axis=-1, keepdims=True)
        p = jnp.exp(s - m)
        l = jnp.sum(p, axis=-1, keepdims=True)
        o = jnp.dot(p.astype(BF16), mv_ref[0, :, cols], preferred_element_type=F32)
        y_ref[0, :, cols] = (o * (1.0 / l) * z_ref[0, :, cols].astype(F32)).astype(y_ref.dtype)


def _xattn(q_src, z_src, mkv3, tm):
    b, s, _ = q_src[0].shape
    m_len = mkv3.shape[1]
    q_off, z_off = q_src[1], z_src[1]
    assert q_off % XATTN_WIDTH == 0 and z_off % XATTN_WIDTH == 0
    return pl.pallas_call(
        _xattn_kernel,
        grid=(b, s // tm),
        in_specs=[pl.BlockSpec((1, tm, XATTN_WIDTH), lambda bi, i: (bi, i, q_off // XATTN_WIDTH)),
                  pl.BlockSpec((1, tm, XATTN_WIDTH), lambda bi, i: (bi, i, z_off // XATTN_WIDTH)),
                  pl.BlockSpec((1, m_len, XATTN_WIDTH), lambda bi, i: (bi, 0, 0)),
                  pl.BlockSpec((1, m_len, XATTN_WIDTH), lambda bi, i: (bi, 0, 1))],
        out_specs=pl.BlockSpec((1, tm, XATTN_WIDTH), lambda bi, i: (bi, i, 0)),
        out_shape=jax.ShapeDtypeStruct((b, s, XATTN_WIDTH), BF16),
        compiler_params=_params(("parallel", "parallel")),
        name="mem_xattn",
    )(q_src[0], z_src[0], mkv3, mkv3)


def _out_proj_kernel(*refs, ni, d, n_mix):
    y_refs, w_refs = refs[:n_mix], refs[n_mix:2 * n_mix]
    x_ref, g_ref, o_ref, y2_ref, ssq_ref, scale_ref = refs[2 * n_mix:]
    i = pl.program_id(0)
    j = pl.program_id(1)

    @pl.when(j == 0)
    def _():
        @pl.when(i > 0)
        def _():
            scale_ref[...] = lax.rsqrt(ssq_ref[...] * (1.0 / d) + EPS)
        ssq_ref[...] = jnp.zeros_like(ssq_ref)

    def finish_previous_tile():
        o_ref[...] = x_ref[...] + y2_ref[j] * scale_ref[...] * g_ref[...]

    def multiply_this_tile():
        y2 = jnp.dot(y_refs[0][...], w_refs[0][...], preferred_element_type=F32)
        for y_ref, w_ref in zip(y_refs[1:], w_refs[1:]):
            y2 = y2 + jnp.dot(y_ref[...], w_ref[...], preferred_element_type=F32)
        y2_ref[j] = y2
        ssq_ref[...] += jnp.sum(y2 * y2, axis=-1, keepdims=True)

    @pl.when(i == 0)
    def _():
        multiply_this_tile()

    @pl.when((i > 0) & (i < ni))
    def _():
        finish_previous_tile()
        multiply_this_tile()

    @pl.when(i == ni)
    def _():
        finish_previous_tile()


def _out_proj(ys, w_bf, x2d, gain, tm, tn):
    rows, d = x2d.shape
    nj = d // tn
    ni = rows // tm

    def y_spec(width):
        return pl.BlockSpec((tm, width), lambda i, j: (jnp.minimum(i, ni - 1), 0))

    def w_spec(width, row_off):
        assert row_off % width == 0
        return pl.BlockSpec((width, tn),
                            lambda i, j: (row_off // width, jnp.where(i == ni, nj - 1, j)))

    io_spec = pl.BlockSpec((tm, tn),
                           lambda i, j: (jnp.maximum(i - 1, 0), jnp.where(i == 0, 0, j)))
    widths = [y.shape[1] for y in ys]
    offsets = [sum(widths[:k]) for k in range(len(widths))]
    return pl.pallas_call(
        functools.partial(_out_proj_kernel, ni=ni, d=d, n_mix=len(ys)),
        grid=(ni + 1, nj),
        in_specs=([y_spec(w) for w in widths]
                  + [w_spec(w, off) for w, off in zip(widths, offsets)]
                  + [io_spec, pl.BlockSpec((1, tn), lambda i, j: (0, j))]),
        out_specs=io_spec,
        out_shape=jax.ShapeDtypeStruct((rows, d), F32),
        scratch_shapes=[pltpu.VMEM((nj, tm, tn), F32),
                        pltpu.VMEM((tm, 1), F32),
                        pltpu.VMEM((tm, 1), F32)],
        compiler_params=_params(("arbitrary", "arbitrary")),
        name="out_proj",
    )(*ys, *([w_bf] * len(ys)), x2d, gain.reshape(1, d))


def _rope_tables(pos, half):
    inv = 1.0 / (ROPE_THETA ** (jnp.arange(half, dtype=F32) / half))
    ang = pos.astype(F32)[:, None] * inv[None, :]
    return jnp.cos(ang), jnp.sin(ang)


def kernel(x, mem, pre_norm, w_in, conv_w, mem_norm, w_mem_kv, w_out, post_norm):
    b, s, d = x.shape
    m_len = mem.shape[1]
    depth = w_in.shape[0]
    assert all(window // dil == BLOCK for window, dil in DIL_GROUPS)
    pos = jnp.arange(s)
    cos_a, sin_a = _rope_tables(pos, HEAD_DIM // 2)
    cosa = jnp.concatenate([cos_a, cos_a], axis=-1)
    sina = jnp.concatenate([-sin_a, sin_a], axis=-1)
    a_scale = HEAD_DIM ** -0.5
    cos_qk = jnp.stack([cosa * a_scale, cosa])
    sin_qk = jnp.stack([sina * a_scale, sina])
    cosx, sinx = _rope_tables(pos + m_len, XATTN_HEAD_DIM // 2)
    x_scale = XATTN_HEAD_DIM ** -0.5
    cos_qx, sin_qx = (cosx * x_scale)[None], (sinx * x_scale)[None]
    cos_m, sin_m = _rope_tables(jnp.arange(m_len), XATTN_HEAD_DIM // 2)
    cosm = jnp.tile(cos_m, (b, 1))
    sinm = jnp.tile(sin_m, (b, 1))

    def tile_of(off, tn):
        assert off % tn == 0
        return off // tn

    def view(t):
        return t.reshape(b, s, t.shape[-1])

    wide = dict(seq=s, tm=512, tn=768)
    narrow = dict(seq=s, tm=1024, tn=512)
    wide_tiles = ATTN_WIDTH // wide["tn"]
    assert CONV_WIDTH == ATTN_WIDTH
    narrow_tiles = XATTN_WIDTH // narrow["tn"]

    for layer in range(depth):
        x2d = x.reshape(b * s, d)
        h = _rmsnorm(x2d, pre_norm[layer], tm=512)
        w = w_in[layer]

        qk = view(_in_proj(h, w, lambda j: tile_of(OFF_QA, wide["tn"]) + j, 2 * wide_tiles,
                           _rope_half_lane, [cos_qk, sin_qk], lambda j: j // wide_tiles,
                           name="in_proj_qk", **wide))
        plain = view(_in_proj(
            h, w, lambda j: jnp.where(j < wide_tiles, tile_of(OFF_VA, wide["tn"]) + j,
                                      tile_of(OFF_UC, wide["tn"]) - wide_tiles + j),
            4 * wide_tiles, lambda acc: acc, [], None, name="in_proj_plain", **wide))
        gates = view(_in_proj(
            h, w, lambda j: jnp.where(j < wide_tiles, tile_of(OFF_ZA, wide["tn"]) + j,
                                      tile_of(OFF_ZC, wide["tn"]) - wide_tiles + j),
            2 * wide_tiles, _silu, [], None, name="in_proj_gates", **wide))
        qx = view(_in_proj(h, w, lambda j: tile_of(OFF_QX, narrow["tn"]) + j, narrow_tiles,
                           _rope_two_tiles, [cos_qx, sin_qx], lambda j: 0, name="in_proj_qx",
                           **narrow))
        zx = view(_in_proj(h, w, lambda j: tile_of(OFF_ZX, narrow["tn"]) + j, narrow_tiles,
                           _silu, [], None, name="in_proj_zx", **narrow))

        hm = _rmsnorm(mem.reshape(b * m_len, d), mem_norm[layer], tm=256)
        mkv = _mem_proj(hm, w_mem_kv[layer], cosm, sinm, tn=512)

        y_groups = _dilated_attn((qk, 0), (qk, ATTN_WIDTH), (plain, 0), (gates, 0))
        y_groups = [y.reshape(b * s, GROUP_WIDTH) for y in y_groups]
        y_conv = _conv((plain, ATTN_WIDTH), (plain, ATTN_WIDTH + CONV_WIDTH),
                       (plain, ATTN_WIDTH + 2 * CONV_WIDTH), (gates, ATTN_WIDTH),
                       conv_w[layer], tc=256).reshape(b * s, CONV_WIDTH)
        y_x = _xattn((qx, 0), (zx, 0), mkv.reshape(b, m_len, 2 * XATTN_WIDTH), tm=512)
        y_x = y_x.reshape(b * s, XATTN_WIDTH)

        out = _out_proj([*y_groups, y_conv, y_x], w_out[layer].astype(BF16), x2d,
                        post_norm[layer], tm=1024, tn=512)
        x = out.reshape(b, s, d)
    return x
```

```python
import functools

import jax
import jax.numpy as jnp
import numpy as np
from jax import lax
from jax.experimental import pallas as pl
from jax.experimental.pallas import tpu as pltpu

D_MODEL = 4096
MEM_LEN = 256
HEAD_DIM = 128
DIL_GROUPS = ((128, 1), (512, 4), (2048, 16))
ATTN_WIDTH = 3 * D_MODEL // 8
CONV_WIDTH = 3 * D_MODEL // 8
XATTN_WIDTH = D_MODEL // 4
N_XATTN_HEADS = 4
XATTN_HEAD_DIM = XATTN_WIDTH // N_XATTN_HEADS
HEADS_PER_DIL = 4
GROUP_WIDTH = HEADS_PER_DIL * HEAD_DIM
CONV_K = 3
IN_COLS = 4 * ATTN_WIDTH + 4 * CONV_WIDTH + 2 * XATTN_WIDTH
BLOCK = 128
ROPE_THETA = 10000.0
EPS = 1e-6
NEG_INF = -1e30

OFF_QA = 0
OFF_KA = OFF_QA + ATTN_WIDTH
OFF_VA = OFF_KA + ATTN_WIDTH
OFF_ZA = OFF_VA + ATTN_WIDTH
OFF_UC = OFF_ZA + ATTN_WIDTH
OFF_BC = OFF_UC + CONV_WIDTH
OFF_CC = OFF_BC + CONV_WIDTH
OFF_ZC = OFF_CC + CONV_WIDTH
OFF_QX = OFF_ZC + CONV_WIDTH
OFF_ZX = OFF_QX + XATTN_WIDTH

V7X_VMEM_LIMIT_BYTES = 56 * 1024 * 1024
LANES = 128
W_PIECES = 4

BF16 = jnp.bfloat16
F32 = jnp.float32


def _params(semantics):
    return pltpu.CompilerParams(dimension_semantics=semantics,
                                vmem_limit_bytes=V7X_VMEM_LIMIT_BYTES)


def _rmsnorm_kernel(x_ref, g_ref, o_ref):
    x = x_ref[...]
    ms = jnp.mean(x * x, axis=-1, keepdims=True)
    o_ref[...] = (x * lax.rsqrt(ms + EPS) * g_ref[...]).astype(o_ref.dtype)


def _rmsnorm(x2d, gain, tm):
    rows, d = x2d.shape
    return pl.pallas_call(
        _rmsnorm_kernel,
        grid=(rows // tm,),
        in_specs=[pl.BlockSpec((tm, d), lambda i: (i, 0)),
                  pl.BlockSpec((1, d), lambda i: (0, 0))],
        out_specs=pl.BlockSpec((tm, d), lambda i: (i, 0)),
        out_shape=jax.ShapeDtypeStruct((rows, d), BF16),
        compiler_params=_params(("parallel",)),
        name="rmsnorm",
    )(x2d, gain.reshape(1, d))


def _silu(z):
    return z * (1.0 / (1.0 + jnp.exp(-z)))


def _rope_half_lane(acc, cos, sin_signed):
    outs = []
    for h in range(acc.shape[1] // HEAD_DIM):
        xh = acc[:, h * HEAD_DIM:(h + 1) * HEAD_DIM]
        outs.append(xh * cos + pltpu.roll(xh, HEAD_DIM // 2, axis=1) * sin_signed)
    return jnp.concatenate(outs, axis=1)


def _rope_two_tiles(acc, cos, sin):
    outs = []
    for h in range(acc.shape[1] // XATTN_HEAD_DIM):
        x1 = acc[:, h * XATTN_HEAD_DIM:h * XATTN_HEAD_DIM + LANES]
        x2 = acc[:, h * XATTN_HEAD_DIM + LANES:(h + 1) * XATTN_HEAD_DIM]
        outs.append(x1 * cos - x2 * sin)
        outs.append(x2 * cos + x1 * sin)
    return jnp.concatenate(outs, axis=1)


def _in_proj_kernel(*refs, ni, n_steps, epilogue, n_tables):
    h_ref = refs[0]
    w_refs = refs[1:1 + W_PIECES]
    table_refs = refs[1 + W_PIECES:1 + W_PIECES + n_tables]
    o_ref, wbf_ref, acc_ref = refs[1 + W_PIECES + n_tables:]
    t = pl.program_id(0)
    piece_rows = wbf_ref.shape[0] // W_PIECES

    @pl.when((t % ni == 0) & (t < n_steps))
    def _():
        for p, w_ref in enumerate(w_refs):
            wbf_ref[p * piece_rows:(p + 1) * piece_rows, :] = w_ref[...].astype(BF16)

    def multiply(slot):
        acc_ref[slot] = jnp.dot(h_ref[...], wbf_ref[...], preferred_element_type=F32)

    def finish(slot):
        o_ref[...] = epilogue(acc_ref[slot], *[r[0] for r in table_refs]).astype(o_ref.dtype)

    @pl.when(t == 0)
    def _():
        multiply(0)

    for parity in range(2):
        @pl.when((t > 0) & (t < n_steps) & (t % 2 == parity))
        def _(parity=parity):
            finish(1 - parity)
            multiply(parity)

    @pl.when(t == n_steps)
    def _():
        finish((n_steps - 1) % 2)


def _in_proj(h, w, col_tile, n_col_tiles, epilogue, tables, table_kind, seq, tm, tn, name):
    rows, d = h.shape
    ni = rows // tm
    n_steps = n_col_tiles * ni
    tiles_per_seq = seq // tm

    def cur(t):
        c = jnp.minimum(t, n_steps - 1)
        return c // ni, c % ni

    def prev(t):
        p = jnp.maximum(t - 1, 0)
        return p // ni, p % ni

    table_spec = pl.BlockSpec(
        (1, tm, LANES), lambda t: (table_kind(prev(t)[0]), prev(t)[1] % tiles_per_seq, 0))

    def w_piece_spec(p):
        def index(t):
            j, i = cur(t)
            ahead = (i >= ni - W_PIECES + p).astype(jnp.int32)
            return p, col_tile(jnp.minimum(j + ahead, n_col_tiles - 1))
        return pl.BlockSpec((d // W_PIECES, tn), index)

    assert ni >= W_PIECES and d % W_PIECES == 0
    return pl.pallas_call(
        functools.partial(_in_proj_kernel, ni=ni, n_steps=n_steps, epilogue=epilogue,
                          n_tables=len(tables)),
        grid=(n_steps + 1,),
        in_specs=[pl.BlockSpec((tm, d), lambda t: (cur(t)[1], 0))]
                 + [w_piece_spec(p) for p in range(W_PIECES)]
                 + [table_spec] * len(tables),
        out_specs=pl.BlockSpec((tm, tn), lambda t: (prev(t)[1], prev(t)[0])),
        out_shape=jax.ShapeDtypeStruct((rows, n_col_tiles * tn), BF16),
        scratch_shapes=[pltpu.VMEM((d, tn), BF16), pltpu.VMEM((2, tm, tn), F32)],
        compiler_params=_params(("arbitrary",)),
        name=name,
    )(h, *([w] * W_PIECES), *tables)


def _mem_proj_kernel(h_ref, w_ref, cos_ref, sin_ref, o_ref, *, tn):
    j = pl.program_id(0)
    acc = jnp.dot(h_ref[...], w_ref[...].astype(BF16), preferred_element_type=F32)

    @pl.when(j < XATTN_WIDTH // tn)
    def _():
        o_ref[...] = _rope_two_tiles(acc, cos_ref[...], sin_ref[...]).astype(o_ref.dtype)

    @pl.when(j >= XATTN_WIDTH // tn)
    def _():
        o_ref[...] = acc.astype(o_ref.dtype)


def _mem_proj(hm, w, cos, sin, tn):
    rows, d = hm.shape
    ncols = w.shape[1]
    tab = pl.BlockSpec((rows, LANES), lambda j: (0, 0))
    return pl.pallas_call(
        functools.partial(_mem_proj_kernel, tn=tn),
        grid=(ncols // tn,),
        in_specs=[pl.BlockSpec((rows, d), lambda j: (0, 0)),
                  pl.BlockSpec((d, tn), lambda j: (0, j)),
                  tab, tab],
        out_specs=pl.BlockSpec((rows, tn), lambda j: (0, j)),
        out_shape=jax.ShapeDtypeStruct((rows, ncols), BF16),
        compiler_params=_params(("parallel",)),
        name="mem_proj",
    )(hm, w, cos, sin)


def _dilated_attn_kernel(q0, q1, q2, k0, k1, k2, v0, v1, v2, z0, z1, z2, y0, y1, y2,
                         stage_ref, sub_ref, osub_ref, lsub_ref, onat_ref, lnat_ref, *, seq):
    q_refs, k_refs, v_refs = (q0, q1, q2), (k0, k1, k2), (v0, v1, v2)
    z_refs, y_refs = (z0, z1, z2), (y0, y1, y2)
    nblk = seq // BLOCK
    qi = lax.broadcasted_iota(jnp.int32, (BLOCK, 2 * BLOCK), 0)
    kk = lax.broadcasted_iota(jnp.int32, (BLOCK, 2 * BLOCK), 1)
    dist = qi + BLOCK - kk
    band = (dist >= 0) & (dist <= BLOCK)
    causal = (lax.broadcasted_iota(jnp.int32, (BLOCK, BLOCK), 1)
              <= lax.broadcasted_iota(jnp.int32, (BLOCK, BLOCK), 0))

    rows_of = []
    for g, (window, dil) in enumerate(DIL_GROUPS):
        sub_len = seq // dil
        if dil == 1:
            def group_rows(which, start, size, g=g):
                return (q_refs, k_refs, v_refs)[which][g][0, start:start + size, :]
        else:
            slab = 3 * (g - 1)
            for which, refs in enumerate((q_refs, k_refs, v_refs)):
                stage_ref[slab + which] = refs[g][0].astype(F32)
                for r in range(dil):
                    sub_ref[slab + which, r * sub_len:(r + 1) * sub_len, :] = (
                        stage_ref[slab + which, pl.ds(r, sub_len, stride=dil), :].astype(BF16))

            def group_rows(which, start, size, slab=slab):
                return sub_ref[slab + which, start:start + size, :]
        rows_of.append(group_rows)

    for c in range(nblk):
        rows = slice(c * BLOCK, (c + 1) * BLOCK)
        for g, (window, dil) in enumerate(DIL_GROUPS):
            nb = seq // dil // BLOCK
            q = rows_of[g](0, c * BLOCK, BLOCK)
            if c % nb == 0:
                k = rows_of[g](1, c * BLOCK, BLOCK)
                v = rows_of[g](2, c * BLOCK, BLOCK)
                mask = causal
            else:
                k = rows_of[g](1, (c - 1) * BLOCK, 2 * BLOCK)
                v = rows_of[g](2, (c - 1) * BLOCK, 2 * BLOCK)
                mask = band
            s = lax.dot_general(q, k, (((1,), (1,)), ((), ())), preferred_element_type=F32)
            s = jnp.where(mask, s, NEG_INF)
            m = jnp.max(s, axis=-1, keepdims=True)
            p = jnp.exp(s - m)
            l = jnp.sum(p, axis=-1, keepdims=True)
            o = jnp.dot(p.astype(BF16), v, preferred_element_type=F32) * (1.0 / l)
            lse = jnp.broadcast_to(m + jnp.log(l), (BLOCK, HEAD_DIM))
            if dil == 1:
                onat_ref[g, rows, :] = o
                lnat_ref[g, rows, :] = lse
            else:
                osub_ref[g - 1, rows, :] = o
                lsub_ref[g - 1, rows, :] = lse

    for g, (window, dil) in enumerate(DIL_GROUPS):
        sub_len = seq // dil
        if dil > 1:
            for r in range(dil):
                src = slice(r * sub_len, (r + 1) * sub_len)
                onat_ref[g, pl.ds(r, sub_len, stride=dil), :] = osub_ref[g - 1, src, :]
                lnat_ref[g, pl.ds(r, sub_len, stride=dil), :] = lsub_ref[g - 1, src, :]

    chunk = 2 * BLOCK
    for t in range(seq // chunk):
        rows = slice(t * chunk, (t + 1) * chunk)
        lse = [lnat_ref[g, rows, :] for g in range(len(DIL_GROUPS))]
        mx = jnp.maximum(jnp.maximum(lse[0], lse[1]), lse[2])
        e = [jnp.exp(x - mx) for x in lse]
        inv = 1.0 / (e[0] + e[1] + e[2])
        for g in range(len(DIL_GROUPS)):
            y_refs[g][0, rows, :] = (onat_ref[g, rows, :] * (e[g] * inv)
                                     * z_refs[g][0, rows, :].astype(F32)).astype(y_refs[g].dtype)


def _dilated_attn(q_src, k_src, v_src, z_src):
    sources = (q_src, k_src, v_src, z_src)
    b, s, _ = q_src[0].shape
    n_groups = len(DIL_GROUPS)
    n_regrouped = n_groups - 1

    def head_spec(off, g):
        first = (off + g * GROUP_WIDTH) // HEAD_DIM
        return pl.BlockSpec((1, s, HEAD_DIM), lambda bi, hg: (bi, 0, first + hg))

    in_specs = [head_spec(off, g) for _, off in sources for g in range(n_groups)]
    operands = [arr for arr, _ in sources for g in range(n_groups)]
    out_spec = pl.BlockSpec((1, s, HEAD_DIM), lambda bi, hg: (bi, 0, hg))
    return pl.pallas_call(
        functools.partial(_dilated_attn_kernel, seq=s),
        grid=(b, HEADS_PER_DIL),
        in_specs=in_specs,
        out_specs=[out_spec] * n_groups,
        out_shape=[jax.ShapeDtypeStruct((b, s, GROUP_WIDTH), BF16)] * n_groups,
        scratch_shapes=[pltpu.VMEM((3 * n_regrouped, s, HEAD_DIM), F32),
                        pltpu.VMEM((3 * n_regrouped, s, HEAD_DIM), BF16),
                        pltpu.VMEM((n_regrouped, s, HEAD_DIM), F32),
                        pltpu.VMEM((n_regrouped, s, HEAD_DIM), F32),
                        pltpu.VMEM((n_groups, s, HEAD_DIM), F32),
                        pltpu.VMEM((n_groups, s, HEAD_DIM), F32)],
        compiler_params=_params(("parallel", "parallel")),
        name="dilated_attn",
    )(*operands)


def _conv_kernel(u_ref, b_ref, c_ref, z_ref, w_ref, y_ref):
    cu = c_ref[0].astype(F32) * u_ref[0].astype(F32)
    row = lax.broadcasted_iota(jnp.int32, cu.shape, 0)
    w = w_ref[...]
    acc = cu * w[CONV_K - 1:CONV_K, :]
    for lag in range(1, CONV_K):
        shifted = jnp.where(row >= lag, pltpu.roll(cu, lag, axis=0), 0.0)
        acc = acc + shifted * w[CONV_K - 1 - lag:CONV_K - lag, :]
    y_ref[0] = (b_ref[0].astype(F32) * acc * z_ref[0].astype(F32)).astype(y_ref.dtype)


def _conv(u_src, b_src, c_src, z_src, conv_w, tc):
    sources = (u_src, b_src, c_src, z_src)
    b, s, _ = u_src[0].shape

    def spec(off):
        assert off % tc == 0
        return pl.BlockSpec((1, s, tc), lambda bi, j: (bi, 0, off // tc + j))

    return pl.pallas_call(
        _conv_kernel,
        grid=(b, CONV_WIDTH // tc),
        in_specs=[spec(off) for _, off in sources]
                 + [pl.BlockSpec((CONV_K, tc), lambda bi, j: (0, j))],
        out_specs=pl.BlockSpec((1, s, tc), lambda bi, j: (bi, 0, j)),
        out_shape=jax.ShapeDtypeStruct((b, s, CONV_WIDTH), BF16),
        compiler_params=_params(("parallel", "parallel")),
        name="gated_conv",
    )(*[arr for arr, _ in sources], conv_w)


def _xattn_kernel(q_ref, z_ref, mk_ref, mv_ref, y_ref):
    for h in range(N_XATTN_HEADS):
        cols = slice(h * XATTN_HEAD_DIM, (h + 1) * XATTN_HEAD_DIM)
        s = lax.dot_general(q_ref[0, :, cols], mk_ref[0, :, cols],
                            (((1,), (1,)), ((), ())), preferred_element_type=F32)
        m = jnp.max(s, axis=-1, keepdims=True)
        p = jnp.exp(s - m)
        l = jnp.sum(p, axis=-1, keepdims=True)
        o = jnp.dot(p.astype(BF16), mv_ref[0, :, cols], preferred_element_type=F32)
        y_ref[0, :, cols] = (o * (1.0 / l) * z_ref[0, :, cols].astype(F32)).astype(y_ref.dtype)


def _xattn(q_src, z_src, mkv3, tm):
    b, s, _ = q_src[0].shape
    m_len = mkv3.shape[1]
    q_off, z_off = q_src[1], z_src[1]
    assert q_off % XATTN_WIDTH == 0 and z_off % XATTN_WIDTH == 0
    return pl.pallas_call(
        _xattn_kernel,
        grid=(b, s // tm),
        in_specs=[pl.BlockSpec((1, tm, XATTN_WIDTH), lambda bi, i: (bi, i, q_off // XATTN_WIDTH)),
                  pl.BlockSpec((1, tm, XATTN_WIDTH), lambda bi, i: (bi, i, z_off // XATTN_WIDTH)),
                  pl.BlockSpec((1, m_len, XATTN_WIDTH), lambda bi, i: (bi, 0, 0)),
                  pl.BlockSpec((1, m_len, XATTN_WIDTH), lambda bi, i: (bi, 0, 1))],
        out_specs=pl.BlockSpec((1, tm, XATTN_WIDTH), lambda bi, i: (bi, i, 0)),
        out_shape=jax.ShapeDtypeStruct((b, s, XATTN_WIDTH), BF16),
        compiler_params=_params(("parallel", "parallel")),
        name="mem_xattn",
    )(q_src[0], z_src[0], mkv3, mkv3)


def _out_proj_kernel(*refs, ni, d, n_mix):
    y_refs, w_refs = refs[:n_mix], refs[n_mix:2 * n_mix]
    x_ref, g_ref, o_ref, y2_ref, ssq_ref, scale_ref = refs[2 * n_mix:]
    i = pl.program_id(0)
    j = pl.program_id(1)

    @pl.when(j == 0)
    def _():
        @pl.when(i > 0)
        def _():
            scale_ref[...] = lax.rsqrt(ssq_ref[...] * (1.0 / d) + EPS)
        ssq_ref[...] = jnp.zeros_like(ssq_ref)

    def finish_previous_tile():
        o_ref[...] = x_ref[...] + y2_ref[j] * scale_ref[...] * g_ref[...]

    def multiply_this_tile():
        y2 = jnp.dot(y_refs[0][...], w_refs[0][...], preferred_element_type=F32)
        for y_ref, w_ref in zip(y_refs[1:], w_refs[1:]):
            y2 = y2 + jnp.dot(y_ref[...], w_ref[...], preferred_element_type=F32)
        y2_ref[j] = y2
        ssq_ref[...] += jnp.sum(y2 * y2, axis=-1, keepdims=True)

    @pl.when(i == 0)
    def _():
        multiply_this_tile()

    @pl.when((i > 0) & (i < ni))
    def _():
        finish_previous_tile()
        multiply_this_tile()

    @pl.when(i == ni)
    def _():
        finish_previous_tile()


def _out_proj(ys, w_bf, x2d, gain, tm, tn):
    rows, d = x2d.shape
    nj = d // tn
    ni = rows // tm

    def y_spec(width):
        return pl.BlockSpec((tm, width), lambda i, j: (jnp.minimum(i, ni - 1), 0))

    def w_spec(width, row_off):
        assert row_off % width == 0
        return pl.BlockSpec((width, tn),
                            lambda i, j: (row_off // width, jnp.where(i == ni, nj - 1, j)))

    io_spec = pl.BlockSpec((tm, tn),
                           lambda i, j: (jnp.maximum(i - 1, 0), jnp.where(i == 0, 0, j)))
    widths = [y.shape[1] for y in ys]
    offsets = [sum(widths[:k]) for k in range(len(widths))]
    return pl.pallas_call(
        functools.partial(_out_proj_kernel, ni=ni, d=d, n_mix=len(ys)),
        grid=(ni + 1, nj),
        in_specs=([y_spec(w) for w in widths]
                  + [w_spec(w, off) for w, off in zip(widths, offsets)]
                  + [io_spec, pl.BlockSpec((1, tn), lambda i, j: (0, j))]),
        out_specs=io_spec,
        out_shape=jax.ShapeDtypeStruct((rows, d), F32),
        scratch_shapes=[pltpu.VMEM((nj, tm, tn), F32),
                        pltpu.VMEM((tm, 1), F32),
                        pltpu.VMEM((tm, 1), F32)],
        compiler_params=_params(("arbitrary", "arbitrary")),
        name="out_proj",
    )(*ys, *([w_bf] * len(ys)), x2d, gain.reshape(1, d))


def _rope_tables(pos, half):
    inv = 1.0 / (ROPE_THETA ** (jnp.arange(half, dtype=F32) / half))
    ang = pos.astype(F32)[:, None] * inv[None, :]
    return jnp.cos(ang), jnp.sin(ang)


def kernel(x, mem, pre_norm, w_in, conv_w, mem_norm, w_mem_kv, w_out, post_norm):
    b, s, d = x.shape
    m_len = mem.shape[1]
    depth = w_in.shape[0]
    assert all(window // dil == BLOCK for window, dil in DIL_GROUPS)
    pos = jnp.arange(s)
    cos_a, sin_a = _rope_tables(pos, HEAD_DIM // 2)
    cosa = jnp.concatenate([cos_a, cos_a], axis=-1)
    sina = jnp.concatenate([-sin_a, sin_a], axis=-1)
    a_scale = HEAD_DIM ** -0.5
    cos_qk = jnp.stack([cosa * a_scale, cosa])
    sin_qk = jnp.stack([sina * a_scale, sina])
    cosx, sinx = _rope_tables(pos + m_len, XATTN_HEAD_DIM // 2)
    x_scale = XATTN_HEAD_DIM ** -0.5
    cos_qx, sin_qx = (cosx * x_scale)[None], (sinx * x_scale)[None]
    cos_m, sin_m = _rope_tables(jnp.arange(m_len), XATTN_HEAD_DIM // 2)
    cosm = jnp.tile(cos_m, (b, 1))
    sinm = jnp.tile(sin_m, (b, 1))

    tm, tn = 1024, 512

    def tile_of(off):
        assert off % tn == 0
        return off // tn

    attn_tiles, conv_tiles, x_tiles = ATTN_WIDTH // tn, CONV_WIDTH // tn, XATTN_WIDTH // tn

    def view(t):
        return t.reshape(b, s, t.shape[-1])

    for layer in range(depth):
        x2d = x.reshape(b * s, d)
        h = _rmsnorm(x2d, pre_norm[layer], tm=512)
        w = w_in[layer]
        common = dict(seq=s, tm=tm, tn=tn)

        qk = view(_in_proj(h, w, lambda j: tile_of(OFF_QA) + j, 2 * attn_tiles, _rope_half_lane,
                           [cos_qk, sin_qk], lambda j: j // attn_tiles, name="in_proj_qk",
                           **common))
        plain = view(_in_proj(
            h, w, lambda j: jnp.where(j < attn_tiles, tile_of(OFF_VA) + j,
                                      tile_of(OFF_UC) - attn_tiles + j),
            attn_tiles + 3 * conv_tiles, lambda acc: acc, [], None, name="in_proj_plain",
            **common))
        gates = view(_in_proj(
            h, w, lambda j: jnp.where(j < attn_tiles, tile_of(OFF_ZA) + j,
                                      jnp.where(j < attn_tiles + conv_tiles,
                                                tile_of(OFF_ZC) - attn_tiles + j,
                                                tile_of(OFF_ZX) - attn_tiles - conv_tiles + j)),
            attn_tiles + conv_tiles + x_tiles, _silu, [], None, name="in_proj_gates", **common))
        qx = view(_in_proj(h, w, lambda j: tile_of(OFF_QX) + j, x_tiles, _rope_two_tiles,
                           [cos_qx, sin_qx], lambda j: 0, name="in_proj_qx", **common))

        hm = _rmsnorm(mem.reshape(b * m_len, d), mem_norm[layer], tm=256)
        mkv = _mem_proj(hm, w_mem_kv[layer], cosm, sinm, tn=512)

        y_groups = _dilated_attn((qk, 0), (qk, ATTN_WIDTH), (plain, 0), (gates, 0))
        y_groups = [y.reshape(b * s, GROUP_WIDTH) for y in y_groups]
        y_conv = _conv((plain, ATTN_WIDTH), (plain, ATTN_WIDTH + CONV_WIDTH),
                       (plain, ATTN_WIDTH + 2 * CONV_WIDTH), (gates, ATTN_WIDTH),
                       conv_w[layer], tc=256).reshape(b * s, CONV_WIDTH)
        y_x = _xattn((qx, 0), (gates, ATTN_WIDTH + CONV_WIDTH),
                     mkv.reshape(b, m_len, 2 * XATTN_WIDTH), tm=512)
        y_x = y_x.reshape(b * s, XATTN_WIDTH)

        out = _out_proj([*y_groups, y_conv, y_x], w_out[layer].astype(BF16), x2d,
                        post_norm[layer], tm=1024, tn=512)
        x = out.reshape(b, s, d)
    return x
```

```python
import functools

import jax
import jax.numpy as jnp
import numpy as np
from jax import lax
from jax.experimental import pallas as pl
from jax.experimental.pallas import tpu as pltpu

D_MODEL = 4096
MEM_LEN = 256
HEAD_DIM = 128
DIL_GROUPS = ((128, 1), (512, 4), (2048, 16))
ATTN_WIDTH = 3 * D_MODEL // 8
CONV_WIDTH = 3 * D_MODEL // 8
XATTN_WIDTH = D_MODEL // 4
N_XATTN_HEADS = 4
XATTN_HEAD_DIM = XATTN_WIDTH // N_XATTN_HEADS
HEADS_PER_DIL = 4
GROUP_WIDTH = HEADS_PER_DIL * HEAD_DIM
CONV_K = 3
IN_COLS = 4 * ATTN_WIDTH + 4 * CONV_WIDTH + 2 * XATTN_WIDTH
BLOCK = 128
ROPE_THETA = 10000.0
EPS = 1e-6
NEG_INF = -1e30

OFF_QA = 0
OFF_KA = OFF_QA + ATTN_WIDTH
OFF_VA = OFF_KA + ATTN_WIDTH
OFF_ZA = OFF_VA + ATTN_WIDTH
OFF_UC = OFF_ZA + ATTN_WIDTH
OFF_BC = OFF_UC + CONV_WIDTH
OFF_CC = OFF_BC + CONV_WIDTH
OFF_ZC = OFF_CC + CONV_WIDTH
OFF_QX = OFF_ZC + CONV_WIDTH
OFF_ZX = OFF_QX + XATTN_WIDTH

V7X_VMEM_LIMIT_BYTES = 56 * 1024 * 1024
LANES = 128
W_PIECES = 4
EPILOGUE_ROWS = 128

BF16 = jnp.bfloat16
F32 = jnp.float32


def _params(semantics):
    return pltpu.CompilerParams(dimension_semantics=semantics,
                                vmem_limit_bytes=V7X_VMEM_LIMIT_BYTES)


def _rmsnorm_kernel(x_ref, g_ref, o_ref):
    x = x_ref[...]
    ms = jnp.mean(x * x, axis=-1, keepdims=True)
    o_ref[...] = (x * lax.rsqrt(ms + EPS) * g_ref[...]).astype(o_ref.dtype)


def _rmsnorm(x2d, gain, tm):
    rows, d = x2d.shape
    return pl.pallas_call(
        _rmsnorm_kernel,
        grid=(rows // tm,),
        in_specs=[pl.BlockSpec((tm, d), lambda i: (i, 0)),
                  pl.BlockSpec((1, d), lambda i: (0, 0))],
        out_specs=pl.BlockSpec((tm, d), lambda i: (i, 0)),
        out_shape=jax.ShapeDtypeStruct((rows, d), BF16),
        compiler_params=_params(("parallel",)),
        name="rmsnorm",
    )(x2d, gain.reshape(1, d))


def _silu(z):
    half = 0.5 * z
    return half + half * jnp.tanh(half)


def _rope_half_lane(acc, cos, sin_signed):
    outs = []
    for h in range(acc.shape[1] // HEAD_DIM):
        xh = acc[:, h * HEAD_DIM:(h + 1) * HEAD_DIM]
        outs.append(xh * cos + pltpu.roll(xh, HEAD_DIM // 2, axis=1) * sin_signed)
    return jnp.concatenate(outs, axis=1)


def _rope_two_tiles(acc, cos, sin):
    outs = []
    for h in range(acc.shape[1] // XATTN_HEAD_DIM):
        x1 = acc[:, h * XATTN_HEAD_DIM:h * XATTN_HEAD_DIM + LANES]
        x2 = acc[:, h * XATTN_HEAD_DIM + LANES:(h + 1) * XATTN_HEAD_DIM]
        outs.append(x1 * cos - x2 * sin)
        outs.append(x2 * cos + x1 * sin)
    return jnp.concatenate(outs, axis=1)


def _in_proj_kernel(*refs, ni, n_steps, epilogue, n_tables, has_rider):
    h_ref = refs[0]
    w_refs = refs[1:1 + W_PIECES]
    table_refs = refs[1 + W_PIECES:1 + W_PIECES + n_tables]
    rest = refs[1 + W_PIECES + n_tables:]
    if has_rider:
        rider_in_ref, o_ref, rider_out_ref, wbf_ref, acc_ref = rest
    else:
        o_ref, wbf_ref, acc_ref = rest
    t = pl.program_id(0)
    piece_rows = wbf_ref.shape[0] // W_PIECES

    @pl.when((t % ni == 0) & (t < n_steps))
    def _():
        for p, w_ref in enumerate(w_refs):
            wbf_ref[p * piece_rows:(p + 1) * piece_rows, :] = w_ref[...].astype(BF16)

    if has_rider:
        @pl.when(t < n_steps)
        def _():
            rider_out_ref[...] = rider_in_ref[...].astype(rider_out_ref.dtype)

    def multiply(slot):
        acc_ref[slot] = jnp.dot(h_ref[...], wbf_ref[...], preferred_element_type=F32)

    def finish(slot):
        for r0 in range(0, o_ref.shape[0], EPILOGUE_ROWS):
            rows = slice(r0, r0 + EPILOGUE_ROWS)
            o_ref[rows, :] = epilogue(acc_ref[slot, rows, :],
                                      *[r[0, rows, :] for r in table_refs]).astype(o_ref.dtype)

    @pl.when(t == 0)
    def _():
        multiply(0)

    for parity in range(2):
        @pl.when((t > 0) & (t < n_steps) & (t % 2 == parity))
        def _(parity=parity):
            finish(1 - parity)
            multiply(parity)

    @pl.when(t == n_steps)
    def _():
        finish((n_steps - 1) % 2)


def _in_proj(h, w, col_tile, n_col_tiles, epilogue, tables, table_kind, seq, tm, tn, name,
             cast_rider=None):
    rows, d = h.shape
    ni = rows // tm
    n_steps = n_col_tiles * ni
    tiles_per_seq = seq // tm
    rider_in_specs, rider_out_specs, rider_out_shapes, rider_operands = [], [], [], []
    if cast_rider is not None:
        r_rows, r_cols = cast_rider.shape
        assert r_rows % n_steps == 0
        rider_spec = pl.BlockSpec((r_rows // n_steps, r_cols),
                                  lambda t: (jnp.minimum(t, n_steps - 1), 0))
        rider_in_specs, rider_out_specs = [rider_spec], [rider_spec]
        rider_out_shapes = [jax.ShapeDtypeStruct(cast_rider.shape, BF16)]
        rider_operands = [cast_rider]

    def cur(t):
        c = jnp.minimum(t, n_steps - 1)
        return c // ni, c % ni

    def prev(t):
        p = jnp.maximum(t - 1, 0)
        return p // ni, p % ni

    table_spec = pl.BlockSpec(
        (1, tm, LANES), lambda t: (table_kind(prev(t)[0]), prev(t)[1] % tiles_per_seq, 0))

    def w_piece_spec(p):
        def index(t):
            j, i = cur(t)
            ahead = (i >= ni - W_PIECES + p).astype(jnp.int32)
            return p, col_tile(jnp.minimum(j + ahead, n_col_tiles - 1))
        return pl.BlockSpec((d // W_PIECES, tn), index)

    assert ni >= W_PIECES and d % W_PIECES == 0
    results = pl.pallas_call(
        functools.partial(_in_proj_kernel, ni=ni, n_steps=n_steps, epilogue=epilogue,
                          n_tables=len(tables), has_rider=cast_rider is not None),
        grid=(n_steps + 1,),
        in_specs=[pl.BlockSpec((tm, d), lambda t: (cur(t)[1], 0))]
                 + [w_piece_spec(p) for p in range(W_PIECES)]
                 + [table_spec] * len(tables) + rider_in_specs,
        out_specs=[pl.BlockSpec((tm, tn), lambda t: (prev(t)[1], prev(t)[0]))] + rider_out_specs,
        out_shape=[jax.ShapeDtypeStruct((rows, n_col_tiles * tn), BF16)] + rider_out_shapes,
        scratch_shapes=[pltpu.VMEM((d, tn), BF16), pltpu.VMEM((2, tm, tn), F32)],
        compiler_params=_params(("arbitrary",)),
        name=name,
    )(h, *([w] * W_PIECES), *tables, *rider_operands)
    return results if cast_rider is not None else results[0]


def _mem_proj_kernel(h_ref, w_ref, cos_ref, sin_ref, o_ref, *, tn):
    j = pl.program_id(0)
    acc = jnp.dot(h_ref[...], w_ref[...].astype(BF16), preferred_element_type=F32)

    @pl.when(j < XATTN_WIDTH // tn)
    def _():
        o_ref[...] = _rope_two_tiles(acc, cos_ref[...], sin_ref[...]).astype(o_ref.dtype)

    @pl.when(j >= XATTN_WIDTH // tn)
    def _():
        o_ref[...] = acc.astype(o_ref.dtype)


def _mem_proj(hm, w, cos, sin, tn):
    rows, d = hm.shape
    ncols = w.shape[1]
    tab = pl.BlockSpec((rows, LANES), lambda j: (0, 0))
    return pl.pallas_call(
        functools.partial(_mem_proj_kernel, tn=tn),
        grid=(ncols // tn,),
        in_specs=[pl.BlockSpec((rows, d), lambda j: (0, 0)),
                  pl.BlockSpec((d, tn), lambda j: (0, j)),
                  tab, tab],
        out_specs=pl.BlockSpec((rows, tn), lambda j: (0, j)),
        out_shape=jax.ShapeDtypeStruct((rows, ncols), BF16),
        compiler_params=_params(("parallel",)),
        name="mem_proj",
    )(hm, w, cos, sin)


def _dilated_attn_kernel(q0, q1, q2, k0, k1, k2, v0, v1, v2, z0, z1, z2, y0, y1, y2,
                         stage_ref, sub_ref, osub_ref, lsub_ref, onat_ref, lnat_ref, *, seq):
    q_refs, k_refs, v_refs = (q0, q1, q2), (k0, k1, k2), (v0, v1, v2)
    z_refs, y_refs = (z0, z1, z2), (y0, y1, y2)
    nblk = seq // BLOCK
    qi = lax.broadcasted_iota(jnp.int32, (BLOCK, 2 * BLOCK), 0)
    kk = lax.broadcasted_iota(jnp.int32, (BLOCK, 2 * BLOCK), 1)
    dist = qi + BLOCK - kk
    band = (dist >= 0) & (dist <= BLOCK)
    causal = (lax.broadcasted_iota(jnp.int32, (BLOCK, BLOCK), 1)
              <= lax.broadcasted_iota(jnp.int32, (BLOCK, BLOCK), 0))

    rows_of = []
    for g, (window, dil) in enumerate(DIL_GROUPS):
        sub_len = seq // dil
        if dil == 1:
            def group_rows(which, start, size, g=g):
                return (q_refs, k_refs, v_refs)[which][g][0, pl.ds(start, size), :]
        else:
            slab = 3 * (g - 1)
            for which, refs in enumerate((q_refs, k_refs, v_refs)):
                stage_ref[slab + which] = refs[g][0].astype(F32)
                for r in range(dil):
                    sub_ref[slab + which, r * sub_len:(r + 1) * sub_len, :] = (
                        stage_ref[slab + which, pl.ds(r, sub_len, stride=dil), :].astype(BF16))

            def group_rows(which, start, size, slab=slab):
                return sub_ref[slab + which, pl.ds(start, size), :]
        rows_of.append(group_rows)

    def attend(g, start, first):
        q = rows_of[g](0, start, BLOCK)
        if first:
            k = rows_of[g](1, start, BLOCK)
            v = rows_of[g](2, start, BLOCK)
            mask = causal
        else:
            k = rows_of[g](1, start - BLOCK, 2 * BLOCK)
            v = rows_of[g](2, start - BLOCK, 2 * BLOCK)
            mask = band
        s = lax.dot_general(q, k, (((1,), (1,)), ((), ())), preferred_element_type=F32)
        s = jnp.where(mask, s, NEG_INF)
        m = jnp.max(s, axis=-1, keepdims=True)
        p = jnp.exp(s - m)
        l = jnp.sum(p, axis=-1, keepdims=True)
        o = jnp.dot(p.astype(BF16), v, preferred_element_type=F32) * (1.0 / l)
        lse = jnp.broadcast_to(m + jnp.log(l), (BLOCK, HEAD_DIM))
        if DIL_GROUPS[g][1] == 1:
            onat_ref[g, pl.ds(start, BLOCK), :] = o
            lnat_ref[g, pl.ds(start, BLOCK), :] = lse
        else:
            osub_ref[g - 1, pl.ds(start, BLOCK), :] = o
            lsub_ref[g - 1, pl.ds(start, BLOCK), :] = lse

    for c in range(nblk):
        for g, (window, dil) in enumerate(DIL_GROUPS):
            attend(g, c * BLOCK, c % (seq // dil // BLOCK) == 0)

    for g, (window, dil) in enumerate(DIL_GROUPS):
        sub_len = seq // dil
        if dil > 1:
            for r in range(dil):
                src = slice(r * sub_len, (r + 1) * sub_len)
                onat_ref[g, pl.ds(r, sub_len, stride=dil), :] = osub_ref[g - 1, src, :]
                lnat_ref[g, pl.ds(r, sub_len, stride=dil), :] = lsub_ref[g - 1, src, :]

    chunk = 2 * BLOCK
    for t in range(seq // chunk):
        rows = slice(t * chunk, (t + 1) * chunk)
        lse = [lnat_ref[g, rows, :] for g in range(len(DIL_GROUPS))]
        mx = jnp.maximum(jnp.maximum(lse[0], lse[1]), lse[2])
        e = [jnp.exp(x - mx) for x in lse]
        inv = 1.0 / (e[0] + e[1] + e[2])
        for g in range(len(DIL_GROUPS)):
            y_refs[g][0, rows, :] = (onat_ref[g, rows, :] * (e[g] * inv)
                                     * z_refs[g][0, rows, :].astype(F32)).astype(y_refs[g].dtype)


def _dilated_attn(q_src, k_src, v_src, z_src):
    sources = (q_src, k_src, v_src, z_src)
    b, s, _ = q_src[0].shape
    n_groups = len(DIL_GROUPS)
    n_regrouped = n_groups - 1

    def head_spec(off, g):
        first = (off + g * GROUP_WIDTH) // HEAD_DIM
        return pl.BlockSpec((1, s, HEAD_DIM), lambda bi, hg: (bi, 0, first + hg))

    in_specs = [head_spec(off, g) for _, off in sources for g in range(n_groups)]
    operands = [arr for arr, _ in sources for g in range(n_groups)]
    out_spec = pl.BlockSpec((1, s, HEAD_DIM), lambda bi, hg: (bi, 0, hg))
    return pl.pallas_call(
        functools.partial(_dilated_attn_kernel, seq=s),
        grid=(b, HEADS_PER_DIL),
        in_specs=in_specs,
        out_specs=[out_spec] * n_groups,
        out_shape=[jax.ShapeDtypeStruct((b, s, GROUP_WIDTH), BF16)] * n_groups,
        scratch_shapes=[pltpu.VMEM((3 * n_regrouped, s, HEAD_DIM), F32),
                        pltpu.VMEM((3 * n_regrouped, s, HEAD_DIM), BF16),
                        pltpu.VMEM((n_regrouped, s, HEAD_DIM), F32),
                        pltpu.VMEM((n_regrouped, s, HEAD_DIM), F32),
                        pltpu.VMEM((n_groups, s, HEAD_DIM), F32),
                        pltpu.VMEM((n_groups, s, HEAD_DIM), F32)],
        compiler_params=_params(("parallel", "parallel")),
        name="dilated_attn",
    )(*operands)


def _conv_kernel(u_ref, b_ref, c_ref, z_ref, w_ref, y_ref):
    cu = c_ref[0].astype(F32) * u_ref[0].astype(F32)
    row = lax.broadcasted_iota(jnp.int32, cu.shape, 0)
    w = w_ref[...]
    acc = cu * w[CONV_K - 1:CONV_K, :]
    for lag in range(1, CONV_K):
        shifted = jnp.where(row >= lag, pltpu.roll(cu, lag, axis=0), 0.0)
        acc = acc + shifted * w[CONV_K - 1 - lag:CONV_K - lag, :]
    y_ref[0] = (b_ref[0].astype(F32) * acc * z_ref[0].astype(F32)).astype(y_ref.dtype)


def _conv(u_src, b_src, c_src, z_src, conv_w, tc):
    sources = (u_src, b_src, c_src, z_src)
    b, s, _ = u_src[0].shape

    def spec(off):
        assert off % tc == 0
        return pl.BlockSpec((1, s, tc), lambda bi, j: (bi, 0, off // tc + j))

    return pl.pallas_call(
        _conv_kernel,
        grid=(b, CONV_WIDTH // tc),
        in_specs=[spec(off) for _, off in sources]
                 + [pl.BlockSpec((CONV_K, tc), lambda bi, j: (0, j))],
        out_specs=pl.BlockSpec((1, s, tc), lambda bi, j: (bi, 0, j)),
        out_shape=jax.ShapeDtypeStruct((b, s, CONV_WIDTH), BF16),
        compiler_params=_params(("parallel", "parallel")),
        name="gated_conv",
    )(*[arr for arr, _ in sources], conv_w)


def _xattn_kernel(q_ref, z_ref, mk_ref, mv_ref, y_ref):
    for h in range(N_XATTN_HEADS):
        cols = slice(h * XATTN_HEAD_DIM, (h + 1) * XATTN_HEAD_DIM)
        s = lax.dot_general(q_ref[0, :, cols], mk_ref[0, :, cols],
                            (((1,), (1,)), ((), ())), preferred_element_type=F32)
        m = jnp.max(s, axis=-1, keepdims=True)
        p = jnp.exp(s - m)
        l = jnp.sum(p, axis=-1, keepdims=True)
        o = jnp.dot(p.astype(BF16), mv_ref[0, :, cols], preferred_element_type=F32)
        y_ref[0, :, cols] = (o * (1.0 / l) * z_ref[0, :, cols].astype(F32)).astype(y_ref.dtype)


def _xattn(q_src, z_src, mkv3, tm):
    b, s, _ = q_src[0].shape
    m_len = mkv3.shape[1]
    q_off, z_off = q_src[1], z_src[1]
    assert q_off % XATTN_WIDTH == 0 and z_off % XATTN_WIDTH == 0
    return pl.pallas_call(
        _xattn_kernel,
        grid=(b, s // tm),
        in_specs=[pl.BlockSpec((1, tm, XATTN_WIDTH), lambda bi, i: (bi, i, q_off // XATTN_WIDTH)),
                  pl.BlockSpec((1, tm, XATTN_WIDTH), lambda bi, i: (bi, i, z_off // XATTN_WIDTH)),
                  pl.BlockSpec((1, m_len, XATTN_WIDTH), lambda bi, i: (bi, 0, 0)),
                  pl.BlockSpec((1, m_len, XATTN_WIDTH), lambda bi, i: (bi, 0, 1))],
        out_specs=pl.BlockSpec((1, tm, XATTN_WIDTH), lambda bi, i: (bi, i, 0)),
        out_shape=jax.ShapeDtypeStruct((b, s, XATTN_WIDTH), BF16),
        compiler_params=_params(("parallel", "parallel")),
        name="mem_xattn",
    )(q_src[0], z_src[0], mkv3, mkv3)


def _out_proj_kernel(*refs, ni, d, n_mix):
    y_refs, w_refs = refs[:n_mix], refs[n_mix:2 * n_mix]
    x_ref, g_ref, o_ref, y2_ref, ssq_ref, scale_ref = refs[2 * n_mix:]
    i = pl.program_id(0)
    j = pl.program_id(1)

    @pl.when(j == 0)
    def _():
        @pl.when(i > 0)
        def _():
            scale_ref[...] = lax.rsqrt(ssq_ref[...] * (1.0 / d) + EPS)
        ssq_ref[...] = jnp.zeros_like(ssq_ref)

    def finish_previous_tile():
        o_ref[...] = x_ref[...] + y2_ref[j] * scale_ref[...] * g_ref[...]

    def multiply_this_tile():
        y2 = jnp.dot(y_refs[0][...], w_refs[0][...], preferred_element_type=F32)
        for y_ref, w_ref in zip(y_refs[1:], w_refs[1:]):
            y2 = y2 + jnp.dot(y_ref[...], w_ref[...], preferred_element_type=F32)
        y2_ref[j] = y2
        ssq_ref[...] += jnp.sum(y2 * y2, axis=-1, keepdims=True)

    @pl.when(i == 0)
    def _():
        multiply_this_tile()

    @pl.when((i > 0) & (i < ni))
    def _():
        finish_previous_tile()
        multiply_this_tile()

    @pl.when(i == ni)
    def _():
        finish_previous_tile()


def _out_proj(ys, w_bf, x2d, gain, tm, tn):
    rows, d = x2d.shape
    nj = d // tn
    ni = rows // tm

    def y_spec(width):
        return pl.BlockSpec((tm, width), lambda i, j: (jnp.minimum(i, ni - 1), 0))

    def w_spec(width, row_off):
        assert row_off % width == 0
        return pl.BlockSpec((width, tn),
                            lambda i, j: (row_off // width, jnp.where(i == ni, nj - 1, j)))

    io_spec = pl.BlockSpec((tm, tn),
                           lambda i, j: (jnp.maximum(i - 1, 0), jnp.where(i == 0, 0, j)))
    widths = [y.shape[1] for y in ys]
    offsets = [sum(widths[:k]) for k in range(len(widths))]
    return pl.pallas_call(
        functools.partial(_out_proj_kernel, ni=ni, d=d, n_mix=len(ys)),
        grid=(ni + 1, nj),
        in_specs=([y_spec(w) for w in widths]
                  + [w_spec(w, off) for w, off in zip(widths, offsets)]
                  + [io_spec, pl.BlockSpec((1, tn), lambda i, j: (0, j))]),
        out_specs=io_spec,
        out_shape=jax.ShapeDtypeStruct((rows, d), F32),
        scratch_shapes=[pltpu.VMEM((nj, tm, tn), F32),
                        pltpu.VMEM((tm, 1), F32),
                        pltpu.VMEM((tm, 1), F32)],
        compiler_params=_params(("arbitrary", "arbitrary")),
        name="out_proj",
    )(*ys, *([w_bf] * len(ys)), x2d, gain.reshape(1, d))


def _rope_tables(pos, half):
    inv = 1.0 / (ROPE_THETA ** (jnp.arange(half, dtype=F32) / half))
    ang = pos.astype(F32)[:, None] * inv[None, :]
    return jnp.cos(ang), jnp.sin(ang)


def kernel(x, mem, pre_norm, w_in, conv_w, mem_norm, w_mem_kv, w_out, post_norm):
    b, s, d = x.shape
    m_len = mem.shape[1]
    depth = w_in.shape[0]
    assert all(window // dil == BLOCK for window, dil in DIL_GROUPS)
    pos = jnp.arange(s)
    cos_a, sin_a = _rope_tables(pos, HEAD_DIM // 2)
    cosa = jnp.concatenate([cos_a, cos_a], axis=-1)
    sina = jnp.concatenate([-sin_a, sin_a], axis=-1)
    a_scale = HEAD_DIM ** -0.5
    cos_qk = jnp.stack([cosa * a_scale, cosa])
    sin_qk = jnp.stack([sina * a_scale, sina])
    cosx, sinx = _rope_tables(pos + m_len, XATTN_HEAD_DIM // 2)
    x_scale = XATTN_HEAD_DIM ** -0.5
    cos_qx, sin_qx = (cosx * x_scale)[None], (sinx * x_scale)[None]
    cos_m, sin_m = _rope_tables(jnp.arange(m_len), XATTN_HEAD_DIM // 2)
    cosm = jnp.tile(cos_m, (b, 1))
    sinm = jnp.tile(sin_m, (b, 1))

    tm, tn = 1024, 512

    def tile_of(off):
        assert off % tn == 0
        return off // tn

    attn_tiles, conv_tiles, x_tiles = ATTN_WIDTH // tn, CONV_WIDTH // tn, XATTN_WIDTH // tn

    def view(t):
        return t.reshape(b, s, t.shape[-1])

    for layer in range(depth):
        x2d = x.reshape(b * s, d)
        h = _rmsnorm(x2d, pre_norm[layer], tm=512)
        w = w_in[layer]
        common = dict(seq=s, tm=tm, tn=tn)

        qk = view(_in_proj(h, w, lambda j: tile_of(OFF_QA) + j, 2 * attn_tiles, _rope_half_lane,
                           [cos_qk, sin_qk], lambda j: j // attn_tiles, name="in_proj_qk",
                           **common))
        plain = view(_in_proj(
            h, w, lambda j: jnp.where(j < attn_tiles, tile_of(OFF_VA) + j,
                                      tile_of(OFF_UC) - attn_tiles + j),
            attn_tiles + 3 * conv_tiles, lambda acc: acc, [], None, name="in_proj_plain",
            **common))
        gates, w_out_bf = _in_proj(
            h, w, lambda j: jnp.where(j < attn_tiles, tile_of(OFF_ZA) + j,
                                      jnp.where(j < attn_tiles + conv_tiles,
                                                tile_of(OFF_ZC) - attn_tiles + j,
                                                tile_of(OFF_ZX) - attn_tiles - conv_tiles + j)),
            attn_tiles + conv_tiles + x_tiles, _silu, [], None, name="in_proj_gates",
            cast_rider=w_out[layer], **common)
        gates = view(gates)
        qx = view(_in_proj(h, w, lambda j: tile_of(OFF_QX) + j, x_tiles, _rope_two_tiles,
                           [cos_qx, sin_qx], lambda j: 0, name="in_proj_qx", **common))

        hm = _rmsnorm(mem.reshape(b * m_len, d), mem_norm[layer], tm=256)
        mkv = _mem_proj(hm, w_mem_kv[layer], cosm, sinm, tn=512)

        y_groups = _dilated_attn((qk, 0), (qk, ATTN_WIDTH), (plain, 0), (gates, 0))
        y_groups = [y.reshape(b * s, GROUP_WIDTH) for y in y_groups]
        y_conv = _conv((plain, ATTN_WIDTH), (plain, ATTN_WIDTH + CONV_WIDTH),
                       (plain, ATTN_WIDTH + 2 * CONV_WIDTH), (gates, ATTN_WIDTH),
                       conv_w[layer], tc=256).reshape(b * s, CONV_WIDTH)
        y_x = _xattn((qx, 0), (gates, ATTN_WIDTH + CONV_WIDTH),
                     mkv.reshape(b, m_len, 2 * XATTN_WIDTH), tm=512)
        y_x = y_x.reshape(b * s, XATTN_WIDTH)

        out = _out_proj([*y_groups, y_conv, y_x], w_out_bf, x2d,
                        post_norm[layer], tm=1024, tn=512)
        x = out.reshape(b, s, d)
    return x
```

```python
import functools

import jax
import jax.numpy as jnp
import numpy as np
from jax import lax
from jax.experimental import pallas as pl
from jax.experimental.pallas import tpu as pltpu

D_MODEL = 4096
MEM_LEN = 256
HEAD_DIM = 128
DIL_GROUPS = ((128, 1), (512, 4), (2048, 16))
ATTN_WIDTH = 3 * D_MODEL // 8
CONV_WIDTH = 3 * D_MODEL // 8
XATTN_WIDTH = D_MODEL // 4
N_XATTN_HEADS = 4
XATTN_HEAD_DIM = XATTN_WIDTH // N_XATTN_HEADS
HEADS_PER_DIL = 4
GROUP_WIDTH = HEADS_PER_DIL * HEAD_DIM
CONV_K = 3
IN_COLS = 4 * ATTN_WIDTH + 4 * CONV_WIDTH + 2 * XATTN_WIDTH
BLOCK = 128
ROPE_THETA = 10000.0
EPS = 1e-6
NEG_INF = -1e30

OFF_QA = 0
OFF_KA = OFF_QA + ATTN_WIDTH
OFF_VA = OFF_KA + ATTN_WIDTH
OFF_ZA = OFF_VA + ATTN_WIDTH
OFF_UC = OFF_ZA + ATTN_WIDTH
OFF_BC = OFF_UC + CONV_WIDTH
OFF_CC = OFF_BC + CONV_WIDTH
OFF_ZC = OFF_CC + CONV_WIDTH
OFF_QX = OFF_ZC + CONV_WIDTH
OFF_ZX = OFF_QX + XATTN_WIDTH

V7X_VMEM_LIMIT_BYTES = 56 * 1024 * 1024
LANES = 128
W_PIECES = 4
EPILOGUE_ROWS = 128

BF16 = jnp.bfloat16
F32 = jnp.float32


def _params(semantics):
    return pltpu.CompilerParams(dimension_semantics=semantics,
                                vmem_limit_bytes=V7X_VMEM_LIMIT_BYTES)


def _rms_normalize(x, gain):
    ms = jnp.mean(x * x, axis=-1, keepdims=True)
    return x * lax.rsqrt(ms + EPS) * gain


def _rmsnorm_kernel(x_ref, g_ref, o_ref):
    o_ref[...] = _rms_normalize(x_ref[...], g_ref[...]).astype(o_ref.dtype)


def _rmsnorm(x2d, gain, tm):
    rows, d = x2d.shape
    return pl.pallas_call(
        _rmsnorm_kernel,
        grid=(rows // tm,),
        in_specs=[pl.BlockSpec((tm, d), lambda i: (i, 0)),
                  pl.BlockSpec((1, d), lambda i: (0, 0))],
        out_specs=pl.BlockSpec((tm, d), lambda i: (i, 0)),
        out_shape=jax.ShapeDtypeStruct((rows, d), BF16),
        compiler_params=_params(("parallel",)),
        name="rmsnorm",
    )(x2d, gain.reshape(1, d))


def _silu(z):
    half = 0.5 * z
    return half + half * jnp.tanh(half)


def _rope_half_lane(acc, cos, sin_signed):
    outs = []
    for h in range(acc.shape[1] // HEAD_DIM):
        xh = acc[:, h * HEAD_DIM:(h + 1) * HEAD_DIM]
        outs.append(xh * cos + pltpu.roll(xh, HEAD_DIM // 2, axis=1) * sin_signed)
    return jnp.concatenate(outs, axis=1)


def _rope_two_tiles(acc, cos, sin):
    outs = []
    for h in range(acc.shape[1] // XATTN_HEAD_DIM):
        x1 = acc[:, h * XATTN_HEAD_DIM:h * XATTN_HEAD_DIM + LANES]
        x2 = acc[:, h * XATTN_HEAD_DIM + LANES:(h + 1) * XATTN_HEAD_DIM]
        outs.append(x1 * cos - x2 * sin)
        outs.append(x2 * cos + x1 * sin)
    return jnp.concatenate(outs, axis=1)


def _in_proj_kernel(*refs, ni, n_steps, epilogue, n_tables, rider_fn):
    h_ref = refs[0]
    w_refs = refs[1:1 + W_PIECES]
    table_refs = refs[1 + W_PIECES:1 + W_PIECES + n_tables]
    rest = refs[1 + W_PIECES + n_tables:]
    if rider_fn is not None:
        rider_in_ref, *rider_row_refs = rest[:-4]
        o_ref, rider_out_ref, wbf_ref, acc_ref = rest[-4:]
    else:
        o_ref, wbf_ref, acc_ref = rest
    t = pl.program_id(0)
    piece_rows = wbf_ref.shape[0] // W_PIECES

    @pl.when((t % ni == 0) & (t < n_steps))
    def _():
        for p, w_ref in enumerate(w_refs):
            wbf_ref[p * piece_rows:(p + 1) * piece_rows, :] = w_ref[...].astype(BF16)

    if rider_fn is not None:
        @pl.when(t < n_steps)
        def _():
            rider_out_ref[...] = rider_fn(
                rider_in_ref[...], *[r[...] for r in rider_row_refs]).astype(rider_out_ref.dtype)

    def multiply(slot):
        acc_ref[slot] = jnp.dot(h_ref[...], wbf_ref[...], preferred_element_type=F32)

    def finish(slot):
        for r0 in range(0, o_ref.shape[0], EPILOGUE_ROWS):
            rows = slice(r0, r0 + EPILOGUE_ROWS)
            o_ref[rows, :] = epilogue(acc_ref[slot, rows, :],
                                      *[r[0, rows, :] for r in table_refs]).astype(o_ref.dtype)

    @pl.when(t == 0)
    def _():
        multiply(0)

    for parity in range(2):
        @pl.when((t > 0) & (t < n_steps) & (t % 2 == parity))
        def _(parity=parity):
            finish(1 - parity)
            multiply(parity)

    @pl.when(t == n_steps)
    def _():
        finish((n_steps - 1) % 2)


def _in_proj(h, w, col_tile, n_col_tiles, epilogue, tables, table_kind, seq, tm, tn, name,
             rider=None):
    rows, d = h.shape
    ni = rows // tm
    n_steps = n_col_tiles * ni
    tiles_per_seq = seq // tm
    rider_in_specs, rider_out_specs, rider_out_shapes, rider_operands = [], [], [], []
    rider_fn = None
    if rider is not None:
        rider_fn, rider_matrix, rider_rows = rider
        r_rows, r_cols = rider_matrix.shape
        assert r_rows % n_steps == 0 and all(r.shape == (1, r_cols) for r in rider_rows)
        rider_spec = pl.BlockSpec((r_rows // n_steps, r_cols),
                                  lambda t: (jnp.minimum(t, n_steps - 1), 0))
        rider_in_specs = [rider_spec] + [pl.BlockSpec((1, r_cols), lambda t: (0, 0))
                                         for _ in rider_rows]
        rider_out_specs = [rider_spec]
        rider_out_shapes = [jax.ShapeDtypeStruct(rider_matrix.shape, BF16)]
        rider_operands = [rider_matrix, *rider_rows]

    def cur(t):
        c = jnp.minimum(t, n_steps - 1)
        return c // ni, c % ni

    def prev(t):
        p = jnp.maximum(t - 1, 0)
        return p // ni, p % ni

    table_spec = pl.BlockSpec(
        (1, tm, LANES), lambda t: (table_kind(prev(t)[0]), prev(t)[1] % tiles_per_seq, 0))

    def w_piece_spec(p):
        def index(t):
            j, i = cur(t)
            ahead = (i >= ni - W_PIECES + p).astype(jnp.int32)
            return p, col_tile(jnp.minimum(j + ahead, n_col_tiles - 1))
        return pl.BlockSpec((d // W_PIECES, tn), index)

    assert ni >= W_PIECES and d % W_PIECES == 0
    results = pl.pallas_call(
        functools.partial(_in_proj_kernel, ni=ni, n_steps=n_steps, epilogue=epilogue,
                          n_tables=len(tables), rider_fn=rider_fn),
        grid=(n_steps + 1,),
        in_specs=[pl.BlockSpec((tm, d), lambda t: (cur(t)[1], 0))]
                 + [w_piece_spec(p) for p in range(W_PIECES)]
                 + [table_spec] * len(tables) + rider_in_specs,
        out_specs=[pl.BlockSpec((tm, tn), lambda t: (prev(t)[1], prev(t)[0]))] + rider_out_specs,
        out_shape=[jax.ShapeDtypeStruct((rows, n_col_tiles * tn), BF16)] + rider_out_shapes,
        scratch_shapes=[pltpu.VMEM((d, tn), BF16), pltpu.VMEM((2, tm, tn), F32)],
        compiler_params=_params(("arbitrary",)),
        name=name,
    )(h, *([w] * W_PIECES), *tables, *rider_operands)
    return results if rider is not None else results[0]


def _mem_proj_kernel(h_ref, w_ref, cos_ref, sin_ref, o_ref, *, tn):
    j = pl.program_id(0)
    acc = jnp.dot(h_ref[...], w_ref[...].astype(BF16), preferred_element_type=F32)

    @pl.when(j < XATTN_WIDTH // tn)
    def _():
        o_ref[...] = _rope_two_tiles(acc, cos_ref[...], sin_ref[...]).astype(o_ref.dtype)

    @pl.when(j >= XATTN_WIDTH // tn)
    def _():
        o_ref[...] = acc.astype(o_ref.dtype)


def _mem_proj(hm, w, cos, sin, tn):
    rows, d = hm.shape
    ncols = w.shape[1]
    tab = pl.BlockSpec((rows, LANES), lambda j: (0, 0))
    return pl.pallas_call(
        functools.partial(_mem_proj_kernel, tn=tn),
        grid=(ncols // tn,),
        in_specs=[pl.BlockSpec((rows, d), lambda j: (0, 0)),
                  pl.BlockSpec((d, tn), lambda j: (0, j)),
                  tab, tab],
        out_specs=pl.BlockSpec((rows, tn), lambda j: (0, j)),
        out_shape=jax.ShapeDtypeStruct((rows, ncols), BF16),
        compiler_params=_params(("parallel",)),
        name="mem_proj",
    )(hm, w, cos, sin)


MAX_SINGLE_ACCESS_STRIDE = 4


def _split_stride(dil):
    if dil <= MAX_SINGLE_ACCESS_STRIDE:
        return dil, 1
    assert dil % MAX_SINGLE_ACCESS_STRIDE == 0
    f2 = dil // MAX_SINGLE_ACCESS_STRIDE
    assert f2 <= MAX_SINGLE_ACCESS_STRIDE
    return MAX_SINGLE_ACCESS_STRIDE, f2


def _dilated_attn_kernel(q0, q1, q2, k0, k1, k2, v0, v1, v2, z0, z1, z2, y0, y1, y2,
                         stage_ref, mid_ref, sub_ref, osub_ref, lsub_ref, onat_ref, lnat_ref,
                         *, seq):
    q_refs, k_refs, v_refs = (q0, q1, q2), (k0, k1, k2), (v0, v1, v2)
    z_refs, y_refs = (z0, z1, z2), (y0, y1, y2)
    nblk = seq // BLOCK
    qi = lax.broadcasted_iota(jnp.int32, (BLOCK, 2 * BLOCK), 0)
    kk = lax.broadcasted_iota(jnp.int32, (BLOCK, 2 * BLOCK), 1)
    dist = qi + BLOCK - kk
    band = (dist >= 0) & (dist <= BLOCK)
    causal = (lax.broadcasted_iota(jnp.int32, (BLOCK, BLOCK), 1)
              <= lax.broadcasted_iota(jnp.int32, (BLOCK, BLOCK), 0))

    rows_of = []
    for g, (window, dil) in enumerate(DIL_GROUPS):
        sub_len = seq // dil
        if dil == 1:
            def group_rows(which, start, size, g=g):
                return (q_refs, k_refs, v_refs)[which][g][0, pl.ds(start, size), :]
        else:
            slab = 3 * (g - 1)
            f1, f2 = _split_stride(dil)
            len1 = seq // f1
            for which, refs in enumerate((q_refs, k_refs, v_refs)):
                stage_ref[slab + which] = refs[g][0].astype(F32)
                if f2 == 1:
                    for r in range(dil):
                        sub_ref[slab + which, r * sub_len:(r + 1) * sub_len, :] = (
                            stage_ref[slab + which, pl.ds(r, sub_len, stride=dil), :].astype(BF16))
                    continue
                for r1 in range(f1):
                    mid_ref[which, r1 * len1:(r1 + 1) * len1, :] = (
                        stage_ref[slab + which, pl.ds(r1, len1, stride=f1), :])
                for r1 in range(f1):
                    for r2 in range(f2):
                        r = r2 * f1 + r1
                        sub_ref[slab + which, r * sub_len:(r + 1) * sub_len, :] = (
                            mid_ref[which, pl.ds(r1 * len1 + r2, sub_len, stride=f2), :]
                            .astype(BF16))

            def group_rows(which, start, size, slab=slab):
                return sub_ref[slab + which, pl.ds(start, size), :]
        rows_of.append(group_rows)

    def attend(g, start, first):
        q = rows_of[g](0, start, BLOCK)
        if first:
            k = rows_of[g](1, start, BLOCK)
            v = rows_of[g](2, start, BLOCK)
            mask = causal
        else:
            k = rows_of[g](1, start - BLOCK, 2 * BLOCK)
            v = rows_of[g](2, start - BLOCK, 2 * BLOCK)
            mask = band
        s = lax.dot_general(q, k, (((1,), (1,)), ((), ())), preferred_element_type=F32)
        s = jnp.where(mask, s, NEG_INF)
        m = jnp.max(s, axis=-1, keepdims=True)
        p = jnp.exp(s - m)
        l = jnp.sum(p, axis=-1, keepdims=True)
        o = jnp.dot(p.astype(BF16), v, preferred_element_type=F32) * (1.0 / l)
        lse = jnp.broadcast_to(m + jnp.log(l), (BLOCK, HEAD_DIM))
        if DIL_GROUPS[g][1] == 1:
            onat_ref[g, pl.ds(start, BLOCK), :] = o
            lnat_ref[g, pl.ds(start, BLOCK), :] = lse
        else:
            osub_ref[g - 1, pl.ds(start, BLOCK), :] = o
            lsub_ref[g - 1, pl.ds(start, BLOCK), :] = lse

    for c in range(nblk):
        for g, (window, dil) in enumerate(DIL_GROUPS):
            attend(g, c * BLOCK, c % (seq // dil // BLOCK) == 0)

    for g, (window, dil) in enumerate(DIL_GROUPS):
        sub_len = seq // dil
        if dil == 1:
            continue
        f1, f2 = _split_stride(dil)
        len1 = seq // f1
        for idx, (nat_ref, subseq_ref) in enumerate(((onat_ref, osub_ref), (lnat_ref, lsub_ref))):
            if f2 == 1:
                for r in range(dil):
                    src = slice(r * sub_len, (r + 1) * sub_len)
                    nat_ref[g, pl.ds(r, sub_len, stride=dil), :] = subseq_ref[g - 1, src, :]
                continue
            mid = 3 + idx
            for r1 in range(f1):
                for r2 in range(f2):
                    r = r2 * f1 + r1
                    mid_ref[mid, pl.ds(r1 * len1 + r2, sub_len, stride=f2), :] = (
                        subseq_ref[g - 1, r * sub_len:(r + 1) * sub_len, :])
            for r1 in range(f1):
                nat_ref[g, pl.ds(r1, len1, stride=f1), :] = mid_ref[mid, r1 * len1:(r1 + 1) * len1, :]

    chunk = 2 * BLOCK
    for t in range(seq // chunk):
        rows = slice(t * chunk, (t + 1) * chunk)
        lse = [lnat_ref[g, rows, :] for g in range(len(DIL_GROUPS))]
        mx = jnp.maximum(jnp.maximum(lse[0], lse[1]), lse[2])
        e = [jnp.exp(x - mx) for x in lse]
        inv = 1.0 / (e[0] + e[1] + e[2])
        for g in range(len(DIL_GROUPS)):
            y_refs[g][0, rows, :] = (onat_ref[g, rows, :] * (e[g] * inv)
                                     * z_refs[g][0, rows, :].astype(F32)).astype(y_refs[g].dtype)


def _dilated_attn(q_src, k_src, v_src, z_src):
    sources = (q_src, k_src, v_src, z_src)
    b, s, _ = q_src[0].shape
    n_groups = len(DIL_GROUPS)
    n_regrouped = n_groups - 1
    assert sum(_split_stride(dil)[1] > 1 for _, dil in DIL_GROUPS) <= 1

    def head_spec(off, g):
        first = (off + g * GROUP_WIDTH) // HEAD_DIM
        return pl.BlockSpec((1, s, HEAD_DIM), lambda bi, hg: (bi, 0, first + hg))

    in_specs = [head_spec(off, g) for _, off in sources for g in range(n_groups)]
    operands = [arr for arr, _ in sources for g in range(n_groups)]
    out_spec = pl.BlockSpec((1, s, HEAD_DIM), lambda bi, hg: (bi, 0, hg))
    return pl.pallas_call(
        functools.partial(_dilated_attn_kernel, seq=s),
        grid=(b, HEADS_PER_DIL),
        in_specs=in_specs,
        out_specs=[out_spec] * n_groups,
        out_shape=[jax.ShapeDtypeStruct((b, s, GROUP_WIDTH), BF16)] * n_groups,
        scratch_shapes=[pltpu.VMEM((3 * n_regrouped, s, HEAD_DIM), F32),
                        pltpu.VMEM((5, s, HEAD_DIM), F32),
                        pltpu.VMEM((3 * n_regrouped, s, HEAD_DIM), BF16),
                        pltpu.VMEM((n_regrouped, s, HEAD_DIM), F32),
                        pltpu.VMEM((n_regrouped, s, HEAD_DIM), F32),
                        pltpu.VMEM((n_groups, s, HEAD_DIM), F32),
                        pltpu.VMEM((n_groups, s, HEAD_DIM), F32)],
        compiler_params=_params(("parallel", "parallel")),
        name="dilated_attn",
    )(*operands)


def _conv_kernel(u_ref, b_ref, c_ref, z_ref, w_ref, y_ref):
    cu = c_ref[0].astype(F32) * u_ref[0].astype(F32)
    row = lax.broadcasted_iota(jnp.int32, cu.shape, 0)
    w = w_ref[...]
    acc = cu * w[CONV_K - 1:CONV_K, :]
    for lag in range(1, CONV_K):
        shifted = jnp.where(row >= lag, pltpu.roll(cu, lag, axis=0), 0.0)
        acc = acc + shifted * w[CONV_K - 1 - lag:CONV_K - lag, :]
    y_ref[0] = (b_ref[0].astype(F32) * acc * z_ref[0].astype(F32)).astype(y_ref.dtype)


def _conv(u_src, b_src, c_src, z_src, conv_w, tc):
    sources = (u_src, b_src, c_src, z_src)
    b, s, _ = u_src[0].shape

    def spec(off):
        assert off % tc == 0
        return pl.BlockSpec((1, s, tc), lambda bi, j: (bi, 0, off // tc + j))

    return pl.pallas_call(
        _conv_kernel,
        grid=(b, CONV_WIDTH // tc),
        in_specs=[spec(off) for _, off in sources]
                 + [pl.BlockSpec((CONV_K, tc), lambda bi, j: (0, j))],
        out_specs=pl.BlockSpec((1, s, tc), lambda bi, j: (bi, 0, j)),
        out_shape=jax.ShapeDtypeStruct((b, s, CONV_WIDTH), BF16),
        compiler_params=_params(("parallel", "parallel")),
        name="gated_conv",
    )(*[arr for arr, _ in sources], conv_w)


def _xattn_kernel(q_ref, z_ref, mk_ref, mv_ref, y_ref):
    for h in range(N_XATTN_HEADS):
        cols = slice(h * XATTN_HEAD_DIM, (h + 1) * XATTN_HEAD_DIM)
        s = lax.dot_general(q_ref[0, :, cols], mk_ref[0, :, cols],
                            (((1,), (1,)), ((), ())), preferred_element_type=F32)
        m = jnp.max(s, axis=-1, keepdims=True)
        p = jnp.exp(s - m)
        l = jnp.sum(p, axis=-1, keepdims=True)
        o = jnp.dot(p.astype(BF16), mv_ref[0, :, cols], preferred_element_type=F32)
        y_ref[0, :, cols] = (o * (1.0 / l) * z_ref[0, :, cols].astype(F32)).astype(y_ref.dtype)


def _xattn(q_src, z_src, mkv3, tm):
    b, s, _ = q_src[0].shape
    m_len = mkv3.shape[1]
    q_off, z_off = q_src[1], z_src[1]
    assert q_off % XATTN_WIDTH == 0 and z_off % XATTN_WIDTH == 0
    return pl.pallas_call(
        _xattn_kernel,
        grid=(b, s // tm),
        in_specs=[pl.BlockSpec((1, tm, XATTN_WIDTH), lambda bi, i: (bi, i, q_off // XATTN_WIDTH)),
                  pl.BlockSpec((1, tm, XATTN_WIDTH), lambda bi, i: (bi, i, z_off // XATTN_WIDTH)),
                  pl.BlockSpec((1, m_len, XATTN_WIDTH), lambda bi, i: (bi, 0, 0)),
                  pl.BlockSpec((1, m_len, XATTN_WIDTH), lambda bi, i: (bi, 0, 1))],
        out_specs=pl.BlockSpec((1, tm, XATTN_WIDTH), lambda bi, i: (bi, i, 0)),
        out_shape=jax.ShapeDtypeStruct((b, s, XATTN_WIDTH), BF16),
        compiler_params=_params(("parallel", "parallel")),
        name="mem_xattn",
    )(q_src[0], z_src[0], mkv3, mkv3)


def _out_proj_kernel(*refs, ni, d, n_mix):
    y_refs, w_refs = refs[:n_mix], refs[n_mix:2 * n_mix]
    x_ref, g_ref, o_ref, y2_ref, ssq_ref, scale_ref = refs[2 * n_mix:]
    i = pl.program_id(0)
    j = pl.program_id(1)

    @pl.when(j == 0)
    def _():
        @pl.when(i > 0)
        def _():
            scale_ref[...] = lax.rsqrt(ssq_ref[...] * (1.0 / d) + EPS)
        ssq_ref[...] = jnp.zeros_like(ssq_ref)

    def finish_previous_tile():
        o_ref[...] = x_ref[...] + y2_ref[j] * scale_ref[...] * g_ref[...]

    def multiply_this_tile():
        y2 = jnp.dot(y_refs[0][...], w_refs[0][...], preferred_element_type=F32)
        for y_ref, w_ref in zip(y_refs[1:], w_refs[1:]):
            y2 = y2 + jnp.dot(y_ref[...], w_ref[...], preferred_element_type=F32)
        y2_ref[j] = y2
        ssq_ref[...] += jnp.sum(y2 * y2, axis=-1, keepdims=True)

    @pl.when(i == 0)
    def _():
        multiply_this_tile()

    @pl.when((i > 0) & (i < ni))
    def _():
        finish_previous_tile()
        multiply_this_tile()

    @pl.when(i == ni)
    def _():
        finish_previous_tile()


def _out_proj(ys, w_bf, x2d, gain, tm, tn):
    rows, d = x2d.shape
    nj = d // tn
    ni = rows // tm

    def y_spec(width):
        return pl.BlockSpec((tm, width), lambda i, j: (jnp.minimum(i, ni - 1), 0))

    def w_spec(width, row_off):
        assert row_off % width == 0
        return pl.BlockSpec((width, tn),
                            lambda i, j: (row_off // width, jnp.where(i == ni, nj - 1, j)))

    io_spec = pl.BlockSpec((tm, tn),
                           lambda i, j: (jnp.maximum(i - 1, 0), jnp.where(i == 0, 0, j)))
    widths = [y.shape[1] for y in ys]
    offsets = [sum(widths[:k]) for k in range(len(widths))]
    return pl.pallas_call(
        functools.partial(_out_proj_kernel, ni=ni, d=d, n_mix=len(ys)),
        grid=(ni + 1, nj),
        in_specs=([y_spec(w) for w in widths]
                  + [w_spec(w, off) for w, off in zip(widths, offsets)]
                  + [io_spec, pl.BlockSpec((1, tn), lambda i, j: (0, j))]),
        out_specs=io_spec,
        out_shape=jax.ShapeDtypeStruct((rows, d), F32),
        scratch_shapes=[pltpu.VMEM((nj, tm, tn), F32),
                        pltpu.VMEM((tm, 1), F32),
                        pltpu.VMEM((tm, 1), F32)],
        compiler_params=_params(("arbitrary", "arbitrary")),
        name="out_proj",
    )(*ys, *([w_bf] * len(ys)), x2d, gain.reshape(1, d))


def _rope_tables(pos, half):
    inv = 1.0 / (ROPE_THETA ** (np.arange(half, dtype=np.float64) / half))
    ang = np.asarray(pos, dtype=np.float64)[:, None] * inv[None, :]
    return np.cos(ang), np.sin(ang)


def kernel(x, mem, pre_norm, w_in, conv_w, mem_norm, w_mem_kv, w_out, post_norm):
    b, s, d = x.shape
    m_len = mem.shape[1]
    depth = w_in.shape[0]
    assert all(window // dil == BLOCK for window, dil in DIL_GROUPS)
    pos = np.arange(s)
    cos_a, sin_a = _rope_tables(pos, HEAD_DIM // 2)
    cosa = np.concatenate([cos_a, cos_a], axis=-1)
    sina = np.concatenate([-sin_a, sin_a], axis=-1)
    a_scale = HEAD_DIM ** -0.5
    cos_qk = jnp.asarray(np.stack([cosa * a_scale, cosa]), F32)
    sin_qk = jnp.asarray(np.stack([sina * a_scale, sina]), F32)
    cosx, sinx = _rope_tables(pos + m_len, XATTN_HEAD_DIM // 2)
    x_scale = XATTN_HEAD_DIM ** -0.5
    cos_qx = jnp.asarray((cosx * x_scale)[None], F32)
    sin_qx = jnp.asarray((sinx * x_scale)[None], F32)
    cos_m, sin_m = _rope_tables(np.arange(m_len), XATTN_HEAD_DIM // 2)
    cosm = jnp.asarray(np.tile(cos_m, (b, 1)), F32)
    sinm = jnp.asarray(np.tile(sin_m, (b, 1)), F32)

    tm, tn = 1024, 512

    def tile_of(off):
        assert off % tn == 0
        return off // tn

    attn_tiles, conv_tiles, x_tiles = ATTN_WIDTH // tn, CONV_WIDTH // tn, XATTN_WIDTH // tn

    def view(t):
        return t.reshape(b, s, t.shape[-1])

    for layer in range(depth):
        x2d = x.reshape(b * s, d)
        h = _rmsnorm(x2d, pre_norm[layer], tm=512)
        w = w_in[layer]
        common = dict(seq=s, tm=tm, tn=tn)

        qk = view(_in_proj(h, w, lambda j: tile_of(OFF_QA) + j, 2 * attn_tiles, _rope_half_lane,
                           [cos_qk, sin_qk], lambda j: j // attn_tiles, name="in_proj_qk",
                           **common))
        plain = view(_in_proj(
            h, w, lambda j: jnp.where(j < attn_tiles, tile_of(OFF_VA) + j,
                                      tile_of(OFF_UC) - attn_tiles + j),
            attn_tiles + 3 * conv_tiles, lambda acc: acc, [], None, name="in_proj_plain",
            **common))
        gates, w_out_bf = _in_proj(
            h, w, lambda j: jnp.where(j < attn_tiles, tile_of(OFF_ZA) + j,
                                      jnp.where(j < attn_tiles + conv_tiles,
                                                tile_of(OFF_ZC) - attn_tiles + j,
                                                tile_of(OFF_ZX) - attn_tiles - conv_tiles + j)),
            attn_tiles + conv_tiles + x_tiles, _silu, [], None, name="in_proj_gates",
            rider=(lambda block: block, w_out[layer], []), **common)
        gates = view(gates)
        qx, hm = _in_proj(h, w, lambda j: tile_of(OFF_QX) + j, x_tiles, _rope_two_tiles,
                          [cos_qx, sin_qx], lambda j: 0, name="in_proj_qx",
                          rider=(_rms_normalize, mem.reshape(b * m_len, d),
                                 [mem_norm[layer].reshape(1, d)]), **common)
        qx = view(qx)
        mkv = _mem_proj(hm, w_mem_kv[layer], cosm, sinm, tn=512)

        y_groups = _dilated_attn((qk, 0), (qk, ATTN_WIDTH), (plain, 0), (gates, 0))
        y_groups = [y.reshape(b * s, GROUP_WIDTH) for y in y_groups]
        y_conv = _conv((plain, ATTN_WIDTH), (plain, ATTN_WIDTH + CONV_WIDTH),
                       (plain, ATTN_WIDTH + 2 * CONV_WIDTH), (gates, ATTN_WIDTH),
                       conv_w[layer], tc=256).reshape(b * s, CONV_WIDTH)
        y_x = _xattn((qx, 0), (gates, ATTN_WIDTH + CONV_WIDTH),
                     mkv.reshape(b, m_len, 2 * XATTN_WIDTH), tm=512)
        y_x = y_x.reshape(b * s, XATTN_WIDTH)

        out = _out_proj([*y_groups, y_conv, y_x], w_out_bf, x2d,
                        post_norm[layer], tm=1024, tn=512)
        x = out.reshape(b, s, d)
    return x
```

```python
import functools

import jax
import jax.numpy as jnp
import numpy as np
from jax import lax
from jax.experimental import pallas as pl
from jax.experimental.pallas import tpu as pltpu

D_MODEL = 4096
MEM_LEN = 256
HEAD_DIM = 128
DIL_GROUPS = ((128, 1), (512, 4), (2048, 16))
ATTN_WIDTH = 3 * D_MODEL // 8
CONV_WIDTH = 3 * D_MODEL // 8
XATTN_WIDTH = D_MODEL // 4
N_XATTN_HEADS = 4
XATTN_HEAD_DIM = XATTN_WIDTH // N_XATTN_HEADS
HEADS_PER_DIL = 4
GROUP_WIDTH = HEADS_PER_DIL * HEAD_DIM
CONV_K = 3
IN_COLS = 4 * ATTN_WIDTH + 4 * CONV_WIDTH + 2 * XATTN_WIDTH
BLOCK = 128
ROPE_THETA = 10000.0
EPS = 1e-6
NEG_INF = -1e30

OFF_QA = 0
OFF_KA = OFF_QA + ATTN_WIDTH
OFF_VA = OFF_KA + ATTN_WIDTH
OFF_ZA = OFF_VA + ATTN_WIDTH
OFF_UC = OFF_ZA + ATTN_WIDTH
OFF_BC = OFF_UC + CONV_WIDTH
OFF_CC = OFF_BC + CONV_WIDTH
OFF_ZC = OFF_CC + CONV_WIDTH
OFF_QX = OFF_ZC + CONV_WIDTH
OFF_ZX = OFF_QX + XATTN_WIDTH

V7X_VMEM_LIMIT_BYTES = 56 * 1024 * 1024
LANES = 128
W_PIECES = 4
EPILOGUE_ROWS = 128
CONV_HALO = 8

BF16 = jnp.bfloat16
F32 = jnp.float32


def _params(semantics):
    return pltpu.CompilerParams(dimension_semantics=semantics,
                                vmem_limit_bytes=V7X_VMEM_LIMIT_BYTES)


def _rms_normalize(x, gain):
    ms = jnp.mean(x * x, axis=-1, keepdims=True)
    return x * lax.rsqrt(ms + EPS) * gain


def _rmsnorm_kernel(x_ref, g_ref, o_ref):
    o_ref[...] = _rms_normalize(x_ref[...], g_ref[...]).astype(o_ref.dtype)


def _rmsnorm(x2d, gain, tm):
    rows, d = x2d.shape
    return pl.pallas_call(
        _rmsnorm_kernel,
        grid=(rows // tm,),
        in_specs=[pl.BlockSpec((tm, d), lambda i: (i, 0)),
                  pl.BlockSpec((1, d), lambda i: (0, 0))],
        out_specs=pl.BlockSpec((tm, d), lambda i: (i, 0)),
        out_shape=jax.ShapeDtypeStruct((rows, d), BF16),
        compiler_params=_params(("parallel",)),
        name="rmsnorm",
    )(x2d, gain.reshape(1, d))


def _silu(z):
    half = 0.5 * z
    return half + half * jnp.tanh(half)


def _rope_half_lane(acc, cos, sin_signed):
    outs = []
    for h in range(acc.shape[1] // HEAD_DIM):
        xh = acc[:, h * HEAD_DIM:(h + 1) * HEAD_DIM]
        outs.append(xh * cos + pltpu.roll(xh, HEAD_DIM // 2, axis=1) * sin_signed)
    return jnp.concatenate(outs, axis=1)


def _rope_two_tiles(acc, cos, sin):
    outs = []
    for h in range(acc.shape[1] // XATTN_HEAD_DIM):
        x1 = acc[:, h * XATTN_HEAD_DIM:h * XATTN_HEAD_DIM + LANES]
        x2 = acc[:, h * XATTN_HEAD_DIM + LANES:(h + 1) * XATTN_HEAD_DIM]
        outs.append(x1 * cos - x2 * sin)
        outs.append(x2 * cos + x1 * sin)
    return jnp.concatenate(outs, axis=1)


def _elementwise_finish(epilogue):
    def finish(acc_ref, slot, o_ref, aux_refs, scratch_refs, row_tile):
        for r0 in range(0, o_ref.shape[0], EPILOGUE_ROWS):
            rows = slice(r0, r0 + EPILOGUE_ROWS)
            o_ref[rows, :] = epilogue(acc_ref[slot, rows, :],
                                      *[r[0, rows, :] for r in aux_refs]).astype(o_ref.dtype)
    return finish


def _gated_conv_finish(acc_ref, slot, o_ref, aux_refs, scratch_refs, row_tile, *, tiles_per_seq):
    (w_ref,), (cu_ref,) = aux_refs, scratch_refs
    tm = o_ref.shape[0]
    u_cols, b_cols, c_cols, z_cols = (slice(k * LANES, (k + 1) * LANES) for k in range(4))
    carried = cu_ref[tm:tm + CONV_HALO, :]
    cu_ref[0:CONV_HALO, :] = jnp.where(row_tile % tiles_per_seq == 0, 0.0, carried)
    for r0 in range(0, tm, EPILOGUE_ROWS):
        rows = slice(r0, r0 + EPILOGUE_ROWS)
        cu_ref[CONV_HALO + r0:CONV_HALO + r0 + EPILOGUE_ROWS, :] = (
            acc_ref[slot, rows, c_cols] * acc_ref[slot, rows, u_cols])
    w = w_ref[...]
    for r0 in range(0, tm, EPILOGUE_ROWS):
        rows = slice(r0, r0 + EPILOGUE_ROWS)
        conv = None
        for lag in range(CONV_K):
            first = CONV_HALO + r0 - lag
            tap = cu_ref[first:first + EPILOGUE_ROWS, :] * w[CONV_K - 1 - lag:CONV_K - lag, :]
            conv = tap if conv is None else conv + tap
        o_ref[rows, :] = (acc_ref[slot, rows, b_cols] * conv
                          * _silu(acc_ref[slot, rows, z_cols])).astype(o_ref.dtype)


def _in_proj_kernel(*refs, ni, n_steps, n_w, n_aux, n_scratch, finish_fn, rider_fn,
                    rider_steps):
    h_ref = refs[0]
    w_refs = refs[1:1 + n_w]
    aux_refs = refs[1 + n_w:1 + n_w + n_aux]
    scratch_refs = refs[len(refs) - n_scratch:] if n_scratch else ()
    rest = refs[1 + n_w + n_aux:len(refs) - n_scratch]
    if rider_fn is not None:
        rider_in_ref, *rider_row_refs = rest[:-4]
        o_ref, rider_out_ref, wbf_ref, acc_ref = rest[-4:]
    else:
        o_ref, wbf_ref, acc_ref = rest
    t = pl.program_id(0)
    piece_rows = wbf_ref.shape[0] // W_PIECES
    slab_cols = wbf_ref.shape[1] // (n_w // W_PIECES)

    @pl.when((t % ni == 0) & (t < n_steps))
    def _():
        for k, w_ref in enumerate(w_refs):
            slab, p = divmod(k, W_PIECES)
            wbf_ref[p * piece_rows:(p + 1) * piece_rows,
                    slab * slab_cols:(slab + 1) * slab_cols] = w_ref[...].astype(BF16)

    if rider_fn is not None:
        @pl.when(t < rider_steps)
        def _():
            rider_out_ref[...] = rider_fn(
                rider_in_ref[...], *[r[...] for r in rider_row_refs]).astype(rider_out_ref.dtype)

    def multiply(slot):
        acc_ref[slot] = jnp.dot(h_ref[...], wbf_ref[...], preferred_element_type=F32)

    def finish(slot):
        finish_fn(acc_ref, slot, o_ref, aux_refs, scratch_refs, ((t - 1) % ni))

    @pl.when(t == 0)
    def _():
        for ref in scratch_refs:
            ref[...] = jnp.zeros_like(ref)
        multiply(0)

    for parity in range(2):
        @pl.when((t > 0) & (t < n_steps) & (t % 2 == parity))
        def _(parity=parity):
            finish(1 - parity)
            multiply(parity)

    @pl.when(t == n_steps)
    def _():
        finish((n_steps - 1) % 2)


def _in_proj(h, w, col_slabs, n_col_tiles, finish_fn, aux, tm, tn, name, out_tn=None,
             scratch=(), rider=None):
    rows, d = h.shape
    ni = rows // tm
    n_steps = n_col_tiles * ni
    out_tn = tn if out_tn is None else out_tn
    slab_cols = tn // len(col_slabs)
    rider_in_specs, rider_out_specs, rider_out_shapes, rider_operands = [], [], [], []
    rider_fn, rider_steps = None, 0
    if rider is not None:
        rider_fn, rider_matrix, rider_rows = rider
        r_rows, r_cols = rider_matrix.shape
        assert all(r.shape == (1, r_cols) for r in rider_rows)
        rider_steps = max(k for k in range(1, n_steps + 1) if r_rows % k == 0)
        rider_spec = pl.BlockSpec((r_rows // rider_steps, r_cols),
                                  lambda t: (jnp.minimum(t, rider_steps - 1), 0))
        rider_in_specs = [rider_spec] + [pl.BlockSpec((1, r_cols), lambda t: (0, 0))
                                         for _ in rider_rows]
        rider_out_specs = [rider_spec]
        rider_out_shapes = [jax.ShapeDtypeStruct(rider_matrix.shape, BF16)]
        rider_operands = [rider_matrix, *rider_rows]

    def cur(t):
        c = jnp.minimum(t, n_steps - 1)
        return c // ni, c % ni

    def prev(t):
        p = jnp.maximum(t - 1, 0)
        return p // ni, p % ni

    def aux_spec(block, index):
        return pl.BlockSpec(block, lambda t: index(*prev(t)))

    def w_piece_spec(col_slab, p):
        def index(t):
            j, i = cur(t)
            ahead = (i >= ni - W_PIECES + p).astype(jnp.int32)
            return p, col_slab(jnp.minimum(j + ahead, n_col_tiles - 1))
        return pl.BlockSpec((d // W_PIECES, slab_cols), index)

    assert ni >= W_PIECES and d % W_PIECES == 0
    w_specs = [w_piece_spec(col_slab, p) for col_slab in col_slabs for p in range(W_PIECES)]
    results = pl.pallas_call(
        functools.partial(_in_proj_kernel, ni=ni, n_steps=n_steps, n_w=len(w_specs),
                          n_aux=len(aux), n_scratch=len(scratch), finish_fn=finish_fn,
                          rider_fn=rider_fn, rider_steps=rider_steps),
        grid=(n_steps + 1,),
        in_specs=[pl.BlockSpec((tm, d), lambda t: (cur(t)[1], 0))] + w_specs
                 + [aux_spec(block, index) for _, block, index in aux] + rider_in_specs,
        out_specs=[pl.BlockSpec((tm, out_tn), lambda t: (prev(t)[1], prev(t)[0]))]
                  + rider_out_specs,
        out_shape=[jax.ShapeDtypeStruct((rows, n_col_tiles * out_tn), BF16)] + rider_out_shapes,
        scratch_shapes=[pltpu.VMEM((d, tn), BF16), pltpu.VMEM((2, tm, tn), F32), *scratch],
        compiler_params=_params(("arbitrary",)),
        name=name,
    )(h, *([w] * len(w_specs)), *[arr for arr, _, _ in aux], *rider_operands)
    return results if rider is not None else results[0]


def _mem_proj_kernel(h_ref, w_ref, cos_ref, sin_ref, o_ref, *, tn):
    j = pl.program_id(0)
    acc = jnp.dot(h_ref[...], w_ref[...].astype(BF16), preferred_element_type=F32)

    @pl.when(j < XATTN_WIDTH // tn)
    def _():
        o_ref[...] = _rope_two_tiles(acc, cos_ref[...], sin_ref[...]).astype(o_ref.dtype)

    @pl.when(j >= XATTN_WIDTH // tn)
    def _():
        o_ref[...] = acc.astype(o_ref.dtype)


def _mem_proj(hm, w, cos, sin, tn):
    rows, d = hm.shape
    ncols = w.shape[1]
    tab = pl.BlockSpec((rows, LANES), lambda j: (0, 0))
    return pl.pallas_call(
        functools.partial(_mem_proj_kernel, tn=tn),
        grid=(ncols // tn,),
        in_specs=[pl.BlockSpec((rows, d), lambda j: (0, 0)),
                  pl.BlockSpec((d, tn), lambda j: (0, j)),
                  tab, tab],
        out_specs=pl.BlockSpec((rows, tn), lambda j: (0, j)),
        out_shape=jax.ShapeDtypeStruct((rows, ncols), BF16),
        compiler_params=_params(("parallel",)),
        name="mem_proj",
    )(hm, w, cos, sin)


MAX_SINGLE_ACCESS_STRIDE = 4


def _split_stride(dil):
    if dil <= MAX_SINGLE_ACCESS_STRIDE:
        return dil, 1
    assert dil % MAX_SINGLE_ACCESS_STRIDE == 0
    f2 = dil // MAX_SINGLE_ACCESS_STRIDE
    assert f2 <= MAX_SINGLE_ACCESS_STRIDE
    return MAX_SINGLE_ACCESS_STRIDE, f2


def _dilated_attn_kernel(q0, q1, q2, k0, k1, k2, v0, v1, v2, z0, z1, z2, y0, y1, y2,
                         stage_ref, mid_ref, sub_ref, osub_ref, lsub_ref, onat_ref, lnat_ref,
                         *, seq):
    q_refs, k_refs, v_refs = (q0, q1, q2), (k0, k1, k2), (v0, v1, v2)
    z_refs, y_refs = (z0, z1, z2), (y0, y1, y2)
    nblk = seq // BLOCK
    qi = lax.broadcasted_iota(jnp.int32, (BLOCK, 2 * BLOCK), 0)
    kk = lax.broadcasted_iota(jnp.int32, (BLOCK, 2 * BLOCK), 1)
    dist = qi + BLOCK - kk
    band = (dist >= 0) & (dist <= BLOCK)
    causal = (lax.broadcasted_iota(jnp.int32, (BLOCK, BLOCK), 1)
              <= lax.broadcasted_iota(jnp.int32, (BLOCK, BLOCK), 0))

    rows_of = []
    for g, (window, dil) in enumerate(DIL_GROUPS):
        sub_len = seq // dil
        if dil == 1:
            def group_rows(which, start, size, g=g):
                return (q_refs, k_refs, v_refs)[which][g][0, pl.ds(start, size), :]
        else:
            slab = 3 * (g - 1)
            f1, f2 = _split_stride(dil)
            len1 = seq // f1
            for which, refs in enumerate((q_refs, k_refs, v_refs)):
                stage_ref[slab + which] = refs[g][0].astype(F32)
                if f2 == 1:
                    for r in range(dil):
                        sub_ref[slab + which, r * sub_len:(r + 1) * sub_len, :] = (
                            stage_ref[slab + which, pl.ds(r, sub_len, stride=dil), :].astype(BF16))
                    continue
                for r1 in range(f1):
                    mid_ref[which, r1 * len1:(r1 + 1) * len1, :] = (
                        stage_ref[slab + which, pl.ds(r1, len1, stride=f1), :])
                for r1 in range(f1):
                    for r2 in range(f2):
                        r = r2 * f1 + r1
                        sub_ref[slab + which, r * sub_len:(r + 1) * sub_len, :] = (
                            mid_ref[which, pl.ds(r1 * len1 + r2, sub_len, stride=f2), :]
                            .astype(BF16))

            def group_rows(which, start, size, slab=slab):
                return sub_ref[slab + which, pl.ds(start, size), :]
        rows_of.append(group_rows)

    def attend(g, start, first):
        q = rows_of[g](0, start, BLOCK)
        if first:
            k = rows_of[g](1, start, BLOCK)
            v = rows_of[g](2, start, BLOCK)
            mask = causal
        else:
            k = rows_of[g](1, start - BLOCK, 2 * BLOCK)
            v = rows_of[g](2, start - BLOCK, 2 * BLOCK)
            mask = band
        s = lax.dot_general(q, k, (((1,), (1,)), ((), ())), preferred_element_type=F32)
        s = jnp.where(mask, s, NEG_INF)
        m = jnp.max(s, axis=-1, keepdims=True)
        p = jnp.exp(s - m)
        l = jnp.sum(p, axis=-1, keepdims=True)
        o = jnp.dot(p.astype(BF16), v, preferred_element_type=F32) * (1.0 / l)
        lse = jnp.broadcast_to(m + jnp.log(l), (BLOCK, HEAD_DIM))
        if DIL_GROUPS[g][1] == 1:
            onat_ref[g, pl.ds(start, BLOCK), :] = o
            lnat_ref[g, pl.ds(start, BLOCK), :] = lse
        else:
            osub_ref[g - 1, pl.ds(start, BLOCK), :] = o
            lsub_ref[g - 1, pl.ds(start, BLOCK), :] = lse

    for c in range(nblk):
        for g, (window, dil) in enumerate(DIL_GROUPS):
            attend(g, c * BLOCK, c % (seq // dil // BLOCK) == 0)

    for g, (window, dil) in enumerate(DIL_GROUPS):
        sub_len = seq // dil
        if dil == 1:
            continue
        f1, f2 = _split_stride(dil)
        len1 = seq // f1
        for idx, (nat_ref, subseq_ref) in enumerate(((onat_ref, osub_ref), (lnat_ref, lsub_ref))):
            if f2 == 1:
                for r in range(dil):
                    src = slice(r * sub_len, (r + 1) * sub_len)
                    nat_ref[g, pl.ds(r, sub_len, stride=dil), :] = subseq_ref[g - 1, src, :]
                continue
            mid = 3 + idx
            for r1 in range(f1):
                for r2 in range(f2):
                    r = r2 * f1 + r1
                    mid_ref[mid, pl.ds(r1 * len1 + r2, sub_len, stride=f2), :] = (
                        subseq_ref[g - 1, r * sub_len:(r + 1) * sub_len, :])
            for r1 in range(f1):
                nat_ref[g, pl.ds(r1, len1, stride=f1), :] = mid_ref[mid, r1 * len1:(r1 + 1) * len1, :]

    chunk = 2 * BLOCK
    for t in range(seq // chunk):
        rows = slice(t * chunk, (t + 1) * chunk)
        lse = [lnat_ref[g, rows, :] for g in range(len(DIL_GROUPS))]
        mx = jnp.maximum(jnp.maximum(lse[0], lse[1]), lse[2])
        e = [jnp.exp(x - mx) for x in lse]
        inv = 1.0 / (e[0] + e[1] + e[2])
        for g in range(len(DIL_GROUPS)):
            y_refs[g][0, rows, :] = (onat_ref[g, rows, :] * (e[g] * inv)
                                     * z_refs[g][0, rows, :].astype(F32)).astype(y_refs[g].dtype)


def _dilated_attn(q_src, k_src, v_src, z_src):
    sources = (q_src, k_src, v_src, z_src)
    b, s, _ = q_src[0].shape
    n_groups = len(DIL_GROUPS)
    n_regrouped = n_groups - 1
    assert sum(_split_stride(dil)[1] > 1 for _, dil in DIL_GROUPS) <= 1

    def head_spec(off, g):
        first = (off + g * GROUP_WIDTH) // HEAD_DIM
        return pl.BlockSpec((1, s, HEAD_DIM), lambda bi, hg: (bi, 0, first + hg))

    in_specs = [head_spec(off, g) for _, off in sources for g in range(n_groups)]
    operands = [arr for arr, _ in sources for g in range(n_groups)]
    out_spec = pl.BlockSpec((1, s, HEAD_DIM), lambda bi, hg: (bi, 0, hg))
    return pl.pallas_call(
        functools.partial(_dilated_attn_kernel, seq=s),
        grid=(b, HEADS_PER_DIL),
        in_specs=in_specs,
        out_specs=[out_spec] * n_groups,
        out_shape=[jax.ShapeDtypeStruct((b, s, GROUP_WIDTH), BF16)] * n_groups,
        scratch_shapes=[pltpu.VMEM((3 * n_regrouped, s, HEAD_DIM), F32),
                        pltpu.VMEM((5, s, HEAD_DIM), F32),
                        pltpu.VMEM((3 * n_regrouped, s, HEAD_DIM), BF16),
                        pltpu.VMEM((n_regrouped, s, HEAD_DIM), F32),
                        pltpu.VMEM((n_regrouped, s, HEAD_DIM), F32),
                        pltpu.VMEM((n_groups, s, HEAD_DIM), F32),
                        pltpu.VMEM((n_groups, s, HEAD_DIM), F32)],
        compiler_params=_params(("parallel", "parallel")),
        name="dilated_attn",
    )(*operands)


def _xattn_kernel(q_ref, z_ref, mk_ref, mv_ref, y_ref):
    for h in range(N_XATTN_HEADS):
        cols = slice(h * XATTN_HEAD_DIM, (h + 1) * XATTN_HEAD_DIM)
        s = lax.dot_general(q_ref[0, :, cols], mk_ref[0, :, cols],
                            (((1,), (1,)), ((), ())), preferred_element_type=F32)
        m = jnp.max(s, axis=-1, keepdims=True)
        p = jnp.exp(s - m)
        l = jnp.sum(p, axis=-1, keepdims=True)
        o = jnp.dot(p.astype(BF16), mv_ref[0, :, cols], preferred_element_type=F32)
        y_ref[0, :, cols] = (o * (1.0 / l) * z_ref[0, :, cols].astype(F32)).astype(y_ref.dtype)


def _xattn(q_src, z_src, mkv3, tm):
    b, s, _ = q_src[0].shape
    m_len = mkv3.shape[1]
    q_off, z_off = q_src[1], z_src[1]
    assert q_off % XATTN_WIDTH == 0 and z_off % XATTN_WIDTH == 0
    return pl.pallas_call(
        _xattn_kernel,
        grid=(b, s // tm),
        in_specs=[pl.BlockSpec((1, tm, XATTN_WIDTH), lambda bi, i: (bi, i, q_off // XATTN_WIDTH)),
                  pl.BlockSpec((1, tm, XATTN_WIDTH), lambda bi, i: (bi, i, z_off // XATTN_WIDTH)),
                  pl.BlockSpec((1, m_len, XATTN_WIDTH), lambda bi, i: (bi, 0, 0)),
                  pl.BlockSpec((1, m_len, XATTN_WIDTH), lambda bi, i: (bi, 0, 1))],
        out_specs=pl.BlockSpec((1, tm, XATTN_WIDTH), lambda bi, i: (bi, i, 0)),
        out_shape=jax.ShapeDtypeStruct((b, s, XATTN_WIDTH), BF16),
        compiler_params=_params(("parallel", "parallel")),
        name="mem_xattn",
    )(q_src[0], z_src[0], mkv3, mkv3)


def _out_proj_kernel(*refs, ni, d, n_mix):
    y_refs, w_refs = refs[:n_mix], refs[n_mix:2 * n_mix]
    x_ref, g_ref, o_ref, y2_ref, ssq_ref, scale_ref = refs[2 * n_mix:]
    i = pl.program_id(0)
    j = pl.program_id(1)

    @pl.when(j == 0)
    def _():
        @pl.when(i > 0)
        def _():
            scale_ref[...] = lax.rsqrt(ssq_ref[...] * (1.0 / d) + EPS)
        ssq_ref[...] = jnp.zeros_like(ssq_ref)

    def finish_previous_tile():
        o_ref[...] = x_ref[...] + y2_ref[j] * scale_ref[...] * g_ref[...]

    def multiply_this_tile():
        y2 = jnp.dot(y_refs[0][...], w_refs[0][...], preferred_element_type=F32)
        for y_ref, w_ref in zip(y_refs[1:], w_refs[1:]):
            y2 = y2 + jnp.dot(y_ref[...], w_ref[...], preferred_element_type=F32)
        y2_ref[j] = y2
        ssq_ref[...] += jnp.sum(y2 * y2, axis=-1, keepdims=True)

    @pl.when(i == 0)
    def _():
        multiply_this_tile()

    @pl.when((i > 0) & (i < ni))
    def _():
        finish_previous_tile()
        multiply_this_tile()

    @pl.when(i == ni)
    def _():
        finish_previous_tile()


def _out_proj(ys, w_bf, x2d, gain, tm, tn):
    rows, d = x2d.shape
    nj = d // tn
    ni = rows // tm

    def y_spec(width):
        return pl.BlockSpec((tm, width), lambda i, j: (jnp.minimum(i, ni - 1), 0))

    def w_spec(width, row_off):
        assert row_off % width == 0
        return pl.BlockSpec((width, tn),
                            lambda i, j: (row_off // width, jnp.where(i == ni, nj - 1, j)))

    io_spec = pl.BlockSpec((tm, tn),
                           lambda i, j: (jnp.maximum(i - 1, 0), jnp.where(i == 0, 0, j)))
    widths = [y.shape[1] for y in ys]
    offsets = [sum(widths[:k]) for k in range(len(widths))]
    return pl.pallas_call(
        functools.partial(_out_proj_kernel, ni=ni, d=d, n_mix=len(ys)),
        grid=(ni + 1, nj),
        in_specs=([y_spec(w) for w in widths]
                  + [w_spec(w, off) for w, off in zip(widths, offsets)]
                  + [io_spec, pl.BlockSpec((1, tn), lambda i, j: (0, j))]),
        out_specs=io_spec,
        out_shape=jax.ShapeDtypeStruct((rows, d), F32),
        scratch_shapes=[pltpu.VMEM((nj, tm, tn), F32),
                        pltpu.VMEM((tm, 1), F32),
                        pltpu.VMEM((tm, 1), F32)],
        compiler_params=_params(("arbitrary", "arbitrary")),
        name="out_proj",
    )(*ys, *([w_bf] * len(ys)), x2d, gain.reshape(1, d))


def _rope_tables(pos, half):
    inv = 1.0 / (ROPE_THETA ** (np.arange(half, dtype=np.float64) / half))
    ang = np.asarray(pos, dtype=np.float64)[:, None] * inv[None, :]
    return np.cos(ang), np.sin(ang)


def kernel(x, mem, pre_norm, w_in, conv_w, mem_norm, w_mem_kv, w_out, post_norm):
    b, s, d = x.shape
    m_len = mem.shape[1]
    depth = w_in.shape[0]
    assert all(window // dil == BLOCK for window, dil in DIL_GROUPS)
    pos = np.arange(s)
    cos_a, sin_a = _rope_tables(pos, HEAD_DIM // 2)
    cosa = np.concatenate([cos_a, cos_a], axis=-1)
    sina = np.concatenate([-sin_a, sin_a], axis=-1)
    a_scale = HEAD_DIM ** -0.5
    cos_qkv = jnp.asarray(np.stack([cosa * a_scale, cosa, np.ones_like(cosa)]), F32)
    sin_qkv = jnp.asarray(np.stack([sina * a_scale, sina, np.zeros_like(sina)]), F32)
    cosx, sinx = _rope_tables(pos + m_len, XATTN_HEAD_DIM // 2)
    x_scale = XATTN_HEAD_DIM ** -0.5
    cos_qx = jnp.asarray((cosx * x_scale)[None], F32)
    sin_qx = jnp.asarray((sinx * x_scale)[None], F32)
    cos_m, sin_m = _rope_tables(np.arange(m_len), XATTN_HEAD_DIM // 2)
    cosm = jnp.asarray(np.tile(cos_m, (b, 1)), F32)
    sinm = jnp.asarray(np.tile(sin_m, (b, 1)), F32)

    tm, tn = 1024, 512

    def tile_of(off):
        assert off % tn == 0
        return off // tn

    attn_tiles, x_tiles = ATTN_WIDTH // tn, XATTN_WIDTH // tn
    tiles_per_seq = s // tm

    def view(t):
        return t.reshape(b, s, t.shape[-1])

    def rope_tables(cos, sin, kind):
        def index(col_tile, row_tile):
            return kind(col_tile), row_tile % tiles_per_seq, 0
        return [(cos, (1, tm, LANES), index), (sin, (1, tm, LANES), index)]

    def lane_slab(off):
        assert off % LANES == 0
        return lambda j: off // LANES + j

    for layer in range(depth):
        x2d = x.reshape(b * s, d)
        h = _rmsnorm(x2d, pre_norm[layer], tm=512)
        w = w_in[layer]
        common = dict(tm=tm, tn=tn)

        qkv = view(_in_proj(h, w, [lambda j: tile_of(OFF_QA) + j], 3 * attn_tiles,
                            _elementwise_finish(_rope_half_lane),
                            rope_tables(cos_qkv, sin_qkv, lambda j: j // attn_tiles),
                            name="in_proj_qkv", **common))
        y_conv, w_out_bf = _in_proj(
            h, w, [lane_slab(OFF_UC), lane_slab(OFF_BC), lane_slab(OFF_CC), lane_slab(OFF_ZC)],
            CONV_WIDTH // LANES,
            functools.partial(_gated_conv_finish, tiles_per_seq=tiles_per_seq),
            [(conv_w[layer], (CONV_K, LANES), lambda col_tile, row_tile: (0, col_tile))],
            out_tn=LANES, scratch=[pltpu.VMEM((CONV_HALO + tm, LANES), F32)],
            rider=(lambda block: block, w_out[layer], []), name="in_proj_conv", **common)
        gates = view(_in_proj(
            h, w, [lambda j: jnp.where(j < x_tiles, tile_of(OFF_ZX) + j,
                                       tile_of(OFF_ZA) - x_tiles + j)],
            x_tiles + attn_tiles, _elementwise_finish(_silu), [], name="in_proj_gates",
            **common))
        qx, hm = _in_proj(h, w, [lambda j: tile_of(OFF_QX) + j], x_tiles,
                          _elementwise_finish(_rope_two_tiles),
                          rope_tables(cos_qx, sin_qx, lambda j: 0), name="in_proj_qx",
                          rider=(_rms_normalize, mem.reshape(b * m_len, d),
                                 [mem_norm[layer].reshape(1, d)]), **common)
        qx = view(qx)
        mkv = _mem_proj(hm, w_mem_kv[layer], cosm, sinm, tn=512)

        y_groups = _dilated_attn((qkv, 0), (qkv, ATTN_WIDTH), (qkv, 2 * ATTN_WIDTH),
                                 (gates, XATTN_WIDTH))
        y_groups = [y.reshape(b * s, GROUP_WIDTH) for y in y_groups]
        y_x = _xattn((qx, 0), (gates, 0), mkv.reshape(b, m_len, 2 * XATTN_WIDTH), tm=512)
        y_x = y_x.reshape(b * s, XATTN_WIDTH)

        out = _out_proj([*y_groups, y_conv, y_x], w_out_bf, x2d,
                        post_norm[layer], tm=1024, tn=512)
        x = out.reshape(b, s, d)
    return x
```

```python
import functools

import jax
import jax.numpy as jnp
import numpy as np
from jax import lax
from jax.experimental import pallas as pl
from jax.experimental.pallas import tpu as pltpu

D_MODEL = 4096
MEM_LEN = 256
HEAD_DIM = 128
DIL_GROUPS = ((128, 1), (512, 4), (2048, 16))
ATTN_WIDTH = 3 * D_MODEL // 8
CONV_WIDTH = 3 * D_MODEL // 8
XATTN_WIDTH = D_MODEL // 4
N_XATTN_HEADS = 4
XATTN_HEAD_DIM = XATTN_WIDTH // N_XATTN_HEADS
HEADS_PER_DIL = 4
GROUP_WIDTH = HEADS_PER_DIL * HEAD_DIM
CONV_K = 3
IN_COLS = 4 * ATTN_WIDTH + 4 * CONV_WIDTH + 2 * XATTN_WIDTH
BLOCK = 128
ROPE_THETA = 10000.0
EPS = 1e-6
NEG_INF = -1e30

OFF_QA = 0
OFF_KA = OFF_QA + ATTN_WIDTH
OFF_VA = OFF_KA + ATTN_WIDTH
OFF_ZA = OFF_VA + ATTN_WIDTH
OFF_UC = OFF_ZA + ATTN_WIDTH
OFF_BC = OFF_UC + CONV_WIDTH
OFF_CC = OFF_BC + CONV_WIDTH
OFF_ZC = OFF_CC + CONV_WIDTH
OFF_QX = OFF_ZC + CONV_WIDTH
OFF_ZX = OFF_QX + XATTN_WIDTH

V7X_VMEM_LIMIT_BYTES = 56 * 1024 * 1024
LANES = 128
W_PIECES = 4
EPILOGUE_ROWS = 128
CONV_HALO = 8
NORM_ROWS = 64

BF16 = jnp.bfloat16
F32 = jnp.float32


def _params(semantics):
    return pltpu.CompilerParams(dimension_semantics=semantics,
                                vmem_limit_bytes=V7X_VMEM_LIMIT_BYTES)


def _rms_normalize(x, gain):
    ms = jnp.mean(x * x, axis=-1, keepdims=True)
    return x * lax.rsqrt(ms + EPS) * gain


def _cast_kernel(x_ref, o_ref):
    o_ref[...] = x_ref[...].astype(o_ref.dtype)


def _cast_columns_bf16(w, col_off, width, tr):
    rows = w.shape[0]
    assert col_off % width == 0
    return pl.pallas_call(
        _cast_kernel,
        grid=(rows // tr,),
        in_specs=[pl.BlockSpec((tr, width), lambda i: (i, col_off // width))],
        out_specs=pl.BlockSpec((tr, width), lambda i: (i, 0)),
        out_shape=jax.ShapeDtypeStruct((rows, width), BF16),
        compiler_params=_params(("parallel",)),
        name="cast_w_qx",
    )(w)


def _norm_project_kernel(x_ref, g_ref, w_ref, cos_ref, sin_ref, h_ref, o_ref, acc_ref):
    tm = x_ref.shape[0]
    for r0 in range(0, tm, NORM_ROWS):
        rows = slice(r0, r0 + NORM_ROWS)
        h_ref[rows, :] = _rms_normalize(x_ref[rows, :], g_ref[...]).astype(h_ref.dtype)
    acc_ref[...] = jnp.dot(h_ref[...], w_ref[...], preferred_element_type=F32)
    for r0 in range(0, tm, EPILOGUE_ROWS):
        rows = slice(r0, r0 + EPILOGUE_ROWS)
        o_ref[rows, :] = _rope_two_tiles(acc_ref[rows, :], cos_ref[0, rows, :],
                                         sin_ref[0, rows, :]).astype(o_ref.dtype)


def _norm_project(x2d, gain, w_bf, cos, sin, seq, tm):
    rows, d = x2d.shape
    width = w_bf.shape[1]
    tiles_per_seq = seq // tm
    table = pl.BlockSpec((1, tm, LANES), lambda i: (0, i % tiles_per_seq, 0))
    return pl.pallas_call(
        _norm_project_kernel,
        grid=(rows // tm,),
        in_specs=[pl.BlockSpec((tm, d), lambda i: (i, 0)),
                  pl.BlockSpec((1, d), lambda i: (0, 0)),
                  pl.BlockSpec((d, width), lambda i: (0, 0)),
                  table, table],
        out_specs=[pl.BlockSpec((tm, d), lambda i: (i, 0)),
                   pl.BlockSpec((tm, width), lambda i: (i, 0))],
        out_shape=[jax.ShapeDtypeStruct((rows, d), BF16),
                   jax.ShapeDtypeStruct((rows, width), BF16)],
        scratch_shapes=[pltpu.VMEM((tm, width), F32)],
        compiler_params=_params(("arbitrary",)),
        name="norm_project_qx",
    )(x2d, gain.reshape(1, d), w_bf, cos, sin)


def _silu(z):
    half = 0.5 * z
    return half + half * jnp.tanh(half)


def _rope_half_lane(acc, cos, sin_signed):
    outs = []
    for h in range(acc.shape[1] // HEAD_DIM):
        xh = acc[:, h * HEAD_DIM:(h + 1) * HEAD_DIM]
        outs.append(xh * cos + pltpu.roll(xh, HEAD_DIM // 2, axis=1) * sin_signed)
    return jnp.concatenate(outs, axis=1)


def _rope_two_tiles(acc, cos, sin):
    outs = []
    for h in range(acc.shape[1] // XATTN_HEAD_DIM):
        x1 = acc[:, h * XATTN_HEAD_DIM:h * XATTN_HEAD_DIM + LANES]
        x2 = acc[:, h * XATTN_HEAD_DIM + LANES:(h + 1) * XATTN_HEAD_DIM]
        outs.append(x1 * cos - x2 * sin)
        outs.append(x2 * cos + x1 * sin)
    return jnp.concatenate(outs, axis=1)


def _elementwise_finish(epilogue):
    def finish(acc_ref, slot, o_ref, aux_refs, scratch_refs, row_tile):
        for r0 in range(0, o_ref.shape[0], EPILOGUE_ROWS):
            rows = slice(r0, r0 + EPILOGUE_ROWS)
            o_ref[rows, :] = epilogue(acc_ref[slot, rows, :],
                                      *[r[0, rows, :] for r in aux_refs]).astype(o_ref.dtype)
    return finish


def _gated_conv_finish(acc_ref, slot, o_ref, aux_refs, scratch_refs, row_tile, *, tiles_per_seq):
    (w_ref,), (cu_ref,) = aux_refs, scratch_refs
    tm = o_ref.shape[0]
    u_cols, b_cols, c_cols, z_cols = (slice(k * LANES, (k + 1) * LANES) for k in range(4))
    carried = cu_ref[tm:tm + CONV_HALO, :]
    cu_ref[0:CONV_HALO, :] = jnp.where(row_tile % tiles_per_seq == 0, 0.0, carried)
    for r0 in range(0, tm, EPILOGUE_ROWS):
        rows = slice(r0, r0 + EPILOGUE_ROWS)
        cu_ref[CONV_HALO + r0:CONV_HALO + r0 + EPILOGUE_ROWS, :] = (
            acc_ref[slot, rows, c_cols] * acc_ref[slot, rows, u_cols])
    w = w_ref[...]
    for r0 in range(0, tm, EPILOGUE_ROWS):
        rows = slice(r0, r0 + EPILOGUE_ROWS)
        conv = None
        for lag in range(CONV_K):
            first = CONV_HALO + r0 - lag
            tap = cu_ref[first:first + EPILOGUE_ROWS, :] * w[CONV_K - 1 - lag:CONV_K - lag, :]
            conv = tap if conv is None else conv + tap
        o_ref[rows, :] = (acc_ref[slot, rows, b_cols] * conv
                          * _silu(acc_ref[slot, rows, z_cols])).astype(o_ref.dtype)


def _in_proj_kernel(*refs, ni, n_steps, n_w, n_aux, n_scratch, finish_fn, rider_fn,
                    rider_steps):
    h_ref = refs[0]
    w_refs = refs[1:1 + n_w]
    aux_refs = refs[1 + n_w:1 + n_w + n_aux]
    scratch_refs = refs[len(refs) - n_scratch:] if n_scratch else ()
    rest = refs[1 + n_w + n_aux:len(refs) - n_scratch]
    if rider_fn is not None:
        rider_in_ref, *rider_row_refs = rest[:-4]
        o_ref, rider_out_ref, wbf_ref, acc_ref = rest[-4:]
    else:
        o_ref, wbf_ref, acc_ref = rest
    t = pl.program_id(0)
    piece_rows = wbf_ref.shape[0] // W_PIECES
    slab_cols = wbf_ref.shape[1] // (n_w // W_PIECES)

    @pl.when((t % ni == 0) & (t < n_steps))
    def _():
        for k, w_ref in enumerate(w_refs):
            slab, p = divmod(k, W_PIECES)
            wbf_ref[p * piece_rows:(p + 1) * piece_rows,
                    slab * slab_cols:(slab + 1) * slab_cols] = w_ref[...].astype(BF16)

    if rider_fn is not None:
        @pl.when(t < rider_steps)
        def _():
            rider_out_ref[...] = rider_fn(
                rider_in_ref[...], *[r[...] for r in rider_row_refs]).astype(rider_out_ref.dtype)

    def multiply(slot):
        acc_ref[slot] = jnp.dot(h_ref[...], wbf_ref[...], preferred_element_type=F32)

    def finish(slot):
        finish_fn(acc_ref, slot, o_ref, aux_refs, scratch_refs, ((t - 1) % ni))

    @pl.when(t == 0)
    def _():
        for ref in scratch_refs:
            ref[...] = jnp.zeros_like(ref)
        multiply(0)

    for parity in range(2):
        @pl.when((t > 0) & (t < n_steps) & (t % 2 == parity))
        def _(parity=parity):
            finish(1 - parity)
            multiply(parity)

    @pl.when(t == n_steps)
    def _():
        finish((n_steps - 1) % 2)


def _in_proj(h, w, col_slabs, n_col_tiles, finish_fn, aux, tm, tn, name, out_tn=None,
             scratch=(), rider=None):
    rows, d = h.shape
    ni = rows // tm
    n_steps = n_col_tiles * ni
    out_tn = tn if out_tn is None else out_tn
    slab_cols = tn // len(col_slabs)
    rider_in_specs, rider_out_specs, rider_out_shapes, rider_operands = [], [], [], []
    rider_fn, rider_steps = None, 0
    if rider is not None:
        rider_fn, rider_matrix, rider_rows = rider
        r_rows, r_cols = rider_matrix.shape
        assert all(r.shape == (1, r_cols) for r in rider_rows)
        rider_steps = max(k for k in range(1, n_steps + 1) if r_rows % k == 0)
        rider_spec = pl.BlockSpec((r_rows // rider_steps, r_cols),
                                  lambda t: (jnp.minimum(t, rider_steps - 1), 0))
        rider_in_specs = [rider_spec] + [pl.BlockSpec((1, r_cols), lambda t: (0, 0))
                                         for _ in rider_rows]
        rider_out_specs = [rider_spec]
        rider_out_shapes = [jax.ShapeDtypeStruct(rider_matrix.shape, BF16)]
        rider_operands = [rider_matrix, *rider_rows]

    def cur(t):
        c = jnp.minimum(t, n_steps - 1)
        return c // ni, c % ni

    def prev(t):
        p = jnp.maximum(t - 1, 0)
        return p // ni, p % ni

    def aux_spec(block, index):
        return pl.BlockSpec(block, lambda t: index(*prev(t)))

    def w_piece_spec(col_slab, p):
        def index(t):
            j, i = cur(t)
            ahead = (i >= ni - W_PIECES + p).astype(jnp.int32)
            return p, col_slab(jnp.minimum(j + ahead, n_col_tiles - 1))
        return pl.BlockSpec((d // W_PIECES, slab_cols), index)

    assert ni >= W_PIECES and d % W_PIECES == 0
    w_specs = [w_piece_spec(col_slab, p) for col_slab in col_slabs for p in range(W_PIECES)]
    results = pl.pallas_call(
        functools.partial(_in_proj_kernel, ni=ni, n_steps=n_steps, n_w=len(w_specs),
                          n_aux=len(aux), n_scratch=len(scratch), finish_fn=finish_fn,
                          rider_fn=rider_fn, rider_steps=rider_steps),
        grid=(n_steps + 1,),
        in_specs=[pl.BlockSpec((tm, d), lambda t: (cur(t)[1], 0))] + w_specs
                 + [aux_spec(block, index) for _, block, index in aux] + rider_in_specs,
        out_specs=[pl.BlockSpec((tm, out_tn), lambda t: (prev(t)[1], prev(t)[0]))]
                  + rider_out_specs,
        out_shape=[jax.ShapeDtypeStruct((rows, n_col_tiles * out_tn), BF16)] + rider_out_shapes,
        scratch_shapes=[pltpu.VMEM((d, tn), BF16), pltpu.VMEM((2, tm, tn), F32), *scratch],
        compiler_params=_params(("arbitrary",)),
        name=name,
    )(h, *([w] * len(w_specs)), *[arr for arr, _, _ in aux], *rider_operands)
    return results if rider is not None else results[0]


def _mem_proj_kernel(h_ref, w_ref, cos_ref, sin_ref, o_ref, *, tn):
    j = pl.program_id(0)
    acc = jnp.dot(h_ref[...], w_ref[...].astype(BF16), preferred_element_type=F32)

    @pl.when(j < XATTN_WIDTH // tn)
    def _():
        o_ref[...] = _rope_two_tiles(acc, cos_ref[...], sin_ref[...]).astype(o_ref.dtype)

    @pl.when(j >= XATTN_WIDTH // tn)
    def _():
        o_ref[...] = acc.astype(o_ref.dtype)


def _mem_proj(hm, w, cos, sin, tn):
    rows, d = hm.shape
    ncols = w.shape[1]
    tab = pl.BlockSpec((rows, LANES), lambda j: (0, 0))
    return pl.pallas_call(
        functools.partial(_mem_proj_kernel, tn=tn),
        grid=(ncols // tn,),
        in_specs=[pl.BlockSpec((rows, d), lambda j: (0, 0)),
                  pl.BlockSpec((d, tn), lambda j: (0, j)),
                  tab, tab],
        out_specs=pl.BlockSpec((rows, tn), lambda j: (0, j)),
        out_shape=jax.ShapeDtypeStruct((rows, ncols), BF16),
        compiler_params=_params(("parallel",)),
        name="mem_proj",
    )(hm, w, cos, sin)


MAX_SINGLE_ACCESS_STRIDE = 4


def _split_stride(dil):
    if dil <= MAX_SINGLE_ACCESS_STRIDE:
        return dil, 1
    assert dil % MAX_SINGLE_ACCESS_STRIDE == 0
    f2 = dil // MAX_SINGLE_ACCESS_STRIDE
    assert f2 <= MAX_SINGLE_ACCESS_STRIDE
    return MAX_SINGLE_ACCESS_STRIDE, f2


def _dilated_attn_kernel(q0, q1, q2, k0, k1, k2, v0, v1, v2, z0, z1, z2, y0, y1, y2,
                         stage_ref, mid_ref, sub_ref, osub_ref, lsub_ref, onat_ref, lnat_ref,
                         *, seq):
    q_refs, k_refs, v_refs = (q0, q1, q2), (k0, k1, k2), (v0, v1, v2)
    z_refs, y_refs = (z0, z1, z2), (y0, y1, y2)
    nblk = seq // BLOCK
    qi = lax.broadcasted_iota(jnp.int32, (BLOCK, 2 * BLOCK), 0)
    kk = lax.broadcasted_iota(jnp.int32, (BLOCK, 2 * BLOCK), 1)
    dist = qi + BLOCK - kk
    band = (dist >= 0) & (dist <= BLOCK)
    causal = (lax.broadcasted_iota(jnp.int32, (BLOCK, BLOCK), 1)
              <= lax.broadcasted_iota(jnp.int32, (BLOCK, BLOCK), 0))

    rows_of = []
    for g, (window, dil) in enumerate(DIL_GROUPS):
        sub_len = seq // dil
        if dil == 1:
            def group_rows(which, start, size, g=g):
                return (q_refs, k_refs, v_refs)[which][g][0, pl.ds(start, size), :]
        else:
            slab = 3 * (g - 1)
            f1, f2 = _split_stride(dil)
            len1 = seq // f1
            for which, refs in enumerate((q_refs, k_refs, v_refs)):
                stage_ref[slab + which] = refs[g][0].astype(F32)
                if f2 == 1:
                    for r in range(dil):
                        sub_ref[slab + which, r * sub_len:(r + 1) * sub_len, :] = (
                            stage_ref[slab + which, pl.ds(r, sub_len, stride=dil), :].astype(BF16))
                    continue
                for r1 in range(f1):
                    mid_ref[which, r1 * len1:(r1 + 1) * len1, :] = (
                        stage_ref[slab + which, pl.ds(r1, len1, stride=f1), :])
                for r1 in range(f1):
                    for r2 in range(f2):
                        r = r2 * f1 + r1
                        sub_ref[slab + which, r * sub_len:(r + 1) * sub_len, :] = (
                            mid_ref[which, pl.ds(r1 * len1 + r2, sub_len, stride=f2), :]
                            .astype(BF16))

            def group_rows(which, start, size, slab=slab):
                return sub_ref[slab + which, pl.ds(start, size), :]
        rows_of.append(group_rows)

    def attend(g, start, first):
        q = rows_of[g](0, start, BLOCK)
        if first:
            k = rows_of[g](1, start, BLOCK)
            v = rows_of[g](2, start, BLOCK)
            mask = causal
        else:
            k = rows_of[g](1, start - BLOCK, 2 * BLOCK)
            v = rows_of[g](2, start - BLOCK, 2 * BLOCK)
            mask = band
        s = lax.dot_general(q, k, (((1,), (1,)), ((), ())), preferred_element_type=F32)
        s = jnp.where(mask, s, NEG_INF)
        m = jnp.max(s, axis=-1, keepdims=True)
        p = jnp.exp(s - m)
        l = jnp.sum(p, axis=-1, keepdims=True)
        o = jnp.dot(p.astype(BF16), v, preferred_element_type=F32) * (1.0 / l)
        lse = jnp.broadcast_to(m + jnp.log(l), (BLOCK, HEAD_DIM))
        if DIL_GROUPS[g][1] == 1:
            onat_ref[g, pl.ds(start, BLOCK), :] = o
            lnat_ref[g, pl.ds(start, BLOCK), :] = lse
        else:
            osub_ref[g - 1, pl.ds(start, BLOCK), :] = o
            lsub_ref[g - 1, pl.ds(start, BLOCK), :] = lse

    for c in range(nblk):
        for g, (window, dil) in enumerate(DIL_GROUPS):
            attend(g, c * BLOCK, c % (seq // dil // BLOCK) == 0)

    for g, (window, dil) in enumerate(DIL_GROUPS):
        sub_len = seq // dil
        if dil == 1:
            continue
        f1, f2 = _split_stride(dil)
        len1 = seq // f1
        for idx, (nat_ref, subseq_ref) in enumerate(((onat_ref, osub_ref), (lnat_ref, lsub_ref))):
            if f2 == 1:
                for r in range(dil):
                    src = slice(r * sub_len, (r + 1) * sub_len)
                    nat_ref[g, pl.ds(r, sub_len, stride=dil), :] = subseq_ref[g - 1, src, :]
                continue
            mid = 3 + idx
            for r1 in range(f1):
                for r2 in range(f2):
                    r = r2 * f1 + r1
                    mid_ref[mid, pl.ds(r1 * len1 + r2, sub_len, stride=f2), :] = (
                        subseq_ref[g - 1, r * sub_len:(r + 1) * sub_len, :])
            for r1 in range(f1):
                nat_ref[g, pl.ds(r1, len1, stride=f1), :] = mid_ref[mid, r1 * len1:(r1 + 1) * len1, :]

    chunk = 2 * BLOCK
    for t in range(seq // chunk):
        rows = slice(t * chunk, (t + 1) * chunk)
        lse = [lnat_ref[g, rows, :] for g in range(len(DIL_GROUPS))]
        mx = jnp.maximum(jnp.maximum(lse[0], lse[1]), lse[2])
        e = [jnp.exp(x - mx) for x in lse]
        inv = 1.0 / (e[0] + e[1] + e[2])
        for g in range(len(DIL_GROUPS)):
            y_refs[g][0, rows, :] = (onat_ref[g, rows, :] * (e[g] * inv)
                                     * z_refs[g][0, rows, :].astype(F32)).astype(y_refs[g].dtype)


def _dilated_attn(q_src, k_src, v_src, z_src):
    sources = (q_src, k_src, v_src, z_src)
    b, s, _ = q_src[0].shape
    n_groups = len(DIL_GROUPS)
    n_regrouped = n_groups - 1
    assert sum(_split_stride(dil)[1] > 1 for _, dil in DIL_GROUPS) <= 1

    def head_spec(off, g):
        first = (off + g * GROUP_WIDTH) // HEAD_DIM
        return pl.BlockSpec((1, s, HEAD_DIM), lambda bi, hg: (bi, 0, first + hg))

    in_specs = [head_spec(off, g) for _, off in sources for g in range(n_groups)]
    operands = [arr for arr, _ in sources for g in range(n_groups)]
    out_spec = pl.BlockSpec((1, s, HEAD_DIM), lambda bi, hg: (bi, 0, hg))
    return pl.pallas_call(
        functools.partial(_dilated_attn_kernel, seq=s),
        grid=(b, HEADS_PER_DIL),
        in_specs=in_specs,
        out_specs=[out_spec] * n_groups,
        out_shape=[jax.ShapeDtypeStruct((b, s, GROUP_WIDTH), BF16)] * n_groups,
        scratch_shapes=[pltpu.VMEM((3 * n_regrouped, s, HEAD_DIM), F32),
                        pltpu.VMEM((5, s, HEAD_DIM), F32),
                        pltpu.VMEM((3 * n_regrouped, s, HEAD_DIM), BF16),
                        pltpu.VMEM((n_regrouped, s, HEAD_DIM), F32),
                        pltpu.VMEM((n_regrouped, s, HEAD_DIM), F32),
                        pltpu.VMEM((n_groups, s, HEAD_DIM), F32),
                        pltpu.VMEM((n_groups, s, HEAD_DIM), F32)],
        compiler_params=_params(("parallel", "parallel")),
        name="dilated_attn",
    )(*operands)


def _xattn_kernel(q_ref, z_ref, mk_ref, mv_ref, y_ref):
    for h in range(N_XATTN_HEADS):
        cols = slice(h * XATTN_HEAD_DIM, (h + 1) * XATTN_HEAD_DIM)
        s = lax.dot_general(q_ref[0, :, cols], mk_ref[0, :, cols],
                            (((1,), (1,)), ((), ())), preferred_element_type=F32)
        m = jnp.max(s, axis=-1, keepdims=True)
        p = jnp.exp(s - m)
        l = jnp.sum(p, axis=-1, keepdims=True)
        o = jnp.dot(p.astype(BF16), mv_ref[0, :, cols], preferred_element_type=F32)
        y_ref[0, :, cols] = (o * (1.0 / l) * z_ref[0, :, cols].astype(F32)).astype(y_ref.dtype)


def _xattn(q_src, z_src, mkv3, tm):
    b, s, _ = q_src[0].shape
    m_len = mkv3.shape[1]
    q_off, z_off = q_src[1], z_src[1]
    assert q_off % XATTN_WIDTH == 0 and z_off % XATTN_WIDTH == 0
    return pl.pallas_call(
        _xattn_kernel,
        grid=(b, s // tm),
        in_specs=[pl.BlockSpec((1, tm, XATTN_WIDTH), lambda bi, i: (bi, i, q_off // XATTN_WIDTH)),
                  pl.BlockSpec((1, tm, XATTN_WIDTH), lambda bi, i: (bi, i, z_off // XATTN_WIDTH)),
                  pl.BlockSpec((1, m_len, XATTN_WIDTH), lambda bi, i: (bi, 0, 0)),
                  pl.BlockSpec((1, m_len, XATTN_WIDTH), lambda bi, i: (bi, 0, 1))],
        out_specs=pl.BlockSpec((1, tm, XATTN_WIDTH), lambda bi, i: (bi, i, 0)),
        out_shape=jax.ShapeDtypeStruct((b, s, XATTN_WIDTH), BF16),
        compiler_params=_params(("parallel", "parallel")),
        name="mem_xattn",
    )(q_src[0], z_src[0], mkv3, mkv3)


def _out_proj_kernel(*refs, ni, d, n_mix):
    y_refs, w_refs = refs[:n_mix], refs[n_mix:2 * n_mix]
    x_ref, g_ref, o_ref, y2_ref, ssq_ref, scale_ref = refs[2 * n_mix:]
    i = pl.program_id(0)
    j = pl.program_id(1)

    @pl.when(j == 0)
    def _():
        @pl.when(i > 0)
        def _():
            scale_ref[...] = lax.rsqrt(ssq_ref[...] * (1.0 / d) + EPS)
        ssq_ref[...] = jnp.zeros_like(ssq_ref)

    def finish_previous_tile():
        o_ref[...] = x_ref[...] + y2_ref[j] * scale_ref[...] * g_ref[...]

    def multiply_this_tile():
        y2 = jnp.dot(y_refs[0][...], w_refs[0][...], preferred_element_type=F32)
        for y_ref, w_ref in zip(y_refs[1:], w_refs[1:]):
            y2 = y2 + jnp.dot(y_ref[...], w_ref[...], preferred_element_type=F32)
        y2_ref[j] = y2
        ssq_ref[...] += jnp.sum(y2 * y2, axis=-1, keepdims=True)

    @pl.when(i == 0)
    def _():
        multiply_this_tile()

    @pl.when((i > 0) & (i < ni))
    def _():
        finish_previous_tile()
        multiply_this_tile()

    @pl.when(i == ni)
    def _():
        finish_previous_tile()


def _out_proj(ys, w_bf, x2d, gain, tm, tn):
    rows, d = x2d.shape
    nj = d // tn
    ni = rows // tm

    def y_spec(width):
        return pl.BlockSpec((tm, width), lambda i, j: (jnp.minimum(i, ni - 1), 0))

    def w_spec(width, row_off):
        assert row_off % width == 0
        return pl.BlockSpec((width, tn),
                            lambda i, j: (row_off // width, jnp.where(i == ni, nj - 1, j)))

    io_spec = pl.BlockSpec((tm, tn),
                           lambda i, j: (jnp.maximum(i - 1, 0), jnp.where(i == 0, 0, j)))
    widths = [y.shape[1] for y in ys]
    offsets = [sum(widths[:k]) for k in range(len(widths))]
    return pl.pallas_call(
        functools.partial(_out_proj_kernel, ni=ni, d=d, n_mix=len(ys)),
        grid=(ni + 1, nj),
        in_specs=([y_spec(w) for w in widths]
                  + [w_spec(w, off) for w, off in zip(widths, offsets)]
                  + [io_spec, pl.BlockSpec((1, tn), lambda i, j: (0, j))]),
        out_specs=io_spec,
        out_shape=jax.ShapeDtypeStruct((rows, d), F32),
        scratch_shapes=[pltpu.VMEM((nj, tm, tn), F32),
                        pltpu.VMEM((tm, 1), F32),
                        pltpu.VMEM((tm, 1), F32)],
        compiler_params=_params(("arbitrary", "arbitrary")),
        name="out_proj",
    )(*ys, *([w_bf] * len(ys)), x2d, gain.reshape(1, d))


def _rope_tables(pos, half):
    inv = 1.0 / (ROPE_THETA ** (np.arange(half, dtype=np.float64) / half))
    ang = np.asarray(pos, dtype=np.float64)[:, None] * inv[None, :]
    return np.cos(ang), np.sin(ang)


def kernel(x, mem, pre_norm, w_in, conv_w, mem_norm, w_mem_kv, w_out, post_norm):
    b, s, d = x.shape
    m_len = mem.shape[1]
    depth = w_in.shape[0]
    assert all(window // dil == BLOCK for window, dil in DIL_GROUPS)
    pos = np.arange(s)
    cos_a, sin_a = _rope_tables(pos, HEAD_DIM // 2)
    cosa = np.concatenate([cos_a, cos_a], axis=-1)
    sina = np.concatenate([-sin_a, sin_a], axis=-1)
    a_scale = HEAD_DIM ** -0.5
    cos_qkv = jnp.asarray(np.stack([cosa * a_scale, cosa, np.ones_like(cosa)]), F32)
    sin_qkv = jnp.asarray(np.stack([sina * a_scale, sina, np.zeros_like(sina)]), F32)
    cosx, sinx = _rope_tables(pos + m_len, XATTN_HEAD_DIM // 2)
    x_scale = XATTN_HEAD_DIM ** -0.5
    cos_qx = jnp.asarray((cosx * x_scale)[None], F32)
    sin_qx = jnp.asarray((sinx * x_scale)[None], F32)
    cos_m, sin_m = _rope_tables(np.arange(m_len), XATTN_HEAD_DIM // 2)
    cosm = jnp.asarray(np.tile(cos_m, (b, 1)), F32)
    sinm = jnp.asarray(np.tile(sin_m, (b, 1)), F32)

    tm, tn = 1024, 512

    def tile_of(off):
        assert off % tn == 0
        return off // tn

    attn_tiles, x_tiles = ATTN_WIDTH // tn, XATTN_WIDTH // tn
    tiles_per_seq = s // tm

    def view(t):
        return t.reshape(b, s, t.shape[-1])

    def rope_tables(cos, sin, kind):
        def index(col_tile, row_tile):
            return kind(col_tile), row_tile % tiles_per_seq, 0
        return [(cos, (1, tm, LANES), index), (sin, (1, tm, LANES), index)]

    def lane_slab(off):
        assert off % LANES == 0
        return lambda j: off // LANES + j

    for layer in range(depth):
        x2d = x.reshape(b * s, d)
        w = w_in[layer]
        common = dict(tm=tm, tn=tn)
        w_qx_bf = _cast_columns_bf16(w, OFF_QX, XATTN_WIDTH, tr=1024)
        h, qx = _norm_project(x2d, pre_norm[layer], w_qx_bf, cos_qx, sin_qx, seq=s, tm=512)
        qx = view(qx)

        qkv = view(_in_proj(h, w, [lambda j: tile_of(OFF_QA) + j], 3 * attn_tiles,
                            _elementwise_finish(_rope_half_lane),
                            rope_tables(cos_qkv, sin_qkv, lambda j: j // attn_tiles),
                            name="in_proj_qkv", **common))
        y_conv, w_out_bf = _in_proj(
            h, w, [lane_slab(OFF_UC), lane_slab(OFF_BC), lane_slab(OFF_CC), lane_slab(OFF_ZC)],
            CONV_WIDTH // LANES,
            functools.partial(_gated_conv_finish, tiles_per_seq=tiles_per_seq),
            [(conv_w[layer], (CONV_K, LANES), lambda col_tile, row_tile: (0, col_tile))],
            out_tn=LANES, scratch=[pltpu.VMEM((CONV_HALO + tm, LANES), F32)],
            rider=(lambda block: block, w_out[layer], []), name="in_proj_conv", **common)
        gates, hm = _in_proj(
            h, w, [lambda j: jnp.where(j < x_tiles, tile_of(OFF_ZX) + j,
                                       tile_of(OFF_ZA) - x_tiles + j)],
            x_tiles + attn_tiles, _elementwise_finish(_silu), [], name="in_proj_gates",
            rider=(_rms_normalize, mem.reshape(b * m_len, d), [mem_norm[layer].reshape(1, d)]),
            **common)
        gates = view(gates)
        mkv = _mem_proj(hm, w_mem_kv[layer], cosm, sinm, tn=512)

        y_groups = _dilated_attn((qkv, 0), (qkv, ATTN_WIDTH), (qkv, 2 * ATTN_WIDTH),
                                 (gates, XATTN_WIDTH))
        y_groups = [y.reshape(b * s, GROUP_WIDTH) for y in y_groups]
        y_x = _xattn((qx, 0), (gates, 0), mkv.reshape(b, m_len, 2 * XATTN_WIDTH), tm=512)
        y_x = y_x.reshape(b * s, XATTN_WIDTH)

        out = _out_proj([*y_groups, y_conv, y_x], w_out_bf, x2d,
                        post_norm[layer], tm=1024, tn=512)
        x = out.reshape(b, s, d)
    return x
```

```python
import functools

import jax
import jax.numpy as jnp
import numpy as np
from jax import lax
from jax.experimental import pallas as pl
from jax.experimental.pallas import tpu as pltpu

D_MODEL = 4096
MEM_LEN = 256
HEAD_DIM = 128
DIL_GROUPS = ((128, 1), (512, 4), (2048, 16))
ATTN_WIDTH = 3 * D_MODEL // 8
CONV_WIDTH = 3 * D_MODEL // 8
XATTN_WIDTH = D_MODEL // 4
N_XATTN_HEADS = 4
XATTN_HEAD_DIM = XATTN_WIDTH // N_XATTN_HEADS
HEADS_PER_DIL = 4
GROUP_WIDTH = HEADS_PER_DIL * HEAD_DIM
CONV_K = 3
IN_COLS = 4 * ATTN_WIDTH + 4 * CONV_WIDTH + 2 * XATTN_WIDTH
BLOCK = 128
ROPE_THETA = 10000.0
EPS = 1e-6
NEG_INF = -1e30

OFF_QA = 0
OFF_KA = OFF_QA + ATTN_WIDTH
OFF_VA = OFF_KA + ATTN_WIDTH
OFF_ZA = OFF_VA + ATTN_WIDTH
OFF_UC = OFF_ZA + ATTN_WIDTH
OFF_BC = OFF_UC + CONV_WIDTH
OFF_CC = OFF_BC + CONV_WIDTH
OFF_ZC = OFF_CC + CONV_WIDTH
OFF_QX = OFF_ZC + CONV_WIDTH
OFF_ZX = OFF_QX + XATTN_WIDTH

V7X_VMEM_LIMIT_BYTES = 56 * 1024 * 1024
LANES = 128
W_PIECES = 4
EPILOGUE_ROWS = 128
CONV_HALO = 8
NORM_ROWS = 64

BF16 = jnp.bfloat16
F32 = jnp.float32


def _params(semantics):
    return pltpu.CompilerParams(dimension_semantics=semantics,
                                vmem_limit_bytes=V7X_VMEM_LIMIT_BYTES)


def _rms_normalize(x, gain):
    ms = jnp.mean(x * x, axis=-1, keepdims=True)
    return x * lax.rsqrt(ms + EPS) * gain


def _cast_kernel(x_ref, o_ref):
    o_ref[...] = x_ref[...].astype(o_ref.dtype)


def _cast_columns_bf16(w, col_off, width, tr):
    rows = w.shape[0]
    assert col_off % width == 0
    return pl.pallas_call(
        _cast_kernel,
        grid=(rows // tr,),
        in_specs=[pl.BlockSpec((tr, width), lambda i: (i, col_off // width))],
        out_specs=pl.BlockSpec((tr, width), lambda i: (i, 0)),
        out_shape=jax.ShapeDtypeStruct((rows, width), BF16),
        compiler_params=_params(("parallel",)),
        name="cast_w_qx",
    )(w)


def _norm_project_kernel(x_ref, g_ref, w_ref, cos_ref, sin_ref, h_ref, o_ref, acc_ref):
    tm = x_ref.shape[0]
    for r0 in range(0, tm, NORM_ROWS):
        rows = slice(r0, r0 + NORM_ROWS)
        h_ref[rows, :] = _rms_normalize(x_ref[rows, :], g_ref[...]).astype(h_ref.dtype)
    acc_ref[...] = jnp.dot(h_ref[...], w_ref[...], preferred_element_type=F32)
    for r0 in range(0, tm, EPILOGUE_ROWS):
        rows = slice(r0, r0 + EPILOGUE_ROWS)
        o_ref[rows, :] = _rope_two_tiles(acc_ref[rows, :], cos_ref[0, rows, :],
                                         sin_ref[0, rows, :]).astype(o_ref.dtype)


def _norm_project(x2d, gain, w_bf, cos, sin, seq, tm):
    rows, d = x2d.shape
    width = w_bf.shape[1]
    tiles_per_seq = seq // tm
    table = pl.BlockSpec((1, tm, LANES), lambda i: (0, i % tiles_per_seq, 0))
    return pl.pallas_call(
        _norm_project_kernel,
        grid=(rows // tm,),
        in_specs=[pl.BlockSpec((tm, d), lambda i: (i, 0)),
                  pl.BlockSpec((1, d), lambda i: (0, 0)),
                  pl.BlockSpec((d, width), lambda i: (0, 0)),
                  table, table],
        out_specs=[pl.BlockSpec((tm, d), lambda i: (i, 0)),
                   pl.BlockSpec((tm, width), lambda i: (i, 0))],
        out_shape=[jax.ShapeDtypeStruct((rows, d), BF16),
                   jax.ShapeDtypeStruct((rows, width), BF16)],
        scratch_shapes=[pltpu.VMEM((tm, width), F32)],
        compiler_params=_params(("arbitrary",)),
        name="norm_project_qx",
    )(x2d, gain.reshape(1, d), w_bf, cos, sin)


def _silu(z):
    half = 0.5 * z
    return half + half * jnp.tanh(half)


def _rope_half_lane(acc, cos, sin_signed):
    outs = []
    for h in range(acc.shape[1] // HEAD_DIM):
        xh = acc[:, h * HEAD_DIM:(h + 1) * HEAD_DIM]
        outs.append(xh * cos + pltpu.roll(xh, HEAD_DIM // 2, axis=1) * sin_signed)
    return jnp.concatenate(outs, axis=1)


def _rope_two_tiles(acc, cos, sin):
    outs = []
    for h in range(acc.shape[1] // XATTN_HEAD_DIM):
        x1 = acc[:, h * XATTN_HEAD_DIM:h * XATTN_HEAD_DIM + LANES]
        x2 = acc[:, h * XATTN_HEAD_DIM + LANES:(h + 1) * XATTN_HEAD_DIM]
        outs.append(x1 * cos - x2 * sin)
        outs.append(x2 * cos + x1 * sin)
    return jnp.concatenate(outs, axis=1)


def _elementwise_finish(epilogue):
    def finish(acc_ref, slot, o_ref, aux_refs, scratch_refs, row_tile):
        for r0 in range(0, o_ref.shape[0], EPILOGUE_ROWS):
            rows = slice(r0, r0 + EPILOGUE_ROWS)
            o_ref[rows, :] = epilogue(acc_ref[slot, rows, :],
                                      *[r[0, rows, :] for r in aux_refs]).astype(o_ref.dtype)
    return finish


def _gated_conv_finish(acc_ref, slot, o_ref, aux_refs, scratch_refs, row_tile, *, tiles_per_seq):
    (w_ref,), (carry_ref,) = aux_refs, scratch_refs
    tm = o_ref.shape[0]
    u_cols, b_cols, c_cols, z_cols = (slice(k * LANES, (k + 1) * LANES) for k in range(4))
    w = w_ref[...]
    tail = jnp.where(row_tile % tiles_per_seq == 0, 0.0, carry_ref[...])
    for r0 in range(0, tm, EPILOGUE_ROWS):
        rows = slice(r0, r0 + EPILOGUE_ROWS)
        cu = acc_ref[slot, rows, c_cols] * acc_ref[slot, rows, u_cols]
        window = jnp.concatenate([tail, cu], axis=0)
        conv = cu * w[CONV_K - 1:CONV_K, :]
        for lag in range(1, CONV_K):
            shifted = pltpu.roll(window, lag, axis=0)[CONV_HALO:, :]
            conv = conv + shifted * w[CONV_K - 1 - lag:CONV_K - lag, :]
        o_ref[rows, :] = (acc_ref[slot, rows, b_cols] * conv
                          * _silu(acc_ref[slot, rows, z_cols])).astype(o_ref.dtype)
        tail = cu[EPILOGUE_ROWS - CONV_HALO:, :]
    carry_ref[...] = tail


def _in_proj_kernel(*refs, ni, n_steps, n_w, n_aux, n_scratch, finish_fn, rider_fn,
                    rider_steps):
    h_ref = refs[0]
    w_refs = refs[1:1 + n_w]
    aux_refs = refs[1 + n_w:1 + n_w + n_aux]
    scratch_refs = refs[len(refs) - n_scratch:] if n_scratch else ()
    rest = refs[1 + n_w + n_aux:len(refs) - n_scratch]
    if rider_fn is not None:
        rider_in_ref, *rider_row_refs = rest[:-4]
        o_ref, rider_out_ref, wbf_ref, acc_ref = rest[-4:]
    else:
        o_ref, wbf_ref, acc_ref = rest
    t = pl.program_id(0)
    piece_rows = wbf_ref.shape[0] // W_PIECES
    slab_cols = wbf_ref.shape[1] // (n_w // W_PIECES)

    @pl.when((t % ni == 0) & (t < n_steps))
    def _():
        for k, w_ref in enumerate(w_refs):
            slab, p = divmod(k, W_PIECES)
            wbf_ref[p * piece_rows:(p + 1) * piece_rows,
                    slab * slab_cols:(slab + 1) * slab_cols] = w_ref[...].astype(BF16)

    if rider_fn is not None:
        @pl.when(t < rider_steps)
        def _():
            rider_out_ref[...] = rider_fn(
                rider_in_ref[...], *[r[...] for r in rider_row_refs]).astype(rider_out_ref.dtype)

    def multiply(slot):
        acc_ref[slot] = jnp.dot(h_ref[...], wbf_ref[...], preferred_element_type=F32)

    def finish(slot):
        finish_fn(acc_ref, slot, o_ref, aux_refs, scratch_refs, ((t - 1) % ni))

    @pl.when(t == 0)
    def _():
        for ref in scratch_refs:
            ref[...] = jnp.zeros_like(ref)
        multiply(0)

    for parity in range(2):
        @pl.when((t > 0) & (t < n_steps) & (t % 2 == parity))
        def _(parity=parity):
            finish(1 - parity)
            multiply(parity)

    @pl.when(t == n_steps)
    def _():
        finish((n_steps - 1) % 2)


def _in_proj(h, w, col_slabs, n_col_tiles, finish_fn, aux, tm, tn, name, out_tn=None,
             scratch=(), rider=None):
    rows, d = h.shape
    ni = rows // tm
    n_steps = n_col_tiles * ni
    out_tn = tn if out_tn is None else out_tn
    slab_cols = tn // len(col_slabs)
    rider_in_specs, rider_out_specs, rider_out_shapes, rider_operands = [], [], [], []
    rider_fn, rider_steps = None, 0
    if rider is not None:
        rider_fn, rider_matrix, rider_rows = rider
        r_rows, r_cols = rider_matrix.shape
        assert all(r.shape == (1, r_cols) for r in rider_rows)
        rider_steps = max(k for k in range(1, n_steps + 1) if r_rows % k == 0)
        rider_spec = pl.BlockSpec((r_rows // rider_steps, r_cols),
                                  lambda t: (jnp.minimum(t, rider_steps - 1), 0))
        rider_in_specs = [rider_spec] + [pl.BlockSpec((1, r_cols), lambda t: (0, 0))
                                         for _ in rider_rows]
        rider_out_specs = [rider_spec]
        rider_out_shapes = [jax.ShapeDtypeStruct(rider_matrix.shape, BF16)]
        rider_operands = [rider_matrix, *rider_rows]

    def cur(t):
        c = jnp.minimum(t, n_steps - 1)
        return c // ni, c % ni

    def prev(t):
        p = jnp.maximum(t - 1, 0)
        return p // ni, p % ni

    def aux_spec(block, index):
        return pl.BlockSpec(block, lambda t: index(*prev(t)))

    def w_piece_spec(col_slab, p):
        def index(t):
            j, i = cur(t)
            ahead = (i >= ni - W_PIECES + p).astype(jnp.int32)
            return p, col_slab(jnp.minimum(j + ahead, n_col_tiles - 1))
        return pl.BlockSpec((d // W_PIECES, slab_cols), index)

    assert ni >= W_PIECES and d % W_PIECES == 0
    w_specs = [w_piece_spec(col_slab, p) for col_slab in col_slabs for p in range(W_PIECES)]
    results = pl.pallas_call(
        functools.partial(_in_proj_kernel, ni=ni, n_steps=n_steps, n_w=len(w_specs),
                          n_aux=len(aux), n_scratch=len(scratch), finish_fn=finish_fn,
                          rider_fn=rider_fn, rider_steps=rider_steps),
        grid=(n_steps + 1,),
        in_specs=[pl.BlockSpec((tm, d), lambda t: (cur(t)[1], 0))] + w_specs
                 + [aux_spec(block, index) for _, block, index in aux] + rider_in_specs,
        out_specs=[pl.BlockSpec((tm, out_tn), lambda t: (prev(t)[1], prev(t)[0]))]
                  + rider_out_specs,
        out_shape=[jax.ShapeDtypeStruct((rows, n_col_tiles * out_tn), BF16)] + rider_out_shapes,
        scratch_shapes=[pltpu.VMEM((d, tn), BF16), pltpu.VMEM((2, tm, tn), F32), *scratch],
        compiler_params=_params(("arbitrary",)),
        name=name,
    )(h, *([w] * len(w_specs)), *[arr for arr, _, _ in aux], *rider_operands)
    return results if rider is not None else results[0]


def _mem_proj_kernel(h_ref, w_ref, cos_ref, sin_ref, o_ref, *, tn):
    j = pl.program_id(0)
    acc = jnp.dot(h_ref[...], w_ref[...].astype(BF16), preferred_element_type=F32)

    @pl.when(j < XATTN_WIDTH // tn)
    def _():
        o_ref[...] = _rope_two_tiles(acc, cos_ref[...], sin_ref[...]).astype(o_ref.dtype)

    @pl.when(j >= XATTN_WIDTH // tn)
    def _():
        o_ref[...] = acc.astype(o_ref.dtype)


def _mem_proj(hm, w, cos, sin, tn):
    rows, d = hm.shape
    ncols = w.shape[1]
    tab = pl.BlockSpec((rows, LANES), lambda j: (0, 0))
    return pl.pallas_call(
        functools.partial(_mem_proj_kernel, tn=tn),
        grid=(ncols // tn,),
        in_specs=[pl.BlockSpec((rows, d), lambda j: (0, 0)),
                  pl.BlockSpec((d, tn), lambda j: (0, j)),
                  tab, tab],
        out_specs=pl.BlockSpec((rows, tn), lambda j: (0, j)),
        out_shape=jax.ShapeDtypeStruct((rows, ncols), BF16),
        compiler_params=_params(("parallel",)),
        name="mem_proj",
    )(hm, w, cos, sin)


MAX_SINGLE_ACCESS_STRIDE = 4


def _split_stride(dil):
    if dil <= MAX_SINGLE_ACCESS_STRIDE:
        return dil, 1
    assert dil % MAX_SINGLE_ACCESS_STRIDE == 0
    f2 = dil // MAX_SINGLE_ACCESS_STRIDE
    assert f2 <= MAX_SINGLE_ACCESS_STRIDE
    return MAX_SINGLE_ACCESS_STRIDE, f2


def _dilated_attn_kernel(q0, q1, q2, k0, k1, k2, v0, v1, v2, z0, z1, z2, y0, y1, y2,
                         stage_ref, mid_ref, sub_ref, osub_ref, lsub_ref, onat_ref, lnat_ref,
                         *, seq):
    q_refs, k_refs, v_refs = (q0, q1, q2), (k0, k1, k2), (v0, v1, v2)
    z_refs, y_refs = (z0, z1, z2), (y0, y1, y2)
    nblk = seq // BLOCK
    qi = lax.broadcasted_iota(jnp.int32, (BLOCK, 2 * BLOCK), 0)
    kk = lax.broadcasted_iota(jnp.int32, (BLOCK, 2 * BLOCK), 1)
    dist = qi + BLOCK - kk
    band = (dist >= 0) & (dist <= BLOCK)
    causal = (lax.broadcasted_iota(jnp.int32, (BLOCK, BLOCK), 1)
              <= lax.broadcasted_iota(jnp.int32, (BLOCK, BLOCK), 0))

    rows_of = []
    for g, (window, dil) in enumerate(DIL_GROUPS):
        sub_len = seq // dil
        if dil == 1:
            def group_rows(which, start, size, g=g):
                return (q_refs, k_refs, v_refs)[which][g][0, pl.ds(start, size), :]
        else:
            slab = 3 * (g - 1)
            f1, f2 = _split_stride(dil)
            len1 = seq // f1
            for which, refs in enumerate((q_refs, k_refs, v_refs)):
                stage_ref[slab + which] = refs[g][0].astype(F32)
                if f2 == 1:
                    for r in range(dil):
                        sub_ref[slab + which, r * sub_len:(r + 1) * sub_len, :] = (
                            stage_ref[slab + which, pl.ds(r, sub_len, stride=dil), :].astype(BF16))
                    continue
                for r1 in range(f1):
                    mid_ref[which, r1 * len1:(r1 + 1) * len1, :] = (
                        stage_ref[slab + which, pl.ds(r1, len1, stride=f1), :])
                for r1 in range(f1):
                    for r2 in range(f2):
                        r = r2 * f1 + r1
                        sub_ref[slab + which, r * sub_len:(r + 1) * sub_len, :] = (
                            mid_ref[which, pl.ds(r1 * len1 + r2, sub_len, stride=f2), :]
                            .astype(BF16))

            def group_rows(which, start, size, slab=slab):
                return sub_ref[slab + which, pl.ds(start, size), :]
        rows_of.append(group_rows)

    def attend(g, start, first):
        q = rows_of[g](0, start, BLOCK)
        if first:
            k = rows_of[g](1, start, BLOCK)
            v = rows_of[g](2, start, BLOCK)
            mask = causal
        else:
            k = rows_of[g](1, start - BLOCK, 2 * BLOCK)
            v = rows_of[g](2, start - BLOCK, 2 * BLOCK)
            mask = band
        s = lax.dot_general(q, k, (((1,), (1,)), ((), ())), preferred_element_type=F32)
        s = jnp.where(mask, s, NEG_INF)
        m = jnp.max(s, axis=-1, keepdims=True)
        p = jnp.exp(s - m)
        l = jnp.sum(p, axis=-1, keepdims=True)
        o = jnp.dot(p.astype(BF16), v, preferred_element_type=F32) * (1.0 / l)
        lse = jnp.broadcast_to(m + jnp.log(l), (BLOCK, HEAD_DIM))
        if DIL_GROUPS[g][1] == 1:
            onat_ref[g, pl.ds(start, BLOCK), :] = o
            lnat_ref[g, pl.ds(start, BLOCK), :] = lse
        else:
            osub_ref[g - 1, pl.ds(start, BLOCK), :] = o
            lsub_ref[g - 1, pl.ds(start, BLOCK), :] = lse

    for c in range(nblk):
        for g, (window, dil) in enumerate(DIL_GROUPS):
            attend(g, c * BLOCK, c % (seq // dil // BLOCK) == 0)

    for g, (window, dil) in enumerate(DIL_GROUPS):
        sub_len = seq // dil
        if dil == 1:
            continue
        f1, f2 = _split_stride(dil)
        len1 = seq // f1
        for idx, (nat_ref, subseq_ref) in enumerate(((onat_ref, osub_ref), (lnat_ref, lsub_ref))):
            if f2 == 1:
                for r in range(dil):
                    src = slice(r * sub_len, (r + 1) * sub_len)
                    nat_ref[g, pl.ds(r, sub_len, stride=dil), :] = subseq_ref[g - 1, src, :]
                continue
            mid = 3 + idx
            for r1 in range(f1):
                for r2 in range(f2):
                    r = r2 * f1 + r1
                    mid_ref[mid, pl.ds(r1 * len1 + r2, sub_len, stride=f2), :] = (
                        subseq_ref[g - 1, r * sub_len:(r + 1) * sub_len, :])
            for r1 in range(f1):
                nat_ref[g, pl.ds(r1, len1, stride=f1), :] = mid_ref[mid, r1 * len1:(r1 + 1) * len1, :]

    chunk = 2 * BLOCK
    for t in range(seq // chunk):
        rows = slice(t * chunk, (t + 1) * chunk)
        lse = [lnat_ref[g, rows, :] for g in range(len(DIL_GROUPS))]
        mx = jnp.maximum(jnp.maximum(lse[0], lse[1]), lse[2])
        e = [jnp.exp(x - mx) for x in lse]
        inv = 1.0 / (e[0] + e[1] + e[2])
        for g in range(len(DIL_GROUPS)):
            y_refs[g][0, rows, :] = (onat_ref[g, rows, :] * (e[g] * inv)
                                     * z_refs[g][0, rows, :].astype(F32)).astype(y_refs[g].dtype)


def _dilated_attn(q_src, k_src, v_src, z_src):
    sources = (q_src, k_src, v_src, z_src)
    b, s, _ = q_src[0].shape
    n_groups = len(DIL_GROUPS)
    n_regrouped = n_groups - 1
    assert sum(_split_stride(dil)[1] > 1 for _, dil in DIL_GROUPS) <= 1

    def head_spec(off, g):
        first = (off + g * GROUP_WIDTH) // HEAD_DIM
        return pl.BlockSpec((1, s, HEAD_DIM), lambda bi, hg: (bi, 0, first + hg))

    in_specs = [head_spec(off, g) for _, off in sources for g in range(n_groups)]
    operands = [arr for arr, _ in sources for g in range(n_groups)]
    out_spec = pl.BlockSpec((1, s, HEAD_DIM), lambda bi, hg: (bi, 0, hg))
    return pl.pallas_call(
        functools.partial(_dilated_attn_kernel, seq=s),
        grid=(b, HEADS_PER_DIL),
        in_specs=in_specs,
        out_specs=[out_spec] * n_groups,
        out_shape=[jax.ShapeDtypeStruct((b, s, GROUP_WIDTH), BF16)] * n_groups,
        scratch_shapes=[pltpu.VMEM((3 * n_regrouped, s, HEAD_DIM), F32),
                        pltpu.VMEM((5, s, HEAD_DIM), F32),
                        pltpu.VMEM((3 * n_regrouped, s, HEAD_DIM), BF16),
                        pltpu.VMEM((n_regrouped, s, HEAD_DIM), F32),
                        pltpu.VMEM((n_regrouped, s, HEAD_DIM), F32),
                        pltpu.VMEM((n_groups, s, HEAD_DIM), F32),
                        pltpu.VMEM((n_groups, s, HEAD_DIM), F32)],
        compiler_params=_params(("parallel", "parallel")),
        name="dilated_attn",
    )(*operands)


def _xattn_kernel(q_ref, z_ref, mk_ref, mv_ref, y_ref):
    for h in range(N_XATTN_HEADS):
        cols = slice(h * XATTN_HEAD_DIM, (h + 1) * XATTN_HEAD_DIM)
        s = lax.dot_general(q_ref[0, :, cols], mk_ref[0, :, cols],
                            (((1,), (1,)), ((), ())), preferred_element_type=F32)
        m = jnp.max(s, axis=-1, keepdims=True)
        p = jnp.exp(s - m)
        l = jnp.sum(p, axis=-1, keepdims=True)
        o = jnp.dot(p.astype(BF16), mv_ref[0, :, cols], preferred_element_type=F32)
        y_ref[0, :, cols] = (o * (1.0 / l) * z_ref[0, :, cols].astype(F32)).astype(y_ref.dtype)


def _xattn(q_src, z_src, mkv3, tm):
    b, s, _ = q_src[0].shape
    m_len = mkv3.shape[1]
    q_off, z_off = q_src[1], z_src[1]
    assert q_off % XATTN_WIDTH == 0 and z_off % XATTN_WIDTH == 0
    return pl.pallas_call(
        _xattn_kernel,
        grid=(b, s // tm),
        in_specs=[pl.BlockSpec((1, tm, XATTN_WIDTH), lambda bi, i: (bi, i, q_off // XATTN_WIDTH)),
                  pl.BlockSpec((1, tm, XATTN_WIDTH), lambda bi, i: (bi, i, z_off // XATTN_WIDTH)),
                  pl.BlockSpec((1, m_len, XATTN_WIDTH), lambda bi, i: (bi, 0, 0)),
                  pl.BlockSpec((1, m_len, XATTN_WIDTH), lambda bi, i: (bi, 0, 1))],
        out_specs=pl.BlockSpec((1, tm, XATTN_WIDTH), lambda bi, i: (bi, i, 0)),
        out_shape=jax.ShapeDtypeStruct((b, s, XATTN_WIDTH), BF16),
        compiler_params=_params(("parallel", "parallel")),
        name="mem_xattn",
    )(q_src[0], z_src[0], mkv3, mkv3)


def _out_proj_kernel(*refs, ni, d, n_mix):
    y_refs, w_refs = refs[:n_mix], refs[n_mix:2 * n_mix]
    x_ref, g_ref, o_ref, y2_ref, ssq_ref, scale_ref = refs[2 * n_mix:]
    i = pl.program_id(0)
    j = pl.program_id(1)

    @pl.when(j == 0)
    def _():
        @pl.when(i > 0)
        def _():
            scale_ref[...] = lax.rsqrt(ssq_ref[...] * (1.0 / d) + EPS)
        ssq_ref[...] = jnp.zeros_like(ssq_ref)

    def finish_previous_tile():
        o_ref[...] = x_ref[...] + y2_ref[j] * scale_ref[...] * g_ref[...]

    def multiply_this_tile():
        y2 = jnp.dot(y_refs[0][...], w_refs[0][...], preferred_element_type=F32)
        for y_ref, w_ref in zip(y_refs[1:], w_refs[1:]):
            y2 = y2 + jnp.dot(y_ref[...], w_ref[...], preferred_element_type=F32)
        y2_ref[j] = y2
        ssq_ref[...] += jnp.sum(y2 * y2, axis=-1, keepdims=True)

    @pl.when(i == 0)
    def _():
        multiply_this_tile()

    @pl.when((i > 0) & (i < ni))
    def _():
        finish_previous_tile()
        multiply_this_tile()

    @pl.when(i == ni)
    def _():
        finish_previous_tile()


def _out_proj(ys, w_bf, x2d, gain, tm, tn):
    rows, d = x2d.shape
    nj = d // tn
    ni = rows // tm

    def y_spec(width):
        return pl.BlockSpec((tm, width), lambda i, j: (jnp.minimum(i, ni - 1), 0))

    def w_spec(width, row_off):
        assert row_off % width == 0
        return pl.BlockSpec((width, tn),
                            lambda i, j: (row_off // width, jnp.where(i == ni, nj - 1, j)))

    io_spec = pl.BlockSpec((tm, tn),
                           lambda i, j: (jnp.maximum(i - 1, 0), jnp.where(i == 0, 0, j)))
    widths = [y.shape[1] for y in ys]
    offsets = [sum(widths[:k]) for k in range(len(widths))]
    return pl.pallas_call(
        functools.partial(_out_proj_kernel, ni=ni, d=d, n_mix=len(ys)),
        grid=(ni + 1, nj),
        in_specs=([y_spec(w) for w in widths]
                  + [w_spec(w, off) for w, off in zip(widths, offsets)]
                  + [io_spec, pl.BlockSpec((1, tn), lambda i, j: (0, j))]),
        out_specs=io_spec,
        out_shape=jax.ShapeDtypeStruct((rows, d), F32),
        scratch_shapes=[pltpu.VMEM((nj, tm, tn), F32),
                        pltpu.VMEM((tm, 1), F32),
                        pltpu.VMEM((tm, 1), F32)],
        compiler_params=_params(("arbitrary", "arbitrary")),
        name="out_proj",
    )(*ys, *([w_bf] * len(ys)), x2d, gain.reshape(1, d))


def _rope_tables(pos, half):
    inv = 1.0 / (ROPE_THETA ** (np.arange(half, dtype=np.float64) / half))
    ang = np.asarray(pos, dtype=np.float64)[:, None] * inv[None, :]
    return np.cos(ang), np.sin(ang)


def kernel(x, mem, pre_norm, w_in, conv_w, mem_norm, w_mem_kv, w_out, post_norm):
    b, s, d = x.shape
    m_len = mem.shape[1]
    depth = w_in.shape[0]
    assert all(window // dil == BLOCK for window, dil in DIL_GROUPS)
    pos = np.arange(s)
    cos_a, sin_a = _rope_tables(pos, HEAD_DIM // 2)
    cosa = np.concatenate([cos_a, cos_a], axis=-1)
    sina = np.concatenate([-sin_a, sin_a], axis=-1)
    a_scale = HEAD_DIM ** -0.5
    cos_qkv = jnp.asarray(np.stack([cosa * a_scale, cosa, np.ones_like(cosa)]), F32)
    sin_qkv = jnp.asarray(np.stack([sina * a_scale, sina, np.zeros_like(sina)]), F32)
    cosx, sinx = _rope_tables(pos + m_len, XATTN_HEAD_DIM // 2)
    x_scale = XATTN_HEAD_DIM ** -0.5
    cos_qx = jnp.asarray((cosx * x_scale)[None], F32)
    sin_qx = jnp.asarray((sinx * x_scale)[None], F32)
    cos_m, sin_m = _rope_tables(np.arange(m_len), XATTN_HEAD_DIM // 2)
    cosm = jnp.asarray(np.tile(cos_m, (b, 1)), F32)
    sinm = jnp.asarray(np.tile(sin_m, (b, 1)), F32)

    tm, tn = 1024, 512

    def tile_of(off):
        assert off % tn == 0
        return off // tn

    attn_tiles, x_tiles = ATTN_WIDTH // tn, XATTN_WIDTH // tn
    tiles_per_seq = s // tm

    def view(t):
        return t.reshape(b, s, t.shape[-1])

    def rope_tables(cos, sin, kind):
        def index(col_tile, row_tile):
            return kind(col_tile), row_tile % tiles_per_seq, 0
        return [(cos, (1, tm, LANES), index), (sin, (1, tm, LANES), index)]

    def lane_slab(off):
        assert off % LANES == 0
        return lambda j: off // LANES + j

    for layer in range(depth):
        x2d = x.reshape(b * s, d)
        w = w_in[layer]
        common = dict(tm=tm, tn=tn)
        w_qx_bf = _cast_columns_bf16(w, OFF_QX, XATTN_WIDTH, tr=1024)
        h, qx = _norm_project(x2d, pre_norm[layer], w_qx_bf, cos_qx, sin_qx, seq=s, tm=512)
        qx = view(qx)

        qkv = view(_in_proj(h, w, [lambda j: tile_of(OFF_QA) + j], 3 * attn_tiles,
                            _elementwise_finish(_rope_half_lane),
                            rope_tables(cos_qkv, sin_qkv, lambda j: j // attn_tiles),
                            name="in_proj_qkv", **common))
        y_conv, w_out_bf = _in_proj(
            h, w, [lane_slab(OFF_UC), lane_slab(OFF_BC), lane_slab(OFF_CC), lane_slab(OFF_ZC)],
            CONV_WIDTH // LANES,
            functools.partial(_gated_conv_finish, tiles_per_seq=tiles_per_seq),
            [(conv_w[layer], (CONV_K, LANES), lambda col_tile, row_tile: (0, col_tile))],
            out_tn=LANES, scratch=[pltpu.VMEM((CONV_HALO, LANES), F32)],
            rider=(lambda block: block, w_out[layer], []), name="in_proj_conv", **common)
        gates, hm = _in_proj(
            h, w, [lambda j: jnp.where(j < x_tiles, tile_of(OFF_ZX) + j,
                                       tile_of(OFF_ZA) - x_tiles + j)],
            x_tiles + attn_tiles, _elementwise_finish(_silu), [], name="in_proj_gates",
            rider=(_rms_normalize, mem.reshape(b * m_len, d), [mem_norm[layer].reshape(1, d)]),
            **common)
        gates = view(gates)
        mkv = _mem_proj(hm, w_mem_kv[layer], cosm, sinm, tn=512)

        y_groups = _dilated_attn((qkv, 0), (qkv, ATTN_WIDTH), (qkv, 2 * ATTN_WIDTH),
                                 (gates, XATTN_WIDTH))
        y_groups = [y.reshape(b * s, GROUP_WIDTH) for y in y_groups]
        y_x = _xattn((qx, 0), (gates, 0), mkv.reshape(b, m_len, 2 * XATTN_WIDTH), tm=512)
        y_x = y_x.reshape(b * s, XATTN_WIDTH)

        out = _out_proj([*y_groups, y_conv, y_x], w_out_bf, x2d,
                        post_norm[layer], tm=1024, tn=512)
        x = out.reshape(b, s, d)
    return x
```

```python
import functools

import jax
import jax.numpy as jnp
import numpy as np
from jax import lax
from jax.experimental import pallas as pl
from jax.experimental.pallas import tpu as pltpu

D_MODEL = 4096
MEM_LEN = 256
HEAD_DIM = 128
DIL_GROUPS = ((128, 1), (512, 4), (2048, 16))
ATTN_WIDTH = 3 * D_MODEL // 8
CONV_WIDTH = 3 * D_MODEL // 8
XATTN_WIDTH = D_MODEL // 4
N_XATTN_HEADS = 4
XATTN_HEAD_DIM = XATTN_WIDTH // N_XATTN_HEADS
HEADS_PER_DIL = 4
GROUP_WIDTH = HEADS_PER_DIL * HEAD_DIM
CONV_K = 3
IN_COLS = 4 * ATTN_WIDTH + 4 * CONV_WIDTH + 2 * XATTN_WIDTH
BLOCK = 128
ROPE_THETA = 10000.0
EPS = 1e-6
NEG_INF = -1e30

OFF_QA = 0
OFF_KA = OFF_QA + ATTN_WIDTH
OFF_VA = OFF_KA + ATTN_WIDTH
OFF_ZA = OFF_VA + ATTN_WIDTH
OFF_UC = OFF_ZA + ATTN_WIDTH
OFF_BC = OFF_UC + CONV_WIDTH
OFF_CC = OFF_BC + CONV_WIDTH
OFF_ZC = OFF_CC + CONV_WIDTH
OFF_QX = OFF_ZC + CONV_WIDTH
OFF_ZX = OFF_QX + XATTN_WIDTH

V7X_VMEM_LIMIT_BYTES = 56 * 1024 * 1024
LANES = 128
W_PIECES = 4
EPILOGUE_ROWS = 128
CONV_HALO = 8
NORM_ROWS = 64

BF16 = jnp.bfloat16
F32 = jnp.float32


def _params(semantics):
    return pltpu.CompilerParams(dimension_semantics=semantics,
                                vmem_limit_bytes=V7X_VMEM_LIMIT_BYTES)


def _rms_normalize(x, gain):
    ms = jnp.mean(x * x, axis=-1, keepdims=True)
    return x * lax.rsqrt(ms + EPS) * gain


def _cast_kernel(x_ref, o_ref):
    o_ref[...] = x_ref[...].astype(o_ref.dtype)


def _cast_columns_bf16(w, col_off, width, tr):
    rows = w.shape[0]
    assert col_off % width == 0
    return pl.pallas_call(
        _cast_kernel,
        grid=(rows // tr,),
        in_specs=[pl.BlockSpec((tr, width), lambda i: (i, col_off // width))],
        out_specs=pl.BlockSpec((tr, width), lambda i: (i, 0)),
        out_shape=jax.ShapeDtypeStruct((rows, width), BF16),
        compiler_params=_params(("parallel",)),
        name="cast_w_qx",
    )(w)


def _norm_project_kernel(x_ref, g_ref, w_ref, cos_ref, sin_ref, h_ref, o_ref, acc_ref):
    tm = x_ref.shape[0]
    for r0 in range(0, tm, NORM_ROWS):
        rows = slice(r0, r0 + NORM_ROWS)
        h_ref[rows, :] = _rms_normalize(x_ref[rows, :], g_ref[...]).astype(h_ref.dtype)
    acc_ref[...] = jnp.dot(h_ref[...], w_ref[...], preferred_element_type=F32)
    for r0 in range(0, tm, EPILOGUE_ROWS):
        rows = slice(r0, r0 + EPILOGUE_ROWS)
        o_ref[rows, :] = _rope_two_tiles(acc_ref[rows, :], cos_ref[0, rows, :],
                                         sin_ref[0, rows, :]).astype(o_ref.dtype)


def _norm_project(x2d, gain, w_bf, cos, sin, seq, tm):
    rows, d = x2d.shape
    width = w_bf.shape[1]
    tiles_per_seq = seq // tm
    table = pl.BlockSpec((1, tm, LANES), lambda i: (0, i % tiles_per_seq, 0))
    return pl.pallas_call(
        _norm_project_kernel,
        grid=(rows // tm,),
        in_specs=[pl.BlockSpec((tm, d), lambda i: (i, 0)),
                  pl.BlockSpec((1, d), lambda i: (0, 0)),
                  pl.BlockSpec((d, width), lambda i: (0, 0)),
                  table, table],
        out_specs=[pl.BlockSpec((tm, d), lambda i: (i, 0)),
                   pl.BlockSpec((tm, width), lambda i: (i, 0))],
        out_shape=[jax.ShapeDtypeStruct((rows, d), BF16),
                   jax.ShapeDtypeStruct((rows, width), BF16)],
        scratch_shapes=[pltpu.VMEM((tm, width), F32)],
        compiler_params=_params(("arbitrary",)),
        name="norm_project_qx",
    )(x2d, gain.reshape(1, d), w_bf, cos, sin)


def _silu(z):
    half = 0.5 * z
    return half + half * jnp.tanh(half)


def _rope_half_lane(acc, cos, sin_signed):
    outs = []
    for h in range(acc.shape[1] // HEAD_DIM):
        xh = acc[:, h * HEAD_DIM:(h + 1) * HEAD_DIM]
        outs.append(xh * cos + pltpu.roll(xh, HEAD_DIM // 2, axis=1) * sin_signed)
    return jnp.concatenate(outs, axis=1)


def _rope_two_tiles(acc, cos, sin):
    outs = []
    for h in range(acc.shape[1] // XATTN_HEAD_DIM):
        x1 = acc[:, h * XATTN_HEAD_DIM:h * XATTN_HEAD_DIM + LANES]
        x2 = acc[:, h * XATTN_HEAD_DIM + LANES:(h + 1) * XATTN_HEAD_DIM]
        outs.append(x1 * cos - x2 * sin)
        outs.append(x2 * cos + x1 * sin)
    return jnp.concatenate(outs, axis=1)


def _elementwise_finish(epilogue):
    def finish(acc_ref, slot, o_ref, aux_refs, scratch_refs, row_tile):
        for r0 in range(0, o_ref.shape[0], EPILOGUE_ROWS):
            rows = slice(r0, r0 + EPILOGUE_ROWS)
            o_ref[rows, :] = epilogue(acc_ref[slot, rows, :],
                                      *[r[0, rows, :] for r in aux_refs]).astype(o_ref.dtype)
    return finish


def _gated_conv_finish(acc_ref, slot, o_ref, aux_refs, scratch_refs, row_tile, *, tiles_per_seq):
    (w_ref,), (carry_ref,) = aux_refs, scratch_refs
    tm = o_ref.shape[0]
    u_cols, b_cols, c_cols, z_cols = (slice(k * LANES, (k + 1) * LANES) for k in range(4))
    w = w_ref[...]
    tail = jnp.where(row_tile % tiles_per_seq == 0, 0.0, carry_ref[...])
    for r0 in range(0, tm, EPILOGUE_ROWS):
        rows = slice(r0, r0 + EPILOGUE_ROWS)
        cu = acc_ref[slot, rows, c_cols] * acc_ref[slot, rows, u_cols]
        window = jnp.concatenate([tail, cu], axis=0)
        conv = cu * w[CONV_K - 1:CONV_K, :]
        for lag in range(1, CONV_K):
            shifted = pltpu.roll(window, lag, axis=0)[CONV_HALO:, :]
            conv = conv + shifted * w[CONV_K - 1 - lag:CONV_K - lag, :]
        o_ref[rows, :] = (acc_ref[slot, rows, b_cols] * conv
                          * _silu(acc_ref[slot, rows, z_cols])).astype(o_ref.dtype)
        tail = cu[EPILOGUE_ROWS - CONV_HALO:, :]
    carry_ref[...] = tail


def _in_proj_kernel(*refs, ni, n_steps, n_w, w_pieces, n_aux, n_scratch, finish_fn, rider_fn,
                    rider_steps):
    h_ref = refs[0]
    w_refs = refs[1:1 + n_w]
    aux_refs = refs[1 + n_w:1 + n_w + n_aux]
    scratch_refs = refs[len(refs) - n_scratch:] if n_scratch else ()
    rest = refs[1 + n_w + n_aux:len(refs) - n_scratch]
    if rider_fn is not None:
        rider_in_ref, *rider_row_refs = rest[:-4]
        o_ref, rider_out_ref, wbf_ref, acc_ref = rest[-4:]
    else:
        o_ref, wbf_ref, acc_ref = rest
    t = pl.program_id(0)
    piece_rows = wbf_ref.shape[0] // w_pieces
    slab_cols = wbf_ref.shape[1] // (n_w // w_pieces)

    @pl.when((t % ni == 0) & (t < n_steps))
    def _():
        for k, w_ref in enumerate(w_refs):
            slab, p = divmod(k, w_pieces)
            wbf_ref[p * piece_rows:(p + 1) * piece_rows,
                    slab * slab_cols:(slab + 1) * slab_cols] = w_ref[...].astype(BF16)

    if rider_fn is not None:
        @pl.when(t < rider_steps)
        def _():
            rider_out_ref[...] = rider_fn(
                rider_in_ref[...], *[r[...] for r in rider_row_refs]).astype(rider_out_ref.dtype)

    def multiply(slot):
        acc_ref[slot] = jnp.dot(h_ref[...], wbf_ref[...], preferred_element_type=F32)

    def finish(slot):
        finish_fn(acc_ref, slot, o_ref, aux_refs, scratch_refs, ((t - 1) % ni))

    @pl.when(t == 0)
    def _():
        for ref in scratch_refs:
            ref[...] = jnp.zeros_like(ref)
        multiply(0)

    for parity in range(2):
        @pl.when((t > 0) & (t < n_steps) & (t % 2 == parity))
        def _(parity=parity):
            finish(1 - parity)
            multiply(parity)

    @pl.when(t == n_steps)
    def _():
        finish((n_steps - 1) % 2)


def _in_proj(h, w, col_slabs, n_col_tiles, finish_fn, aux, tm, tn, name, out_tn=None,
             scratch=(), rider=None, w_pieces=W_PIECES):
    rows, d = h.shape
    ni = rows // tm
    n_steps = n_col_tiles * ni
    out_tn = tn if out_tn is None else out_tn
    slab_cols = tn // len(col_slabs)
    rider_in_specs, rider_out_specs, rider_out_shapes, rider_operands = [], [], [], []
    rider_fn, rider_steps = None, 0
    if rider is not None:
        rider_fn, rider_matrix, rider_rows = rider
        r_rows, r_cols = rider_matrix.shape
        assert all(r.shape == (1, r_cols) for r in rider_rows)
        rider_steps = max(k for k in range(1, n_steps + 1) if r_rows % k == 0)
        rider_spec = pl.BlockSpec((r_rows // rider_steps, r_cols),
                                  lambda t: (jnp.minimum(t, rider_steps - 1), 0))
        rider_in_specs = [rider_spec] + [pl.BlockSpec((1, r_cols), lambda t: (0, 0))
                                         for _ in rider_rows]
        rider_out_specs = [rider_spec]
        rider_out_shapes = [jax.ShapeDtypeStruct(rider_matrix.shape, BF16)]
        rider_operands = [rider_matrix, *rider_rows]

    def cur(t):
        c = jnp.minimum(t, n_steps - 1)
        return c // ni, c % ni

    def prev(t):
        p = jnp.maximum(t - 1, 0)
        return p // ni, p % ni

    def aux_spec(block, index):
        return pl.BlockSpec(block, lambda t: index(*prev(t)))

    def w_piece_spec(col_slab, p):
        def index(t):
            j, i = cur(t)
            ahead = (i >= ni - w_pieces + p).astype(jnp.int32)
            return p, col_slab(jnp.minimum(j + ahead, n_col_tiles - 1))
        return pl.BlockSpec((d // w_pieces, slab_cols), index)

    assert ni >= w_pieces and d % w_pieces == 0
    w_specs = [w_piece_spec(col_slab, p) for col_slab in col_slabs for p in range(w_pieces)]
    results = pl.pallas_call(
        functools.partial(_in_proj_kernel, ni=ni, n_steps=n_steps, n_w=len(w_specs),
                          w_pieces=w_pieces, n_aux=len(aux), n_scratch=len(scratch), finish_fn=finish_fn,
                          rider_fn=rider_fn, rider_steps=rider_steps),
        grid=(n_steps + 1,),
        in_specs=[pl.BlockSpec((tm, d), lambda t: (cur(t)[1], 0))] + w_specs
                 + [aux_spec(block, index) for _, block, index in aux] + rider_in_specs,
        out_specs=[pl.BlockSpec((tm, out_tn), lambda t: (prev(t)[1], prev(t)[0]))]
                  + rider_out_specs,
        out_shape=[jax.ShapeDtypeStruct((rows, n_col_tiles * out_tn), BF16)] + rider_out_shapes,
        scratch_shapes=[pltpu.VMEM((d, tn), BF16), pltpu.VMEM((2, tm, tn), F32), *scratch],
        compiler_params=_params(("arbitrary",)),
        name=name,
    )(h, *([w] * len(w_specs)), *[arr for arr, _, _ in aux], *rider_operands)
    return results if rider is not None else results[0]


def _mem_proj_kernel(h_ref, w_ref, cos_ref, sin_ref, o_ref, *, tn):
    j = pl.program_id(0)
    acc = jnp.dot(h_ref[...], w_ref[...].astype(BF16), preferred_element_type=F32)

    @pl.when(j < XATTN_WIDTH // tn)
    def _():
        o_ref[...] = _rope_two_tiles(acc, cos_ref[...], sin_ref[...]).astype(o_ref.dtype)

    @pl.when(j >= XATTN_WIDTH // tn)
    def _():
        o_ref[...] = acc.astype(o_ref.dtype)


def _mem_proj(hm, w, cos, sin, tn):
    rows, d = hm.shape
    ncols = w.shape[1]
    tab = pl.BlockSpec((rows, LANES), lambda j: (0, 0))
    return pl.pallas_call(
        functools.partial(_mem_proj_kernel, tn=tn),
        grid=(ncols // tn,),
        in_specs=[pl.BlockSpec((rows, d), lambda j: (0, 0)),
                  pl.BlockSpec((d, tn), lambda j: (0, j)),
                  tab, tab],
        out_specs=pl.BlockSpec((rows, tn), lambda j: (0, j)),
        out_shape=jax.ShapeDtypeStruct((rows, ncols), BF16),
        compiler_params=_params(("parallel",)),
        name="mem_proj",
    )(hm, w, cos, sin)


MAX_SINGLE_ACCESS_STRIDE = 4


def _split_stride(dil):
    if dil <= MAX_SINGLE_ACCESS_STRIDE:
        return dil, 1
    assert dil % MAX_SINGLE_ACCESS_STRIDE == 0
    f2 = dil // MAX_SINGLE_ACCESS_STRIDE
    assert f2 <= MAX_SINGLE_ACCESS_STRIDE
    return MAX_SINGLE_ACCESS_STRIDE, f2


def _dilated_attn_kernel(q0, q1, q2, k0, k1, k2, v0, v1, v2, z0, z1, z2, y0, y1, y2,
                         stage_ref, mid_ref, sub_ref, osub_ref, lsub_ref, onat_ref, lnat_ref,
                         *, seq):
    q_refs, k_refs, v_refs = (q0, q1, q2), (k0, k1, k2), (v0, v1, v2)
    z_refs, y_refs = (z0, z1, z2), (y0, y1, y2)
    nblk = seq // BLOCK
    qi = lax.broadcasted_iota(jnp.int32, (BLOCK, 2 * BLOCK), 0)
    kk = lax.broadcasted_iota(jnp.int32, (BLOCK, 2 * BLOCK), 1)
    dist = qi + BLOCK - kk
    band = (dist >= 0) & (dist <= BLOCK)
    causal = (lax.broadcasted_iota(jnp.int32, (BLOCK, BLOCK), 1)
              <= lax.broadcasted_iota(jnp.int32, (BLOCK, BLOCK), 0))

    rows_of = []
    for g, (window, dil) in enumerate(DIL_GROUPS):
        sub_len = seq // dil
        if dil == 1:
            def group_rows(which, start, size, g=g):
                return (q_refs, k_refs, v_refs)[which][g][0, pl.ds(start, size), :]
        else:
            slab = 3 * (g - 1)
            f1, f2 = _split_stride(dil)
            len1 = seq // f1
            for which, refs in enumerate((q_refs, k_refs, v_refs)):
                stage_ref[slab + which] = refs[g][0].astype(F32)
                if f2 == 1:
                    for r in range(dil):
                        sub_ref[slab + which, r * sub_len:(r + 1) * sub_len, :] = (
                            stage_ref[slab + which, pl.ds(r, sub_len, stride=dil), :].astype(BF16))
                    continue
                for r1 in range(f1):
                    mid_ref[which, r1 * len1:(r1 + 1) * len1, :] = (
                        stage_ref[slab + which, pl.ds(r1, len1, stride=f1), :])
                for r1 in range(f1):
                    for r2 in range(f2):
                        r = r2 * f1 + r1
                        sub_ref[slab + which, r * sub_len:(r + 1) * sub_len, :] = (
                            mid_ref[which, pl.ds(r1 * len1 + r2, sub_len, stride=f2), :]
                            .astype(BF16))

            def group_rows(which, start, size, slab=slab):
                return sub_ref[slab + which, pl.ds(start, size), :]
        rows_of.append(group_rows)

    def attend(g, start, first):
        q = rows_of[g](0, start, BLOCK)
        if first:
            k = rows_of[g](1, start, BLOCK)
            v = rows_of[g](2, start, BLOCK)
            mask = causal
        else:
            k = rows_of[g](1, start - BLOCK, 2 * BLOCK)
            v = rows_of[g](2, start - BLOCK, 2 * BLOCK)
            mask = band
        s = lax.dot_general(q, k, (((1,), (1,)), ((), ())), preferred_element_type=F32)
        s = jnp.where(mask, s, NEG_INF)
        m = jnp.max(s, axis=-1, keepdims=True)
        p = jnp.exp(s - m)
        l = jnp.sum(p, axis=-1, keepdims=True)
        o = jnp.dot(p.astype(BF16), v, preferred_element_type=F32) * (1.0 / l)
        lse = jnp.broadcast_to(m + jnp.log(l), (BLOCK, HEAD_DIM))
        if DIL_GROUPS[g][1] == 1:
            onat_ref[g, pl.ds(start, BLOCK), :] = o
            lnat_ref[g, pl.ds(start, BLOCK), :] = lse
        else:
            osub_ref[g - 1, pl.ds(start, BLOCK), :] = o
            lsub_ref[g - 1, pl.ds(start, BLOCK), :] = lse

    for c in range(nblk):
        for g, (window, dil) in enumerate(DIL_GROUPS):
            attend(g, c * BLOCK, c % (seq // dil // BLOCK) == 0)

    for g, (window, dil) in enumerate(DIL_GROUPS):
        sub_len = seq // dil
        if dil == 1:
            continue
        f1, f2 = _split_stride(dil)
        len1 = seq // f1
        for idx, (nat_ref, subseq_ref) in enumerate(((onat_ref, osub_ref), (lnat_ref, lsub_ref))):
            if f2 == 1:
                for r in range(dil):
                    src = slice(r * sub_len, (r + 1) * sub_len)
                    nat_ref[g, pl.ds(r, sub_len, stride=dil), :] = subseq_ref[g - 1, src, :]
                continue
            mid = 3 + idx
            for r1 in range(f1):
                for r2 in range(f2):
                    r = r2 * f1 + r1
                    mid_ref[mid, pl.ds(r1 * len1 + r2, sub_len, stride=f2), :] = (
                        subseq_ref[g - 1, r * sub_len:(r + 1) * sub_len, :])
            for r1 in range(f1):
                nat_ref[g, pl.ds(r1, len1, stride=f1), :] = mid_ref[mid, r1 * len1:(r1 + 1) * len1, :]

    chunk = 2 * BLOCK
    for t in range(seq // chunk):
        rows = slice(t * chunk, (t + 1) * chunk)
        lse = [lnat_ref[g, rows, :] for g in range(len(DIL_GROUPS))]
        mx = jnp.maximum(jnp.maximum(lse[0], lse[1]), lse[2])
        e = [jnp.exp(x - mx) for x in lse]
        inv = 1.0 / (e[0] + e[1] + e[2])
        for g in range(len(DIL_GROUPS)):
            y_refs[g][0, rows, :] = (onat_ref[g, rows, :] * (e[g] * inv)
                                     * z_refs[g][0, rows, :].astype(F32)).astype(y_refs[g].dtype)


def _dilated_attn(q_src, k_src, v_src, z_src):
    sources = (q_src, k_src, v_src, z_src)
    b, s, _ = q_src[0].shape
    n_groups = len(DIL_GROUPS)
    n_regrouped = n_groups - 1
    assert sum(_split_stride(dil)[1] > 1 for _, dil in DIL_GROUPS) <= 1

    def head_spec(off, g):
        first = (off + g * GROUP_WIDTH) // HEAD_DIM
        return pl.BlockSpec((1, s, HEAD_DIM), lambda bi, hg: (bi, 0, first + hg))

    in_specs = [head_spec(off, g) for _, off in sources for g in range(n_groups)]
    operands = [arr for arr, _ in sources for g in range(n_groups)]
    out_spec = pl.BlockSpec((1, s, HEAD_DIM), lambda bi, hg: (bi, 0, hg))
    return pl.pallas_call(
        functools.partial(_dilated_attn_kernel, seq=s),
        grid=(b, HEADS_PER_DIL),
        in_specs=in_specs,
        out_specs=[out_spec] * n_groups,
        out_shape=[jax.ShapeDtypeStruct((b, s, GROUP_WIDTH), BF16)] * n_groups,
        scratch_shapes=[pltpu.VMEM((3 * n_regrouped, s, HEAD_DIM), F32),
                        pltpu.VMEM((5, s, HEAD_DIM), F32),
                        pltpu.VMEM((3 * n_regrouped, s, HEAD_DIM), BF16),
                        pltpu.VMEM((n_regrouped, s, HEAD_DIM), F32),
                        pltpu.VMEM((n_regrouped, s, HEAD_DIM), F32),
                        pltpu.VMEM((n_groups, s, HEAD_DIM), F32),
                        pltpu.VMEM((n_groups, s, HEAD_DIM), F32)],
        compiler_params=_params(("parallel", "parallel")),
        name="dilated_attn",
    )(*operands)


def _xattn_kernel(q_ref, z_ref, mk_ref, mv_ref, y_ref):
    for h in range(N_XATTN_HEADS):
        cols = slice(h * XATTN_HEAD_DIM, (h + 1) * XATTN_HEAD_DIM)
        s = lax.dot_general(q_ref[0, :, cols], mk_ref[0, :, cols],
                            (((1,), (1,)), ((), ())), preferred_element_type=F32)
        m = jnp.max(s, axis=-1, keepdims=True)
        p = jnp.exp(s - m)
        l = jnp.sum(p, axis=-1, keepdims=True)
        o = jnp.dot(p.astype(BF16), mv_ref[0, :, cols], preferred_element_type=F32)
        y_ref[0, :, cols] = (o * (1.0 / l) * z_ref[0, :, cols].astype(F32)).astype(y_ref.dtype)


def _xattn(q_src, z_src, mkv3, tm):
    b, s, _ = q_src[0].shape
    m_len = mkv3.shape[1]
    q_off, z_off = q_src[1], z_src[1]
    assert q_off % XATTN_WIDTH == 0 and z_off % XATTN_WIDTH == 0
    return pl.pallas_call(
        _xattn_kernel,
        grid=(b, s // tm),
        in_specs=[pl.BlockSpec((1, tm, XATTN_WIDTH), lambda bi, i: (bi, i, q_off // XATTN_WIDTH)),
                  pl.BlockSpec((1, tm, XATTN_WIDTH), lambda bi, i: (bi, i, z_off // XATTN_WIDTH)),
                  pl.BlockSpec((1, m_len, XATTN_WIDTH), lambda bi, i: (bi, 0, 0)),
                  pl.BlockSpec((1, m_len, XATTN_WIDTH), lambda bi, i: (bi, 0, 1))],
        out_specs=pl.BlockSpec((1, tm, XATTN_WIDTH), lambda bi, i: (bi, i, 0)),
        out_shape=jax.ShapeDtypeStruct((b, s, XATTN_WIDTH), BF16),
        compiler_params=_params(("parallel", "parallel")),
        name="mem_xattn",
    )(q_src[0], z_src[0], mkv3, mkv3)


def _out_proj_kernel(*refs, ni, d, n_mix):
    y_refs, w_refs = refs[:n_mix], refs[n_mix:2 * n_mix]
    x_ref, g_ref, o_ref, y2_ref, ssq_ref, scale_ref = refs[2 * n_mix:]
    i = pl.program_id(0)
    j = pl.program_id(1)

    @pl.when(j == 0)
    def _():
        @pl.when(i > 0)
        def _():
            scale_ref[...] = lax.rsqrt(ssq_ref[...] * (1.0 / d) + EPS)
        ssq_ref[...] = jnp.zeros_like(ssq_ref)

    def finish_previous_tile():
        o_ref[...] = x_ref[...] + y2_ref[j] * scale_ref[...] * g_ref[...]

    def multiply_this_tile():
        y2 = jnp.dot(y_refs[0][...], w_refs[0][...], preferred_element_type=F32)
        for y_ref, w_ref in zip(y_refs[1:], w_refs[1:]):
            y2 = y2 + jnp.dot(y_ref[...], w_ref[...], preferred_element_type=F32)
        y2_ref[j] = y2
        ssq_ref[...] += jnp.sum(y2 * y2, axis=-1, keepdims=True)

    @pl.when(i == 0)
    def _():
        multiply_this_tile()

    @pl.when((i > 0) & (i < ni))
    def _():
        finish_previous_tile()
        multiply_this_tile()

    @pl.when(i == ni)
    def _():
        finish_previous_tile()


def _out_proj(ys, w_bf, x2d, gain, tm, tn):
    rows, d = x2d.shape
    nj = d // tn
    ni = rows // tm

    def y_spec(width):
        return pl.BlockSpec((tm, width), lambda i, j: (jnp.minimum(i, ni - 1), 0))

    def w_spec(width, row_off):
        assert row_off % width == 0
        return pl.BlockSpec((width, tn),
                            lambda i, j: (row_off // width, jnp.where(i == ni, nj - 1, j)))

    io_spec = pl.BlockSpec((tm, tn),
                           lambda i, j: (jnp.maximum(i - 1, 0), jnp.where(i == 0, 0, j)))
    widths = [y.shape[1] for y in ys]
    offsets = [sum(widths[:k]) for k in range(len(widths))]
    return pl.pallas_call(
        functools.partial(_out_proj_kernel, ni=ni, d=d, n_mix=len(ys)),
        grid=(ni + 1, nj),
        in_specs=([y_spec(w) for w in widths]
                  + [w_spec(w, off) for w, off in zip(widths, offsets)]
                  + [io_spec, pl.BlockSpec((1, tn), lambda i, j: (0, j))]),
        out_specs=io_spec,
        out_shape=jax.ShapeDtypeStruct((rows, d), F32),
        scratch_shapes=[pltpu.VMEM((nj, tm, tn), F32),
                        pltpu.VMEM((tm, 1), F32),
                        pltpu.VMEM((tm, 1), F32)],
        compiler_params=_params(("arbitrary", "arbitrary")),
        name="out_proj",
    )(*ys, *([w_bf] * len(ys)), x2d, gain.reshape(1, d))


def _rope_tables(pos, half):
    inv = 1.0 / (ROPE_THETA ** (np.arange(half, dtype=np.float64) / half))
    ang = np.asarray(pos, dtype=np.float64)[:, None] * inv[None, :]
    return np.cos(ang), np.sin(ang)


def kernel(x, mem, pre_norm, w_in, conv_w, mem_norm, w_mem_kv, w_out, post_norm):
    b, s, d = x.shape
    m_len = mem.shape[1]
    depth = w_in.shape[0]
    assert all(window // dil == BLOCK for window, dil in DIL_GROUPS)
    pos = np.arange(s)
    cos_a, sin_a = _rope_tables(pos, HEAD_DIM // 2)
    cosa = np.concatenate([cos_a, cos_a], axis=-1)
    sina = np.concatenate([-sin_a, sin_a], axis=-1)
    a_scale = HEAD_DIM ** -0.5
    cos_qkv = jnp.asarray(np.stack([cosa * a_scale, cosa, np.ones_like(cosa)]), F32)
    sin_qkv = jnp.asarray(np.stack([sina * a_scale, sina, np.zeros_like(sina)]), F32)
    cosx, sinx = _rope_tables(pos + m_len, XATTN_HEAD_DIM // 2)
    x_scale = XATTN_HEAD_DIM ** -0.5
    cos_qx = jnp.asarray((cosx * x_scale)[None], F32)
    sin_qx = jnp.asarray((sinx * x_scale)[None], F32)
    cos_m, sin_m = _rope_tables(np.arange(m_len), XATTN_HEAD_DIM // 2)
    cosm = jnp.asarray(np.tile(cos_m, (b, 1)), F32)
    sinm = jnp.asarray(np.tile(sin_m, (b, 1)), F32)

    tm, tn = 1024, 512

    def tile_of(off):
        assert off % tn == 0
        return off // tn

    attn_tiles, x_tiles = ATTN_WIDTH // tn, XATTN_WIDTH // tn
    tiles_per_seq = s // tm

    def view(t):
        return t.reshape(b, s, t.shape[-1])

    def rope_tables(cos, sin, kind):
        def index(col_tile, row_tile):
            return kind(col_tile), row_tile % tiles_per_seq, 0
        return [(cos, (1, tm, LANES), index), (sin, (1, tm, LANES), index)]

    def lane_slab(off):
        assert off % LANES == 0
        return lambda j: off // LANES + j

    for layer in range(depth):
        x2d = x.reshape(b * s, d)
        w = w_in[layer]
        common = dict(tm=tm, tn=tn)
        w_qx_bf = _cast_columns_bf16(w, OFF_QX, XATTN_WIDTH, tr=1024)
        h, qx = _norm_project(x2d, pre_norm[layer], w_qx_bf, cos_qx, sin_qx, seq=s, tm=512)
        qx = view(qx)

        qkv = view(_in_proj(h, w, [lambda j: tile_of(OFF_QA) + j], 3 * attn_tiles,
                            _elementwise_finish(_rope_half_lane),
                            rope_tables(cos_qkv, sin_qkv, lambda j: j // attn_tiles),
                            name="in_proj_qkv", w_pieces=2, **common))
        y_conv, w_out_bf = _in_proj(
            h, w, [lane_slab(OFF_UC), lane_slab(OFF_BC), lane_slab(OFF_CC), lane_slab(OFF_ZC)],
            CONV_WIDTH // LANES,
            functools.partial(_gated_conv_finish, tiles_per_seq=tiles_per_seq),
            [(conv_w[layer], (CONV_K, LANES), lambda col_tile, row_tile: (0, col_tile))],
            out_tn=LANES, scratch=[pltpu.VMEM((CONV_HALO, LANES), F32)],
            rider=(lambda block: block, w_out[layer], []), name="in_proj_conv", w_pieces=1,
            **common)
        gates, hm = _in_proj(
            h, w, [lambda j: jnp.where(j < x_tiles, tile_of(OFF_ZX) + j,
                                       tile_of(OFF_ZA) - x_tiles + j)],
            x_tiles + attn_tiles, _elementwise_finish(_silu), [], name="in_proj_gates",
            rider=(_rms_normalize, mem.reshape(b * m_len, d), [mem_norm[layer].reshape(1, d)]),
            w_pieces=2, **common)
        gates = view(gates)
        mkv = _mem_proj(hm, w_mem_kv[layer], cosm, sinm, tn=512)

        y_groups = _dilated_attn((qkv, 0), (qkv, ATTN_WIDTH), (qkv, 2 * ATTN_WIDTH),
                                 (gates, XATTN_WIDTH))
        y_groups = [y.reshape(b * s, GROUP_WIDTH) for y in y_groups]
        y_x = _xattn((qx, 0), (gates, 0), mkv.reshape(b, m_len, 2 * XATTN_WIDTH), tm=512)
        y_x = y_x.reshape(b * s, XATTN_WIDTH)

        out = _out_proj([*y_groups, y_conv, y_x], w_out_bf, x2d,
                        post_norm[layer], tm=1024, tn=512)
        x = out.reshape(b, s, d)
    return x
```

```python
import functools

import jax
import jax.numpy as jnp
import numpy as np
from jax import lax
from jax.experimental import pallas as pl
from jax.experimental.pallas import tpu as pltpu

D_MODEL = 4096
MEM_LEN = 256
HEAD_DIM = 128
DIL_GROUPS = ((128, 1), (512, 4), (2048, 16))
ATTN_WIDTH = 3 * D_MODEL // 8
CONV_WIDTH = 3 * D_MODEL // 8
XATTN_WIDTH = D_MODEL // 4
N_XATTN_HEADS = 4
XATTN_HEAD_DIM = XATTN_WIDTH // N_XATTN_HEADS
HEADS_PER_DIL = 4
GROUP_WIDTH = HEADS_PER_DIL * HEAD_DIM
CONV_K = 3
IN_COLS = 4 * ATTN_WIDTH + 4 * CONV_WIDTH + 2 * XATTN_WIDTH
BLOCK = 128
ROPE_THETA = 10000.0
EPS = 1e-6
NEG_INF = -1e30

OFF_QA = 0
OFF_KA = OFF_QA + ATTN_WIDTH
OFF_VA = OFF_KA + ATTN_WIDTH
OFF_ZA = OFF_VA + ATTN_WIDTH
OFF_UC = OFF_ZA + ATTN_WIDTH
OFF_BC = OFF_UC + CONV_WIDTH
OFF_CC = OFF_BC + CONV_WIDTH
OFF_ZC = OFF_CC + CONV_WIDTH
OFF_QX = OFF_ZC + CONV_WIDTH
OFF_ZX = OFF_QX + XATTN_WIDTH

V7X_VMEM_LIMIT_BYTES = 56 * 1024 * 1024
LANES = 128
W_PIECES = 4
EPILOGUE_ROWS = 128
CONV_HALO = 8
NORM_ROWS = 64

BF16 = jnp.bfloat16
F32 = jnp.float32


def _params(semantics):
    return pltpu.CompilerParams(dimension_semantics=semantics,
                                vmem_limit_bytes=V7X_VMEM_LIMIT_BYTES)


def _rms_normalize(x, gain):
    ms = jnp.mean(x * x, axis=-1, keepdims=True)
    return x * lax.rsqrt(ms + EPS) * gain


def _cast_kernel(x_ref, o_ref):
    o_ref[...] = x_ref[...].astype(o_ref.dtype)


def _cast_columns_bf16(w, col_off, width, tr):
    rows = w.shape[0]
    assert col_off % width == 0
    return pl.pallas_call(
        _cast_kernel,
        grid=(rows // tr,),
        in_specs=[pl.BlockSpec((tr, width), lambda i: (i, col_off // width))],
        out_specs=pl.BlockSpec((tr, width), lambda i: (i, 0)),
        out_shape=jax.ShapeDtypeStruct((rows, width), BF16),
        compiler_params=_params(("parallel",)),
        name="cast_w_qx",
    )(w)


def _norm_project_kernel(x_ref, g_ref, w_ref, cos_ref, sin_ref, h_ref, o_ref, acc_ref):
    tm = x_ref.shape[0]
    for r0 in range(0, tm, NORM_ROWS):
        rows = slice(r0, r0 + NORM_ROWS)
        h_ref[rows, :] = _rms_normalize(x_ref[rows, :], g_ref[...]).astype(h_ref.dtype)
    acc_ref[...] = jnp.dot(h_ref[...], w_ref[...], preferred_element_type=F32)
    for r0 in range(0, tm, EPILOGUE_ROWS):
        rows = slice(r0, r0 + EPILOGUE_ROWS)
        o_ref[rows, :] = _rope_two_tiles(acc_ref[rows, :], cos_ref[0, rows, :],
                                         sin_ref[0, rows, :]).astype(o_ref.dtype)


def _norm_project(x2d, gain, w_bf, cos, sin, seq, tm):
    rows, d = x2d.shape
    width = w_bf.shape[1]
    tiles_per_seq = seq // tm
    table = pl.BlockSpec((1, tm, LANES), lambda i: (0, i % tiles_per_seq, 0))
    return pl.pallas_call(
        _norm_project_kernel,
        grid=(rows // tm,),
        in_specs=[pl.BlockSpec((tm, d), lambda i: (i, 0)),
                  pl.BlockSpec((1, d), lambda i: (0, 0)),
                  pl.BlockSpec((d, width), lambda i: (0, 0)),
                  table, table],
        out_specs=[pl.BlockSpec((tm, d), lambda i: (i, 0)),
                   pl.BlockSpec((tm, width), lambda i: (i, 0))],
        out_shape=[jax.ShapeDtypeStruct((rows, d), BF16),
                   jax.ShapeDtypeStruct((rows, width), BF16)],
        scratch_shapes=[pltpu.VMEM((tm, width), F32)],
        compiler_params=_params(("arbitrary",)),
        name="norm_project_qx",
    )(x2d, gain.reshape(1, d), w_bf, cos, sin)


def _silu(z):
    half = 0.5 * z
    return half + half * jnp.tanh(half)


def _rope_half_lane(acc, cos, sin_signed):
    outs = []
    for h in range(acc.shape[1] // HEAD_DIM):
        xh = acc[:, h * HEAD_DIM:(h + 1) * HEAD_DIM]
        outs.append(xh * cos + pltpu.roll(xh, HEAD_DIM // 2, axis=1) * sin_signed)
    return jnp.concatenate(outs, axis=1)


def _rope_two_tiles(acc, cos, sin):
    outs = []
    for h in range(acc.shape[1] // XATTN_HEAD_DIM):
        x1 = acc[:, h * XATTN_HEAD_DIM:h * XATTN_HEAD_DIM + LANES]
        x2 = acc[:, h * XATTN_HEAD_DIM + LANES:(h + 1) * XATTN_HEAD_DIM]
        outs.append(x1 * cos - x2 * sin)
        outs.append(x2 * cos + x1 * sin)
    return jnp.concatenate(outs, axis=1)


def _elementwise_finish(epilogue):
    def finish(acc_ref, slot, o_ref, aux_refs, scratch_refs, row_tile):
        for r0 in range(0, o_ref.shape[0], EPILOGUE_ROWS):
            rows = slice(r0, r0 + EPILOGUE_ROWS)
            o_ref[rows, :] = epilogue(acc_ref[slot, rows, :],
                                      *[r[0, rows, :] for r in aux_refs]).astype(o_ref.dtype)
    return finish


def _gated_conv_finish(acc_ref, slot, o_ref, aux_refs, scratch_refs, row_tile, *, tiles_per_seq):
    (w_ref,), (carry_ref,) = aux_refs, scratch_refs
    tm = o_ref.shape[0]
    u_cols, b_cols, c_cols, z_cols = (slice(k * LANES, (k + 1) * LANES) for k in range(4))
    w = w_ref[...]
    tail = jnp.where(row_tile % tiles_per_seq == 0, 0.0, carry_ref[...])
    for r0 in range(0, tm, EPILOGUE_ROWS):
        rows = slice(r0, r0 + EPILOGUE_ROWS)
        cu = acc_ref[slot, rows, c_cols] * acc_ref[slot, rows, u_cols]
        window = jnp.concatenate([tail, cu], axis=0)
        conv = cu * w[CONV_K - 1:CONV_K, :]
        for lag in range(1, CONV_K):
            shifted = pltpu.roll(window, lag, axis=0)[CONV_HALO:, :]
            conv = conv + shifted * w[CONV_K - 1 - lag:CONV_K - lag, :]
        o_ref[rows, :] = (acc_ref[slot, rows, b_cols] * conv
                          * _silu(acc_ref[slot, rows, z_cols])).astype(o_ref.dtype)
        tail = cu[EPILOGUE_ROWS - CONV_HALO:, :]
    carry_ref[...] = tail


def _in_proj_kernel(*refs, ni, n_steps, n_w, w_pieces, n_aux, n_scratch, finish_fn, rider_fn,
                    rider_steps):
    h_ref = refs[0]
    w_refs = refs[1:1 + n_w]
    aux_refs = refs[1 + n_w:1 + n_w + n_aux]
    scratch_refs = refs[len(refs) - n_scratch:] if n_scratch else ()
    rest = refs[1 + n_w + n_aux:len(refs) - n_scratch]
    if rider_fn is not None:
        rider_in_ref, *rider_row_refs = rest[:-4]
        o_ref, rider_out_ref, wbf_ref, acc_ref = rest[-4:]
    else:
        o_ref, wbf_ref, acc_ref = rest
    t = pl.program_id(0)
    piece_rows = wbf_ref.shape[0] // w_pieces
    slab_cols = wbf_ref.shape[1] // (n_w // w_pieces)

    @pl.when((t % ni == 0) & (t < n_steps))
    def _():
        for k, w_ref in enumerate(w_refs):
            slab, p = divmod(k, w_pieces)
            wbf_ref[p * piece_rows:(p + 1) * piece_rows,
                    slab * slab_cols:(slab + 1) * slab_cols] = w_ref[...].astype(BF16)

    if rider_fn is not None:
        @pl.when(t < rider_steps)
        def _():
            rider_out_ref[...] = rider_fn(
                rider_in_ref[...], *[r[...] for r in rider_row_refs]).astype(rider_out_ref.dtype)

    def multiply(slot):
        acc_ref[slot] = jnp.dot(h_ref[...], wbf_ref[...], preferred_element_type=F32)

    def finish(slot):
        finish_fn(acc_ref, slot, o_ref, aux_refs, scratch_refs, ((t - 1) % ni))

    @pl.when(t == 0)
    def _():
        for ref in scratch_refs:
            ref[...] = jnp.zeros_like(ref)
        multiply(0)

    for parity in range(2):
        @pl.when((t > 0) & (t < n_steps) & (t % 2 == parity))
        def _(parity=parity):
            finish(1 - parity)
            multiply(parity)

    @pl.when(t == n_steps)
    def _():
        finish((n_steps - 1) % 2)


def _in_proj(h, w, col_slabs, n_col_tiles, finish_fn, aux, tm, tn, name, out_tn=None,
             scratch=(), rider=None, w_pieces=W_PIECES):
    rows, d = h.shape
    ni = rows // tm
    n_steps = n_col_tiles * ni
    out_tn = tn if out_tn is None else out_tn
    slab_cols = tn // len(col_slabs)
    rider_in_specs, rider_out_specs, rider_out_shapes, rider_operands = [], [], [], []
    rider_fn, rider_steps = None, 0
    if rider is not None:
        rider_fn, rider_matrix, rider_rows, place = rider
        r_rows, r_cols = rider_matrix.shape
        assert all(r.shape == (1, r_cols) for r in rider_rows)
        rider_steps = max(k for k in range(1, n_steps + 1) if r_rows % k == 0)
        block_rows = r_rows // rider_steps

        def rider_block(t):
            return jnp.minimum(t, rider_steps - 1)

        rider_spec = pl.BlockSpec((block_rows, r_cols), lambda t: (rider_block(t), 0))
        rider_in_specs = [rider_spec] + [pl.BlockSpec((1, r_cols), lambda t: (0, 0))
                                         for _ in rider_rows]
        rider_out_specs = [rider_spec if place is None else pl.BlockSpec(
            (block_rows, r_cols), lambda t: (place(rider_block(t), block_rows), 0))]
        rider_out_shapes = [jax.ShapeDtypeStruct(rider_matrix.shape, BF16)]
        rider_operands = [rider_matrix, *rider_rows]

    def cur(t):
        c = jnp.minimum(t, n_steps - 1)
        return c // ni, c % ni

    def prev(t):
        p = jnp.maximum(t - 1, 0)
        return p // ni, p % ni

    def aux_spec(block, index):
        return pl.BlockSpec(block, lambda t: index(*prev(t)))

    def w_piece_spec(col_slab, p):
        def index(t):
            j, i = cur(t)
            ahead = (i >= ni - w_pieces + p).astype(jnp.int32)
            return p, col_slab(jnp.minimum(j + ahead, n_col_tiles - 1))
        return pl.BlockSpec((d // w_pieces, slab_cols), index)

    assert ni >= w_pieces and d % w_pieces == 0
    w_specs = [w_piece_spec(col_slab, p) for col_slab in col_slabs for p in range(w_pieces)]
    results = pl.pallas_call(
        functools.partial(_in_proj_kernel, ni=ni, n_steps=n_steps, n_w=len(w_specs),
                          w_pieces=w_pieces, n_aux=len(aux), n_scratch=len(scratch), finish_fn=finish_fn,
                          rider_fn=rider_fn, rider_steps=rider_steps),
        grid=(n_steps + 1,),
        in_specs=[pl.BlockSpec((tm, d), lambda t: (cur(t)[1], 0))] + w_specs
                 + [aux_spec(block, index) for _, block, index in aux] + rider_in_specs,
        out_specs=[pl.BlockSpec((tm, out_tn), lambda t: (prev(t)[1], prev(t)[0]))]
                  + rider_out_specs,
        out_shape=[jax.ShapeDtypeStruct((rows, n_col_tiles * out_tn), BF16)] + rider_out_shapes,
        scratch_shapes=[pltpu.VMEM((d, tn), BF16), pltpu.VMEM((2, tm, tn), F32), *scratch],
        compiler_params=_params(("arbitrary",)),
        name=name,
    )(h, *([w] * len(w_specs)), *[arr for arr, _, _ in aux], *rider_operands)
    return results if rider is not None else results[0]


def _mem_proj_kernel(h_ref, w_ref, cos_ref, sin_ref, o_ref, *, tn):
    j = pl.program_id(0)
    acc = jnp.dot(h_ref[...], w_ref[...].astype(BF16), preferred_element_type=F32)

    @pl.when(j < XATTN_WIDTH // tn)
    def _():
        o_ref[...] = _rope_two_tiles(acc, cos_ref[...], sin_ref[...]).astype(o_ref.dtype)

    @pl.when(j >= XATTN_WIDTH // tn)
    def _():
        o_ref[...] = acc.astype(o_ref.dtype)


def _mem_proj(hm, w, cos, sin, tn):
    rows, d = hm.shape
    ncols = w.shape[1]
    tab = pl.BlockSpec((rows, LANES), lambda j: (0, 0))
    return pl.pallas_call(
        functools.partial(_mem_proj_kernel, tn=tn),
        grid=(ncols // tn,),
        in_specs=[pl.BlockSpec((rows, d), lambda j: (0, 0)),
                  pl.BlockSpec((d, tn), lambda j: (0, j)),
                  tab, tab],
        out_specs=pl.BlockSpec((rows, tn), lambda j: (0, j)),
        out_shape=jax.ShapeDtypeStruct((rows, ncols), BF16),
        compiler_params=_params(("parallel",)),
        name="mem_proj",
    )(hm, w, cos, sin)


MAX_SINGLE_ACCESS_STRIDE = 4


def _split_stride(dil):
    if dil <= MAX_SINGLE_ACCESS_STRIDE:
        return dil, 1
    assert dil % MAX_SINGLE_ACCESS_STRIDE == 0
    f2 = dil // MAX_SINGLE_ACCESS_STRIDE
    assert f2 <= MAX_SINGLE_ACCESS_STRIDE
    return MAX_SINGLE_ACCESS_STRIDE, f2


def _dilated_attn_kernel(q0, q1, q2, k0, k1, k2, v0, v1, v2, z0, z1, z2, y_ref,
                         stage_ref, mid_ref, sub_ref, osub_ref, lsub_ref, onat_ref, lnat_ref,
                         *, seq):
    q_refs, k_refs, v_refs = (q0, q1, q2), (k0, k1, k2), (v0, v1, v2)
    z_refs = (z0, z1, z2)
    nblk = seq // BLOCK
    qi = lax.broadcasted_iota(jnp.int32, (BLOCK, 2 * BLOCK), 0)
    kk = lax.broadcasted_iota(jnp.int32, (BLOCK, 2 * BLOCK), 1)
    dist = qi + BLOCK - kk
    band = (dist >= 0) & (dist <= BLOCK)
    causal = (lax.broadcasted_iota(jnp.int32, (BLOCK, BLOCK), 1)
              <= lax.broadcasted_iota(jnp.int32, (BLOCK, BLOCK), 0))

    rows_of = []
    for g, (window, dil) in enumerate(DIL_GROUPS):
        sub_len = seq // dil
        if dil == 1:
            def group_rows(which, start, size, g=g):
                return (q_refs, k_refs, v_refs)[which][g][0, pl.ds(start, size), :]
        else:
            slab = 3 * (g - 1)
            f1, f2 = _split_stride(dil)
            len1 = seq // f1
            for which, refs in enumerate((q_refs, k_refs, v_refs)):
                stage_ref[slab + which] = refs[g][0].astype(F32)
                if f2 == 1:
                    for r in range(dil):
                        sub_ref[slab + which, r * sub_len:(r + 1) * sub_len, :] = (
                            stage_ref[slab + which, pl.ds(r, sub_len, stride=dil), :].astype(BF16))
                    continue
                for r1 in range(f1):
                    mid_ref[which, r1 * len1:(r1 + 1) * len1, :] = (
                        stage_ref[slab + which, pl.ds(r1, len1, stride=f1), :])
                for r1 in range(f1):
                    for r2 in range(f2):
                        r = r2 * f1 + r1
                        sub_ref[slab + which, r * sub_len:(r + 1) * sub_len, :] = (
                            mid_ref[which, pl.ds(r1 * len1 + r2, sub_len, stride=f2), :]
                            .astype(BF16))

            def group_rows(which, start, size, slab=slab):
                return sub_ref[slab + which, pl.ds(start, size), :]
        rows_of.append(group_rows)

    def attend(g, start, first):
        q = rows_of[g](0, start, BLOCK)
        if first:
            k = rows_of[g](1, start, BLOCK)
            v = rows_of[g](2, start, BLOCK)
            mask = causal
        else:
            k = rows_of[g](1, start - BLOCK, 2 * BLOCK)
            v = rows_of[g](2, start - BLOCK, 2 * BLOCK)
            mask = band
        s = lax.dot_general(q, k, (((1,), (1,)), ((), ())), preferred_element_type=F32)
        s = jnp.where(mask, s, NEG_INF)
        m = jnp.max(s, axis=-1, keepdims=True)
        p = jnp.exp(s - m)
        l = jnp.sum(p, axis=-1, keepdims=True)
        o = jnp.dot(p.astype(BF16), v, preferred_element_type=F32) * (1.0 / l)
        lse = jnp.broadcast_to(m + jnp.log(l), (BLOCK, HEAD_DIM))
        if DIL_GROUPS[g][1] == 1:
            onat_ref[g, pl.ds(start, BLOCK), :] = o
            lnat_ref[g, pl.ds(start, BLOCK), :] = lse
        else:
            osub_ref[g - 1, pl.ds(start, BLOCK), :] = o
            lsub_ref[g - 1, pl.ds(start, BLOCK), :] = lse

    for c in range(nblk):
        for g, (window, dil) in enumerate(DIL_GROUPS):
            attend(g, c * BLOCK, c % (seq // dil // BLOCK) == 0)

    for g, (window, dil) in enumerate(DIL_GROUPS):
        sub_len = seq // dil
        if dil == 1:
            continue
        f1, f2 = _split_stride(dil)
        len1 = seq // f1
        for idx, (nat_ref, subseq_ref) in enumerate(((onat_ref, osub_ref), (lnat_ref, lsub_ref))):
            if f2 == 1:
                for r in range(dil):
                    src = slice(r * sub_len, (r + 1) * sub_len)
                    nat_ref[g, pl.ds(r, sub_len, stride=dil), :] = subseq_ref[g - 1, src, :]
                continue
            mid = 3 + idx
            for r1 in range(f1):
                for r2 in range(f2):
                    r = r2 * f1 + r1
                    mid_ref[mid, pl.ds(r1 * len1 + r2, sub_len, stride=f2), :] = (
                        subseq_ref[g - 1, r * sub_len:(r + 1) * sub_len, :])
            for r1 in range(f1):
                nat_ref[g, pl.ds(r1, len1, stride=f1), :] = mid_ref[mid, r1 * len1:(r1 + 1) * len1, :]

    chunk = 2 * BLOCK
    for t in range(seq // chunk):
        rows = slice(t * chunk, (t + 1) * chunk)
        lse = [lnat_ref[g, rows, :] for g in range(len(DIL_GROUPS))]
        mx = jnp.maximum(jnp.maximum(lse[0], lse[1]), lse[2])
        e = [jnp.exp(x - mx) for x in lse]
        inv = 1.0 / (e[0] + e[1] + e[2])
        for g in range(len(DIL_GROUPS)):
            y_ref[0, rows, g * HEAD_DIM:(g + 1) * HEAD_DIM] = (
                onat_ref[g, rows, :] * (e[g] * inv)
                * z_refs[g][0, rows, :].astype(F32)).astype(y_ref.dtype)


def _attn_out_column(g, hg):
    return (hg * len(DIL_GROUPS) + g) * HEAD_DIM


def _dilated_attn(q_src, k_src, v_src, z_src):
    sources = (q_src, k_src, v_src, z_src)
    b, s, _ = q_src[0].shape
    n_groups = len(DIL_GROUPS)
    n_regrouped = n_groups - 1
    assert sum(_split_stride(dil)[1] > 1 for _, dil in DIL_GROUPS) <= 1

    def head_spec(off, g):
        first = (off + g * GROUP_WIDTH) // HEAD_DIM
        return pl.BlockSpec((1, s, HEAD_DIM), lambda bi, hg: (bi, 0, first + hg))

    in_specs = [head_spec(off, g) for _, off in sources for g in range(n_groups)]
    operands = [arr for arr, _ in sources for g in range(n_groups)]
    return pl.pallas_call(
        functools.partial(_dilated_attn_kernel, seq=s),
        grid=(b, HEADS_PER_DIL),
        in_specs=in_specs,
        out_specs=pl.BlockSpec((1, s, n_groups * HEAD_DIM), lambda bi, hg: (bi, 0, hg)),
        out_shape=jax.ShapeDtypeStruct((b, s, ATTN_WIDTH), BF16),
        scratch_shapes=[pltpu.VMEM((3 * n_regrouped, s, HEAD_DIM), F32),
                        pltpu.VMEM((5, s, HEAD_DIM), F32),
                        pltpu.VMEM((3 * n_regrouped, s, HEAD_DIM), BF16),
                        pltpu.VMEM((n_regrouped, s, HEAD_DIM), F32),
                        pltpu.VMEM((n_regrouped, s, HEAD_DIM), F32),
                        pltpu.VMEM((n_groups, s, HEAD_DIM), F32),
                        pltpu.VMEM((n_groups, s, HEAD_DIM), F32)],
        compiler_params=_params(("parallel", "parallel")),
        name="dilated_attn",
    )(*operands)


def _xattn_kernel(q_ref, z_ref, mk_ref, mv_ref, y_ref):
    for h in range(N_XATTN_HEADS):
        cols = slice(h * XATTN_HEAD_DIM, (h + 1) * XATTN_HEAD_DIM)
        s = lax.dot_general(q_ref[0, :, cols], mk_ref[0, :, cols],
                            (((1,), (1,)), ((), ())), preferred_element_type=F32)
        m = jnp.max(s, axis=-1, keepdims=True)
        p = jnp.exp(s - m)
        l = jnp.sum(p, axis=-1, keepdims=True)
        o = jnp.dot(p.astype(BF16), mv_ref[0, :, cols], preferred_element_type=F32)
        y_ref[0, :, cols] = (o * (1.0 / l) * z_ref[0, :, cols].astype(F32)).astype(y_ref.dtype)


def _xattn(q_src, z_src, mkv3, tm):
    b, s, _ = q_src[0].shape
    m_len = mkv3.shape[1]
    q_off, z_off = q_src[1], z_src[1]
    assert q_off % XATTN_WIDTH == 0 and z_off % XATTN_WIDTH == 0
    return pl.pallas_call(
        _xattn_kernel,
        grid=(b, s // tm),
        in_specs=[pl.BlockSpec((1, tm, XATTN_WIDTH), lambda bi, i: (bi, i, q_off // XATTN_WIDTH)),
                  pl.BlockSpec((1, tm, XATTN_WIDTH), lambda bi, i: (bi, i, z_off // XATTN_WIDTH)),
                  pl.BlockSpec((1, m_len, XATTN_WIDTH), lambda bi, i: (bi, 0, 0)),
                  pl.BlockSpec((1, m_len, XATTN_WIDTH), lambda bi, i: (bi, 0, 1))],
        out_specs=pl.BlockSpec((1, tm, XATTN_WIDTH), lambda bi, i: (bi, i, 0)),
        out_shape=jax.ShapeDtypeStruct((b, s, XATTN_WIDTH), BF16),
        compiler_params=_params(("parallel", "parallel")),
        name="mem_xattn",
    )(q_src[0], z_src[0], mkv3, mkv3)


def _out_proj_kernel(*refs, ni, d, n_mix):
    y_refs, w_refs = refs[:n_mix], refs[n_mix:2 * n_mix]
    x_ref, g_ref, o_ref, y2_ref, ssq_ref, scale_ref = refs[2 * n_mix:]
    i = pl.program_id(0)
    j = pl.program_id(1)

    @pl.when(j == 0)
    def _():
        @pl.when(i > 0)
        def _():
            scale_ref[...] = lax.rsqrt(ssq_ref[...] * (1.0 / d) + EPS)
        ssq_ref[...] = jnp.zeros_like(ssq_ref)

    def finish_previous_tile():
        o_ref[...] = x_ref[...] + y2_ref[j] * scale_ref[...] * g_ref[...]

    def multiply_this_tile():
        y2 = jnp.dot(y_refs[0][...], w_refs[0][...], preferred_element_type=F32)
        for y_ref, w_ref in zip(y_refs[1:], w_refs[1:]):
            y2 = y2 + jnp.dot(y_ref[...], w_ref[...], preferred_element_type=F32)
        y2_ref[j] = y2
        ssq_ref[...] += jnp.sum(y2 * y2, axis=-1, keepdims=True)

    @pl.when(i == 0)
    def _():
        multiply_this_tile()

    @pl.when((i > 0) & (i < ni))
    def _():
        finish_previous_tile()
        multiply_this_tile()

    @pl.when(i == ni)
    def _():
        finish_previous_tile()


def _out_proj(ys, w_bf, x2d, gain, tm, tn):
    rows, d = x2d.shape
    nj = d // tn
    ni = rows // tm

    def y_spec(width):
        return pl.BlockSpec((tm, width), lambda i, j: (jnp.minimum(i, ni - 1), 0))

    def w_spec(width, row_off):
        assert row_off % width == 0
        return pl.BlockSpec((width, tn),
                            lambda i, j: (row_off // width, jnp.where(i == ni, nj - 1, j)))

    io_spec = pl.BlockSpec((tm, tn),
                           lambda i, j: (jnp.maximum(i - 1, 0), jnp.where(i == 0, 0, j)))
    widths = [y.shape[1] for y in ys]
    offsets = [sum(widths[:k]) for k in range(len(widths))]
    return pl.pallas_call(
        functools.partial(_out_proj_kernel, ni=ni, d=d, n_mix=len(ys)),
        grid=(ni + 1, nj),
        in_specs=([y_spec(w) for w in widths]
                  + [w_spec(w, off) for w, off in zip(widths, offsets)]
                  + [io_spec, pl.BlockSpec((1, tn), lambda i, j: (0, j))]),
        out_specs=io_spec,
        out_shape=jax.ShapeDtypeStruct((rows, d), F32),
        scratch_shapes=[pltpu.VMEM((nj, tm, tn), F32),
                        pltpu.VMEM((tm, 1), F32),
                        pltpu.VMEM((tm, 1), F32)],
        compiler_params=_params(("arbitrary", "arbitrary")),
        name="out_proj",
    )(*ys, *([w_bf] * len(ys)), x2d, gain.reshape(1, d))


def _rope_tables(pos, half):
    inv = 1.0 / (ROPE_THETA ** (np.arange(half, dtype=np.float64) / half))
    ang = np.asarray(pos, dtype=np.float64)[:, None] * inv[None, :]
    return np.cos(ang), np.sin(ang)


def kernel(x, mem, pre_norm, w_in, conv_w, mem_norm, w_mem_kv, w_out, post_norm):
    b, s, d = x.shape
    m_len = mem.shape[1]
    depth = w_in.shape[0]
    assert all(window // dil == BLOCK for window, dil in DIL_GROUPS)
    pos = np.arange(s)
    cos_a, sin_a = _rope_tables(pos, HEAD_DIM // 2)
    cosa = np.concatenate([cos_a, cos_a], axis=-1)
    sina = np.concatenate([-sin_a, sin_a], axis=-1)
    a_scale = HEAD_DIM ** -0.5
    cos_qkv = jnp.asarray(np.stack([cosa * a_scale, cosa, np.ones_like(cosa)]), F32)
    sin_qkv = jnp.asarray(np.stack([sina * a_scale, sina, np.zeros_like(sina)]), F32)
    cosx, sinx = _rope_tables(pos + m_len, XATTN_HEAD_DIM // 2)
    x_scale = XATTN_HEAD_DIM ** -0.5
    cos_qx = jnp.asarray((cosx * x_scale)[None], F32)
    sin_qx = jnp.asarray((sinx * x_scale)[None], F32)
    cos_m, sin_m = _rope_tables(np.arange(m_len), XATTN_HEAD_DIM // 2)
    cosm = jnp.asarray(np.tile(cos_m, (b, 1)), F32)
    sinm = jnp.asarray(np.tile(sin_m, (b, 1)), F32)

    tm, tn = 1024, 512

    def tile_of(off):
        assert off % tn == 0
        return off // tn

    attn_tiles, x_tiles = ATTN_WIDTH // tn, XATTN_WIDTH // tn
    tiles_per_seq = s // tm

    def view(t):
        return t.reshape(b, s, t.shape[-1])

    def rope_tables(cos, sin, kind):
        def index(col_tile, row_tile):
            return kind(col_tile), row_tile % tiles_per_seq, 0
        return [(cos, (1, tm, LANES), index), (sin, (1, tm, LANES), index)]

    def lane_slab(off):
        assert off % LANES == 0
        return lambda j: off // LANES + j

    def place_w_out_rows(k, block_rows):
        assert HEAD_DIM % block_rows == 0
        row = k * block_rows
        head, within = row // HEAD_DIM, row % HEAD_DIM
        moved = _attn_out_column(head // HEADS_PER_DIL, head % HEADS_PER_DIL) + within
        return jnp.where(row < ATTN_WIDTH, moved, row) // block_rows

    for layer in range(depth):
        x2d = x.reshape(b * s, d)
        w = w_in[layer]
        common = dict(tm=tm, tn=tn)
        w_qx_bf = _cast_columns_bf16(w, OFF_QX, XATTN_WIDTH, tr=1024)
        h, qx = _norm_project(x2d, pre_norm[layer], w_qx_bf, cos_qx, sin_qx, seq=s, tm=512)
        qx = view(qx)

        qkv = view(_in_proj(h, w, [lambda j: tile_of(OFF_QA) + j], 3 * attn_tiles,
                            _elementwise_finish(_rope_half_lane),
                            rope_tables(cos_qkv, sin_qkv, lambda j: j // attn_tiles),
                            name="in_proj_qkv", w_pieces=2, **common))
        y_conv, w_out_bf = _in_proj(
            h, w, [lane_slab(OFF_UC), lane_slab(OFF_BC), lane_slab(OFF_CC), lane_slab(OFF_ZC)],
            CONV_WIDTH // LANES,
            functools.partial(_gated_conv_finish, tiles_per_seq=tiles_per_seq),
            [(conv_w[layer], (CONV_K, LANES), lambda col_tile, row_tile: (0, col_tile))],
            out_tn=LANES, scratch=[pltpu.VMEM((CONV_HALO, LANES), F32)],
            rider=(lambda block: block, w_out[layer], [], place_w_out_rows),
            name="in_proj_conv", w_pieces=1, **common)
        gates, hm = _in_proj(
            h, w, [lambda j: jnp.where(j < x_tiles, tile_of(OFF_ZX) + j,
                                       tile_of(OFF_ZA) - x_tiles + j)],
            x_tiles + attn_tiles, _elementwise_finish(_silu), [], name="in_proj_gates",
            rider=(_rms_normalize, mem.reshape(b * m_len, d), [mem_norm[layer].reshape(1, d)],
                   None),
            w_pieces=2, **common)
        gates = view(gates)
        mkv = _mem_proj(hm, w_mem_kv[layer], cosm, sinm, tn=512)

        y_attn = _dilated_attn((qkv, 0), (qkv, ATTN_WIDTH), (qkv, 2 * ATTN_WIDTH),
                               (gates, XATTN_WIDTH)).reshape(b * s, ATTN_WIDTH)
        y_x = _xattn((qx, 0), (gates, 0), mkv.reshape(b, m_len, 2 * XATTN_WIDTH), tm=512)
        y_x = y_x.reshape(b * s, XATTN_WIDTH)

        out = _out_proj([y_attn, y_conv, y_x], w_out_bf, x2d, post_norm[layer],
                        tm=1024, tn=512)
        x = out.reshape(b, s, d)
    return x
```

```python
import functools

import jax
import jax.numpy as jnp
import numpy as np
from jax import lax
from jax.experimental import pallas as pl
from jax.experimental.pallas import tpu as pltpu

D_MODEL = 4096
MEM_LEN = 256
HEAD_DIM = 128
DIL_GROUPS = ((128, 1), (512, 4), (2048, 16))
ATTN_WIDTH = 3 * D_MODEL // 8
CONV_WIDTH = 3 * D_MODEL // 8
XATTN_WIDTH = D_MODEL // 4
N_XATTN_HEADS = 4
XATTN_HEAD_DIM = XATTN_WIDTH // N_XATTN_HEADS
HEADS_PER_DIL = 4
GROUP_WIDTH = HEADS_PER_DIL * HEAD_DIM
CONV_K = 3
BLOCK = 128
ROPE_THETA = 10000.0
EPS = 1e-6
NEG_INF = -1e30

OFF_QA = 0
OFF_KA = OFF_QA + ATTN_WIDTH
OFF_VA = OFF_KA + ATTN_WIDTH
OFF_ZA = OFF_VA + ATTN_WIDTH
OFF_UC = OFF_ZA + ATTN_WIDTH
OFF_BC = OFF_UC + CONV_WIDTH
OFF_CC = OFF_BC + CONV_WIDTH
OFF_ZC = OFF_CC + CONV_WIDTH
OFF_QX = OFF_ZC + CONV_WIDTH
OFF_ZX = OFF_QX + XATTN_WIDTH

V7X_VMEM_LIMIT_BYTES = 56 * 1024 * 1024
LANES = 128
W_PIECES = 4
EPILOGUE_ROWS = 128
CONV_HALO = 8
NORM_ROWS = 64

BF16 = jnp.bfloat16
F32 = jnp.float32


def _params(semantics):
    return pltpu.CompilerParams(dimension_semantics=semantics,
                                vmem_limit_bytes=V7X_VMEM_LIMIT_BYTES)


def _rms_normalize(x, gain):
    ms = jnp.mean(x * x, axis=-1, keepdims=True)
    return x * lax.rsqrt(ms + EPS) * gain


def _cast_kernel(x_ref, o_ref):
    o_ref[...] = x_ref[...].astype(o_ref.dtype)


def _cast_columns_bf16(w, col_off, width, tr):
    rows = w.shape[0]
    assert col_off % width == 0
    return pl.pallas_call(
        _cast_kernel,
        grid=(rows // tr,),
        in_specs=[pl.BlockSpec((tr, width), lambda i: (i, col_off // width))],
        out_specs=pl.BlockSpec((tr, width), lambda i: (i, 0)),
        out_shape=jax.ShapeDtypeStruct((rows, width), BF16),
        compiler_params=_params(("parallel",)),
        name="cast_w_qx",
    )(w)


def _norm_project_kernel(x_ref, g_ref, w_ref, cos_ref, sin_ref, h_ref, o_ref, acc_ref):
    tm = x_ref.shape[0]
    for r0 in range(0, tm, NORM_ROWS):
        rows = slice(r0, r0 + NORM_ROWS)
        h_ref[rows, :] = _rms_normalize(x_ref[rows, :], g_ref[...]).astype(h_ref.dtype)
    acc_ref[...] = jnp.dot(h_ref[...], w_ref[...], preferred_element_type=F32)
    for r0 in range(0, tm, EPILOGUE_ROWS):
        rows = slice(r0, r0 + EPILOGUE_ROWS)
        o_ref[rows, :] = _rope_two_tiles(acc_ref[rows, :], cos_ref[0, rows, :],
                                         sin_ref[0, rows, :]).astype(o_ref.dtype)


def _norm_project(x2d, gain, w_bf, cos, sin, seq, tm):
    rows, d = x2d.shape
    width = w_bf.shape[1]
    tiles_per_seq = seq // tm
    table = pl.BlockSpec((1, tm, LANES), lambda i: (0, i % tiles_per_seq, 0))
    return pl.pallas_call(
        _norm_project_kernel,
        grid=(rows // tm,),
        in_specs=[pl.BlockSpec((tm, d), lambda i: (i, 0)),
                  pl.BlockSpec((1, d), lambda i: (0, 0)),
                  pl.BlockSpec((d, width), lambda i: (0, 0)),
                  table, table],
        out_specs=[pl.BlockSpec((tm, d), lambda i: (i, 0)),
                   pl.BlockSpec((tm, width), lambda i: (i, 0))],
        out_shape=[jax.ShapeDtypeStruct((rows, d), BF16),
                   jax.ShapeDtypeStruct((rows, width), BF16)],
        scratch_shapes=[pltpu.VMEM((tm, width), F32)],
        compiler_params=_params(("arbitrary",)),
        name="norm_project_qx",
    )(x2d, gain.reshape(1, d), w_bf, cos, sin)


def _silu(z):
    half = 0.5 * z
    return half + half * jnp.tanh(half)


def _rope_half_lane(acc, cos, sin_signed):
    outs = []
    for h in range(acc.shape[1] // HEAD_DIM):
        xh = acc[:, h * HEAD_DIM:(h + 1) * HEAD_DIM]
        outs.append(xh * cos + pltpu.roll(xh, HEAD_DIM // 2, axis=1) * sin_signed)
    return jnp.concatenate(outs, axis=1)


def _rope_two_tiles(acc, cos, sin):
    outs = []
    for h in range(acc.shape[1] // XATTN_HEAD_DIM):
        x1 = acc[:, h * XATTN_HEAD_DIM:h * XATTN_HEAD_DIM + LANES]
        x2 = acc[:, h * XATTN_HEAD_DIM + LANES:(h + 1) * XATTN_HEAD_DIM]
        outs.append(x1 * cos - x2 * sin)
        outs.append(x2 * cos + x1 * sin)
    return jnp.concatenate(outs, axis=1)


def _elementwise_finish(epilogue):
    def finish(acc_ref, slot, o_ref, aux_refs, scratch_refs, row_tile):
        for r0 in range(0, o_ref.shape[0], EPILOGUE_ROWS):
            rows = slice(r0, r0 + EPILOGUE_ROWS)
            o_ref[rows, :] = epilogue(acc_ref[slot, rows, :],
                                      *[r[0, rows, :] for r in aux_refs]).astype(o_ref.dtype)
    return finish


def _gated_conv_finish(acc_ref, slot, o_ref, aux_refs, scratch_refs, row_tile, *, tiles_per_seq):
    (w_ref,), (carry_ref,) = aux_refs, scratch_refs
    tm = o_ref.shape[0]
    u_cols, b_cols, c_cols, z_cols = (slice(k * LANES, (k + 1) * LANES) for k in range(4))
    w = w_ref[...]
    tail = jnp.where(row_tile % tiles_per_seq == 0, 0.0, carry_ref[...])
    for r0 in range(0, tm, EPILOGUE_ROWS):
        rows = slice(r0, r0 + EPILOGUE_ROWS)
        cu = acc_ref[slot, rows, c_cols] * acc_ref[slot, rows, u_cols]
        window = jnp.concatenate([tail, cu], axis=0)
        conv = cu * w[CONV_K - 1:CONV_K, :]
        for lag in range(1, CONV_K):
            shifted = pltpu.roll(window, lag, axis=0)[CONV_HALO:, :]
            conv = conv + shifted * w[CONV_K - 1 - lag:CONV_K - lag, :]
        o_ref[rows, :] = (acc_ref[slot, rows, b_cols] * conv
                          * _silu(acc_ref[slot, rows, z_cols])).astype(o_ref.dtype)
        tail = cu[EPILOGUE_ROWS - CONV_HALO:, :]
    carry_ref[...] = tail


def _in_proj_kernel(*refs, ni, n_steps, n_w, w_pieces, n_aux, n_scratch, finish_fn, rider_fn,
                    rider_steps):
    h_ref = refs[0]
    w_refs = refs[1:1 + n_w]
    aux_refs = refs[1 + n_w:1 + n_w + n_aux]
    scratch_refs = refs[len(refs) - n_scratch:] if n_scratch else ()
    rest = refs[1 + n_w + n_aux:len(refs) - n_scratch]
    if rider_fn is not None:
        rider_in_ref, *rider_row_refs = rest[:-4]
        o_ref, rider_out_ref, wbf_ref, acc_ref = rest[-4:]
    else:
        o_ref, wbf_ref, acc_ref = rest
    t = pl.program_id(0)
    piece_rows = wbf_ref.shape[0] // w_pieces
    slab_cols = wbf_ref.shape[1] // (n_w // w_pieces)

    @pl.when((t % ni == 0) & (t < n_steps))
    def _():
        for k, w_ref in enumerate(w_refs):
            slab, p = divmod(k, w_pieces)
            wbf_ref[p * piece_rows:(p + 1) * piece_rows,
                    slab * slab_cols:(slab + 1) * slab_cols] = w_ref[...].astype(BF16)

    if rider_fn is not None:
        @pl.when(t < rider_steps)
        def _():
            rider_out_ref[...] = rider_fn(
                rider_in_ref[...], *[r[...] for r in rider_row_refs]).astype(rider_out_ref.dtype)

    def multiply(slot):
        acc_ref[slot] = jnp.dot(h_ref[...], wbf_ref[...], preferred_element_type=F32)

    def finish(slot):
        finish_fn(acc_ref, slot, o_ref, aux_refs, scratch_refs, ((t - 1) % ni))

    @pl.when(t == 0)
    def _():
        for ref in scratch_refs:
            ref[...] = jnp.zeros_like(ref)
        multiply(0)

    for parity in range(2):
        @pl.when((t > 0) & (t < n_steps) & (t % 2 == parity))
        def _(parity=parity):
            finish(1 - parity)
            multiply(parity)

    @pl.when(t == n_steps)
    def _():
        finish((n_steps - 1) % 2)


def _in_proj(h, w, col_slabs, n_col_tiles, finish_fn, aux, tm, tn, name, out_tn=None,
             scratch=(), rider=None, w_pieces=W_PIECES):
    rows, d = h.shape
    ni = rows // tm
    n_steps = n_col_tiles * ni
    out_tn = tn if out_tn is None else out_tn
    slab_cols = tn // len(col_slabs)
    rider_in_specs, rider_out_specs, rider_out_shapes, rider_operands = [], [], [], []
    rider_fn, rider_steps = None, 0
    if rider is not None:
        rider_fn, rider_matrix, rider_rows = rider
        r_rows, r_cols = rider_matrix.shape
        assert all(r.shape == (1, r_cols) for r in rider_rows)
        rider_steps = max(k for k in range(1, n_steps + 1) if r_rows % k == 0)
        rider_spec = pl.BlockSpec((r_rows // rider_steps, r_cols),
                                  lambda t: (jnp.minimum(t, rider_steps - 1), 0))
        rider_in_specs = [rider_spec] + [pl.BlockSpec((1, r_cols), lambda t: (0, 0))
                                         for _ in rider_rows]
        rider_out_specs = [rider_spec]
        rider_out_shapes = [jax.ShapeDtypeStruct(rider_matrix.shape, BF16)]
        rider_operands = [rider_matrix, *rider_rows]

    def cur(t):
        c = jnp.minimum(t, n_steps - 1)
        return c // ni, c % ni

    def prev(t):
        p = jnp.maximum(t - 1, 0)
        return p // ni, p % ni

    def aux_spec(block, index):
        return pl.BlockSpec(block, lambda t: index(*prev(t)))

    def w_piece_spec(col_slab, p):
        def index(t):
            j, i = cur(t)
            ahead = (i >= ni - w_pieces + p).astype(jnp.int32)
            return p, col_slab(jnp.minimum(j + ahead, n_col_tiles - 1))
        return pl.BlockSpec((d // w_pieces, slab_cols), index)

    assert ni >= w_pieces and d % w_pieces == 0
    w_specs = [w_piece_spec(col_slab, p) for col_slab in col_slabs for p in range(w_pieces)]
    results = pl.pallas_call(
        functools.partial(_in_proj_kernel, ni=ni, n_steps=n_steps, n_w=len(w_specs),
                          w_pieces=w_pieces, n_aux=len(aux), n_scratch=len(scratch),
                          finish_fn=finish_fn, rider_fn=rider_fn, rider_steps=rider_steps),
        grid=(n_steps + 1,),
        in_specs=[pl.BlockSpec((tm, d), lambda t: (cur(t)[1], 0))] + w_specs
                 + [aux_spec(block, index) for _, block, index in aux] + rider_in_specs,
        out_specs=[pl.BlockSpec((tm, out_tn), lambda t: (prev(t)[1], prev(t)[0]))]
                  + rider_out_specs,
        out_shape=[jax.ShapeDtypeStruct((rows, n_col_tiles * out_tn), BF16)] + rider_out_shapes,
        scratch_shapes=[pltpu.VMEM((d, tn), BF16), pltpu.VMEM((2, tm, tn), F32), *scratch],
        compiler_params=_params(("arbitrary",)),
        name=name,
    )(h, *([w] * len(w_specs)), *[arr for arr, _, _ in aux], *rider_operands)
    return results if rider is not None else results[0]


def _mem_proj_kernel(h_ref, w_ref, cos_ref, sin_ref, o_ref, *, tn):
    j = pl.program_id(0)
    acc = jnp.dot(h_ref[...], w_ref[...].astype(BF16), preferred_element_type=F32)

    @pl.when(j < XATTN_WIDTH // tn)
    def _():
        o_ref[...] = _rope_two_tiles(acc, cos_ref[...], sin_ref[...]).astype(o_ref.dtype)

    @pl.when(j >= XATTN_WIDTH // tn)
    def _():
        o_ref[...] = acc.astype(o_ref.dtype)


def _mem_proj(hm, w, cos, sin, tn):
    rows, d = hm.shape
    ncols = w.shape[1]
    tab = pl.BlockSpec((rows, LANES), lambda j: (0, 0))
    return pl.pallas_call(
        functools.partial(_mem_proj_kernel, tn=tn),
        grid=(ncols // tn,),
        in_specs=[pl.BlockSpec((rows, d), lambda j: (0, 0)),
                  pl.BlockSpec((d, tn), lambda j: (0, j)),
                  tab, tab],
        out_specs=pl.BlockSpec((rows, tn), lambda j: (0, j)),
        out_shape=jax.ShapeDtypeStruct((rows, ncols), BF16),
        compiler_params=_params(("parallel",)),
        name="mem_proj",
    )(hm, w, cos, sin)


MAX_SINGLE_ACCESS_STRIDE = 4


def _split_stride(dil):
    if dil <= MAX_SINGLE_ACCESS_STRIDE:
        return dil, 1
    assert dil % MAX_SINGLE_ACCESS_STRIDE == 0
    f2 = dil // MAX_SINGLE_ACCESS_STRIDE
    assert f2 <= MAX_SINGLE_ACCESS_STRIDE
    return MAX_SINGLE_ACCESS_STRIDE, f2


def _dilated_attn_kernel(q0, q1, q2, k0, k1, k2, v0, v1, v2, z0, z1, z2, y0, y1, y2,
                         stage_ref, mid_ref, sub_ref, osub_ref, lsub_ref, onat_ref, lnat_ref,
                         *, seq):
    q_refs, k_refs, v_refs = (q0, q1, q2), (k0, k1, k2), (v0, v1, v2)
    z_refs, y_refs = (z0, z1, z2), (y0, y1, y2)
    nblk = seq // BLOCK
    qi = lax.broadcasted_iota(jnp.int32, (BLOCK, 2 * BLOCK), 0)
    kk = lax.broadcasted_iota(jnp.int32, (BLOCK, 2 * BLOCK), 1)
    dist = qi + BLOCK - kk
    band = (dist >= 0) & (dist <= BLOCK)
    causal = (lax.broadcasted_iota(jnp.int32, (BLOCK, BLOCK), 1)
              <= lax.broadcasted_iota(jnp.int32, (BLOCK, BLOCK), 0))

    rows_of = []
    for g, (window, dil) in enumerate(DIL_GROUPS):
        sub_len = seq // dil
        if dil == 1:
            def group_rows(which, start, size, g=g):
                return (q_refs, k_refs, v_refs)[which][g][0, pl.ds(start, size), :]
        else:
            slab = 3 * (g - 1)
            f1, f2 = _split_stride(dil)
            len1 = seq // f1
            for which, refs in enumerate((q_refs, k_refs, v_refs)):
                stage_ref[slab + which] = refs[g][0].astype(F32)
                if f2 == 1:
                    for r in range(dil):
                        sub_ref[slab + which, r * sub_len:(r + 1) * sub_len, :] = (
                            stage_ref[slab + which, pl.ds(r, sub_len, stride=dil), :].astype(BF16))
                    continue
                for r1 in range(f1):
                    mid_ref[which, r1 * len1:(r1 + 1) * len1, :] = (
                        stage_ref[slab + which, pl.ds(r1, len1, stride=f1), :])
                for r1 in range(f1):
                    for r2 in range(f2):
                        r = r2 * f1 + r1
                        sub_ref[slab + which, r * sub_len:(r + 1) * sub_len, :] = (
                            mid_ref[which, pl.ds(r1 * len1 + r2, sub_len, stride=f2), :]
                            .astype(BF16))

            def group_rows(which, start, size, slab=slab):
                return sub_ref[slab + which, pl.ds(start, size), :]
        rows_of.append(group_rows)

    def attend(g, start, first):
        q = rows_of[g](0, start, BLOCK)
        if first:
            k = rows_of[g](1, start, BLOCK)
            v = rows_of[g](2, start, BLOCK)
            mask = causal
        else:
            k = rows_of[g](1, start - BLOCK, 2 * BLOCK)
            v = rows_of[g](2, start - BLOCK, 2 * BLOCK)
            mask = band
        s = lax.dot_general(q, k, (((1,), (1,)), ((), ())), preferred_element_type=F32)
        s = jnp.where(mask, s, NEG_INF)
        m = jnp.max(s, axis=-1, keepdims=True)
        p = jnp.exp(s - m)
        l = jnp.sum(p, axis=-1, keepdims=True)
        o = jnp.dot(p.astype(BF16), v, preferred_element_type=F32) * (1.0 / l)
        lse = jnp.broadcast_to(m + jnp.log(l), (BLOCK, HEAD_DIM))
        if DIL_GROUPS[g][1] == 1:
            onat_ref[g, pl.ds(start, BLOCK), :] = o
            lnat_ref[g, pl.ds(start, BLOCK), :] = lse
        else:
            osub_ref[g - 1, pl.ds(start, BLOCK), :] = o
            lsub_ref[g - 1, pl.ds(start, BLOCK), :] = lse

    for c in range(nblk):
        for g, (window, dil) in enumerate(DIL_GROUPS):
            attend(g, c * BLOCK, c % (seq // dil // BLOCK) == 0)

    for g, (window, dil) in enumerate(DIL_GROUPS):
        sub_len = seq // dil
        if dil == 1:
            continue
        f1, f2 = _split_stride(dil)
        len1 = seq // f1
        for idx, (nat_ref, subseq_ref) in enumerate(((onat_ref, osub_ref), (lnat_ref, lsub_ref))):
            if f2 == 1:
                for r in range(dil):
                    src = slice(r * sub_len, (r + 1) * sub_len)
                    nat_ref[g, pl.ds(r, sub_len, stride=dil), :] = subseq_ref[g - 1, src, :]
                continue
            mid = 3 + idx
            for r1 in range(f1):
                for r2 in range(f2):
                    r = r2 * f1 + r1
                    mid_ref[mid, pl.ds(r1 * len1 + r2, sub_len, stride=f2), :] = (
                        subseq_ref[g - 1, r * sub_len:(r + 1) * sub_len, :])
            for r1 in range(f1):
                nat_ref[g, pl.ds(r1, len1, stride=f1), :] = mid_ref[mid, r1 * len1:(r1 + 1) * len1, :]

    chunk = 2 * BLOCK
    for t in range(seq // chunk):
        rows = slice(t * chunk, (t + 1) * chunk)
        lse = [lnat_ref[g, rows, :] for g in range(len(DIL_GROUPS))]
        mx = jnp.maximum(jnp.maximum(lse[0], lse[1]), lse[2])
        e = [jnp.exp(x - mx) for x in lse]
        inv = 1.0 / (e[0] + e[1] + e[2])
        for g in range(len(DIL_GROUPS)):
            y_refs[g][0, rows, :] = (onat_ref[g, rows, :] * (e[g] * inv)
                                     * z_refs[g][0, rows, :].astype(F32)).astype(y_refs[g].dtype)


def _dilated_attn(q_src, k_src, v_src, z_src):
    sources = (q_src, k_src, v_src, z_src)
    b, s, _ = q_src[0].shape
    n_groups = len(DIL_GROUPS)
    n_regrouped = n_groups - 1
    assert sum(_split_stride(dil)[1] > 1 for _, dil in DIL_GROUPS) <= 1

    def head_spec(off, g):
        first = (off + g * GROUP_WIDTH) // HEAD_DIM
        return pl.BlockSpec((1, s, HEAD_DIM), lambda bi, hg: (bi, 0, first + hg))

    in_specs = [head_spec(off, g) for _, off in sources for g in range(n_groups)]
    operands = [arr for arr, _ in sources for g in range(n_groups)]
    out_spec = pl.BlockSpec((1, s, HEAD_DIM), lambda bi, hg: (bi, 0, hg))
    return pl.pallas_call(
        functools.partial(_dilated_attn_kernel, seq=s),
        grid=(b, HEADS_PER_DIL),
        in_specs=in_specs,
        out_specs=[out_spec] * n_groups,
        out_shape=[jax.ShapeDtypeStruct((b, s, GROUP_WIDTH), BF16)] * n_groups,
        scratch_shapes=[pltpu.VMEM((3 * n_regrouped, s, HEAD_DIM), F32),
                        pltpu.VMEM((5, s, HEAD_DIM), F32),
                        pltpu.VMEM((3 * n_regrouped, s, HEAD_DIM), BF16),
                        pltpu.VMEM((n_regrouped, s, HEAD_DIM), F32),
                        pltpu.VMEM((n_regrouped, s, HEAD_DIM), F32),
                        pltpu.VMEM((n_groups, s, HEAD_DIM), F32),
                        pltpu.VMEM((n_groups, s, HEAD_DIM), F32)],
        compiler_params=_params(("parallel", "parallel")),
        name="dilated_attn",
    )(*operands)


def _xattn_kernel(q_ref, z_ref, mk_ref, mv_ref, y_ref):
    for h in range(N_XATTN_HEADS):
        cols = slice(h * XATTN_HEAD_DIM, (h + 1) * XATTN_HEAD_DIM)
        s = lax.dot_general(q_ref[0, :, cols], mk_ref[0, :, cols],
                            (((1,), (1,)), ((), ())), preferred_element_type=F32)
        m = jnp.max(s, axis=-1, keepdims=True)
        p = jnp.exp(s - m)
        l = jnp.sum(p, axis=-1, keepdims=True)
        o = jnp.dot(p.astype(BF16), mv_ref[0, :, cols], preferred_element_type=F32)
        y_ref[0, :, cols] = (o * (1.0 / l) * z_ref[0, :, cols].astype(F32)).astype(y_ref.dtype)


def _xattn(q_src, z_src, mkv3, tm):
    b, s, _ = q_src[0].shape
    m_len = mkv3.shape[1]
    q_off, z_off = q_src[1], z_src[1]
    assert q_off % XATTN_WIDTH == 0 and z_off % XATTN_WIDTH == 0
    return pl.pallas_call(
        _xattn_kernel,
        grid=(b, s // tm),
        in_specs=[pl.BlockSpec((1, tm, XATTN_WIDTH), lambda bi, i: (bi, i, q_off // XATTN_WIDTH)),
                  pl.BlockSpec((1, tm, XATTN_WIDTH), lambda bi, i: (bi, i, z_off // XATTN_WIDTH)),
                  pl.BlockSpec((1, m_len, XATTN_WIDTH), lambda bi, i: (bi, 0, 0)),
                  pl.BlockSpec((1, m_len, XATTN_WIDTH), lambda bi, i: (bi, 0, 1))],
        out_specs=pl.BlockSpec((1, tm, XATTN_WIDTH), lambda bi, i: (bi, i, 0)),
        out_shape=jax.ShapeDtypeStruct((b, s, XATTN_WIDTH), BF16),
        compiler_params=_params(("parallel", "parallel")),
        name="mem_xattn",
    )(q_src[0], z_src[0], mkv3, mkv3)


def _out_proj_kernel(*refs, ni, d, n_mix):
    y_refs, w_refs = refs[:n_mix], refs[n_mix:2 * n_mix]
    x_ref, g_ref, o_ref, y2_ref, ssq_ref, scale_ref = refs[2 * n_mix:]
    i = pl.program_id(0)
    j = pl.program_id(1)

    @pl.when(j == 0)
    def _():
        @pl.when(i > 0)
        def _():
            scale_ref[...] = lax.rsqrt(ssq_ref[...] * (1.0 / d) + EPS)
        ssq_ref[...] = jnp.zeros_like(ssq_ref)

    def finish_previous_tile():
        o_ref[...] = x_ref[...] + y2_ref[j] * scale_ref[...] * g_ref[...]

    def multiply_this_tile():
        y2 = jnp.dot(y_refs[0][...], w_refs[0][...], preferred_element_type=F32)
        for y_ref, w_ref in zip(y_refs[1:], w_refs[1:]):
            y2 = y2 + jnp.dot(y_ref[...], w_ref[...], preferred_element_type=F32)
        y2_ref[j] = y2
        ssq_ref[...] += jnp.sum(y2 * y2, axis=-1, keepdims=True)

    @pl.when(i == 0)
    def _():
        multiply_this_tile()

    @pl.when((i > 0) & (i < ni))
    def _():
        finish_previous_tile()
        multiply_this_tile()

    @pl.when(i == ni)
    def _():
        finish_previous_tile()


def _out_proj(ys, w_bf, x2d, gain, tm, tn):
    rows, d = x2d.shape
    nj = d // tn
    ni = rows // tm

    def y_spec(width):
        return pl.BlockSpec((tm, width), lambda i, j: (jnp.minimum(i, ni - 1), 0))

    def w_spec(width, row_off):
        assert row_off % width == 0
        return pl.BlockSpec((width, tn),
                            lambda i, j: (row_off // width, jnp.where(i == ni, nj - 1, j)))

    io_spec = pl.BlockSpec((tm, tn),
                           lambda i, j: (jnp.maximum(i - 1, 0), jnp.where(i == 0, 0, j)))
    widths = [y.shape[1] for y in ys]
    offsets = [sum(widths[:k]) for k in range(len(widths))]
    return pl.pallas_call(
        functools.partial(_out_proj_kernel, ni=ni, d=d, n_mix=len(ys)),
        grid=(ni + 1, nj),
        in_specs=([y_spec(w) for w in widths]
                  + [w_spec(w, off) for w, off in zip(widths, offsets)]
                  + [io_spec, pl.BlockSpec((1, tn), lambda i, j: (0, j))]),
        out_specs=io_spec,
        out_shape=jax.ShapeDtypeStruct((rows, d), F32),
        scratch_shapes=[pltpu.VMEM((nj, tm, tn), F32),
                        pltpu.VMEM((tm, 1), F32),
                        pltpu.VMEM((tm, 1), F32)],
        compiler_params=_params(("arbitrary", "arbitrary")),
        name="out_proj",
    )(*ys, *([w_bf] * len(ys)), x2d, gain.reshape(1, d))


def _rope_tables(pos, half):
    inv = 1.0 / (ROPE_THETA ** (np.arange(half, dtype=np.float64) / half))
    ang = np.asarray(pos, dtype=np.float64)[:, None] * inv[None, :]
    return np.cos(ang), np.sin(ang)


def kernel(x, mem, pre_norm, w_in, conv_w, mem_norm, w_mem_kv, w_out, post_norm):
    b, s, d = x.shape
    m_len = mem.shape[1]
    depth = w_in.shape[0]
    assert all(window // dil == BLOCK for window, dil in DIL_GROUPS)
    pos = np.arange(s)
    cos_a, sin_a = _rope_tables(pos, HEAD_DIM // 2)
    cosa = np.concatenate([cos_a, cos_a], axis=-1)
    sina = np.concatenate([-sin_a, sin_a], axis=-1)
    a_scale = HEAD_DIM ** -0.5
    cos_qkv = jnp.asarray(np.stack([cosa * a_scale, cosa, np.ones_like(cosa)]), F32)
    sin_qkv = jnp.asarray(np.stack([sina * a_scale, sina, np.zeros_like(sina)]), F32)
    cosx, sinx = _rope_tables(pos + m_len, XATTN_HEAD_DIM // 2)
    x_scale = XATTN_HEAD_DIM ** -0.5
    cos_qx = jnp.asarray((cosx * x_scale)[None], F32)
    sin_qx = jnp.asarray((sinx * x_scale)[None], F32)
    cos_m, sin_m = _rope_tables(np.arange(m_len), XATTN_HEAD_DIM // 2)
    cosm = jnp.asarray(np.tile(cos_m, (b, 1)), F32)
    sinm = jnp.asarray(np.tile(sin_m, (b, 1)), F32)

    tm, tn = 1024, 512

    def tile_of(off):
        assert off % tn == 0
        return off // tn

    attn_tiles, x_tiles = ATTN_WIDTH // tn, XATTN_WIDTH // tn
    tiles_per_seq = s // tm

    def view(t):
        return t.reshape(b, s, t.shape[-1])

    def rope_tables(cos, sin, kind):
        def index(col_tile, row_tile):
            return kind(col_tile), row_tile % tiles_per_seq, 0
        return [(cos, (1, tm, LANES), index), (sin, (1, tm, LANES), index)]

    def lane_slab(off):
        assert off % LANES == 0
        return lambda j: off // LANES + j

    for layer in range(depth):
        x2d = x.reshape(b * s, d)
        w = w_in[layer]
        common = dict(tm=tm, tn=tn)
        w_qx_bf = _cast_columns_bf16(w, OFF_QX, XATTN_WIDTH, tr=1024)
        h, qx = _norm_project(x2d, pre_norm[layer], w_qx_bf, cos_qx, sin_qx, seq=s, tm=512)
        qx = view(qx)

        qkv = view(_in_proj(h, w, [lambda j: tile_of(OFF_QA) + j], 3 * attn_tiles,
                            _elementwise_finish(_rope_half_lane),
                            rope_tables(cos_qkv, sin_qkv, lambda j: j // attn_tiles),
                            name="in_proj_qkv", w_pieces=2, **common))
        y_conv, w_out_bf = _in_proj(
            h, w, [lane_slab(OFF_UC), lane_slab(OFF_BC), lane_slab(OFF_CC), lane_slab(OFF_ZC)],
            CONV_WIDTH // LANES,
            functools.partial(_gated_conv_finish, tiles_per_seq=tiles_per_seq),
            [(conv_w[layer], (CONV_K, LANES), lambda col_tile, row_tile: (0, col_tile))],
            out_tn=LANES, scratch=[pltpu.VMEM((CONV_HALO, LANES), F32)],
            rider=(lambda block: block, w_out[layer], []), name="in_proj_conv", w_pieces=1,
            **common)
        gates, hm = _in_proj(
            h, w, [lambda j: jnp.where(j < x_tiles, tile_of(OFF_ZX) + j,
                                       tile_of(OFF_ZA) - x_tiles + j)],
            x_tiles + attn_tiles, _elementwise_finish(_silu), [], name="in_proj_gates",
            rider=(_rms_normalize, mem.reshape(b * m_len, d), [mem_norm[layer].reshape(1, d)]),
            w_pieces=2, **common)
        gates = view(gates)
        mkv = _mem_proj(hm, w_mem_kv[layer], cosm, sinm, tn=512)

        y_groups = _dilated_attn((qkv, 0), (qkv, ATTN_WIDTH), (qkv, 2 * ATTN_WIDTH),
                                 (gates, XATTN_WIDTH))
        y_groups = [y.reshape(b * s, GROUP_WIDTH) for y in y_groups]
        y_x = _xattn((qx, 0), (gates, 0), mkv.reshape(b, m_len, 2 * XATTN_WIDTH), tm=1024)
        y_x = y_x.reshape(b * s, XATTN_WIDTH)

        out = _out_proj([*y_groups, y_conv, y_x], w_out_bf, x2d,
                        post_norm[layer], tm=1024, tn=512)
        x = out.reshape(b, s, d)
    return x
```

```python
import functools

import jax
import jax.numpy as jnp
import numpy as np
from jax import lax
from jax.experimental import pallas as pl
from jax.experimental.pallas import tpu as pltpu

D_MODEL = 4096
MEM_LEN = 256
HEAD_DIM = 128
DIL_GROUPS = ((128, 1), (512, 4), (2048, 16))
ATTN_WIDTH = 3 * D_MODEL // 8
CONV_WIDTH = 3 * D_MODEL // 8
XATTN_WIDTH = D_MODEL // 4
N_XATTN_HEADS = 4
XATTN_HEAD_DIM = XATTN_WIDTH // N_XATTN_HEADS
HEADS_PER_DIL = 4
GROUP_WIDTH = HEADS_PER_DIL * HEAD_DIM
CONV_K = 3
BLOCK = 128
ROPE_THETA = 10000.0
EPS = 1e-6
NEG_INF = -1e30

OFF_QA = 0
OFF_KA = OFF_QA + ATTN_WIDTH
OFF_VA = OFF_KA + ATTN_WIDTH
OFF_ZA = OFF_VA + ATTN_WIDTH
OFF_UC = OFF_ZA + ATTN_WIDTH
OFF_BC = OFF_UC + CONV_WIDTH
OFF_CC = OFF_BC + CONV_WIDTH
OFF_ZC = OFF_CC + CONV_WIDTH
OFF_QX = OFF_ZC + CONV_WIDTH
OFF_ZX = OFF_QX + XATTN_WIDTH

V7X_VMEM_LIMIT_BYTES = 56 * 1024 * 1024
LANES = 128
W_PIECES = 4
EPILOGUE_ROWS = 128
CONV_HALO = 8
NORM_ROWS = 64

BF16 = jnp.bfloat16
F32 = jnp.float32


def _params(semantics):
    return pltpu.CompilerParams(dimension_semantics=semantics,
                                vmem_limit_bytes=V7X_VMEM_LIMIT_BYTES)


def _rms_normalize(x, gain):
    ms = jnp.mean(x * x, axis=-1, keepdims=True)
    return x * lax.rsqrt(ms + EPS) * gain


def _cast_kernel(x_ref, o_ref):
    o_ref[...] = x_ref[...].astype(o_ref.dtype)


def _cast_columns_bf16(w, col_off, width, tr):
    rows = w.shape[0]
    assert col_off % width == 0
    return pl.pallas_call(
        _cast_kernel,
        grid=(rows // tr,),
        in_specs=[pl.BlockSpec((tr, width), lambda i: (i, col_off // width))],
        out_specs=pl.BlockSpec((tr, width), lambda i: (i, 0)),
        out_shape=jax.ShapeDtypeStruct((rows, width), BF16),
        compiler_params=_params(("parallel",)),
        name="cast_w_qx",
    )(w)


def _norm_project_kernel(x_ref, g_ref, w_ref, cos_ref, sin_ref, h_ref, o_ref, acc_ref):
    tm = x_ref.shape[0]
    for r0 in range(0, tm, NORM_ROWS):
        rows = slice(r0, r0 + NORM_ROWS)
        h_ref[rows, :] = _rms_normalize(x_ref[rows, :], g_ref[...]).astype(h_ref.dtype)
    acc_ref[...] = jnp.dot(h_ref[...], w_ref[...], preferred_element_type=F32)
    for r0 in range(0, tm, EPILOGUE_ROWS):
        rows = slice(r0, r0 + EPILOGUE_ROWS)
        o_ref[rows, :] = _rope_two_tiles(acc_ref[rows, :], cos_ref[0, rows, :],
                                         sin_ref[0, rows, :]).astype(o_ref.dtype)


def _norm_project(x2d, gain, w_bf, cos, sin, seq, tm):
    rows, d = x2d.shape
    width = w_bf.shape[1]
    tiles_per_seq = seq // tm
    table = pl.BlockSpec((1, tm, LANES), lambda i: (0, i % tiles_per_seq, 0))
    return pl.pallas_call(
        _norm_project_kernel,
        grid=(rows // tm,),
        in_specs=[pl.BlockSpec((tm, d), lambda i: (i, 0)),
                  pl.BlockSpec((1, d), lambda i: (0, 0)),
                  pl.BlockSpec((d, width), lambda i: (0, 0)),
                  table, table],
        out_specs=[pl.BlockSpec((tm, d), lambda i: (i, 0)),
                   pl.BlockSpec((tm, width), lambda i: (i, 0))],
        out_shape=[jax.ShapeDtypeStruct((rows, d), BF16),
                   jax.ShapeDtypeStruct((rows, width), BF16)],
        scratch_shapes=[pltpu.VMEM((tm, width), F32)],
        compiler_params=_params(("arbitrary",)),
        name="norm_project_qx",
    )(x2d, gain.reshape(1, d), w_bf, cos, sin)


def _silu(z):
    half = 0.5 * z
    return half + half * jnp.tanh(half)


def _rope_half_lane(acc, cos, sin_signed):
    outs = []
    for h in range(acc.shape[1] // HEAD_DIM):
        xh = acc[:, h * HEAD_DIM:(h + 1) * HEAD_DIM]
        outs.append(xh * cos + pltpu.roll(xh, HEAD_DIM // 2, axis=1) * sin_signed)
    return jnp.concatenate(outs, axis=1)


def _rope_two_tiles(acc, cos, sin):
    outs = []
    for h in range(acc.shape[1] // XATTN_HEAD_DIM):
        x1 = acc[:, h * XATTN_HEAD_DIM:h * XATTN_HEAD_DIM + LANES]
        x2 = acc[:, h * XATTN_HEAD_DIM + LANES:(h + 1) * XATTN_HEAD_DIM]
        outs.append(x1 * cos - x2 * sin)
        outs.append(x2 * cos + x1 * sin)
    return jnp.concatenate(outs, axis=1)


def _elementwise_finish(epilogue):
    def finish(acc_ref, slot, o_ref, aux_refs, scratch_refs, row_tile):
        for r0 in range(0, o_ref.shape[0], EPILOGUE_ROWS):
            rows = slice(r0, r0 + EPILOGUE_ROWS)
            o_ref[rows, :] = epilogue(acc_ref[slot, rows, :],
                                      *[r[0, rows, :] for r in aux_refs]).astype(o_ref.dtype)
    return finish


def _gated_conv_finish(acc_ref, slot, o_ref, aux_refs, scratch_refs, row_tile, *, tiles_per_seq):
    (w_ref,), (carry_ref,) = aux_refs, scratch_refs
    tm = o_ref.shape[0]
    u_cols, b_cols, c_cols, z_cols = (slice(k * LANES, (k + 1) * LANES) for k in range(4))
    w = w_ref[...]
    tail = jnp.where(row_tile % tiles_per_seq == 0, 0.0, carry_ref[...])
    for r0 in range(0, tm, EPILOGUE_ROWS):
        rows = slice(r0, r0 + EPILOGUE_ROWS)
        cu = acc_ref[slot, rows, c_cols] * acc_ref[slot, rows, u_cols]
        window = jnp.concatenate([tail, cu], axis=0)
        conv = cu * w[CONV_K - 1:CONV_K, :]
        for lag in range(1, CONV_K):
            shifted = pltpu.roll(window, lag, axis=0)[CONV_HALO:, :]
            conv = conv + shifted * w[CONV_K - 1 - lag:CONV_K - lag, :]
        o_ref[rows, :] = (acc_ref[slot, rows, b_cols] * conv
                          * _silu(acc_ref[slot, rows, z_cols])).astype(o_ref.dtype)
        tail = cu[EPILOGUE_ROWS - CONV_HALO:, :]
    carry_ref[...] = tail


def _in_proj_kernel(*refs, ni, n_steps, n_w, w_pieces, n_aux, n_scratch, finish_fn, rider_fn,
                    rider_steps):
    h_ref = refs[0]
    w_refs = refs[1:1 + n_w]
    aux_refs = refs[1 + n_w:1 + n_w + n_aux]
    scratch_refs = refs[len(refs) - n_scratch:] if n_scratch else ()
    rest = refs[1 + n_w + n_aux:len(refs) - n_scratch]
    if rider_fn is not None:
        rider_in_ref, *rider_row_refs = rest[:-4]
        o_ref, rider_out_ref, wbf_ref, acc_ref = rest[-4:]
    else:
        o_ref, wbf_ref, acc_ref = rest
    t = pl.program_id(0)
    piece_rows = wbf_ref.shape[0] // w_pieces
    slab_cols = wbf_ref.shape[1] // (n_w // w_pieces)

    @pl.when((t % ni == 0) & (t < n_steps))
    def _():
        for k, w_ref in enumerate(w_refs):
            slab, p = divmod(k, w_pieces)
            wbf_ref[p * piece_rows:(p + 1) * piece_rows,
                    slab * slab_cols:(slab + 1) * slab_cols] = w_ref[...].astype(BF16)

    if rider_fn is not None:
        @pl.when(t < rider_steps)
        def _():
            rider_out_ref[...] = rider_fn(
                rider_in_ref[...], *[r[...] for r in rider_row_refs]).astype(rider_out_ref.dtype)

    def multiply(slot):
        acc_ref[slot] = jnp.dot(h_ref[...], wbf_ref[...], preferred_element_type=F32)

    def finish(slot):
        finish_fn(acc_ref, slot, o_ref, aux_refs, scratch_refs, ((t - 1) % ni))

    @pl.when(t == 0)
    def _():
        for ref in scratch_refs:
            ref[...] = jnp.zeros_like(ref)
        multiply(0)

    for parity in range(2):
        @pl.when((t > 0) & (t < n_steps) & (t % 2 == parity))
        def _(parity=parity):
            finish(1 - parity)
            multiply(parity)

    @pl.when(t == n_steps)
    def _():
        finish((n_steps - 1) % 2)


def _in_proj(h, w, col_slabs, n_col_tiles, finish_fn, aux, tm, tn, name, out_tn=None,
             scratch=(), rider=None, w_pieces=W_PIECES):
    rows, d = h.shape
    ni = rows // tm
    n_steps = n_col_tiles * ni
    out_tn = tn if out_tn is None else out_tn
    slab_cols = tn // len(col_slabs)
    rider_in_specs, rider_out_specs, rider_out_shapes, rider_operands = [], [], [], []
    rider_fn, rider_steps = None, 0
    if rider is not None:
        rider_fn, rider_matrix, rider_rows = rider
        r_rows, r_cols = rider_matrix.shape
        assert all(r.shape == (1, r_cols) for r in rider_rows)
        rider_steps = max(k for k in range(1, n_steps + 1) if r_rows % k == 0)
        rider_spec = pl.BlockSpec((r_rows // rider_steps, r_cols),
                                  lambda t: (jnp.minimum(t, rider_steps - 1), 0))
        rider_in_specs = [rider_spec] + [pl.BlockSpec((1, r_cols), lambda t: (0, 0))
                                         for _ in rider_rows]
        rider_out_specs = [rider_spec]
        rider_out_shapes = [jax.ShapeDtypeStruct(rider_matrix.shape, BF16)]
        rider_operands = [rider_matrix, *rider_rows]

    def cur(t):
        c = jnp.minimum(t, n_steps - 1)
        return c // ni, c % ni

    def prev(t):
        p = jnp.maximum(t - 1, 0)
        return p // ni, p % ni

    def aux_spec(block, index):
        return pl.BlockSpec(block, lambda t: index(*prev(t)))

    def w_piece_spec(col_slab, p):
        def index(t):
            j, i = cur(t)
            ahead = (i >= ni - w_pieces + p).astype(jnp.int32)
            return p, col_slab(jnp.minimum(j + ahead, n_col_tiles - 1))
        return pl.BlockSpec((d // w_pieces, slab_cols), index)

    assert ni >= w_pieces and d % w_pieces == 0
    w_specs = [w_piece_spec(col_slab, p) for col_slab in col_slabs for p in range(w_pieces)]
    results = pl.pallas_call(
        functools.partial(_in_proj_kernel, ni=ni, n_steps=n_steps, n_w=len(w_specs),
                          w_pieces=w_pieces, n_aux=len(aux), n_scratch=len(scratch),
                          finish_fn=finish_fn, rider_fn=rider_fn, rider_steps=rider_steps),
        grid=(n_steps + 1,),
        in_specs=[pl.BlockSpec((tm, d), lambda t: (cur(t)[1], 0))] + w_specs
                 + [aux_spec(block, index) for _, block, index in aux] + rider_in_specs,
        out_specs=[pl.BlockSpec((tm, out_tn), lambda t: (prev(t)[1], prev(t)[0]))]
                  + rider_out_specs,
        out_shape=[jax.ShapeDtypeStruct((rows, n_col_tiles * out_tn), BF16)] + rider_out_shapes,
        scratch_shapes=[pltpu.VMEM((d, tn), BF16), pltpu.VMEM((2, tm, tn), F32), *scratch],
        compiler_params=_params(("arbitrary",)),
        name=name,
    )(h, *([w] * len(w_specs)), *[arr for arr, _, _ in aux], *rider_operands)
    return results if rider is not None else results[0]


def _mem_proj_kernel(h_ref, w_ref, cos_ref, sin_ref, o_ref, *, tn):
    j = pl.program_id(0)
    acc = jnp.dot(h_ref[...], w_ref[...].astype(BF16), preferred_element_type=F32)

    @pl.when(j < XATTN_WIDTH // tn)
    def _():
        o_ref[...] = _rope_two_tiles(acc, cos_ref[...], sin_ref[...]).astype(o_ref.dtype)

    @pl.when(j >= XATTN_WIDTH // tn)
    def _():
        o_ref[...] = acc.astype(o_ref.dtype)


def _mem_proj(hm, w, cos, sin, tn):
    rows, d = hm.shape
    ncols = w.shape[1]
    tab = pl.BlockSpec((rows, LANES), lambda j: (0, 0))
    return pl.pallas_call(
        functools.partial(_mem_proj_kernel, tn=tn),
        grid=(ncols // tn,),
        in_specs=[pl.BlockSpec((rows, d), lambda j: (0, 0)),
                  pl.BlockSpec((d, tn), lambda j: (0, j)),
                  tab, tab],
        out_specs=pl.BlockSpec((rows, tn), lambda j: (0, j)),
        out_shape=jax.ShapeDtypeStruct((rows, ncols), BF16),
        compiler_params=_params(("parallel",)),
        name="mem_proj",
    )(hm, w, cos, sin)


MAX_SINGLE_ACCESS_STRIDE = 4


def _split_stride(dil):
    if dil <= MAX_SINGLE_ACCESS_STRIDE:
        return dil, 1
    assert dil % MAX_SINGLE_ACCESS_STRIDE == 0
    f2 = dil // MAX_SINGLE_ACCESS_STRIDE
    assert f2 <= MAX_SINGLE_ACCESS_STRIDE
    return MAX_SINGLE_ACCESS_STRIDE, f2


def _dilated_attn_kernel(q0, q1, q2, k0, k1, k2, v0, v1, v2, z0, z1, z2, y0, y1, y2,
                         stage_ref, mid_ref, sub_ref, osub_ref, lsub_ref, onat_ref, lnat_ref,
                         *, seq):
    q_refs, k_refs, v_refs = (q0, q1, q2), (k0, k1, k2), (v0, v1, v2)
    z_refs, y_refs = (z0, z1, z2), (y0, y1, y2)
    nblk = seq // BLOCK
    qi = lax.broadcasted_iota(jnp.int32, (BLOCK, 2 * BLOCK), 0)
    kk = lax.broadcasted_iota(jnp.int32, (BLOCK, 2 * BLOCK), 1)
    dist = qi + BLOCK - kk
    band = (dist >= 0) & (dist <= BLOCK)
    causal = (lax.broadcasted_iota(jnp.int32, (BLOCK, BLOCK), 1)
              <= lax.broadcasted_iota(jnp.int32, (BLOCK, BLOCK), 0))

    rows_of = []
    for g, (window, dil) in enumerate(DIL_GROUPS):
        sub_len = seq // dil
        if dil == 1:
            def group_rows(which, start, size, g=g):
                return (q_refs, k_refs, v_refs)[which][g][0, pl.ds(start, size), :]
        else:
            slab = 3 * (g - 1)
            f1, f2 = _split_stride(dil)
            len1 = seq // f1
            for which, refs in enumerate((q_refs, k_refs, v_refs)):
                stage_ref[slab + which] = refs[g][0].astype(F32)
                if f2 == 1:
                    for r in range(dil):
                        sub_ref[slab + which, r * sub_len:(r + 1) * sub_len, :] = (
                            stage_ref[slab + which, pl.ds(r, sub_len, stride=dil), :].astype(BF16))
                    continue
                for r1 in range(f1):
                    mid_ref[which, r1 * len1:(r1 + 1) * len1, :] = (
                        stage_ref[slab + which, pl.ds(r1, len1, stride=f1), :])
                for r1 in range(f1):
                    for r2 in range(f2):
                        r = r2 * f1 + r1
                        sub_ref[slab + which, r * sub_len:(r + 1) * sub_len, :] = (
                            mid_ref[which, pl.ds(r1 * len1 + r2, sub_len, stride=f2), :]
                            .astype(BF16))

            def group_rows(which, start, size, slab=slab):
                return sub_ref[slab + which, pl.ds(start, size), :]
        rows_of.append(group_rows)

    def attend(g, start, first):
        q = rows_of[g](0, start, BLOCK)
        if first:
            k = rows_of[g](1, start, BLOCK)
            v = rows_of[g](2, start, BLOCK)
            mask = causal
        else:
            k = rows_of[g](1, start - BLOCK, 2 * BLOCK)
            v = rows_of[g](2, start - BLOCK, 2 * BLOCK)
            mask = band
        s = lax.dot_general(q, k, (((1,), (1,)), ((), ())), preferred_element_type=F32)
        s = jnp.where(mask, s, NEG_INF)
        m = jnp.max(s, axis=-1, keepdims=True)
        p = jnp.exp(s - m)
        l = jnp.sum(p, axis=-1, keepdims=True)
        o = jnp.dot(p.astype(BF16), v, preferred_element_type=F32) * (1.0 / l)
        lse = jnp.broadcast_to(m + jnp.log(l), (BLOCK, HEAD_DIM))
        if DIL_GROUPS[g][1] == 1:
            onat_ref[g, pl.ds(start, BLOCK), :] = o
            lnat_ref[g, pl.ds(start, BLOCK), :] = lse
        else:
            osub_ref[g - 1, pl.ds(start, BLOCK), :] = o
            lsub_ref[g - 1, pl.ds(start, BLOCK), :] = lse

    for c in range(nblk):
        for g, (window, dil) in enumerate(DIL_GROUPS):
            attend(g, c * BLOCK, c % (seq // dil // BLOCK) == 0)

    for g, (window, dil) in enumerate(DIL_GROUPS):
        sub_len = seq // dil
        if dil == 1:
            continue
        f1, f2 = _split_stride(dil)
        len1 = seq // f1
        for idx, (nat_ref, subseq_ref) in enumerate(((onat_ref, osub_ref), (lnat_ref, lsub_ref))):
            if f2 == 1:
                for r in range(dil):
                    src = slice(r * sub_len, (r + 1) * sub_len)
                    nat_ref[g, pl.ds(r, sub_len, stride=dil), :] = subseq_ref[g - 1, src, :]
                continue
            mid = 3 + idx
            for r1 in range(f1):
                for r2 in range(f2):
                    r = r2 * f1 + r1
                    mid_ref[mid, pl.ds(r1 * len1 + r2, sub_len, stride=f2), :] = (
                        subseq_ref[g - 1, r * sub_len:(r + 1) * sub_len, :])
            for r1 in range(f1):
                nat_ref[g, pl.ds(r1, len1, stride=f1), :] = mid_ref[mid, r1 * len1:(r1 + 1) * len1, :]

    chunk = 2 * BLOCK
    for t in range(seq // chunk):
        rows = slice(t * chunk, (t + 1) * chunk)
        lse = [lnat_ref[g, rows, :] for g in range(len(DIL_GROUPS))]
        mx = jnp.maximum(jnp.maximum(lse[0], lse[1]), lse[2])
        e = [jnp.exp(x - mx) for x in lse]
        inv = 1.0 / (e[0] + e[1] + e[2])
        for g in range(len(DIL_GROUPS)):
            y_refs[g][0, rows, :] = (onat_ref[g, rows, :] * (e[g] * inv)
                                     * z_refs[g][0, rows, :].astype(F32)).astype(y_refs[g].dtype)


def _dilated_attn(q_src, k_src, v_src, z_src):
    sources = (q_src, k_src, v_src, z_src)
    b, s, _ = q_src[0].shape
    n_groups = len(DIL_GROUPS)
    n_regrouped = n_groups - 1
    assert sum(_split_stride(dil)[1] > 1 for _, dil in DIL_GROUPS) <= 1

    def head_spec(off, g):
        first = (off + g * GROUP_WIDTH) // HEAD_DIM
        return pl.BlockSpec((1, s, HEAD_DIM), lambda bi, hg: (bi, 0, first + hg))

    in_specs = [head_spec(off, g) for _, off in sources for g in range(n_groups)]
    operands = [arr for arr, _ in sources for g in range(n_groups)]
    out_spec = pl.BlockSpec((1, s, HEAD_DIM), lambda bi, hg: (bi, 0, hg))
    return pl.pallas_call(
        functools.partial(_dilated_attn_kernel, seq=s),
        grid=(b, HEADS_PER_DIL),
        in_specs=in_specs,
        out_specs=[out_spec] * n_groups,
        out_shape=[jax.ShapeDtypeStruct((b, s, GROUP_WIDTH), BF16)] * n_groups,
        scratch_shapes=[pltpu.VMEM((3 * n_regrouped, s, HEAD_DIM), F32),
                        pltpu.VMEM((5, s, HEAD_DIM), F32),
                        pltpu.VMEM((3 * n_regrouped, s, HEAD_DIM), BF16),
                        pltpu.VMEM((n_regrouped, s, HEAD_DIM), F32),
                        pltpu.VMEM((n_regrouped, s, HEAD_DIM), F32),
                        pltpu.VMEM((n_groups, s, HEAD_DIM), F32),
                        pltpu.VMEM((n_groups, s, HEAD_DIM), F32)],
        compiler_params=_params(("parallel", "parallel")),
        name="dilated_attn",
    )(*operands)


def _xattn_kernel(q_ref, z_ref, mk_ref, mv_ref, y_ref):
    for h in range(N_XATTN_HEADS):
        cols = slice(h * XATTN_HEAD_DIM, (h + 1) * XATTN_HEAD_DIM)
        s = lax.dot_general(q_ref[0, :, cols], mk_ref[0, :, cols],
                            (((1,), (1,)), ((), ())), preferred_element_type=F32)
        m = jnp.max(s, axis=-1, keepdims=True)
        p = jnp.exp(s - m)
        l = jnp.sum(p, axis=-1, keepdims=True)
        o = jnp.dot(p.astype(BF16), mv_ref[0, :, cols], preferred_element_type=F32)
        y_ref[0, :, cols] = (o * (1.0 / l) * z_ref[0, :, cols].astype(F32)).astype(y_ref.dtype)


def _xattn(q_src, z_src, mkv3, tm):
    b, s, _ = q_src[0].shape
    m_len = mkv3.shape[1]
    q_off, z_off = q_src[1], z_src[1]
    assert q_off % XATTN_WIDTH == 0 and z_off % XATTN_WIDTH == 0
    return pl.pallas_call(
        _xattn_kernel,
        grid=(b, s // tm),
        in_specs=[pl.BlockSpec((1, tm, XATTN_WIDTH), lambda bi, i: (bi, i, q_off // XATTN_WIDTH)),
                  pl.BlockSpec((1, tm, XATTN_WIDTH), lambda bi, i: (bi, i, z_off // XATTN_WIDTH)),
                  pl.BlockSpec((1, m_len, XATTN_WIDTH), lambda bi, i: (bi, 0, 0)),
                  pl.BlockSpec((1, m_len, XATTN_WIDTH), lambda bi, i: (bi, 0, 1))],
        out_specs=pl.BlockSpec((1, tm, XATTN_WIDTH), lambda bi, i: (bi, i, 0)),
        out_shape=jax.ShapeDtypeStruct((b, s, XATTN_WIDTH), BF16),
        compiler_params=_params(("parallel", "parallel")),
        name="mem_xattn",
    )(q_src[0], z_src[0], mkv3, mkv3)


def _out_proj_kernel(*refs, ni, d, n_mix):
    y_refs, w_refs = refs[:n_mix], refs[n_mix:2 * n_mix]
    x_ref, g_ref, o_ref, y2_ref, ssq_ref, scale_ref = refs[2 * n_mix:]
    i = pl.program_id(0)
    j = pl.program_id(1)

    @pl.when(j == 0)
    def _():
        @pl.when(i > 0)
        def _():
            scale_ref[...] = lax.rsqrt(ssq_ref[...] * (1.0 / d) + EPS)
        ssq_ref[...] = jnp.zeros_like(ssq_ref)

    def finish_previous_tile():
        o_ref[...] = x_ref[...] + y2_ref[j] * scale_ref[...] * g_ref[...]

    def multiply_this_tile():
        y2 = jnp.dot(y_refs[0][...], w_refs[0][...], preferred_element_type=F32)
        for y_ref, w_ref in zip(y_refs[1:], w_refs[1:]):
            y2 = y2 + jnp.dot(y_ref[...], w_ref[...], preferred_element_type=F32)
        y2_ref[j] = y2
        ssq_ref[...] += jnp.sum(y2 * y2, axis=-1, keepdims=True)

    @pl.when(i == 0)
    def _():
        multiply_this_tile()

    @pl.when((i > 0) & (i < ni))
    def _():
        finish_previous_tile()
        multiply_this_tile()

    @pl.when(i == ni)
    def _():
        finish_previous_tile()


def _out_proj(ys, w_bf, x2d, gain, tm, tn):
    rows, d = x2d.shape
    nj = d // tn
    ni = rows // tm

    def y_spec(width):
        return pl.BlockSpec((tm, width), lambda i, j: (jnp.minimum(i, ni - 1), 0))

    def w_spec(width, row_off):
        assert row_off % width == 0
        return pl.BlockSpec((width, tn),
                            lambda i, j: (row_off // width, jnp.where(i == ni, nj - 1, j)))

    io_spec = pl.BlockSpec((tm, tn),
                           lambda i, j: (jnp.maximum(i - 1, 0), jnp.where(i == 0, 0, j)))
    widths = [y.shape[1] for y in ys]
    offsets = [sum(widths[:k]) for k in range(len(widths))]
    return pl.pallas_call(
        functools.partial(_out_proj_kernel, ni=ni, d=d, n_mix=len(ys)),
        grid=(ni + 1, nj),
        in_specs=([y_spec(w) for w in widths]
                  + [w_spec(w, off) for w, off in zip(widths, offsets)]
                  + [io_spec, pl.BlockSpec((1, tn), lambda i, j: (0, j))]),
        out_specs=io_spec,
        out_shape=jax.ShapeDtypeStruct((rows, d), F32),
        scratch_shapes=[pltpu.VMEM((nj, tm, tn), F32),
                        pltpu.VMEM((tm, 1), F32),
                        pltpu.VMEM((tm, 1), F32)],
        compiler_params=_params(("arbitrary", "arbitrary")),
        name="out_proj",
    )(*ys, *([w_bf] * len(ys)), x2d, gain.reshape(1, d))


def _rope_tables(pos, half):
    inv = 1.0 / (ROPE_THETA ** (np.arange(half, dtype=np.float64) / half))
    ang = np.asarray(pos, dtype=np.float64)[:, None] * inv[None, :]
    return np.cos(ang), np.sin(ang)


def kernel(x, mem, pre_norm, w_in, conv_w, mem_norm, w_mem_kv, w_out, post_norm):
    b, s, d = x.shape
    m_len = mem.shape[1]
    depth = w_in.shape[0]
    assert all(window // dil == BLOCK for window, dil in DIL_GROUPS)
    pos = np.arange(s)
    cos_a, sin_a = _rope_tables(pos, HEAD_DIM // 2)
    cosa = np.concatenate([cos_a, cos_a], axis=-1)
    sina = np.concatenate([-sin_a, sin_a], axis=-1)
    a_scale = HEAD_DIM ** -0.5
    cos_qkv = jnp.asarray(np.stack([cosa * a_scale, cosa, np.ones_like(cosa)]), F32)
    sin_qkv = jnp.asarray(np.stack([sina * a_scale, sina, np.zeros_like(sina)]), F32)
    cosx, sinx = _rope_tables(pos + m_len, XATTN_HEAD_DIM // 2)
    x_scale = XATTN_HEAD_DIM ** -0.5
    cos_qx = jnp.asarray((cosx * x_scale)[None], F32)
    sin_qx = jnp.asarray((sinx * x_scale)[None], F32)
    cos_m, sin_m = _rope_tables(np.arange(m_len), XATTN_HEAD_DIM // 2)
    cosm = jnp.asarray(np.tile(cos_m, (b, 1)), F32)
    sinm = jnp.asarray(np.tile(sin_m, (b, 1)), F32)

    tm, tn = 1024, 512

    def tile_of(off):
        assert off % tn == 0
        return off // tn

    attn_tiles, x_tiles = ATTN_WIDTH // tn, XATTN_WIDTH // tn
    tiles_per_seq = s // tm

    def view(t):
        return t.reshape(b, s, t.shape[-1])

    def rope_tables(cos, sin, kind):
        def index(col_tile, row_tile):
            return kind(col_tile), row_tile % tiles_per_seq, 0
        return [(cos, (1, tm, LANES), index), (sin, (1, tm, LANES), index)]

    def lane_slab(off):
        assert off % LANES == 0
        return lambda j: off // LANES + j

    for layer in range(depth):
        x2d = x.reshape(b * s, d)
        w = w_in[layer]
        common = dict(tm=tm, tn=tn)
        w_qx_bf = _cast_columns_bf16(w, OFF_QX, XATTN_WIDTH, tr=2048)
        h, qx = _norm_project(x2d, pre_norm[layer], w_qx_bf, cos_qx, sin_qx, seq=s, tm=512)
        qx = view(qx)

        qkv = view(_in_proj(h, w, [lambda j: tile_of(OFF_QA) + j], 3 * attn_tiles,
                            _elementwise_finish(_rope_half_lane),
                            rope_tables(cos_qkv, sin_qkv, lambda j: j // attn_tiles),
                            name="in_proj_qkv", w_pieces=2, **common))
        y_conv, w_out_bf = _in_proj(
            h, w, [lane_slab(OFF_UC), lane_slab(OFF_BC), lane_slab(OFF_CC), lane_slab(OFF_ZC)],
            CONV_WIDTH // LANES,
            functools.partial(_gated_conv_finish, tiles_per_seq=tiles_per_seq),
            [(conv_w[layer], (CONV_K, LANES), lambda col_tile, row_tile: (0, col_tile))],
            out_tn=LANES, scratch=[pltpu.VMEM((CONV_HALO, LANES), F32)],
            rider=(lambda block: block, w_out[layer], []), name="in_proj_conv", w_pieces=1,
            **common)
        gates, hm = _in_proj(
            h, w, [lambda j: jnp.where(j < x_tiles, tile_of(OFF_ZX) + j,
                                       tile_of(OFF_ZA) - x_tiles + j)],
            x_tiles + attn_tiles, _elementwise_finish(_silu), [], name="in_proj_gates",
            rider=(_rms_normalize, mem.reshape(b * m_len, d), [mem_norm[layer].reshape(1, d)]),
            w_pieces=2, **common)
        gates = view(gates)
        mkv = _mem_proj(hm, w_mem_kv[layer], cosm, sinm, tn=512)

        y_groups = _dilated_attn((qkv, 0), (qkv, ATTN_WIDTH), (qkv, 2 * ATTN_WIDTH),
                                 (gates, XATTN_WIDTH))
        y_groups = [y.reshape(b * s, GROUP_WIDTH) for y in y_groups]
        y_x = _xattn((qx, 0), (gates, 0), mkv.reshape(b, m_len, 2 * XATTN_WIDTH), tm=s)
        y_x = y_x.reshape(b * s, XATTN_WIDTH)

        out = _out_proj([*y_groups, y_conv, y_x], w_out_bf, x2d,
                        post_norm[layer], tm=1024, tn=512)
        x = out.reshape(b, s, d)
    return x
```

```python
import functools

import jax
import jax.numpy as jnp
import numpy as np
from jax import lax
from jax.experimental import pallas as pl
from jax.experimental.pallas import tpu as pltpu

D_MODEL = 4096
MEM_LEN = 256
HEAD_DIM = 128
DIL_GROUPS = ((128, 1), (512, 4), (2048, 16))
ATTN_WIDTH = 3 * D_MODEL // 8
CONV_WIDTH = 3 * D_MODEL // 8
XATTN_WIDTH = D_MODEL // 4
N_XATTN_HEADS = 4
XATTN_HEAD_DIM = XATTN_WIDTH // N_XATTN_HEADS
HEADS_PER_DIL = 4
GROUP_WIDTH = HEADS_PER_DIL * HEAD_DIM
CONV_K = 3
BLOCK = 128
ROPE_THETA = 10000.0
EPS = 1e-6
NEG_INF = -1e30

OFF_QA = 0
OFF_KA = OFF_QA + ATTN_WIDTH
OFF_VA = OFF_KA + ATTN_WIDTH
OFF_ZA = OFF_VA + ATTN_WIDTH
OFF_UC = OFF_ZA + ATTN_WIDTH
OFF_BC = OFF_UC + CONV_WIDTH
OFF_CC = OFF_BC + CONV_WIDTH
OFF_ZC = OFF_CC + CONV_WIDTH
OFF_QX = OFF_ZC + CONV_WIDTH
OFF_ZX = OFF_QX + XATTN_WIDTH

V7X_VMEM_LIMIT_BYTES = 56 * 1024 * 1024
LANES = 128
W_PIECES = 4
EPILOGUE_ROWS = 128
CONV_HALO = 8
NORM_ROWS = 64
RIDER_ROWS = 16

BF16 = jnp.bfloat16
F32 = jnp.float32


def _params(semantics):
    return pltpu.CompilerParams(dimension_semantics=semantics,
                                vmem_limit_bytes=V7X_VMEM_LIMIT_BYTES)


def _rms_normalize(x, gain):
    ms = jnp.mean(x * x, axis=-1, keepdims=True)
    return x * lax.rsqrt(ms + EPS) * gain


def _cast_kernel(x_ref, o_ref):
    o_ref[...] = x_ref[...].astype(o_ref.dtype)


def _cast_columns_bf16(w, col_off, width, tr):
    rows = w.shape[0]
    assert col_off % width == 0
    return pl.pallas_call(
        _cast_kernel,
        grid=(rows // tr,),
        in_specs=[pl.BlockSpec((tr, width), lambda i: (i, col_off // width))],
        out_specs=pl.BlockSpec((tr, width), lambda i: (i, 0)),
        out_shape=jax.ShapeDtypeStruct((rows, width), BF16),
        compiler_params=_params(("parallel",)),
        name="cast_w_qx",
    )(w)


def _norm_project_kernel(x_ref, g_ref, w_ref, cos_ref, sin_ref, h_ref, o_ref, acc_ref):
    tm = x_ref.shape[0]
    for r0 in range(0, tm, NORM_ROWS):
        rows = slice(r0, r0 + NORM_ROWS)
        h_ref[rows, :] = _rms_normalize(x_ref[rows, :], g_ref[...]).astype(h_ref.dtype)
    acc_ref[...] = jnp.dot(h_ref[...], w_ref[...], preferred_element_type=F32)
    for r0 in range(0, tm, EPILOGUE_ROWS):
        rows = slice(r0, r0 + EPILOGUE_ROWS)
        o_ref[rows, :] = _rope_two_tiles(acc_ref[rows, :], cos_ref[0, rows, :],
                                         sin_ref[0, rows, :]).astype(o_ref.dtype)


def _norm_project(x2d, gain, w_bf, cos, sin, seq, tm):
    rows, d = x2d.shape
    width = w_bf.shape[1]
    tiles_per_seq = seq // tm
    table = pl.BlockSpec((1, tm, LANES), lambda i: (0, i % tiles_per_seq, 0))
    return pl.pallas_call(
        _norm_project_kernel,
        grid=(rows // tm,),
        in_specs=[pl.BlockSpec((tm, d), lambda i: (i, 0)),
                  pl.BlockSpec((1, d), lambda i: (0, 0)),
                  pl.BlockSpec((d, width), lambda i: (0, 0)),
                  table, table],
        out_specs=[pl.BlockSpec((tm, d), lambda i: (i, 0)),
                   pl.BlockSpec((tm, width), lambda i: (i, 0))],
        out_shape=[jax.ShapeDtypeStruct((rows, d), BF16),
                   jax.ShapeDtypeStruct((rows, width), BF16)],
        scratch_shapes=[pltpu.VMEM((tm, width), F32)],
        compiler_params=_params(("arbitrary",)),
        name="norm_project_qx",
    )(x2d, gain.reshape(1, d), w_bf, cos, sin)


def _silu(z):
    half = 0.5 * z
    return half + half * jnp.tanh(half)


def _rope_half_lane(acc, cos, sin_signed):
    outs = []
    for h in range(acc.shape[1] // HEAD_DIM):
        xh = acc[:, h * HEAD_DIM:(h + 1) * HEAD_DIM]
        outs.append(xh * cos + pltpu.roll(xh, HEAD_DIM // 2, axis=1) * sin_signed)
    return jnp.concatenate(outs, axis=1)


def _rope_two_tiles(acc, cos, sin):
    outs = []
    for h in range(acc.shape[1] // XATTN_HEAD_DIM):
        x1 = acc[:, h * XATTN_HEAD_DIM:h * XATTN_HEAD_DIM + LANES]
        x2 = acc[:, h * XATTN_HEAD_DIM + LANES:(h + 1) * XATTN_HEAD_DIM]
        outs.append(x1 * cos - x2 * sin)
        outs.append(x2 * cos + x1 * sin)
    return jnp.concatenate(outs, axis=1)


def _elementwise_finish(epilogue):
    def finish(acc_ref, slot, o_ref, aux_refs, scratch_refs, row_tile):
        for r0 in range(0, o_ref.shape[0], EPILOGUE_ROWS):
            rows = slice(r0, r0 + EPILOGUE_ROWS)
            o_ref[rows, :] = epilogue(acc_ref[slot, rows, :],
                                      *[r[0, rows, :] for r in aux_refs]).astype(o_ref.dtype)
    return finish


def _gated_conv_finish(acc_ref, slot, o_ref, aux_refs, scratch_refs, row_tile, *, tiles_per_seq):
    (w_ref,), (carry_ref,) = aux_refs, scratch_refs
    tm = o_ref.shape[0]
    u_cols, b_cols, c_cols, z_cols = (slice(k * LANES, (k + 1) * LANES) for k in range(4))
    w = w_ref[...]
    tail = jnp.where(row_tile % tiles_per_seq == 0, 0.0, carry_ref[...])
    for r0 in range(0, tm, EPILOGUE_ROWS):
        rows = slice(r0, r0 + EPILOGUE_ROWS)
        cu = acc_ref[slot, rows, c_cols] * acc_ref[slot, rows, u_cols]
        window = jnp.concatenate([tail, cu], axis=0)
        conv = cu * w[CONV_K - 1:CONV_K, :]
        for lag in range(1, CONV_K):
            shifted = pltpu.roll(window, lag, axis=0)[CONV_HALO:, :]
            conv = conv + shifted * w[CONV_K - 1 - lag:CONV_K - lag, :]
        o_ref[rows, :] = (acc_ref[slot, rows, b_cols] * conv
                          * _silu(acc_ref[slot, rows, z_cols])).astype(o_ref.dtype)
        tail = cu[EPILOGUE_ROWS - CONV_HALO:, :]
    carry_ref[...] = tail


def _in_proj_kernel(*refs, ni, n_steps, n_w, w_pieces, n_aux, n_scratch, finish_fn, rider_fn):
    h_ref = refs[0]
    w_refs = refs[1:1 + n_w]
    aux_refs = refs[1 + n_w:1 + n_w + n_aux]
    scratch_refs = refs[len(refs) - n_scratch:] if n_scratch else ()
    rest = refs[1 + n_w + n_aux:len(refs) - n_scratch]
    if rider_fn is not None:
        rider_in_ref, *rider_row_refs = rest[:-4]
        o_ref, rider_out_ref, wbf_ref, acc_ref = rest[-4:]
    else:
        o_ref, wbf_ref, acc_ref = rest
    t = pl.program_id(0)
    piece_rows = wbf_ref.shape[0] // w_pieces
    slab_cols = wbf_ref.shape[1] // (n_w // w_pieces)

    @pl.when((t % ni == 0) & (t < n_steps))
    def _():
        for k, w_ref in enumerate(w_refs):
            slab, p = divmod(k, w_pieces)
            wbf_ref[p * piece_rows:(p + 1) * piece_rows,
                    slab * slab_cols:(slab + 1) * slab_cols] = w_ref[...].astype(BF16)

    def ride():
        for r0 in range(0, rider_in_ref.shape[0], RIDER_ROWS):
            rows = slice(r0, r0 + RIDER_ROWS)
            rider_out_ref[rows, :] = rider_fn(
                rider_in_ref[rows, :], *[r[...] for r in rider_row_refs]
            ).astype(rider_out_ref.dtype)

    def multiply(slot):
        acc_ref[slot] = jnp.dot(h_ref[...], wbf_ref[...], preferred_element_type=F32)
        if rider_fn is not None:
            ride()

    def finish(slot):
        finish_fn(acc_ref, slot, o_ref, aux_refs, scratch_refs, ((t - 1) % ni))

    @pl.when(t == 0)
    def _():
        for ref in scratch_refs:
            ref[...] = jnp.zeros_like(ref)
        multiply(0)

    for parity in range(2):
        @pl.when((t > 0) & (t < n_steps) & (t % 2 == parity))
        def _(parity=parity):
            finish(1 - parity)
            multiply(parity)

    @pl.when(t == n_steps)
    def _():
        finish((n_steps - 1) % 2)


def _in_proj(h, w, col_slabs, n_col_tiles, finish_fn, aux, tm, tn, name, out_tn=None,
             scratch=(), rider=None, w_pieces=W_PIECES):
    rows, d = h.shape
    ni = rows // tm
    n_steps = n_col_tiles * ni
    out_tn = tn if out_tn is None else out_tn
    slab_cols = tn // len(col_slabs)
    rider_in_specs, rider_out_specs, rider_out_shapes, rider_operands = [], [], [], []
    rider_fn, rider_steps = None, 0
    if rider is not None:
        rider_fn, rider_matrix, rider_rows = rider
        r_rows, r_cols = rider_matrix.shape
        assert all(r.shape == (1, r_cols) for r in rider_rows)
        rider_steps = max(k for k in range(1, n_steps + 1) if r_rows % k == 0)
        rider_spec = pl.BlockSpec((r_rows // rider_steps, r_cols),
                                  lambda t: (jnp.minimum(t, rider_steps - 1), 0))
        rider_in_specs = [rider_spec] + [pl.BlockSpec((1, r_cols), lambda t: (0, 0))
                                         for _ in rider_rows]
        rider_out_specs = [rider_spec]
        rider_out_shapes = [jax.ShapeDtypeStruct(rider_matrix.shape, BF16)]
        rider_operands = [rider_matrix, *rider_rows]

    def cur(t):
        c = jnp.minimum(t, n_steps - 1)
        return c // ni, c % ni

    def prev(t):
        p = jnp.maximum(t - 1, 0)
        return p // ni, p % ni

    def aux_spec(block, index):
        return pl.BlockSpec(block, lambda t: index(*prev(t)))

    def w_piece_spec(col_slab, p):
        def index(t):
            j, i = cur(t)
            ahead = (i >= ni - w_pieces + p).astype(jnp.int32)
            return p, col_slab(jnp.minimum(j + ahead, n_col_tiles - 1))
        return pl.BlockSpec((d // w_pieces, slab_cols), index)

    assert ni >= w_pieces and d % w_pieces == 0
    w_specs = [w_piece_spec(col_slab, p) for col_slab in col_slabs for p in range(w_pieces)]
    results = pl.pallas_call(
        functools.partial(_in_proj_kernel, ni=ni, n_steps=n_steps, n_w=len(w_specs),
                          w_pieces=w_pieces, n_aux=len(aux), n_scratch=len(scratch),
                          finish_fn=finish_fn, rider_fn=rider_fn),
        grid=(n_steps + 1,),
        in_specs=[pl.BlockSpec((tm, d), lambda t: (cur(t)[1], 0))] + w_specs
                 + [aux_spec(block, index) for _, block, index in aux] + rider_in_specs,
        out_specs=[pl.BlockSpec((tm, out_tn), lambda t: (prev(t)[1], prev(t)[0]))]
                  + rider_out_specs,
        out_shape=[jax.ShapeDtypeStruct((rows, n_col_tiles * out_tn), BF16)] + rider_out_shapes,
        scratch_shapes=[pltpu.VMEM((d, tn), BF16), pltpu.VMEM((2, tm, tn), F32), *scratch],
        compiler_params=_params(("arbitrary",)),
        name=name,
    )(h, *([w] * len(w_specs)), *[arr for arr, _, _ in aux], *rider_operands)
    return results if rider is not None else results[0]


def _mem_proj_kernel(h_ref, w_ref, cos_ref, sin_ref, o_ref, *, tn):
    j = pl.program_id(0)
    acc = jnp.dot(h_ref[...], w_ref[...].astype(BF16), preferred_element_type=F32)

    @pl.when(j < XATTN_WIDTH // tn)
    def _():
        o_ref[...] = _rope_two_tiles(acc, cos_ref[...], sin_ref[...]).astype(o_ref.dtype)

    @pl.when(j >= XATTN_WIDTH // tn)
    def _():
        o_ref[...] = acc.astype(o_ref.dtype)


def _mem_proj(hm, w, cos, sin, tn):
    rows, d = hm.shape
    ncols = w.shape[1]
    tab = pl.BlockSpec((rows, LANES), lambda j: (0, 0))
    return pl.pallas_call(
        functools.partial(_mem_proj_kernel, tn=tn),
        grid=(ncols // tn,),
        in_specs=[pl.BlockSpec((rows, d), lambda j: (0, 0)),
                  pl.BlockSpec((d, tn), lambda j: (0, j)),
                  tab, tab],
        out_specs=pl.BlockSpec((rows, tn), lambda j: (0, j)),
        out_shape=jax.ShapeDtypeStruct((rows, ncols), BF16),
        compiler_params=_params(("parallel",)),
        name="mem_proj",
    )(hm, w, cos, sin)


MAX_SINGLE_ACCESS_STRIDE = 4


def _split_stride(dil):
    if dil <= MAX_SINGLE_ACCESS_STRIDE:
        return dil, 1
    assert dil % MAX_SINGLE_ACCESS_STRIDE == 0
    f2 = dil // MAX_SINGLE_ACCESS_STRIDE
    assert f2 <= MAX_SINGLE_ACCESS_STRIDE
    return MAX_SINGLE_ACCESS_STRIDE, f2


def _dilated_attn_kernel(q0, q1, q2, k0, k1, k2, v0, v1, v2, z0, z1, z2, y0, y1, y2,
                         stage_ref, mid_ref, sub_ref, osub_ref, lsub_ref, onat_ref, lnat_ref,
                         *, seq):
    q_refs, k_refs, v_refs = (q0, q1, q2), (k0, k1, k2), (v0, v1, v2)
    z_refs, y_refs = (z0, z1, z2), (y0, y1, y2)
    nblk = seq // BLOCK
    qi = lax.broadcasted_iota(jnp.int32, (BLOCK, 2 * BLOCK), 0)
    kk = lax.broadcasted_iota(jnp.int32, (BLOCK, 2 * BLOCK), 1)
    dist = qi + BLOCK - kk
    band = (dist >= 0) & (dist <= BLOCK)
    causal = (lax.broadcasted_iota(jnp.int32, (BLOCK, BLOCK), 1)
              <= lax.broadcasted_iota(jnp.int32, (BLOCK, BLOCK), 0))

    rows_of = []
    for g, (window, dil) in enumerate(DIL_GROUPS):
        sub_len = seq // dil
        if dil == 1:
            def group_rows(which, start, size, g=g):
                return (q_refs, k_refs, v_refs)[which][g][0, pl.ds(start, size), :]
        else:
            slab = 3 * (g - 1)
            f1, f2 = _split_stride(dil)
            len1 = seq // f1
            for which, refs in enumerate((q_refs, k_refs, v_refs)):
                stage_ref[slab + which] = refs[g][0].astype(F32)
                if f2 == 1:
                    for r in range(dil):
                        sub_ref[slab + which, r * sub_len:(r + 1) * sub_len, :] = (
                            stage_ref[slab + which, pl.ds(r, sub_len, stride=dil), :].astype(BF16))
                    continue
                for r1 in range(f1):
                    mid_ref[which, r1 * len1:(r1 + 1) * len1, :] = (
                        stage_ref[slab + which, pl.ds(r1, len1, stride=f1), :])
                for r1 in range(f1):
                    for r2 in range(f2):
                        r = r2 * f1 + r1
                        sub_ref[slab + which, r * sub_len:(r + 1) * sub_len, :] = (
                            mid_ref[which, pl.ds(r1 * len1 + r2, sub_len, stride=f2), :]
                            .astype(BF16))

            def group_rows(which, start, size, slab=slab):
                return sub_ref[slab + which, pl.ds(start, size), :]
        rows_of.append(group_rows)

    def attend(g, start, first):
        q = rows_of[g](0, start, BLOCK)
        if first:
            k = rows_of[g](1, start, BLOCK)
            v = rows_of[g](2, start, BLOCK)
            mask = causal
        else:
            k = rows_of[g](1, start - BLOCK, 2 * BLOCK)
            v = rows_of[g](2, start - BLOCK, 2 * BLOCK)
            mask = band
        s = lax.dot_general(q, k, (((1,), (1,)), ((), ())), preferred_element_type=F32)
        s = jnp.where(mask, s, NEG_INF)
        m = jnp.max(s, axis=-1, keepdims=True)
        p = jnp.exp(s - m)
        l = jnp.sum(p, axis=-1, keepdims=True)
        o = jnp.dot(p.astype(BF16), v, preferred_element_type=F32) * (1.0 / l)
        lse = jnp.broadcast_to(m + jnp.log(l), (BLOCK, HEAD_DIM))
        if DIL_GROUPS[g][1] == 1:
            onat_ref[g, pl.ds(start, BLOCK), :] = o
            lnat_ref[g, pl.ds(start, BLOCK), :] = lse
        else:
            osub_ref[g - 1, pl.ds(start, BLOCK), :] = o
            lsub_ref[g - 1, pl.ds(start, BLOCK), :] = lse

    for c in range(nblk):
        for g, (window, dil) in enumerate(DIL_GROUPS):
            attend(g, c * BLOCK, c % (seq // dil // BLOCK) == 0)

    for g, (window, dil) in enumerate(DIL_GROUPS):
        sub_len = seq // dil
        if dil == 1:
            continue
        f1, f2 = _split_stride(dil)
        len1 = seq // f1
        for idx, (nat_ref, subseq_ref) in enumerate(((onat_ref, osub_ref), (lnat_ref, lsub_ref))):
            if f2 == 1:
                for r in range(dil):
                    src = slice(r * sub_len, (r + 1) * sub_len)
                    nat_ref[g, pl.ds(r, sub_len, stride=dil), :] = subseq_ref[g - 1, src, :]
                continue
            mid = 3 + idx
            for r1 in range(f1):
                for r2 in range(f2):
                    r = r2 * f1 + r1
                    mid_ref[mid, pl.ds(r1 * len1 + r2, sub_len, stride=f2), :] = (
                        subseq_ref[g - 1, r * sub_len:(r + 1) * sub_len, :])
            for r1 in range(f1):
                nat_ref[g, pl.ds(r1, len1, stride=f1), :] = mid_ref[mid, r1 * len1:(r1 + 1) * len1, :]

    chunk = 2 * BLOCK
    for t in range(seq // chunk):
        rows = slice(t * chunk, (t + 1) * chunk)
        lse = [lnat_ref[g, rows, :] for g in range(len(DIL_GROUPS))]
        mx = jnp.maximum(jnp.maximum(lse[0], lse[1]), lse[2])
        e = [jnp.exp(x - mx) for x in lse]
        inv = 1.0 / (e[0] + e[1] + e[2])
        for g in range(len(DIL_GROUPS)):
            y_refs[g][0, rows, :] = (onat_ref[g, rows, :] * (e[g] * inv)
                                     * z_refs[g][0, rows, :].astype(F32)).astype(y_refs[g].dtype)


def _dilated_attn(q_src, k_src, v_src, z_src):
    sources = (q_src, k_src, v_src, z_src)
    b, s, _ = q_src[0].shape
    n_groups = len(DIL_GROUPS)
    n_regrouped = n_groups - 1
    assert sum(_split_stride(dil)[1] > 1 for _, dil in DIL_GROUPS) <= 1

    def head_spec(off, g):
        first = (off + g * GROUP_WIDTH) // HEAD_DIM
        return pl.BlockSpec((1, s, HEAD_DIM), lambda bi, hg: (bi, 0, first + hg))

    in_specs = [head_spec(off, g) for _, off in sources for g in range(n_groups)]
    operands = [arr for arr, _ in sources for g in range(n_groups)]
    out_spec = pl.BlockSpec((1, s, HEAD_DIM), lambda bi, hg: (bi, 0, hg))
    return pl.pallas_call(
        functools.partial(_dilated_attn_kernel, seq=s),
        grid=(b, HEADS_PER_DIL),
        in_specs=in_specs,
        out_specs=[out_spec] * n_groups,
        out_shape=[jax.ShapeDtypeStruct((b, s, GROUP_WIDTH), BF16)] * n_groups,
        scratch_shapes=[pltpu.VMEM((3 * n_regrouped, s, HEAD_DIM), F32),
                        pltpu.VMEM((5, s, HEAD_DIM), F32),
                        pltpu.VMEM((3 * n_regrouped, s, HEAD_DIM), BF16),
                        pltpu.VMEM((n_regrouped, s, HEAD_DIM), F32),
                        pltpu.VMEM((n_regrouped, s, HEAD_DIM), F32),
                        pltpu.VMEM((n_groups, s, HEAD_DIM), F32),
                        pltpu.VMEM((n_groups, s, HEAD_DIM), F32)],
        compiler_params=_params(("parallel", "parallel")),
        name="dilated_attn",
    )(*operands)


def _xattn_kernel(q_ref, z_ref, mk_ref, mv_ref, y_ref):
    for h in range(N_XATTN_HEADS):
        cols = slice(h * XATTN_HEAD_DIM, (h + 1) * XATTN_HEAD_DIM)
        s = lax.dot_general(q_ref[0, :, cols], mk_ref[0, :, cols],
                            (((1,), (1,)), ((), ())), preferred_element_type=F32)
        m = jnp.max(s, axis=-1, keepdims=True)
        p = jnp.exp(s - m)
        l = jnp.sum(p, axis=-1, keepdims=True)
        o = jnp.dot(p.astype(BF16), mv_ref[0, :, cols], preferred_element_type=F32)
        y_ref[0, :, cols] = (o * (1.0 / l) * z_ref[0, :, cols].astype(F32)).astype(y_ref.dtype)


def _xattn(q_src, z_src, mkv3, tm):
    b, s, _ = q_src[0].shape
    m_len = mkv3.shape[1]
    q_off, z_off = q_src[1], z_src[1]
    assert q_off % XATTN_WIDTH == 0 and z_off % XATTN_WIDTH == 0
    return pl.pallas_call(
        _xattn_kernel,
        grid=(b, s // tm),
        in_specs=[pl.BlockSpec((1, tm, XATTN_WIDTH), lambda bi, i: (bi, i, q_off // XATTN_WIDTH)),
                  pl.BlockSpec((1, tm, XATTN_WIDTH), lambda bi, i: (bi, i, z_off // XATTN_WIDTH)),
                  pl.BlockSpec((1, m_len, XATTN_WIDTH), lambda bi, i: (bi, 0, 0)),
                  pl.BlockSpec((1, m_len, XATTN_WIDTH), lambda bi, i: (bi, 0, 1))],
        out_specs=pl.BlockSpec((1, tm, XATTN_WIDTH), lambda bi, i: (bi, i, 0)),
        out_shape=jax.ShapeDtypeStruct((b, s, XATTN_WIDTH), BF16),
        compiler_params=_params(("parallel", "parallel")),
        name="mem_xattn",
    )(q_src[0], z_src[0], mkv3, mkv3)


def _out_proj_kernel(*refs, ni, d, n_mix):
    y_refs, w_refs = refs[:n_mix], refs[n_mix:2 * n_mix]
    x_ref, g_ref, o_ref, y2_ref, ssq_ref, scale_ref = refs[2 * n_mix:]
    i = pl.program_id(0)
    j = pl.program_id(1)

    @pl.when(j == 0)
    def _():
        @pl.when(i > 0)
        def _():
            scale_ref[...] = lax.rsqrt(ssq_ref[...] * (1.0 / d) + EPS)
        ssq_ref[...] = jnp.zeros_like(ssq_ref)

    def finish_previous_tile():
        o_ref[...] = x_ref[...] + y2_ref[j] * scale_ref[...] * g_ref[...]

    def multiply_this_tile():
        y2 = jnp.dot(y_refs[0][...], w_refs[0][...], preferred_element_type=F32)
        for y_ref, w_ref in zip(y_refs[1:], w_refs[1:]):
            y2 = y2 + jnp.dot(y_ref[...], w_ref[...], preferred_element_type=F32)
        y2_ref[j] = y2
        ssq_ref[...] += jnp.sum(y2 * y2, axis=-1, keepdims=True)

    @pl.when(i == 0)
    def _():
        multiply_this_tile()

    @pl.when((i > 0) & (i < ni))
    def _():
        finish_previous_tile()
        multiply_this_tile()

    @pl.when(i == ni)
    def _():
        finish_previous_tile()


def _out_proj(ys, w_bf, x2d, gain, tm, tn):
    rows, d = x2d.shape
    nj = d // tn
    ni = rows // tm

    def y_spec(width):
        return pl.BlockSpec((tm, width), lambda i, j: (jnp.minimum(i, ni - 1), 0))

    def w_spec(width, row_off):
        assert row_off % width == 0
        return pl.BlockSpec((width, tn),
                            lambda i, j: (row_off // width, jnp.where(i == ni, nj - 1, j)))

    io_spec = pl.BlockSpec((tm, tn),
                           lambda i, j: (jnp.maximum(i - 1, 0), jnp.where(i == 0, 0, j)))
    widths = [y.shape[1] for y in ys]
    offsets = [sum(widths[:k]) for k in range(len(widths))]
    return pl.pallas_call(
        functools.partial(_out_proj_kernel, ni=ni, d=d, n_mix=len(ys)),
        grid=(ni + 1, nj),
        in_specs=([y_spec(w) for w in widths]
                  + [w_spec(w, off) for w, off in zip(widths, offsets)]
                  + [io_spec, pl.BlockSpec((1, tn), lambda i, j: (0, j))]),
        out_specs=io_spec,
        out_shape=jax.ShapeDtypeStruct((rows, d), F32),
        scratch_shapes=[pltpu.VMEM((nj, tm, tn), F32),
                        pltpu.VMEM((tm, 1), F32),
                        pltpu.VMEM((tm, 1), F32)],
        compiler_params=_params(("arbitrary", "arbitrary")),
        name="out_proj",
    )(*ys, *([w_bf] * len(ys)), x2d, gain.reshape(1, d))


def _rope_tables(pos, half):
    inv = 1.0 / (ROPE_THETA ** (np.arange(half, dtype=np.float64) / half))
    ang = np.asarray(pos, dtype=np.float64)[:, None] * inv[None, :]
    return np.cos(ang), np.sin(ang)


def kernel(x, mem, pre_norm, w_in, conv_w, mem_norm, w_mem_kv, w_out, post_norm):
    b, s, d = x.shape
    m_len = mem.shape[1]
    depth = w_in.shape[0]
    assert all(window // dil == BLOCK for window, dil in DIL_GROUPS)
    pos = np.arange(s)
    cos_a, sin_a = _rope_tables(pos, HEAD_DIM // 2)
    cosa = np.concatenate([cos_a, cos_a], axis=-1)
    sina = np.concatenate([-sin_a, sin_a], axis=-1)
    a_scale = HEAD_DIM ** -0.5
    cos_qkv = jnp.asarray(np.stack([cosa * a_scale, cosa, np.ones_like(cosa)]), F32)
    sin_qkv = jnp.asarray(np.stack([sina * a_scale, sina, np.zeros_like(sina)]), F32)
    cosx, sinx = _rope_tables(pos + m_len, XATTN_HEAD_DIM // 2)
    x_scale = XATTN_HEAD_DIM ** -0.5
    cos_qx = jnp.asarray((cosx * x_scale)[None], F32)
    sin_qx = jnp.asarray((sinx * x_scale)[None], F32)
    cos_m, sin_m = _rope_tables(np.arange(m_len), XATTN_HEAD_DIM // 2)
    cosm = jnp.asarray(np.tile(cos_m, (b, 1)), F32)
    sinm = jnp.asarray(np.tile(sin_m, (b, 1)), F32)

    tm, tn = 1024, 512

    def tile_of(off):
        assert off % tn == 0
        return off // tn

    attn_tiles, x_tiles = ATTN_WIDTH // tn, XATTN_WIDTH // tn
    tiles_per_seq = s // tm

    def view(t):
        return t.reshape(b, s, t.shape[-1])

    def rope_tables(cos, sin, kind):
        def index(col_tile, row_tile):
            return kind(col_tile), row_tile % tiles_per_seq, 0
        return [(cos, (1, tm, LANES), index), (sin, (1, tm, LANES), index)]

    def lane_slab(off):
        assert off % LANES == 0
        return lambda j: off // LANES + j

    for layer in range(depth):
        x2d = x.reshape(b * s, d)
        w = w_in[layer]
        common = dict(tm=tm, tn=tn)
        w_qx_bf = _cast_columns_bf16(w, OFF_QX, XATTN_WIDTH, tr=2048)
        h, qx = _norm_project(x2d, pre_norm[layer], w_qx_bf, cos_qx, sin_qx, seq=s, tm=512)
        qx = view(qx)

        qkv = view(_in_proj(h, w, [lambda j: tile_of(OFF_QA) + j], 3 * attn_tiles,
                            _elementwise_finish(_rope_half_lane),
                            rope_tables(cos_qkv, sin_qkv, lambda j: j // attn_tiles),
                            name="in_proj_qkv", w_pieces=2, **common))
        y_conv, w_out_bf = _in_proj(
            h, w, [lane_slab(OFF_UC), lane_slab(OFF_BC), lane_slab(OFF_CC), lane_slab(OFF_ZC)],
            CONV_WIDTH // LANES,
            functools.partial(_gated_conv_finish, tiles_per_seq=tiles_per_seq),
            [(conv_w[layer], (CONV_K, LANES), lambda col_tile, row_tile: (0, col_tile))],
            out_tn=LANES, scratch=[pltpu.VMEM((CONV_HALO, LANES), F32)],
            rider=(lambda block: block, w_out[layer], []), name="in_proj_conv", w_pieces=1,
            **common)
        gates, hm = _in_proj(
            h, w, [lambda j: jnp.where(j < x_tiles, tile_of(OFF_ZX) + j,
                                       tile_of(OFF_ZA) - x_tiles + j)],
            x_tiles + attn_tiles, _elementwise_finish(_silu), [], name="in_proj_gates",
            rider=(_rms_normalize, mem.reshape(b * m_len, d), [mem_norm[layer].reshape(1, d)]),
            w_pieces=2, **common)
        gates = view(gates)
        mkv = _mem_proj(hm, w_mem_kv[layer], cosm, sinm, tn=512)

        y_groups = _dilated_attn((qkv, 0), (qkv, ATTN_WIDTH), (qkv, 2 * ATTN_WIDTH),
                                 (gates, XATTN_WIDTH))
        y_groups = [y.reshape(b * s, GROUP_WIDTH) for y in y_groups]
        y_x = _xattn((qx, 0), (gates, 0), mkv.reshape(b, m_len, 2 * XATTN_WIDTH), tm=s)
        y_x = y_x.reshape(b * s, XATTN_WIDTH)

        out = _out_proj([*y_groups, y_conv, y_x], w_out_bf, x2d,
                        post_norm[layer], tm=1024, tn=512)
        x = out.reshape(b, s, d)
    return x
```

```python
import functools

import jax
import jax.numpy as jnp
import numpy as np
from jax import lax
from jax.experimental import pallas as pl
from jax.experimental.pallas import tpu as pltpu

D_MODEL = 4096
MEM_LEN = 256
HEAD_DIM = 128
DIL_GROUPS = ((128, 1), (512, 4), (2048, 16))
ATTN_WIDTH = 3 * D_MODEL // 8
CONV_WIDTH = 3 * D_MODEL // 8
XATTN_WIDTH = D_MODEL // 4
N_XATTN_HEADS = 4
XATTN_HEAD_DIM = XATTN_WIDTH // N_XATTN_HEADS
HEADS_PER_DIL = 4
GROUP_WIDTH = HEADS_PER_DIL * HEAD_DIM
CONV_K = 3
BLOCK = 128
ROPE_THETA = 10000.0
EPS = 1e-6
NEG_INF = -1e30

OFF_QA = 0
OFF_KA = OFF_QA + ATTN_WIDTH
OFF_VA = OFF_KA + ATTN_WIDTH
OFF_ZA = OFF_VA + ATTN_WIDTH
OFF_UC = OFF_ZA + ATTN_WIDTH
OFF_BC = OFF_UC + CONV_WIDTH
OFF_CC = OFF_BC + CONV_WIDTH
OFF_ZC = OFF_CC + CONV_WIDTH
OFF_QX = OFF_ZC + CONV_WIDTH
OFF_ZX = OFF_QX + XATTN_WIDTH

V7X_VMEM_LIMIT_BYTES = 56 * 1024 * 1024
LANES = 128
W_PIECES = 4
EPILOGUE_ROWS = 128
CONV_HALO = 8
NORM_ROWS = 64
RIDER_ROWS = 16

BF16 = jnp.bfloat16
F32 = jnp.float32


def _params(semantics):
    return pltpu.CompilerParams(dimension_semantics=semantics,
                                vmem_limit_bytes=V7X_VMEM_LIMIT_BYTES)


def _rms_normalize(x, gain):
    ms = jnp.mean(x * x, axis=-1, keepdims=True)
    return x * lax.rsqrt(ms + EPS) * gain


def _cast_kernel(x_ref, o_ref):
    o_ref[...] = x_ref[...].astype(o_ref.dtype)


def _cast_columns_bf16(w, col_off, width, tr):
    rows = w.shape[0]
    assert col_off % width == 0
    return pl.pallas_call(
        _cast_kernel,
        grid=(rows // tr,),
        in_specs=[pl.BlockSpec((tr, width), lambda i: (i, col_off // width))],
        out_specs=pl.BlockSpec((tr, width), lambda i: (i, 0)),
        out_shape=jax.ShapeDtypeStruct((rows, width), BF16),
        compiler_params=_params(("parallel",)),
        name="cast_w_qx",
    )(w)


def _norm_project_kernel(x_ref, g_ref, w_ref, cos_ref, sin_ref, h_ref, o_ref, acc_ref):
    tm = x_ref.shape[0]
    for r0 in range(0, tm, NORM_ROWS):
        rows = slice(r0, r0 + NORM_ROWS)
        h_ref[rows, :] = _rms_normalize(x_ref[rows, :], g_ref[...]).astype(h_ref.dtype)
    acc_ref[...] = jnp.dot(h_ref[...], w_ref[...], preferred_element_type=F32)
    for r0 in range(0, tm, EPILOGUE_ROWS):
        rows = slice(r0, r0 + EPILOGUE_ROWS)
        o_ref[rows, :] = _rope_two_tiles(acc_ref[rows, :], cos_ref[0, rows, :],
                                         sin_ref[0, rows, :]).astype(o_ref.dtype)


def _norm_project(x2d, gain, w_bf, cos, sin, seq, tm):
    rows, d = x2d.shape
    width = w_bf.shape[1]
    tiles_per_seq = seq // tm
    table = pl.BlockSpec((1, tm, LANES), lambda i: (0, i % tiles_per_seq, 0))
    return pl.pallas_call(
        _norm_project_kernel,
        grid=(rows // tm,),
        in_specs=[pl.BlockSpec((tm, d), lambda i: (i, 0)),
                  pl.BlockSpec((1, d), lambda i: (0, 0)),
                  pl.BlockSpec((d, width), lambda i: (0, 0)),
                  table, table],
        out_specs=[pl.BlockSpec((tm, d), lambda i: (i, 0)),
                   pl.BlockSpec((tm, width), lambda i: (i, 0))],
        out_shape=[jax.ShapeDtypeStruct((rows, d), BF16),
                   jax.ShapeDtypeStruct((rows, width), BF16)],
        scratch_shapes=[pltpu.VMEM((tm, width), F32)],
        compiler_params=_params(("arbitrary",)),
        name="norm_project_qx",
    )(x2d, gain.reshape(1, d), w_bf, cos, sin)


def _silu(z):
    half = 0.5 * z
    return half + half * jnp.tanh(half)


def _rope_half_lane(acc, cos, sin_signed):
    outs = []
    for h in range(acc.shape[1] // HEAD_DIM):
        xh = acc[:, h * HEAD_DIM:(h + 1) * HEAD_DIM]
        outs.append(xh * cos + pltpu.roll(xh, HEAD_DIM // 2, axis=1) * sin_signed)
    return jnp.concatenate(outs, axis=1)


def _rope_two_tiles(acc, cos, sin):
    outs = []
    for h in range(acc.shape[1] // XATTN_HEAD_DIM):
        x1 = acc[:, h * XATTN_HEAD_DIM:h * XATTN_HEAD_DIM + LANES]
        x2 = acc[:, h * XATTN_HEAD_DIM + LANES:(h + 1) * XATTN_HEAD_DIM]
        outs.append(x1 * cos - x2 * sin)
        outs.append(x2 * cos + x1 * sin)
    return jnp.concatenate(outs, axis=1)


def _elementwise_finish(epilogue):
    def finish(acc_ref, slot, o_ref, aux_refs, scratch_refs, row_tile):
        for r0 in range(0, o_ref.shape[0], EPILOGUE_ROWS):
            rows = slice(r0, r0 + EPILOGUE_ROWS)
            o_ref[rows, :] = epilogue(acc_ref[slot, rows, :],
                                      *[r[0, rows, :] for r in aux_refs]).astype(o_ref.dtype)
    return finish


def _gated_conv_finish(acc_ref, slot, o_ref, aux_refs, scratch_refs, row_tile, *, tiles_per_seq):
    (w_ref,), (carry_ref,) = aux_refs, scratch_refs
    tm = o_ref.shape[0]
    u_cols, b_cols, c_cols, z_cols = (slice(k * LANES, (k + 1) * LANES) for k in range(4))
    w = w_ref[...]
    tail = jnp.where(row_tile % tiles_per_seq == 0, 0.0, carry_ref[...])
    for r0 in range(0, tm, EPILOGUE_ROWS):
        rows = slice(r0, r0 + EPILOGUE_ROWS)
        cu = acc_ref[slot, rows, c_cols] * acc_ref[slot, rows, u_cols]
        window = jnp.concatenate([tail, cu], axis=0)
        conv = cu * w[CONV_K - 1:CONV_K, :]
        for lag in range(1, CONV_K):
            shifted = pltpu.roll(window, lag, axis=0)[CONV_HALO:, :]
            conv = conv + shifted * w[CONV_K - 1 - lag:CONV_K - lag, :]
        o_ref[rows, :] = (acc_ref[slot, rows, b_cols] * conv
                          * _silu(acc_ref[slot, rows, z_cols])).astype(o_ref.dtype)
        tail = cu[EPILOGUE_ROWS - CONV_HALO:, :]
    carry_ref[...] = tail


def _in_proj_kernel(*refs, ni, n_steps, n_w, w_pieces, n_aux, n_scratch, finish_fn, rider_fn):
    h_ref = refs[0]
    w_refs = refs[1:1 + n_w]
    aux_refs = refs[1 + n_w:1 + n_w + n_aux]
    scratch_refs = refs[len(refs) - n_scratch:] if n_scratch else ()
    rest = refs[1 + n_w + n_aux:len(refs) - n_scratch]
    if rider_fn is not None:
        rider_in_ref, *rider_row_refs = rest[:-4]
        o_ref, rider_out_ref, wbf_ref, acc_ref = rest[-4:]
    else:
        o_ref, wbf_ref, acc_ref = rest
    t = pl.program_id(0)
    piece_rows = wbf_ref.shape[0] // w_pieces
    slab_cols = wbf_ref.shape[1] // (n_w // w_pieces)

    @pl.when((t % ni == 0) & (t < n_steps))
    def _():
        for k, w_ref in enumerate(w_refs):
            slab, p = divmod(k, w_pieces)
            wbf_ref[p * piece_rows:(p + 1) * piece_rows,
                    slab * slab_cols:(slab + 1) * slab_cols] = w_ref[...].astype(BF16)

    def ride():
        for r0 in range(0, rider_in_ref.shape[0], RIDER_ROWS):
            rows = slice(r0, r0 + RIDER_ROWS)
            rider_out_ref[rows, :] = rider_fn(
                rider_in_ref[rows, :], *[r[...] for r in rider_row_refs]
            ).astype(rider_out_ref.dtype)

    def multiply(slot):
        acc_ref[slot] = jnp.dot(h_ref[...], wbf_ref[...], preferred_element_type=F32)
        if rider_fn is not None:
            ride()

    def finish(slot):
        finish_fn(acc_ref, slot, o_ref, aux_refs, scratch_refs, ((t - 1) % ni))

    @pl.when(t == 0)
    def _():
        for ref in scratch_refs:
            ref[...] = jnp.zeros_like(ref)
        multiply(0)

    for parity in range(2):
        @pl.when((t > 0) & (t < n_steps) & (t % 2 == parity))
        def _(parity=parity):
            finish(1 - parity)
            multiply(parity)

    @pl.when(t == n_steps)
    def _():
        finish((n_steps - 1) % 2)


def _in_proj(h, w, col_slabs, n_col_tiles, finish_fn, aux, tm, tn, name, out_tn=None,
             scratch=(), rider=None, w_pieces=W_PIECES):
    rows, d = h.shape
    ni = rows // tm
    n_steps = n_col_tiles * ni
    out_tn = tn if out_tn is None else out_tn
    slab_cols = tn // len(col_slabs)
    rider_in_specs, rider_out_specs, rider_out_shapes, rider_operands = [], [], [], []
    rider_fn, rider_steps = None, 0
    if rider is not None:
        rider_fn, rider_matrix, rider_rows = rider
        r_rows, r_cols = rider_matrix.shape
        assert all(r.shape == (1, r_cols) for r in rider_rows)
        rider_steps = max(k for k in range(1, n_steps + 1) if r_rows % k == 0)
        rider_spec = pl.BlockSpec((r_rows // rider_steps, r_cols),
                                  lambda t: (jnp.minimum(t, rider_steps - 1), 0))
        rider_in_specs = [rider_spec] + [pl.BlockSpec((1, r_cols), lambda t: (0, 0))
                                         for _ in rider_rows]
        rider_out_specs = [rider_spec]
        rider_out_shapes = [jax.ShapeDtypeStruct(rider_matrix.shape, BF16)]
        rider_operands = [rider_matrix, *rider_rows]

    def cur(t):
        c = jnp.minimum(t, n_steps - 1)
        return c // ni, c % ni

    def prev(t):
        p = jnp.maximum(t - 1, 0)
        return p // ni, p % ni

    def aux_spec(block, index):
        return pl.BlockSpec(block, lambda t: index(*prev(t)))

    def w_piece_spec(col_slab, p):
        def index(t):
            j, i = cur(t)
            ahead = (i >= ni - w_pieces + p).astype(jnp.int32)
            return p, col_slab(jnp.minimum(j + ahead, n_col_tiles - 1))
        return pl.BlockSpec((d // w_pieces, slab_cols), index)

    assert ni >= w_pieces and d % w_pieces == 0
    w_specs = [w_piece_spec(col_slab, p) for col_slab in col_slabs for p in range(w_pieces)]
    results = pl.pallas_call(
        functools.partial(_in_proj_kernel, ni=ni, n_steps=n_steps, n_w=len(w_specs),
                          w_pieces=w_pieces, n_aux=len(aux), n_scratch=len(scratch),
                          finish_fn=finish_fn, rider_fn=rider_fn),
        grid=(n_steps + 1,),
        in_specs=[pl.BlockSpec((tm, d), lambda t: (cur(t)[1], 0))] + w_specs
                 + [aux_spec(block, index) for _, block, index in aux] + rider_in_specs,
        out_specs=[pl.BlockSpec((tm, out_tn), lambda t: (prev(t)[1], prev(t)[0]))]
                  + rider_out_specs,
        out_shape=[jax.ShapeDtypeStruct((rows, n_col_tiles * out_tn), BF16)] + rider_out_shapes,
        scratch_shapes=[pltpu.VMEM((d, tn), BF16), pltpu.VMEM((2, tm, tn), F32), *scratch],
        compiler_params=_params(("arbitrary",)),
        name=name,
    )(h, *([w] * len(w_specs)), *[arr for arr, _, _ in aux], *rider_operands)
    return results if rider is not None else results[0]


def _mem_proj_kernel(h_ref, w_ref, cos_ref, sin_ref, o_ref, *, tn):
    j = pl.program_id(0)
    acc = jnp.dot(h_ref[...], w_ref[...], preferred_element_type=F32)

    @pl.when(j < XATTN_WIDTH // tn)
    def _():
        o_ref[...] = _rope_two_tiles(acc, cos_ref[...], sin_ref[...]).astype(o_ref.dtype)

    @pl.when(j >= XATTN_WIDTH // tn)
    def _():
        o_ref[...] = acc.astype(o_ref.dtype)


def _mem_proj(hm, w_bf, cos, sin, tn):
    rows, d = hm.shape
    ncols = w_bf.shape[1]
    assert w_bf.dtype == BF16 and XATTN_WIDTH % tn == 0
    tab = pl.BlockSpec((rows, LANES), lambda j: (0, 0))
    return pl.pallas_call(
        functools.partial(_mem_proj_kernel, tn=tn),
        grid=(ncols // tn,),
        in_specs=[pl.BlockSpec((rows, d), lambda j: (0, 0)),
                  pl.BlockSpec((d, tn), lambda j: (0, j)),
                  tab, tab],
        out_specs=pl.BlockSpec((rows, tn), lambda j: (0, j)),
        out_shape=jax.ShapeDtypeStruct((rows, ncols), BF16),
        compiler_params=_params(("parallel",)),
        name="mem_proj",
    )(hm, w_bf, cos, sin)


MAX_SINGLE_ACCESS_STRIDE = 4


def _split_stride(dil):
    if dil <= MAX_SINGLE_ACCESS_STRIDE:
        return dil, 1
    assert dil % MAX_SINGLE_ACCESS_STRIDE == 0
    f2 = dil // MAX_SINGLE_ACCESS_STRIDE
    assert f2 <= MAX_SINGLE_ACCESS_STRIDE
    return MAX_SINGLE_ACCESS_STRIDE, f2


def _dilated_attn_kernel(q0, q1, q2, k0, k1, k2, v0, v1, v2, z0, z1, z2, y0, y1, y2,
                         stage_ref, mid_ref, sub_ref, osub_ref, lsub_ref, onat_ref, lnat_ref,
                         *, seq):
    q_refs, k_refs, v_refs = (q0, q1, q2), (k0, k1, k2), (v0, v1, v2)
    z_refs, y_refs = (z0, z1, z2), (y0, y1, y2)
    nblk = seq // BLOCK
    qi = lax.broadcasted_iota(jnp.int32, (BLOCK, 2 * BLOCK), 0)
    kk = lax.broadcasted_iota(jnp.int32, (BLOCK, 2 * BLOCK), 1)
    dist = qi + BLOCK - kk
    band = (dist >= 0) & (dist <= BLOCK)
    causal = (lax.broadcasted_iota(jnp.int32, (BLOCK, BLOCK), 1)
              <= lax.broadcasted_iota(jnp.int32, (BLOCK, BLOCK), 0))

    rows_of = []
    for g, (window, dil) in enumerate(DIL_GROUPS):
        sub_len = seq // dil
        if dil == 1:
            def group_rows(which, start, size, g=g):
                return (q_refs, k_refs, v_refs)[which][g][0, pl.ds(start, size), :]
        else:
            slab = 3 * (g - 1)
            f1, f2 = _split_stride(dil)
            len1 = seq // f1
            for which, refs in enumerate((q_refs, k_refs, v_refs)):
                stage_ref[slab + which] = refs[g][0].astype(F32)
                if f2 == 1:
                    for r in range(dil):
                        sub_ref[slab + which, r * sub_len:(r + 1) * sub_len, :] = (
                            stage_ref[slab + which, pl.ds(r, sub_len, stride=dil), :].astype(BF16))
                    continue
                for r1 in range(f1):
                    mid_ref[which, r1 * len1:(r1 + 1) * len1, :] = (
                        stage_ref[slab + which, pl.ds(r1, len1, stride=f1), :])
                for r1 in range(f1):
                    for r2 in range(f2):
                        r = r2 * f1 + r1
                        sub_ref[slab + which, r * sub_len:(r + 1) * sub_len, :] = (
                            mid_ref[which, pl.ds(r1 * len1 + r2, sub_len, stride=f2), :]
                            .astype(BF16))

            def group_rows(which, start, size, slab=slab):
                return sub_ref[slab + which, pl.ds(start, size), :]
        rows_of.append(group_rows)

    def attend(g, start, first):
        q = rows_of[g](0, start, BLOCK)
        if first:
            k = rows_of[g](1, start, BLOCK)
            v = rows_of[g](2, start, BLOCK)
            mask = causal
        else:
            k = rows_of[g](1, start - BLOCK, 2 * BLOCK)
            v = rows_of[g](2, start - BLOCK, 2 * BLOCK)
            mask = band
        s = lax.dot_general(q, k, (((1,), (1,)), ((), ())), preferred_element_type=F32)
        s = jnp.where(mask, s, NEG_INF)
        m = jnp.max(s, axis=-1, keepdims=True)
        p = jnp.exp(s - m)
        l = jnp.sum(p, axis=-1, keepdims=True)
        o = jnp.dot(p.astype(BF16), v, preferred_element_type=F32) * (1.0 / l)
        lse = jnp.broadcast_to(m + jnp.log(l), (BLOCK, HEAD_DIM))
        if DIL_GROUPS[g][1] == 1:
            onat_ref[g, pl.ds(start, BLOCK), :] = o
            lnat_ref[g, pl.ds(start, BLOCK), :] = lse
        else:
            osub_ref[g - 1, pl.ds(start, BLOCK), :] = o
            lsub_ref[g - 1, pl.ds(start, BLOCK), :] = lse

    for c in range(nblk):
        for g, (window, dil) in enumerate(DIL_GROUPS):
            attend(g, c * BLOCK, c % (seq // dil // BLOCK) == 0)

    for g, (window, dil) in enumerate(DIL_GROUPS):
        sub_len = seq // dil
        if dil == 1:
            continue
        f1, f2 = _split_stride(dil)
        len1 = seq // f1
        for idx, (nat_ref, subseq_ref) in enumerate(((onat_ref, osub_ref), (lnat_ref, lsub_ref))):
            if f2 == 1:
                for r in range(dil):
                    src = slice(r * sub_len, (r + 1) * sub_len)
                    nat_ref[g, pl.ds(r, sub_len, stride=dil), :] = subseq_ref[g - 1, src, :]
                continue
            mid = 3 + idx
            for r1 in range(f1):
                for r2 in range(f2):
                    r = r2 * f1 + r1
                    mid_ref[mid, pl.ds(r1 * len1 + r2, sub_len, stride=f2), :] = (
                        subseq_ref[g - 1, r * sub_len:(r + 1) * sub_len, :])
            for r1 in range(f1):
                nat_ref[g, pl.ds(r1, len1, stride=f1), :] = mid_ref[mid, r1 * len1:(r1 + 1) * len1, :]

    chunk = 2 * BLOCK
    for t in range(seq // chunk):
        rows = slice(t * chunk, (t + 1) * chunk)
        lse = [lnat_ref[g, rows, :] for g in range(len(DIL_GROUPS))]
        mx = jnp.maximum(jnp.maximum(lse[0], lse[1]), lse[2])
        e = [jnp.exp(x - mx) for x in lse]
        inv = 1.0 / (e[0] + e[1] + e[2])
        for g in range(len(DIL_GROUPS)):
            y_refs[g][0, rows, :] = (onat_ref[g, rows, :] * (e[g] * inv)
                                     * z_refs[g][0, rows, :].astype(F32)).astype(y_refs[g].dtype)


def _dilated_attn(q_src, k_src, v_src, z_src):
    sources = (q_src, k_src, v_src, z_src)
    b, s, _ = q_src[0].shape
    n_groups = len(DIL_GROUPS)
    n_regrouped = n_groups - 1
    assert sum(_split_stride(dil)[1] > 1 for _, dil in DIL_GROUPS) <= 1

    def head_spec(off, g):
        first = (off + g * GROUP_WIDTH) // HEAD_DIM
        return pl.BlockSpec((1, s, HEAD_DIM), lambda bi, hg: (bi, 0, first + hg))

    in_specs = [head_spec(off, g) for _, off in sources for g in range(n_groups)]
    operands = [arr for arr, _ in sources for g in range(n_groups)]
    out_spec = pl.BlockSpec((1, s, HEAD_DIM), lambda bi, hg: (bi, 0, hg))
    return pl.pallas_call(
        functools.partial(_dilated_attn_kernel, seq=s),
        grid=(b, HEADS_PER_DIL),
        in_specs=in_specs,
        out_specs=[out_spec] * n_groups,
        out_shape=[jax.ShapeDtypeStruct((b, s, GROUP_WIDTH), BF16)] * n_groups,
        scratch_shapes=[pltpu.VMEM((3 * n_regrouped, s, HEAD_DIM), F32),
                        pltpu.VMEM((5, s, HEAD_DIM), F32),
                        pltpu.VMEM((3 * n_regrouped, s, HEAD_DIM), BF16),
                        pltpu.VMEM((n_regrouped, s, HEAD_DIM), F32),
                        pltpu.VMEM((n_regrouped, s, HEAD_DIM), F32),
                        pltpu.VMEM((n_groups, s, HEAD_DIM), F32),
                        pltpu.VMEM((n_groups, s, HEAD_DIM), F32)],
        compiler_params=_params(("parallel", "parallel")),
        name="dilated_attn",
    )(*operands)


def _xattn_kernel(q_ref, z_ref, mk_ref, mv_ref, y_ref):
    for h in range(N_XATTN_HEADS):
        cols = slice(h * XATTN_HEAD_DIM, (h + 1) * XATTN_HEAD_DIM)
        s = lax.dot_general(q_ref[0, :, cols], mk_ref[0, :, cols],
                            (((1,), (1,)), ((), ())), preferred_element_type=F32)
        m = jnp.max(s, axis=-1, keepdims=True)
        p = jnp.exp(s - m)
        l = jnp.sum(p, axis=-1, keepdims=True)
        o = jnp.dot(p.astype(BF16), mv_ref[0, :, cols], preferred_element_type=F32)
        y_ref[0, :, cols] = (o * (1.0 / l) * z_ref[0, :, cols].astype(F32)).astype(y_ref.dtype)


def _xattn(q_src, z_src, mkv3, tm):
    b, s, _ = q_src[0].shape
    m_len = mkv3.shape[1]
    q_off, z_off = q_src[1], z_src[1]
    assert q_off % XATTN_WIDTH == 0 and z_off % XATTN_WIDTH == 0
    return pl.pallas_call(
        _xattn_kernel,
        grid=(b, s // tm),
        in_specs=[pl.BlockSpec((1, tm, XATTN_WIDTH), lambda bi, i: (bi, i, q_off // XATTN_WIDTH)),
                  pl.BlockSpec((1, tm, XATTN_WIDTH), lambda bi, i: (bi, i, z_off // XATTN_WIDTH)),
                  pl.BlockSpec((1, m_len, XATTN_WIDTH), lambda bi, i: (bi, 0, 0)),
                  pl.BlockSpec((1, m_len, XATTN_WIDTH), lambda bi, i: (bi, 0, 1))],
        out_specs=pl.BlockSpec((1, tm, XATTN_WIDTH), lambda bi, i: (bi, i, 0)),
        out_shape=jax.ShapeDtypeStruct((b, s, XATTN_WIDTH), BF16),
        compiler_params=_params(("parallel", "parallel")),
        name="mem_xattn",
    )(q_src[0], z_src[0], mkv3, mkv3)


def _out_proj_kernel(*refs, ni, d, n_mix):
    y_refs, w_refs = refs[:n_mix], refs[n_mix:2 * n_mix]
    x_ref, g_ref, o_ref, y2_ref, ssq_ref, scale_ref = refs[2 * n_mix:]
    i = pl.program_id(0)
    j = pl.program_id(1)

    @pl.when(j == 0)
    def _():
        @pl.when(i > 0)
        def _():
            scale_ref[...] = lax.rsqrt(ssq_ref[...] * (1.0 / d) + EPS)
        ssq_ref[...] = jnp.zeros_like(ssq_ref)

    def finish_previous_tile():
        o_ref[...] = x_ref[...] + y2_ref[j] * scale_ref[...] * g_ref[...]

    def multiply_this_tile():
        y2 = jnp.dot(y_refs[0][...], w_refs[0][...], preferred_element_type=F32)
        for y_ref, w_ref in zip(y_refs[1:], w_refs[1:]):
            y2 = y2 + jnp.dot(y_ref[...], w_ref[...], preferred_element_type=F32)
        y2_ref[j] = y2
        ssq_ref[...] += jnp.sum(y2 * y2, axis=-1, keepdims=True)

    @pl.when(i == 0)
    def _():
        multiply_this_tile()

    @pl.when((i > 0) & (i < ni))
    def _():
        finish_previous_tile()
        multiply_this_tile()

    @pl.when(i == ni)
    def _():
        finish_previous_tile()


def _out_proj(ys, w_bf, x2d, gain, tm, tn):
    rows, d = x2d.shape
    nj = d // tn
    ni = rows // tm

    def y_spec(width):
        return pl.BlockSpec((tm, width), lambda i, j: (jnp.minimum(i, ni - 1), 0))

    def w_spec(width, row_off):
        assert row_off % width == 0
        return pl.BlockSpec((width, tn),
                            lambda i, j: (row_off // width, jnp.where(i == ni, nj - 1, j)))

    io_spec = pl.BlockSpec((tm, tn),
                           lambda i, j: (jnp.maximum(i - 1, 0), jnp.where(i == 0, 0, j)))
    widths = [y.shape[1] for y in ys]
    offsets = [sum(widths[:k]) for k in range(len(widths))]
    return pl.pallas_call(
        functools.partial(_out_proj_kernel, ni=ni, d=d, n_mix=len(ys)),
        grid=(ni + 1, nj),
        in_specs=([y_spec(w) for w in widths]
                  + [w_spec(w, off) for w, off in zip(widths, offsets)]
                  + [io_spec, pl.BlockSpec((1, tn), lambda i, j: (0, j))]),
        out_specs=io_spec,
        out_shape=jax.ShapeDtypeStruct((rows, d), F32),
        scratch_shapes=[pltpu.VMEM((nj, tm, tn), F32),
                        pltpu.VMEM((tm, 1), F32),
                        pltpu.VMEM((tm, 1), F32)],
        compiler_params=_params(("arbitrary", "arbitrary")),
        name="out_proj",
    )(*ys, *([w_bf] * len(ys)), x2d, gain.reshape(1, d))


def _rope_tables(pos, half):
    inv = 1.0 / (ROPE_THETA ** (np.arange(half, dtype=np.float64) / half))
    ang = np.asarray(pos, dtype=np.float64)[:, None] * inv[None, :]
    return np.cos(ang), np.sin(ang)


def kernel(x, mem, pre_norm, w_in, conv_w, mem_norm, w_mem_kv, w_out, post_norm):
    b, s, d = x.shape
    m_len = mem.shape[1]
    depth = w_in.shape[0]
    assert all(window // dil == BLOCK for window, dil in DIL_GROUPS)
    pos = np.arange(s)
    cos_a, sin_a = _rope_tables(pos, HEAD_DIM // 2)
    cosa = np.concatenate([cos_a, cos_a], axis=-1)
    sina = np.concatenate([-sin_a, sin_a], axis=-1)
    a_scale = HEAD_DIM ** -0.5
    cos_qkv = jnp.asarray(np.stack([cosa * a_scale, cosa, np.ones_like(cosa)]), F32)
    sin_qkv = jnp.asarray(np.stack([sina * a_scale, sina, np.zeros_like(sina)]), F32)
    cosx, sinx = _rope_tables(pos + m_len, XATTN_HEAD_DIM // 2)
    x_scale = XATTN_HEAD_DIM ** -0.5
    cos_qx = jnp.asarray((cosx * x_scale)[None], F32)
    sin_qx = jnp.asarray((sinx * x_scale)[None], F32)
    cos_m, sin_m = _rope_tables(np.arange(m_len), XATTN_HEAD_DIM // 2)
    cosm = jnp.asarray(np.tile(cos_m, (b, 1)), F32)
    sinm = jnp.asarray(np.tile(sin_m, (b, 1)), F32)

    tm, tn = 1024, 512

    def tile_of(off):
        assert off % tn == 0
        return off // tn

    attn_tiles, x_tiles = ATTN_WIDTH // tn, XATTN_WIDTH // tn
    tiles_per_seq = s // tm

    def view(t):
        return t.reshape(b, s, t.shape[-1])

    def rope_tables(cos, sin, kind):
        def index(col_tile, row_tile):
            return kind(col_tile), row_tile % tiles_per_seq, 0
        return [(cos, (1, tm, LANES), index), (sin, (1, tm, LANES), index)]

    def lane_slab(off):
        assert off % LANES == 0
        return lambda j: off // LANES + j

    for layer in range(depth):
        x2d = x.reshape(b * s, d)
        w = w_in[layer]
        common = dict(tm=tm, tn=tn)
        w_qx_bf = _cast_columns_bf16(w, OFF_QX, XATTN_WIDTH, tr=2048)
        h, qx = _norm_project(x2d, pre_norm[layer], w_qx_bf, cos_qx, sin_qx, seq=s, tm=512)
        qx = view(qx)

        qkv, w_mem_bf = _in_proj(h, w, [lambda j: tile_of(OFF_QA) + j], 3 * attn_tiles,
                                 _elementwise_finish(_rope_half_lane),
                                 rope_tables(cos_qkv, sin_qkv, lambda j: j // attn_tiles),
                                 rider=(lambda block: block, w_mem_kv[layer], []),
                                 name="in_proj_qkv", w_pieces=2, **common)
        qkv = view(qkv)
        y_conv, w_out_bf = _in_proj(
            h, w, [lane_slab(OFF_UC), lane_slab(OFF_BC), lane_slab(OFF_CC), lane_slab(OFF_ZC)],
            CONV_WIDTH // LANES,
            functools.partial(_gated_conv_finish, tiles_per_seq=tiles_per_seq),
            [(conv_w[layer], (CONV_K, LANES), lambda col_tile, row_tile: (0, col_tile))],
            out_tn=LANES, scratch=[pltpu.VMEM((CONV_HALO, LANES), F32)],
            rider=(lambda block: block, w_out[layer], []), name="in_proj_conv", w_pieces=1,
            **common)
        gates, hm = _in_proj(
            h, w, [lambda j: jnp.where(j < x_tiles, tile_of(OFF_ZX) + j,
                                       tile_of(OFF_ZA) - x_tiles + j)],
            x_tiles + attn_tiles, _elementwise_finish(_silu), [], name="in_proj_gates",
            rider=(_rms_normalize, mem.reshape(b * m_len, d), [mem_norm[layer].reshape(1, d)]),
            w_pieces=2, **common)
        gates = view(gates)
        mkv = _mem_proj(hm, w_mem_bf, cosm, sinm, tn=1024)

        y_groups = _dilated_attn((qkv, 0), (qkv, ATTN_WIDTH), (qkv, 2 * ATTN_WIDTH),
                                 (gates, XATTN_WIDTH))
        y_groups = [y.reshape(b * s, GROUP_WIDTH) for y in y_groups]
        y_x = _xattn((qx, 0), (gates, 0), mkv.reshape(b, m_len, 2 * XATTN_WIDTH), tm=s)
        y_x = y_x.reshape(b * s, XATTN_WIDTH)

        out = _out_proj([*y_groups, y_conv, y_x], w_out_bf, x2d,
                        post_norm[layer], tm=1024, tn=512)
        x = out.reshape(b, s, d)
    return x
```

```python
import functools

import jax
import jax.numpy as jnp
import numpy as np
from jax import lax
from jax.experimental import pallas as pl
from jax.experimental.pallas import tpu as pltpu

D_MODEL = 4096
MEM_LEN = 256
HEAD_DIM = 128
DIL_GROUPS = ((128, 1), (512, 4), (2048, 16))
ATTN_WIDTH = 3 * D_MODEL // 8
CONV_WIDTH = 3 * D_MODEL // 8
XATTN_WIDTH = D_MODEL // 4
N_XATTN_HEADS = 4
XATTN_HEAD_DIM = XATTN_WIDTH // N_XATTN_HEADS
HEADS_PER_DIL = 4
GROUP_WIDTH = HEADS_PER_DIL * HEAD_DIM
CONV_K = 3
BLOCK = 128
ROPE_THETA = 10000.0
EPS = 1e-6
NEG_INF = -1e30

OFF_QA = 0
OFF_KA = OFF_QA + ATTN_WIDTH
OFF_VA = OFF_KA + ATTN_WIDTH
OFF_ZA = OFF_VA + ATTN_WIDTH
OFF_UC = OFF_ZA + ATTN_WIDTH
OFF_BC = OFF_UC + CONV_WIDTH
OFF_CC = OFF_BC + CONV_WIDTH
OFF_ZC = OFF_CC + CONV_WIDTH
OFF_QX = OFF_ZC + CONV_WIDTH
OFF_ZX = OFF_QX + XATTN_WIDTH

V7X_VMEM_LIMIT_BYTES = 56 * 1024 * 1024
LANES = 128
W_PIECES = 4
EPILOGUE_ROWS = 128
CONV_HALO = 8
NORM_ROWS = 64
RIDER_ROWS = 16

BF16 = jnp.bfloat16
F32 = jnp.float32


def _params(semantics):
    return pltpu.CompilerParams(dimension_semantics=semantics,
                                vmem_limit_bytes=V7X_VMEM_LIMIT_BYTES)


def _rms_normalize(x, gain):
    ms = jnp.mean(x * x, axis=-1, keepdims=True)
    return x * lax.rsqrt(ms + EPS) * gain


def _cast_kernel(x_ref, o_ref):
    o_ref[...] = x_ref[...].astype(o_ref.dtype)


def _cast_columns_bf16(w, col_off, width, tr):
    rows = w.shape[0]
    assert col_off % width == 0
    return pl.pallas_call(
        _cast_kernel,
        grid=(rows // tr,),
        in_specs=[pl.BlockSpec((tr, width), lambda i: (i, col_off // width))],
        out_specs=pl.BlockSpec((tr, width), lambda i: (i, 0)),
        out_shape=jax.ShapeDtypeStruct((rows, width), BF16),
        compiler_params=_params(("parallel",)),
        name="cast_w_qx",
    )(w)


def _norm_project_kernel(x_ref, g_ref, w_ref, cos_ref, sin_ref, h_ref, o_ref, acc_ref):
    tm = x_ref.shape[0]
    for r0 in range(0, tm, NORM_ROWS):
        rows = slice(r0, r0 + NORM_ROWS)
        h_ref[rows, :] = _rms_normalize(x_ref[rows, :], g_ref[...]).astype(h_ref.dtype)
    acc_ref[...] = jnp.dot(h_ref[...], w_ref[...], preferred_element_type=F32)
    for r0 in range(0, tm, EPILOGUE_ROWS):
        rows = slice(r0, r0 + EPILOGUE_ROWS)
        o_ref[rows, :] = _rope_two_tiles(acc_ref[rows, :], cos_ref[0, rows, :],
                                         sin_ref[0, rows, :]).astype(o_ref.dtype)


def _norm_project(x2d, gain, w_bf, cos, sin, seq, tm):
    rows, d = x2d.shape
    width = w_bf.shape[1]
    tiles_per_seq = seq // tm
    table = pl.BlockSpec((1, tm, LANES), lambda i: (0, i % tiles_per_seq, 0))
    return pl.pallas_call(
        _norm_project_kernel,
        grid=(rows // tm,),
        in_specs=[pl.BlockSpec((tm, d), lambda i: (i, 0)),
                  pl.BlockSpec((1, d), lambda i: (0, 0)),
                  pl.BlockSpec((d, width), lambda i: (0, 0)),
                  table, table],
        out_specs=[pl.BlockSpec((tm, d), lambda i: (i, 0)),
                   pl.BlockSpec((tm, width), lambda i: (i, 0))],
        out_shape=[jax.ShapeDtypeStruct((rows, d), BF16),
                   jax.ShapeDtypeStruct((rows, width), BF16)],
        scratch_shapes=[pltpu.VMEM((tm, width), F32)],
        compiler_params=_params(("arbitrary",)),
        name="norm_project_qx",
    )(x2d, gain.reshape(1, d), w_bf, cos, sin)


def _silu(z):
    half = 0.5 * z
    return half + half * jnp.tanh(half)


def _rope_half_lane(acc, cos, sin_signed):
    outs = []
    for h in range(acc.shape[1] // HEAD_DIM):
        xh = acc[:, h * HEAD_DIM:(h + 1) * HEAD_DIM]
        outs.append(xh * cos + pltpu.roll(xh, HEAD_DIM // 2, axis=1) * sin_signed)
    return jnp.concatenate(outs, axis=1)


def _rope_two_tiles(acc, cos, sin):
    outs = []
    for h in range(acc.shape[1] // XATTN_HEAD_DIM):
        x1 = acc[:, h * XATTN_HEAD_DIM:h * XATTN_HEAD_DIM + LANES]
        x2 = acc[:, h * XATTN_HEAD_DIM + LANES:(h + 1) * XATTN_HEAD_DIM]
        outs.append(x1 * cos - x2 * sin)
        outs.append(x2 * cos + x1 * sin)
    return jnp.concatenate(outs, axis=1)


def _elementwise_finish(epilogue):
    def finish(acc_ref, slot, o_ref, aux_refs, scratch_refs, row_tile):
        for r0 in range(0, o_ref.shape[0], EPILOGUE_ROWS):
            rows = slice(r0, r0 + EPILOGUE_ROWS)
            o_ref[rows, :] = epilogue(acc_ref[slot, rows, :],
                                      *[r[0, rows, :] for r in aux_refs]).astype(o_ref.dtype)
    return finish


def _gated_conv_finish(acc_ref, slot, o_ref, aux_refs, scratch_refs, row_tile, *, tiles_per_seq):
    (w_ref,), (carry_ref,) = aux_refs, scratch_refs
    tm = o_ref.shape[0]
    u_cols, b_cols, c_cols, z_cols = (slice(k * LANES, (k + 1) * LANES) for k in range(4))
    w = w_ref[...]
    tail = jnp.where(row_tile % tiles_per_seq == 0, 0.0, carry_ref[...])
    for r0 in range(0, tm, EPILOGUE_ROWS):
        rows = slice(r0, r0 + EPILOGUE_ROWS)
        cu = acc_ref[slot, rows, c_cols] * acc_ref[slot, rows, u_cols]
        window = jnp.concatenate([tail, cu], axis=0)
        conv = cu * w[CONV_K - 1:CONV_K, :]
        for lag in range(1, CONV_K):
            shifted = pltpu.roll(window, lag, axis=0)[CONV_HALO:, :]
            conv = conv + shifted * w[CONV_K - 1 - lag:CONV_K - lag, :]
        o_ref[rows, :] = (acc_ref[slot, rows, b_cols] * conv
                          * _silu(acc_ref[slot, rows, z_cols])).astype(o_ref.dtype)
        tail = cu[EPILOGUE_ROWS - CONV_HALO:, :]
    carry_ref[...] = tail


def _in_proj_kernel(*refs, ni, n_steps, n_w, w_pieces, n_aux, n_scratch, finish_fn, rider_fn):
    h_ref = refs[0]
    w_refs = refs[1:1 + n_w]
    aux_refs = refs[1 + n_w:1 + n_w + n_aux]
    scratch_refs = refs[len(refs) - n_scratch:] if n_scratch else ()
    rest = refs[1 + n_w + n_aux:len(refs) - n_scratch]
    if rider_fn is not None:
        rider_in_ref, *rider_row_refs = rest[:-4]
        o_ref, rider_out_ref, wbf_ref, acc_ref = rest[-4:]
    else:
        o_ref, wbf_ref, acc_ref = rest
    t = pl.program_id(0)
    piece_rows = wbf_ref.shape[0] // w_pieces
    slab_cols = wbf_ref.shape[1] // (n_w // w_pieces)

    @pl.when((t % ni == 0) & (t < n_steps))
    def _():
        for k, w_ref in enumerate(w_refs):
            slab, p = divmod(k, w_pieces)
            wbf_ref[p * piece_rows:(p + 1) * piece_rows,
                    slab * slab_cols:(slab + 1) * slab_cols] = w_ref[...].astype(BF16)

    def ride():
        for r0 in range(0, rider_in_ref.shape[0], RIDER_ROWS):
            rows = slice(r0, r0 + RIDER_ROWS)
            rider_out_ref[rows, :] = rider_fn(
                rider_in_ref[rows, :], *[r[...] for r in rider_row_refs]
            ).astype(rider_out_ref.dtype)

    def multiply(slot):
        acc_ref[slot] = jnp.dot(h_ref[...], wbf_ref[...], preferred_element_type=F32)
        if rider_fn is not None:
            ride()

    def finish(slot):
        finish_fn(acc_ref, slot, o_ref, aux_refs, scratch_refs, ((t - 1) % ni))

    @pl.when(t == 0)
    def _():
        for ref in scratch_refs:
            ref[...] = jnp.zeros_like(ref)
        multiply(0)

    for parity in range(2):
        @pl.when((t > 0) & (t < n_steps) & (t % 2 == parity))
        def _(parity=parity):
            finish(1 - parity)
            multiply(parity)

    @pl.when(t == n_steps)
    def _():
        finish((n_steps - 1) % 2)


def _in_proj(h, w, col_slabs, n_col_tiles, finish_fn, aux, tm, tn, name, out_tn=None,
             scratch=(), rider=None, w_pieces=W_PIECES):
    rows, d = h.shape
    ni = rows // tm
    n_steps = n_col_tiles * ni
    out_tn = tn if out_tn is None else out_tn
    slab_cols = tn // len(col_slabs)
    rider_in_specs, rider_out_specs, rider_out_shapes, rider_operands = [], [], [], []
    rider_fn, rider_steps = None, 0
    if rider is not None:
        rider_fn, rider_matrix, rider_rows = rider
        r_rows, r_cols = rider_matrix.shape
        assert all(r.shape == (1, r_cols) for r in rider_rows)
        rider_steps = max(k for k in range(1, n_steps + 1) if r_rows % k == 0)
        rider_spec = pl.BlockSpec((r_rows // rider_steps, r_cols),
                                  lambda t: (jnp.minimum(t, rider_steps - 1), 0))
        rider_in_specs = [rider_spec] + [pl.BlockSpec((1, r_cols), lambda t: (0, 0))
                                         for _ in rider_rows]
        rider_out_specs = [rider_spec]
        rider_out_shapes = [jax.ShapeDtypeStruct(rider_matrix.shape, BF16)]
        rider_operands = [rider_matrix, *rider_rows]

    def cur(t):
        c = jnp.minimum(t, n_steps - 1)
        return c // ni, c % ni

    def prev(t):
        p = jnp.maximum(t - 1, 0)
        return p // ni, p % ni

    def aux_spec(block, index):
        return pl.BlockSpec(block, lambda t: index(*prev(t)))

    def w_piece_spec(col_slab, p):
        def index(t):
            j, i = cur(t)
            ahead = (i >= ni - w_pieces + p).astype(jnp.int32)
            return p, col_slab(jnp.minimum(j + ahead, n_col_tiles - 1))
        return pl.BlockSpec((d // w_pieces, slab_cols), index)

    assert ni >= w_pieces and d % w_pieces == 0
    w_specs = [w_piece_spec(col_slab, p) for col_slab in col_slabs for p in range(w_pieces)]
    results = pl.pallas_call(
        functools.partial(_in_proj_kernel, ni=ni, n_steps=n_steps, n_w=len(w_specs),
                          w_pieces=w_pieces, n_aux=len(aux), n_scratch=len(scratch),
                          finish_fn=finish_fn, rider_fn=rider_fn),
        grid=(n_steps + 1,),
        in_specs=[pl.BlockSpec((tm, d), lambda t: (cur(t)[1], 0))] + w_specs
                 + [aux_spec(block, index) for _, block, index in aux] + rider_in_specs,
        out_specs=[pl.BlockSpec((tm, out_tn), lambda t: (prev(t)[1], prev(t)[0]))]
                  + rider_out_specs,
        out_shape=[jax.ShapeDtypeStruct((rows, n_col_tiles * out_tn), BF16)] + rider_out_shapes,
        scratch_shapes=[pltpu.VMEM((d, tn), BF16), pltpu.VMEM((2, tm, tn), F32), *scratch],
        compiler_params=_params(("arbitrary",)),
        name=name,
    )(h, *([w] * len(w_specs)), *[arr for arr, _, _ in aux], *rider_operands)
    return results if rider is not None else results[0]


def _mem_proj_kernel(h_ref, w_ref, cos_ref, sin_ref, o_ref, *, tn):
    j = pl.program_id(0)
    acc = jnp.dot(h_ref[...], w_ref[...].astype(BF16), preferred_element_type=F32)

    @pl.when(j < XATTN_WIDTH // tn)
    def _():
        o_ref[...] = _rope_two_tiles(acc, cos_ref[...], sin_ref[...]).astype(o_ref.dtype)

    @pl.when(j >= XATTN_WIDTH // tn)
    def _():
        o_ref[...] = acc.astype(o_ref.dtype)


def _mem_proj(hm, w, cos, sin, tn):
    rows, d = hm.shape
    ncols = w.shape[1]
    tab = pl.BlockSpec((rows, LANES), lambda j: (0, 0))
    return pl.pallas_call(
        functools.partial(_mem_proj_kernel, tn=tn),
        grid=(ncols // tn,),
        in_specs=[pl.BlockSpec((rows, d), lambda j: (0, 0)),
                  pl.BlockSpec((d, tn), lambda j: (0, j)),
                  tab, tab],
        out_specs=pl.BlockSpec((rows, tn), lambda j: (0, j)),
        out_shape=jax.ShapeDtypeStruct((rows, ncols), BF16),
        compiler_params=_params(("parallel",)),
        name="mem_proj",
    )(hm, w, cos, sin)


MAX_SINGLE_ACCESS_STRIDE = 4


def _split_stride(dil):
    if dil <= MAX_SINGLE_ACCESS_STRIDE:
        return dil, 1
    assert dil % MAX_SINGLE_ACCESS_STRIDE == 0
    f2 = dil // MAX_SINGLE_ACCESS_STRIDE
    assert f2 <= MAX_SINGLE_ACCESS_STRIDE
    return MAX_SINGLE_ACCESS_STRIDE, f2


def _dilated_attn_kernel(q0, q1, q2, k0, k1, k2, v0, v1, v2, z0, z1, z2, y0, y1, y2,
                         stage_ref, mid_ref, sub_ref, osub_ref, lsub_ref, onat_ref, lnat_ref,
                         *, seq):
    q_refs, k_refs, v_refs = (q0, q1, q2), (k0, k1, k2), (v0, v1, v2)
    z_refs, y_refs = (z0, z1, z2), (y0, y1, y2)
    nblk = seq // BLOCK
    qi = lax.broadcasted_iota(jnp.int32, (BLOCK, 2 * BLOCK), 0)
    kk = lax.broadcasted_iota(jnp.int32, (BLOCK, 2 * BLOCK), 1)
    dist = qi + BLOCK - kk
    band = (dist >= 0) & (dist <= BLOCK)
    causal = (lax.broadcasted_iota(jnp.int32, (BLOCK, BLOCK), 1)
              <= lax.broadcasted_iota(jnp.int32, (BLOCK, BLOCK), 0))

    rows_of = []
    for g, (window, dil) in enumerate(DIL_GROUPS):
        sub_len = seq // dil
        if dil == 1:
            def group_rows(which, start, size, g=g):
                return (q_refs, k_refs, v_refs)[which][g][0, pl.ds(start, size), :]
        else:
            slab = 3 * (g - 1)
            f1, f2 = _split_stride(dil)
            len1 = seq // f1
            for which, refs in enumerate((q_refs, k_refs, v_refs)):
                stage_ref[slab + which] = refs[g][0].astype(F32)
                if f2 == 1:
                    for r in range(dil):
                        sub_ref[slab + which, r * sub_len:(r + 1) * sub_len, :] = (
                            stage_ref[slab + which, pl.ds(r, sub_len, stride=dil), :].astype(BF16))
                    continue
                for r1 in range(f1):
                    mid_ref[which, r1 * len1:(r1 + 1) * len1, :] = (
                        stage_ref[slab + which, pl.ds(r1, len1, stride=f1), :])
                for r1 in range(f1):
                    for r2 in range(f2):
                        r = r2 * f1 + r1
                        sub_ref[slab + which, r * sub_len:(r + 1) * sub_len, :] = (
                            mid_ref[which, pl.ds(r1 * len1 + r2, sub_len, stride=f2), :]
                            .astype(BF16))

            def group_rows(which, start, size, slab=slab):
                return sub_ref[slab + which, pl.ds(start, size), :]
        rows_of.append(group_rows)

    def attend(g, start, first):
        q = rows_of[g](0, start, BLOCK)
        if first:
            k = rows_of[g](1, start, BLOCK)
            v = rows_of[g](2, start, BLOCK)
            mask = causal
        else:
            k = rows_of[g](1, start - BLOCK, 2 * BLOCK)
            v = rows_of[g](2, start - BLOCK, 2 * BLOCK)
            mask = band
        s = lax.dot_general(q, k, (((1,), (1,)), ((), ())), preferred_element_type=F32)
        s = jnp.where(mask, s, NEG_INF)
        m = jnp.max(s, axis=-1, keepdims=True)
        p = jnp.exp(s - m)
        l = jnp.sum(p, axis=-1, keepdims=True)
        o = jnp.dot(p.astype(BF16), v, preferred_element_type=F32) * (1.0 / l)
        lse = jnp.broadcast_to(m + jnp.log(l), (BLOCK, HEAD_DIM))
        if DIL_GROUPS[g][1] == 1:
            onat_ref[g, pl.ds(start, BLOCK), :] = o
            lnat_ref[g, pl.ds(start, BLOCK), :] = lse
        else:
            osub_ref[g - 1, pl.ds(start, BLOCK), :] = o
            lsub_ref[g - 1, pl.ds(start, BLOCK), :] = lse

    for c in range(nblk):
        for g, (window, dil) in enumerate(DIL_GROUPS):
            attend(g, c * BLOCK, c % (seq // dil // BLOCK) == 0)

    for g, (window, dil) in enumerate(DIL_GROUPS):
        sub_len = seq // dil
        if dil == 1:
            continue
        f1, f2 = _split_stride(dil)
        len1 = seq // f1
        for idx, (nat_ref, subseq_ref) in enumerate(((onat_ref, osub_ref), (lnat_ref, lsub_ref))):
            if f2 == 1:
                for r in range(dil):
                    src = slice(r * sub_len, (r + 1) * sub_len)
                    nat_ref[g, pl.ds(r, sub_len, stride=dil), :] = subseq_ref[g - 1, src, :]
                continue
            mid = 3 + idx
            for r1 in range(f1):
                for r2 in range(f2):
                    r = r2 * f1 + r1
                    mid_ref[mid, pl.ds(r1 * len1 + r2, sub_len, stride=f2), :] = (
                        subseq_ref[g - 1, r * sub_len:(r + 1) * sub_len, :])
            for r1 in range(f1):
                nat_ref[g, pl.ds(r1, len1, stride=f1), :] = mid_ref[mid, r1 * len1:(r1 + 1) * len1, :]

    chunk = 2 * BLOCK
    for t in range(seq // chunk):
        rows = slice(t * chunk, (t + 1) * chunk)
        lse = [lnat_ref[g, rows, :] for g in range(len(DIL_GROUPS))]
        mx = jnp.maximum(jnp.maximum(lse[0], lse[1]), lse[2])
        e = [jnp.exp(x - mx) for x in lse]
        inv = 1.0 / (e[0] + e[1] + e[2])
        for g in range(len(DIL_GROUPS)):
            y_refs[g][0, rows, :] = (onat_ref[g, rows, :] * (e[g] * inv)
                                     * z_refs[g][0, rows, :].astype(F32)).astype(y_refs[g].dtype)


def _dilated_attn(q_src, k_src, v_src, z_src):
    sources = (q_src, k_src, v_src, z_src)
    b, s, _ = q_src[0].shape
    n_groups = len(DIL_GROUPS)
    n_regrouped = n_groups - 1
    assert sum(_split_stride(dil)[1] > 1 for _, dil in DIL_GROUPS) <= 1

    def head_spec(off, g):
        first = (off + g * GROUP_WIDTH) // HEAD_DIM
        return pl.BlockSpec((1, s, HEAD_DIM), lambda bi, hg: (bi, 0, first + hg))

    in_specs = [head_spec(off, g) for _, off in sources for g in range(n_groups)]
    operands = [arr for arr, _ in sources for g in range(n_groups)]
    out_spec = pl.BlockSpec((1, s, HEAD_DIM), lambda bi, hg: (bi, 0, hg))
    return pl.pallas_call(
        functools.partial(_dilated_attn_kernel, seq=s),
        grid=(b, HEADS_PER_DIL),
        in_specs=in_specs,
        out_specs=[out_spec] * n_groups,
        out_shape=[jax.ShapeDtypeStruct((b, s, GROUP_WIDTH), BF16)] * n_groups,
        scratch_shapes=[pltpu.VMEM((3 * n_regrouped, s, HEAD_DIM), F32),
                        pltpu.VMEM((5, s, HEAD_DIM), F32),
                        pltpu.VMEM((3 * n_regrouped, s, HEAD_DIM), BF16),
                        pltpu.VMEM((n_regrouped, s, HEAD_DIM), F32),
                        pltpu.VMEM((n_regrouped, s, HEAD_DIM), F32),
                        pltpu.VMEM((n_groups, s, HEAD_DIM), F32),
                        pltpu.VMEM((n_groups, s, HEAD_DIM), F32)],
        compiler_params=_params(("parallel", "parallel")),
        name="dilated_attn",
    )(*operands)


def _xattn_kernel(q_ref, z_ref, mk_ref, mv_ref, y_ref):
    for h in range(N_XATTN_HEADS):
        cols = slice(h * XATTN_HEAD_DIM, (h + 1) * XATTN_HEAD_DIM)
        s = lax.dot_general(q_ref[0, :, cols], mk_ref[0, :, cols],
                            (((1,), (1,)), ((), ())), preferred_element_type=F32)
        m = jnp.max(s, axis=-1, keepdims=True)
        p = jnp.exp(s - m)
        l = jnp.sum(p, axis=-1, keepdims=True)
        o = jnp.dot(p.astype(BF16), mv_ref[0, :, cols], preferred_element_type=F32)
        y_ref[0, :, cols] = (o * (1.0 / l) * z_ref[0, :, cols].astype(F32)).astype(y_ref.dtype)


def _xattn(q_src, z_src, mkv3, tm):
    b, s, _ = q_src[0].shape
    m_len = mkv3.shape[1]
    q_off, z_off = q_src[1], z_src[1]
    assert q_off % XATTN_WIDTH == 0 and z_off % XATTN_WIDTH == 0
    return pl.pallas_call(
        _xattn_kernel,
        grid=(b, s // tm),
        in_specs=[pl.BlockSpec((1, tm, XATTN_WIDTH), lambda bi, i: (bi, i, q_off // XATTN_WIDTH)),
                  pl.BlockSpec((1, tm, XATTN_WIDTH), lambda bi, i: (bi, i, z_off // XATTN_WIDTH)),
                  pl.BlockSpec((1, m_len, XATTN_WIDTH), lambda bi, i: (bi, 0, 0)),
                  pl.BlockSpec((1, m_len, XATTN_WIDTH), lambda bi, i: (bi, 0, 1))],
        out_specs=pl.BlockSpec((1, tm, XATTN_WIDTH), lambda bi, i: (bi, i, 0)),
        out_shape=jax.ShapeDtypeStruct((b, s, XATTN_WIDTH), BF16),
        compiler_params=_params(("parallel", "parallel")),
        name="mem_xattn",
    )(q_src[0], z_src[0], mkv3, mkv3)


def _out_proj_kernel(*refs, ni, d, n_mix):
    y_refs, w_refs = refs[:n_mix], refs[n_mix:2 * n_mix]
    x_ref, g_ref, o_ref, y2_ref, ssq_ref, scale_ref = refs[2 * n_mix:]
    i = pl.program_id(0)
    j = pl.program_id(1)

    @pl.when(j == 0)
    def _():
        @pl.when(i > 0)
        def _():
            scale_ref[...] = lax.rsqrt(ssq_ref[...] * (1.0 / d) + EPS)
        ssq_ref[...] = jnp.zeros_like(ssq_ref)

    def finish_previous_tile():
        o_ref[...] = x_ref[...] + y2_ref[j] * scale_ref[...] * g_ref[...]

    def multiply_this_tile():
        y2 = jnp.dot(y_refs[0][...], w_refs[0][...], preferred_element_type=F32)
        for y_ref, w_ref in zip(y_refs[1:], w_refs[1:]):
            y2 = y2 + jnp.dot(y_ref[...], w_ref[...], preferred_element_type=F32)
        y2_ref[j] = y2
        ssq_ref[...] += jnp.sum(y2 * y2, axis=-1, keepdims=True)

    @pl.when(i == 0)
    def _():
        multiply_this_tile()

    @pl.when((i > 0) & (i < ni))
    def _():
        finish_previous_tile()
        multiply_this_tile()

    @pl.when(i == ni)
    def _():
        finish_previous_tile()


def _out_proj(ys, w_bf, x2d, gain, tm, tn):
    rows, d = x2d.shape
    nj = d // tn
    ni = rows // tm

    def y_spec(width):
        return pl.BlockSpec((tm, width), lambda i, j: (jnp.minimum(i, ni - 1), 0))

    def w_spec(width, row_off):
        assert row_off % width == 0
        return pl.BlockSpec((width, tn),
                            lambda i, j: (row_off // width, jnp.where(i == ni, nj - 1, j)))

    io_spec = pl.BlockSpec((tm, tn),
                           lambda i, j: (jnp.maximum(i - 1, 0), jnp.where(i == 0, 0, j)))
    widths = [y.shape[1] for y in ys]
    offsets = [sum(widths[:k]) for k in range(len(widths))]
    return pl.pallas_call(
        functools.partial(_out_proj_kernel, ni=ni, d=d, n_mix=len(ys)),
        grid=(ni + 1, nj),
        in_specs=([y_spec(w) for w in widths]
                  + [w_spec(w, off) for w, off in zip(widths, offsets)]
                  + [io_spec, pl.BlockSpec((1, tn), lambda i, j: (0, j))]),
        out_specs=io_spec,
        out_shape=jax.ShapeDtypeStruct((rows, d), F32),
        scratch_shapes=[pltpu.VMEM((nj, tm, tn), F32),
                        pltpu.VMEM((tm, 1), F32),
                        pltpu.VMEM((tm, 1), F32)],
        compiler_params=_params(("arbitrary", "arbitrary")),
        name="out_proj",
    )(*ys, *([w_bf] * len(ys)), x2d, gain.reshape(1, d))


def _rope_tables(pos, half):
    inv = 1.0 / (ROPE_THETA ** (np.arange(half, dtype=np.float64) / half))
    ang = np.asarray(pos, dtype=np.float64)[:, None] * inv[None, :]
    return np.cos(ang), np.sin(ang)


def kernel(x, mem, pre_norm, w_in, conv_w, mem_norm, w_mem_kv, w_out, post_norm):
    b, s, d = x.shape
    m_len = mem.shape[1]
    depth = w_in.shape[0]
    assert all(window // dil == BLOCK for window, dil in DIL_GROUPS)
    pos = np.arange(s)
    cos_a, sin_a = _rope_tables(pos, HEAD_DIM // 2)
    cosa = np.concatenate([cos_a, cos_a], axis=-1)
    sina = np.concatenate([-sin_a, sin_a], axis=-1)
    a_scale = HEAD_DIM ** -0.5
    cos_qkv = jnp.asarray(np.stack([cosa * a_scale, cosa, np.ones_like(cosa)]), F32)
    sin_qkv = jnp.asarray(np.stack([sina * a_scale, sina, np.zeros_like(sina)]), F32)
    cosx, sinx = _rope_tables(pos + m_len, XATTN_HEAD_DIM // 2)
    x_scale = XATTN_HEAD_DIM ** -0.5
    cos_qx = jnp.asarray((cosx * x_scale)[None], F32)
    sin_qx = jnp.asarray((sinx * x_scale)[None], F32)
    cos_m, sin_m = _rope_tables(np.arange(m_len), XATTN_HEAD_DIM // 2)
    cosm = jnp.asarray(np.tile(cos_m, (b, 1)), F32)
    sinm = jnp.asarray(np.tile(sin_m, (b, 1)), F32)

    tm, tn = 1024, 512

    def tile_of(off):
        assert off % tn == 0
        return off // tn

    attn_tiles, x_tiles = ATTN_WIDTH // tn, XATTN_WIDTH // tn
    tiles_per_seq = s // tm

    def view(t):
        return t.reshape(b, s, t.shape[-1])

    def rope_tables(cos, sin, kind):
        def index(col_tile, row_tile):
            return kind(col_tile), row_tile % tiles_per_seq, 0
        return [(cos, (1, tm, LANES), index), (sin, (1, tm, LANES), index)]

    def lane_slab(off):
        assert off % LANES == 0
        return lambda j: off // LANES + j

    for layer in range(depth):
        x2d = x.reshape(b * s, d)
        w = w_in[layer]
        common = dict(tm=tm, tn=tn)
        w_qx_bf = _cast_columns_bf16(w, OFF_QX, XATTN_WIDTH, tr=2048)
        h, qx = _norm_project(x2d, pre_norm[layer], w_qx_bf, cos_qx, sin_qx, seq=s, tm=512)
        qx = view(qx)

        qkv = view(_in_proj(h, w, [lambda j: tile_of(OFF_QA) + j], 3 * attn_tiles,
                            _elementwise_finish(_rope_half_lane),
                            rope_tables(cos_qkv, sin_qkv, lambda j: j // attn_tiles),
                            name="in_proj_qkv", w_pieces=1, **common))
        y_conv, w_out_bf = _in_proj(
            h, w, [lane_slab(OFF_UC), lane_slab(OFF_BC), lane_slab(OFF_CC), lane_slab(OFF_ZC)],
            CONV_WIDTH // LANES,
            functools.partial(_gated_conv_finish, tiles_per_seq=tiles_per_seq),
            [(conv_w[layer], (CONV_K, LANES), lambda col_tile, row_tile: (0, col_tile))],
            out_tn=LANES, scratch=[pltpu.VMEM((CONV_HALO, LANES), F32)],
            rider=(lambda block: block, w_out[layer], []), name="in_proj_conv", w_pieces=1,
            **common)
        gates, hm = _in_proj(
            h, w, [lambda j: jnp.where(j < x_tiles, tile_of(OFF_ZX) + j,
                                       tile_of(OFF_ZA) - x_tiles + j)],
            x_tiles + attn_tiles, _elementwise_finish(_silu), [], name="in_proj_gates",
            rider=(_rms_normalize, mem.reshape(b * m_len, d), [mem_norm[layer].reshape(1, d)]),
            w_pieces=1, **common)
        gates = view(gates)
        mkv = _mem_proj(hm, w_mem_kv[layer], cosm, sinm, tn=512)

        y_groups = _dilated_attn((qkv, 0), (qkv, ATTN_WIDTH), (qkv, 2 * ATTN_WIDTH),
                                 (gates, XATTN_WIDTH))
        y_groups = [y.reshape(b * s, GROUP_WIDTH) for y in y_groups]
        y_x = _xattn((qx, 0), (gates, 0), mkv.reshape(b, m_len, 2 * XATTN_WIDTH), tm=s)
        y_x = y_x.reshape(b * s, XATTN_WIDTH)

        out = _out_proj([*y_groups, y_conv, y_x], w_out_bf, x2d,
                        post_norm[layer], tm=1024, tn=512)
        x = out.reshape(b, s, d)
    return x
```

```python
import functools

import jax
import jax.numpy as jnp
import numpy as np
from jax import lax
from jax.experimental import pallas as pl
from jax.experimental.pallas import tpu as pltpu

D_MODEL = 4096
MEM_LEN = 256
HEAD_DIM = 128
DIL_GROUPS = ((128, 1), (512, 4), (2048, 16))
ATTN_WIDTH = 3 * D_MODEL // 8
CONV_WIDTH = 3 * D_MODEL // 8
XATTN_WIDTH = D_MODEL // 4
N_XATTN_HEADS = 4
XATTN_HEAD_DIM = XATTN_WIDTH // N_XATTN_HEADS
HEADS_PER_DIL = 4
GROUP_WIDTH = HEADS_PER_DIL * HEAD_DIM
CONV_K = 3
BLOCK = 128
ROPE_THETA = 10000.0
EPS = 1e-6
NEG_INF = -1e30

OFF_QA = 0
OFF_KA = OFF_QA + ATTN_WIDTH
OFF_VA = OFF_KA + ATTN_WIDTH
OFF_ZA = OFF_VA + ATTN_WIDTH
OFF_UC = OFF_ZA + ATTN_WIDTH
OFF_BC = OFF_UC + CONV_WIDTH
OFF_CC = OFF_BC + CONV_WIDTH
OFF_ZC = OFF_CC + CONV_WIDTH
OFF_QX = OFF_ZC + CONV_WIDTH
OFF_ZX = OFF_QX + XATTN_WIDTH

V7X_VMEM_LIMIT_BYTES = 56 * 1024 * 1024
LANES = 128
W_PIECES = 4
EPILOGUE_ROWS = 128
CONV_HALO = 8
NORM_ROWS = 64
RIDER_ROWS = 16

BF16 = jnp.bfloat16
F32 = jnp.float32


def _params(semantics):
    return pltpu.CompilerParams(dimension_semantics=semantics,
                                vmem_limit_bytes=V7X_VMEM_LIMIT_BYTES)


def _rms_normalize(x, gain):
    ms = jnp.mean(x * x, axis=-1, keepdims=True)
    return x * lax.rsqrt(ms + EPS) * gain


def _cast_kernel(x_ref, o_ref):
    o_ref[...] = x_ref[...].astype(o_ref.dtype)


def _cast_columns_bf16(w, col_off, width, tr):
    rows = w.shape[0]
    assert col_off % width == 0
    return pl.pallas_call(
        _cast_kernel,
        grid=(rows // tr,),
        in_specs=[pl.BlockSpec((tr, width), lambda i: (i, col_off // width))],
        out_specs=pl.BlockSpec((tr, width), lambda i: (i, 0)),
        out_shape=jax.ShapeDtypeStruct((rows, width), BF16),
        compiler_params=_params(("parallel",)),
        name="cast_w_qx",
    )(w)


def _norm_project_kernel(x_ref, g_ref, w_ref, cos_ref, sin_ref, h_ref, o_ref, acc_ref):
    tm = x_ref.shape[0]
    for r0 in range(0, tm, NORM_ROWS):
        rows = slice(r0, r0 + NORM_ROWS)
        h_ref[rows, :] = _rms_normalize(x_ref[rows, :], g_ref[...]).astype(h_ref.dtype)
    acc_ref[...] = jnp.dot(h_ref[...], w_ref[...], preferred_element_type=F32)
    for r0 in range(0, tm, EPILOGUE_ROWS):
        rows = slice(r0, r0 + EPILOGUE_ROWS)
        o_ref[rows, :] = _rope_two_tiles(acc_ref[rows, :], cos_ref[0, rows, :],
                                         sin_ref[0, rows, :]).astype(o_ref.dtype)


def _norm_project(x2d, gain, w_bf, cos, sin, seq, tm):
    rows, d = x2d.shape
    width = w_bf.shape[1]
    tiles_per_seq = seq // tm
    table = pl.BlockSpec((1, tm, LANES), lambda i: (0, i % tiles_per_seq, 0))
    return pl.pallas_call(
        _norm_project_kernel,
        grid=(rows // tm,),
        in_specs=[pl.BlockSpec((tm, d), lambda i: (i, 0)),
                  pl.BlockSpec((1, d), lambda i: (0, 0)),
                  pl.BlockSpec((d, width), lambda i: (0, 0)),
                  table, table],
        out_specs=[pl.BlockSpec((tm, d), lambda i: (i, 0)),
                   pl.BlockSpec((tm, width), lambda i: (i, 0))],
        out_shape=[jax.ShapeDtypeStruct((rows, d), BF16),
                   jax.ShapeDtypeStruct((rows, width), BF16)],
        scratch_shapes=[pltpu.VMEM((tm, width), F32)],
        compiler_params=_params(("arbitrary",)),
        name="norm_project_qx",
    )(x2d, gain.reshape(1, d), w_bf, cos, sin)


def _silu(z):
    half = 0.5 * z
    return half + half * jnp.tanh(half)


def _rope_half_lane(acc, cos, sin_signed):
    outs = []
    for h in range(acc.shape[1] // HEAD_DIM):
        xh = acc[:, h * HEAD_DIM:(h + 1) * HEAD_DIM]
        outs.append(xh * cos + pltpu.roll(xh, HEAD_DIM // 2, axis=1) * sin_signed)
    return jnp.concatenate(outs, axis=1)


def _rope_two_tiles(acc, cos, sin):
    outs = []
    for h in range(acc.shape[1] // XATTN_HEAD_DIM):
        x1 = acc[:, h * XATTN_HEAD_DIM:h * XATTN_HEAD_DIM + LANES]
        x2 = acc[:, h * XATTN_HEAD_DIM + LANES:(h + 1) * XATTN_HEAD_DIM]
        outs.append(x1 * cos - x2 * sin)
        outs.append(x2 * cos + x1 * sin)
    return jnp.concatenate(outs, axis=1)


def _elementwise_finish(epilogue):
    def finish(acc_ref, slot, o_ref, aux_refs, scratch_refs, row_tile):
        for r0 in range(0, o_ref.shape[0], EPILOGUE_ROWS):
            rows = slice(r0, r0 + EPILOGUE_ROWS)
            o_ref[rows, :] = epilogue(acc_ref[slot, rows, :],
                                      *[r[0, rows, :] for r in aux_refs]).astype(o_ref.dtype)
    return finish


def _gated_conv_finish(acc_ref, slot, o_ref, aux_refs, scratch_refs, row_tile, *, tiles_per_seq):
    (w_ref,), (carry_ref,) = aux_refs, scratch_refs
    tm = o_ref.shape[0]
    u_cols, b_cols, c_cols, z_cols = (slice(k * LANES, (k + 1) * LANES) for k in range(4))
    w = w_ref[...]
    tail = jnp.where(row_tile % tiles_per_seq == 0, 0.0, carry_ref[...])
    for r0 in range(0, tm, EPILOGUE_ROWS):
        rows = slice(r0, r0 + EPILOGUE_ROWS)
        cu = acc_ref[slot, rows, c_cols] * acc_ref[slot, rows, u_cols]
        window = jnp.concatenate([tail, cu], axis=0)
        conv = cu * w[CONV_K - 1:CONV_K, :]
        for lag in range(1, CONV_K):
            shifted = pltpu.roll(window, lag, axis=0)[CONV_HALO:, :]
            conv = conv + shifted * w[CONV_K - 1 - lag:CONV_K - lag, :]
        o_ref[rows, :] = (acc_ref[slot, rows, b_cols] * conv
                          * _silu(acc_ref[slot, rows, z_cols])).astype(o_ref.dtype)
        tail = cu[EPILOGUE_ROWS - CONV_HALO:, :]
    carry_ref[...] = tail


def _in_proj_kernel(*refs, ni, n_steps, n_w, w_pieces, n_aux, n_scratch, finish_fn, rider_fn):
    h_ref = refs[0]
    w_refs = refs[1:1 + n_w]
    aux_refs = refs[1 + n_w:1 + n_w + n_aux]
    scratch_refs = refs[len(refs) - n_scratch:] if n_scratch else ()
    rest = refs[1 + n_w + n_aux:len(refs) - n_scratch]
    if rider_fn is not None:
        rider_in_ref, *rider_row_refs = rest[:-4]
        o_ref, rider_out_ref, wbf_ref, acc_ref = rest[-4:]
    else:
        o_ref, wbf_ref, acc_ref = rest
    t = pl.program_id(0)
    piece_rows = wbf_ref.shape[0] // w_pieces
    slab_cols = wbf_ref.shape[1] // (n_w // w_pieces)

    @pl.when((t % ni == 0) & (t < n_steps))
    def _():
        for k, w_ref in enumerate(w_refs):
            slab, p = divmod(k, w_pieces)
            wbf_ref[p * piece_rows:(p + 1) * piece_rows,
                    slab * slab_cols:(slab + 1) * slab_cols] = w_ref[...].astype(BF16)

    def ride():
        for r0 in range(0, rider_in_ref.shape[0], RIDER_ROWS):
            rows = slice(r0, r0 + RIDER_ROWS)
            rider_out_ref[rows, :] = rider_fn(
                rider_in_ref[rows, :], *[r[...] for r in rider_row_refs]
            ).astype(rider_out_ref.dtype)

    def multiply(slot):
        acc_ref[slot] = jnp.dot(h_ref[...], wbf_ref[...], preferred_element_type=F32)
        if rider_fn is not None:
            ride()

    def finish(slot):
        finish_fn(acc_ref, slot, o_ref, aux_refs, scratch_refs, ((t - 1) % ni))

    @pl.when(t == 0)
    def _():
        for ref in scratch_refs:
            ref[...] = jnp.zeros_like(ref)
        multiply(0)

    for parity in range(2):
        @pl.when((t > 0) & (t < n_steps) & (t % 2 == parity))
        def _(parity=parity):
            finish(1 - parity)
            multiply(parity)

    @pl.when(t == n_steps)
    def _():
        finish((n_steps - 1) % 2)


def _in_proj(h, w, col_slabs, n_col_tiles, finish_fn, aux, tm, tn, name, out_tn=None,
             scratch=(), rider=None, w_pieces=W_PIECES):
    rows, d = h.shape
    ni = rows // tm
    n_steps = n_col_tiles * ni
    out_tn = tn if out_tn is None else out_tn
    slab_cols = tn // len(col_slabs)
    rider_in_specs, rider_out_specs, rider_out_shapes, rider_operands = [], [], [], []
    rider_fn, rider_steps = None, 0
    if rider is not None:
        rider_fn, rider_matrix, rider_rows = rider
        r_rows, r_cols = rider_matrix.shape
        assert all(r.shape == (1, r_cols) for r in rider_rows)
        rider_steps = max(k for k in range(1, n_steps + 1) if r_rows % k == 0)
        rider_spec = pl.BlockSpec((r_rows // rider_steps, r_cols),
                                  lambda t: (jnp.minimum(t, rider_steps - 1), 0))
        rider_in_specs = [rider_spec] + [pl.BlockSpec((1, r_cols), lambda t: (0, 0))
                                         for _ in rider_rows]
        rider_out_specs = [rider_spec]
        rider_out_shapes = [jax.ShapeDtypeStruct(rider_matrix.shape, BF16)]
        rider_operands = [rider_matrix, *rider_rows]

    def cur(t):
        c = jnp.minimum(t, n_steps - 1)
        return c // ni, c % ni

    def prev(t):
        p = jnp.maximum(t - 1, 0)
        return p // ni, p % ni

    def aux_spec(block, index):
        return pl.BlockSpec(block, lambda t: index(*prev(t)))

    def w_piece_spec(col_slab, p):
        def index(t):
            j, i = cur(t)
            ahead = (i >= ni - w_pieces + p).astype(jnp.int32)
            return p, col_slab(jnp.minimum(j + ahead, n_col_tiles - 1))
        return pl.BlockSpec((d // w_pieces, slab_cols), index)

    assert ni >= w_pieces and d % w_pieces == 0
    w_specs = [w_piece_spec(col_slab, p) for col_slab in col_slabs for p in range(w_pieces)]
    results = pl.pallas_call(
        functools.partial(_in_proj_kernel, ni=ni, n_steps=n_steps, n_w=len(w_specs),
                          w_pieces=w_pieces, n_aux=len(aux), n_scratch=len(scratch),
                          finish_fn=finish_fn, rider_fn=rider_fn),
        grid=(n_steps + 1,),
        in_specs=[pl.BlockSpec((tm, d), lambda t: (cur(t)[1], 0))] + w_specs
                 + [aux_spec(block, index) for _, block, index in aux] + rider_in_specs,
        out_specs=[pl.BlockSpec((tm, out_tn), lambda t: (prev(t)[1], prev(t)[0]))]
                  + rider_out_specs,
        out_shape=[jax.ShapeDtypeStruct((rows, n_col_tiles * out_tn), BF16)] + rider_out_shapes,
        scratch_shapes=[pltpu.VMEM((d, tn), BF16), pltpu.VMEM((2, tm, tn), F32), *scratch],
        compiler_params=_params(("arbitrary",)),
        name=name,
    )(h, *([w] * len(w_specs)), *[arr for arr, _, _ in aux], *rider_operands)
    return results if rider is not None else results[0]


def _mem_proj_kernel(h_ref, w_ref, cos_ref, sin_ref, o_ref, *, tn):
    j = pl.program_id(0)
    acc = jnp.dot(h_ref[...], w_ref[...].astype(BF16), preferred_element_type=F32)

    @pl.when(j < XATTN_WIDTH // tn)
    def _():
        o_ref[...] = _rope_two_tiles(acc, cos_ref[...], sin_ref[...]).astype(o_ref.dtype)

    @pl.when(j >= XATTN_WIDTH // tn)
    def _():
        o_ref[...] = acc.astype(o_ref.dtype)


def _mem_proj(hm, w, cos, sin, tn):
    rows, d = hm.shape
    ncols = w.shape[1]
    tab = pl.BlockSpec((rows, LANES), lambda j: (0, 0))
    return pl.pallas_call(
        functools.partial(_mem_proj_kernel, tn=tn),
        grid=(ncols // tn,),
        in_specs=[pl.BlockSpec((rows, d), lambda j: (0, 0)),
                  pl.BlockSpec((d, tn), lambda j: (0, j)),
                  tab, tab],
        out_specs=pl.BlockSpec((rows, tn), lambda j: (0, j)),
        out_shape=jax.ShapeDtypeStruct((rows, ncols), BF16),
        compiler_params=_params(("parallel",)),
        name="mem_proj",
    )(hm, w, cos, sin)


MAX_SINGLE_ACCESS_STRIDE = 4


def _split_stride(dil):
    if dil <= MAX_SINGLE_ACCESS_STRIDE:
        return dil, 1
    assert dil % MAX_SINGLE_ACCESS_STRIDE == 0
    f2 = dil // MAX_SINGLE_ACCESS_STRIDE
    assert f2 <= MAX_SINGLE_ACCESS_STRIDE
    return MAX_SINGLE_ACCESS_STRIDE, f2


def _dilated_attn_kernel(q0, q1, q2, k0, k1, k2, v0, v1, v2, z0, z1, z2, y0, y1, y2,
                         stage_ref, mid_ref, sub_ref, osub_ref, lsub_ref, onat_ref, lnat_ref,
                         *, seq):
    q_refs, k_refs, v_refs = (q0, q1, q2), (k0, k1, k2), (v0, v1, v2)
    z_refs, y_refs = (z0, z1, z2), (y0, y1, y2)
    nblk = seq // BLOCK
    qi = lax.broadcasted_iota(jnp.int32, (BLOCK, 2 * BLOCK), 0)
    kk = lax.broadcasted_iota(jnp.int32, (BLOCK, 2 * BLOCK), 1)
    dist = qi + BLOCK - kk
    band = (dist >= 0) & (dist <= BLOCK)
    causal = (lax.broadcasted_iota(jnp.int32, (BLOCK, BLOCK), 1)
              <= lax.broadcasted_iota(jnp.int32, (BLOCK, BLOCK), 0))

    rows_of = []
    for g, (window, dil) in enumerate(DIL_GROUPS):
        sub_len = seq // dil
        if dil == 1:
            def group_rows(which, start, size, g=g):
                return (q_refs, k_refs, v_refs)[which][g][0, pl.ds(start, size), :]
        else:
            slab = 3 * (g - 1)
            f1, f2 = _split_stride(dil)
            len1 = seq // f1
            for which, refs in enumerate((q_refs, k_refs, v_refs)):
                stage_ref[slab + which] = refs[g][0].astype(F32)
                if f2 == 1:
                    for r in range(dil):
                        sub_ref[slab + which, r * sub_len:(r + 1) * sub_len, :] = (
                            stage_ref[slab + which, pl.ds(r, sub_len, stride=dil), :].astype(BF16))
                    continue
                for r1 in range(f1):
                    mid_ref[which, r1 * len1:(r1 + 1) * len1, :] = (
                        stage_ref[slab + which, pl.ds(r1, len1, stride=f1), :])
                for r1 in range(f1):
                    for r2 in range(f2):
                        r = r2 * f1 + r1
                        sub_ref[slab + which, r * sub_len:(r + 1) * sub_len, :] = (
                            mid_ref[which, pl.ds(r1 * len1 + r2, sub_len, stride=f2), :]
                            .astype(BF16))

            def group_rows(which, start, size, slab=slab):
                return sub_ref[slab + which, pl.ds(start, size), :]
        rows_of.append(group_rows)

    def attend(g, start, first):
        q = rows_of[g](0, start, BLOCK)
        if first:
            k = rows_of[g](1, start, BLOCK)
            v = rows_of[g](2, start, BLOCK)
            mask = causal
        else:
            k = rows_of[g](1, start - BLOCK, 2 * BLOCK)
            v = rows_of[g](2, start - BLOCK, 2 * BLOCK)
            mask = band
        s = lax.dot_general(q, k, (((1,), (1,)), ((), ())), preferred_element_type=F32)
        s = jnp.where(mask, s, NEG_INF)
        m = jnp.max(s, axis=-1, keepdims=True)
        p = jnp.exp(s - m)
        l = jnp.sum(p, axis=-1, keepdims=True)
        o = jnp.dot(p.astype(BF16), v, preferred_element_type=F32) * (1.0 / l)
        lse = jnp.broadcast_to(m + jnp.log(l), (BLOCK, HEAD_DIM))
        if DIL_GROUPS[g][1] == 1:
            onat_ref[g, pl.ds(start, BLOCK), :] = o
            lnat_ref[g, pl.ds(start, BLOCK), :] = lse
        else:
            osub_ref[g - 1, pl.ds(start, BLOCK), :] = o
            lsub_ref[g - 1, pl.ds(start, BLOCK), :] = lse

    for c in range(nblk):
        for g, (window, dil) in enumerate(DIL_GROUPS):
            attend(g, c * BLOCK, c % (seq // dil // BLOCK) == 0)

    for g, (window, dil) in enumerate(DIL_GROUPS):
        sub_len = seq // dil
        if dil == 1:
            continue
        f1, f2 = _split_stride(dil)
        len1 = seq // f1
        for idx, (nat_ref, subseq_ref) in enumerate(((onat_ref, osub_ref), (lnat_ref, lsub_ref))):
            if f2 == 1:
                for r in range(dil):
                    src = slice(r * sub_len, (r + 1) * sub_len)
                    nat_ref[g, pl.ds(r, sub_len, stride=dil), :] = subseq_ref[g - 1, src, :]
                continue
            mid = 3 + idx
            for r1 in range(f1):
                for r2 in range(f2):
                    r = r2 * f1 + r1
                    mid_ref[mid, pl.ds(r1 * len1 + r2, sub_len, stride=f2), :] = (
                        subseq_ref[g - 1, r * sub_len:(r + 1) * sub_len, :])
            for r1 in range(f1):
                nat_ref[g, pl.ds(r1, len1, stride=f1), :] = mid_ref[mid, r1 * len1:(r1 + 1) * len1, :]

    chunk = 2 * BLOCK
    for t in range(seq // chunk):
        rows = slice(t * chunk, (t + 1) * chunk)
        lse = [lnat_ref[g, rows, :] for g in range(len(DIL_GROUPS))]
        mx = jnp.maximum(jnp.maximum(lse[0], lse[1]), lse[2])
        e = [jnp.exp(x - mx) for x in lse]
        inv = 1.0 / (e[0] + e[1] + e[2])
        for g in range(len(DIL_GROUPS)):
            y_refs[g][0, rows, :] = (onat_ref[g, rows, :] * (e[g] * inv)
                                     * z_refs[g][0, rows, :].astype(F32)).astype(y_refs[g].dtype)


def _dilated_attn(q_src, k_src, v_src, z_src):
    sources = (q_src, k_src, v_src, z_src)
    b, s, _ = q_src[0].shape
    n_groups = len(DIL_GROUPS)
    n_regrouped = n_groups - 1
    assert sum(_split_stride(dil)[1] > 1 for _, dil in DIL_GROUPS) <= 1

    def head_spec(off, g):
        first = (off + g * GROUP_WIDTH) // HEAD_DIM
        return pl.BlockSpec((1, s, HEAD_DIM), lambda bi, hg: (bi, 0, first + hg))

    in_specs = [head_spec(off, g) for _, off in sources for g in range(n_groups)]
    operands = [arr for arr, _ in sources for g in range(n_groups)]
    out_spec = pl.BlockSpec((1, s, HEAD_DIM), lambda bi, hg: (bi, 0, hg))
    return pl.pallas_call(
        functools.partial(_dilated_attn_kernel, seq=s),
        grid=(b, HEADS_PER_DIL),
        in_specs=in_specs,
        out_specs=[out_spec] * n_groups,
        out_shape=[jax.ShapeDtypeStruct((b, s, GROUP_WIDTH), BF16)] * n_groups,
        scratch_shapes=[pltpu.VMEM((3 * n_regrouped, s, HEAD_DIM), F32),
                        pltpu.VMEM((5, s, HEAD_DIM), F32),
                        pltpu.VMEM((3 * n_regrouped, s, HEAD_DIM), BF16),
                        pltpu.VMEM((n_regrouped, s, HEAD_DIM), F32),
                        pltpu.VMEM((n_regrouped, s, HEAD_DIM), F32),
                        pltpu.VMEM((n_groups, s, HEAD_DIM), F32),
                        pltpu.VMEM((n_groups, s, HEAD_DIM), F32)],
        compiler_params=_params(("parallel", "parallel")),
        name="dilated_attn",
    )(*operands)


def _xattn_kernel(q_ref, z_ref, mk_ref, mv_ref, y_ref):
    for h in range(N_XATTN_HEADS):
        cols = slice(h * XATTN_HEAD_DIM, (h + 1) * XATTN_HEAD_DIM)
        s = lax.dot_general(q_ref[0, :, cols], mk_ref[0, :, cols],
                            (((1,), (1,)), ((), ())), preferred_element_type=F32)
        m = jnp.max(s, axis=-1, keepdims=True)
        p = jnp.exp(s - m)
        l = jnp.sum(p, axis=-1, keepdims=True)
        o = jnp.dot(p.astype(BF16), mv_ref[0, :, cols], preferred_element_type=F32)
        y_ref[0, :, cols] = (o * (1.0 / l) * z_ref[0, :, cols].astype(F32)).astype(y_ref.dtype)


def _xattn(q_src, z_src, mkv3, tm):
    b, s, _ = q_src[0].shape
    m_len = mkv3.shape[1]
    q_off, z_off = q_src[1], z_src[1]
    assert q_off % XATTN_WIDTH == 0 and z_off % XATTN_WIDTH == 0
    return pl.pallas_call(
        _xattn_kernel,
        grid=(b, s // tm),
        in_specs=[pl.BlockSpec((1, tm, XATTN_WIDTH), lambda bi, i: (bi, i, q_off // XATTN_WIDTH)),
                  pl.BlockSpec((1, tm, XATTN_WIDTH), lambda bi, i: (bi, i, z_off // XATTN_WIDTH)),
                  pl.BlockSpec((1, m_len, XATTN_WIDTH), lambda bi, i: (bi, 0, 0)),
                  pl.BlockSpec((1, m_len, XATTN_WIDTH), lambda bi, i: (bi, 0, 1))],
        out_specs=pl.BlockSpec((1, tm, XATTN_WIDTH), lambda bi, i: (bi, i, 0)),
        out_shape=jax.ShapeDtypeStruct((b, s, XATTN_WIDTH), BF16),
        compiler_params=_params(("parallel", "parallel")),
        name="mem_xattn",
    )(q_src[0], z_src[0], mkv3, mkv3)


def _out_proj_kernel(*refs, ni, d, n_mix):
    y_refs, w_refs = refs[:n_mix], refs[n_mix:2 * n_mix]
    x_ref, g_ref, o_ref, y2_ref, ssq_ref, scale_ref = refs[2 * n_mix:]
    i = pl.program_id(0)
    j = pl.program_id(1)

    @pl.when(j == 0)
    def _():
        @pl.when(i > 0)
        def _():
            scale_ref[...] = lax.rsqrt(ssq_ref[...] * (1.0 / d) + EPS)
        ssq_ref[...] = jnp.zeros_like(ssq_ref)

    def finish_previous_tile():
        o_ref[...] = x_ref[...] + y2_ref[j] * scale_ref[...] * g_ref[...]

    def multiply_this_tile():
        y2 = jnp.dot(y_refs[0][...], w_refs[0][...], preferred_element_type=F32)
        for y_ref, w_ref in zip(y_refs[1:], w_refs[1:]):
            y2 = y2 + jnp.dot(y_ref[...], w_ref[...], preferred_element_type=F32)
        y2_ref[j] = y2
        ssq_ref[...] += jnp.sum(y2 * y2, axis=-1, keepdims=True)

    @pl.when(i == 0)
    def _():
        multiply_this_tile()

    @pl.when((i > 0) & (i < ni))
    def _():
        finish_previous_tile()
        multiply_this_tile()

    @pl.when(i == ni)
    def _():
        finish_previous_tile()


def _out_proj(ys, w_bf, x2d, gain, tm, tn):
    rows, d = x2d.shape
    nj = d // tn
    ni = rows // tm

    def y_spec(width):
        return pl.BlockSpec((tm, width), lambda i, j: (jnp.minimum(i, ni - 1), 0))

    def w_spec(width, row_off):
        assert row_off % width == 0
        return pl.BlockSpec((width, tn),
                            lambda i, j: (row_off // width, jnp.where(i == ni, nj - 1, j)))

    io_spec = pl.BlockSpec((tm, tn),
                           lambda i, j: (jnp.maximum(i - 1, 0), jnp.where(i == 0, 0, j)))
    widths = [y.shape[1] for y in ys]
    offsets = [sum(widths[:k]) for k in range(len(widths))]
    return pl.pallas_call(
        functools.partial(_out_proj_kernel, ni=ni, d=d, n_mix=len(ys)),
        grid=(ni + 1, nj),
        in_specs=([y_spec(w) for w in widths]
                  + [w_spec(w, off) for w, off in zip(widths, offsets)]
                  + [io_spec, pl.BlockSpec((1, tn), lambda i, j: (0, j))]),
        out_specs=io_spec,
        out_shape=jax.ShapeDtypeStruct((rows, d), F32),
        scratch_shapes=[pltpu.VMEM((nj, tm, tn), F32),
                        pltpu.VMEM((tm, 1), F32),
                        pltpu.VMEM((tm, 1), F32)],
        compiler_params=_params(("arbitrary", "arbitrary")),
        name="out_proj",
    )(*ys, *([w_bf] * len(ys)), x2d, gain.reshape(1, d))


def _rope_tables(pos, half):
    inv = 1.0 / (ROPE_THETA ** (np.arange(half, dtype=np.float64) / half))
    ang = np.asarray(pos, dtype=np.float64)[:, None] * inv[None, :]
    return np.cos(ang), np.sin(ang)


def kernel(x, mem, pre_norm, w_in, conv_w, mem_norm, w_mem_kv, w_out, post_norm):
    b, s, d = x.shape
    m_len = mem.shape[1]
    depth = w_in.shape[0]
    assert all(window // dil == BLOCK for window, dil in DIL_GROUPS)
    pos = np.arange(s)
    cos_a, sin_a = _rope_tables(pos, HEAD_DIM // 2)
    cosa = np.concatenate([cos_a, cos_a], axis=-1)
    sina = np.concatenate([-sin_a, sin_a], axis=-1)
    a_scale = HEAD_DIM ** -0.5
    cos_qkv = jnp.asarray(np.stack([cosa * a_scale, cosa, np.ones_like(cosa)]), F32)
    sin_qkv = jnp.asarray(np.stack([sina * a_scale, sina, np.zeros_like(sina)]), F32)
    cosx, sinx = _rope_tables(pos + m_len, XATTN_HEAD_DIM // 2)
    x_scale = XATTN_HEAD_DIM ** -0.5
    cos_qx = jnp.asarray((cosx * x_scale)[None], F32)
    sin_qx = jnp.asarray((sinx * x_scale)[None], F32)
    cos_m, sin_m = _rope_tables(np.arange(m_len), XATTN_HEAD_DIM // 2)
    cosm = jnp.asarray(np.tile(cos_m, (b, 1)), F32)
    sinm = jnp.asarray(np.tile(sin_m, (b, 1)), F32)

    tm, tn = 1024, 512

    def tile_of(off):
        assert off % tn == 0
        return off // tn

    attn_tiles, x_tiles = ATTN_WIDTH // tn, XATTN_WIDTH // tn
    tiles_per_seq = s // tm

    def view(t):
        return t.reshape(b, s, t.shape[-1])

    def rope_tables(cos, sin, kind):
        def index(col_tile, row_tile):
            return kind(col_tile), row_tile % tiles_per_seq, 0
        return [(cos, (1, tm, LANES), index), (sin, (1, tm, LANES), index)]

    def lane_slab(off):
        assert off % LANES == 0
        return lambda j: off // LANES + j

    for layer in range(depth):
        x2d = x.reshape(b * s, d)
        w = w_in[layer]
        common = dict(tm=tm, tn=tn)
        w_qx_bf = _cast_columns_bf16(w, OFF_QX, XATTN_WIDTH, tr=2048)
        h, qx = _norm_project(x2d, pre_norm[layer], w_qx_bf, cos_qx, sin_qx, seq=s, tm=512)
        qx = view(qx)

        qkv = view(_in_proj(h, w, [lambda j: tile_of(OFF_QA) + j], 3 * attn_tiles,
                            _elementwise_finish(_rope_half_lane),
                            rope_tables(cos_qkv, sin_qkv, lambda j: j // attn_tiles),
                            name="in_proj_qkv", **common))
        y_conv, w_out_bf = _in_proj(
            h, w, [lane_slab(OFF_UC), lane_slab(OFF_BC), lane_slab(OFF_CC), lane_slab(OFF_ZC)],
            CONV_WIDTH // LANES,
            functools.partial(_gated_conv_finish, tiles_per_seq=tiles_per_seq),
            [(conv_w[layer], (CONV_K, LANES), lambda col_tile, row_tile: (0, col_tile))],
            out_tn=LANES, scratch=[pltpu.VMEM((CONV_HALO, LANES), F32)],
            rider=(lambda block: block, w_out[layer], []), name="in_proj_conv", **common)
        gates, hm = _in_proj(
            h, w, [lambda j: jnp.where(j < x_tiles, tile_of(OFF_ZX) + j,
                                       tile_of(OFF_ZA) - x_tiles + j)],
            x_tiles + attn_tiles, _elementwise_finish(_silu), [], name="in_proj_gates",
            rider=(_rms_normalize, mem.reshape(b * m_len, d), [mem_norm[layer].reshape(1, d)]),
            **common)
        gates = view(gates)
        mkv = _mem_proj(hm, w_mem_kv[layer], cosm, sinm, tn=512)

        y_groups = _dilated_attn((qkv, 0), (qkv, ATTN_WIDTH), (qkv, 2 * ATTN_WIDTH),
                                 (gates, XATTN_WIDTH))
        y_groups = [y.reshape(b * s, GROUP_WIDTH) for y in y_groups]
        y_x = _xattn((qx, 0), (gates, 0), mkv.reshape(b, m_len, 2 * XATTN_WIDTH), tm=s)
        y_x = y_x.reshape(b * s, XATTN_WIDTH)

        out = _out_proj([*y_groups, y_conv, y_x], w_out_bf, x2d,
                        post_norm[layer], tm=1024, tn=512)
        x = out.reshape(b, s, d)
    return x
```

```python
import functools

import jax
import jax.numpy as jnp
import numpy as np
from jax import lax
from jax.experimental import pallas as pl
from jax.experimental.pallas import tpu as pltpu

D_MODEL = 4096
HEAD_DIM = 128
DIL_GROUPS = ((128, 1), (512, 4), (2048, 16))
ATTN_WIDTH = 3 * D_MODEL // 8
CONV_WIDTH = 3 * D_MODEL // 8
XATTN_WIDTH = D_MODEL // 4
N_XATTN_HEADS = 4
XATTN_HEAD_DIM = XATTN_WIDTH // N_XATTN_HEADS
HEADS_PER_DIL = 4
GROUP_WIDTH = HEADS_PER_DIL * HEAD_DIM
CONV_K = 3
BLOCK = 128
ROPE_THETA = 10000.0
EPS = 1e-6
NEG_INF = -1e30

OFF_QA = 0
OFF_KA = OFF_QA + ATTN_WIDTH
OFF_VA = OFF_KA + ATTN_WIDTH
OFF_ZA = OFF_VA + ATTN_WIDTH
OFF_UC = OFF_ZA + ATTN_WIDTH
OFF_BC = OFF_UC + CONV_WIDTH
OFF_CC = OFF_BC + CONV_WIDTH
OFF_ZC = OFF_CC + CONV_WIDTH
OFF_QX = OFF_ZC + CONV_WIDTH
OFF_ZX = OFF_QX + XATTN_WIDTH

V7X_VMEM_LIMIT_BYTES = 56 * 1024 * 1024
LANES = 128
W_PIECES = 4
EPILOGUE_ROWS = 128
CONV_HALO = 8
NORM_ROWS = 64
RIDER_ROWS = 16

BF16 = jnp.bfloat16
F32 = jnp.float32


def _params(semantics):
    return pltpu.CompilerParams(dimension_semantics=semantics,
                                vmem_limit_bytes=V7X_VMEM_LIMIT_BYTES)


def _rms_normalize(x, gain):
    ms = jnp.mean(x * x, axis=-1, keepdims=True)
    return x * lax.rsqrt(ms + EPS) * gain


def _cast_kernel(x_ref, o_ref):
    o_ref[...] = x_ref[...].astype(o_ref.dtype)


def _cast_columns_bf16(w, col_off, width, tr):
    rows = w.shape[0]
    assert col_off % width == 0
    return pl.pallas_call(
        _cast_kernel,
        grid=(rows // tr,),
        in_specs=[pl.BlockSpec((tr, width), lambda i: (i, col_off // width))],
        out_specs=pl.BlockSpec((tr, width), lambda i: (i, 0)),
        out_shape=jax.ShapeDtypeStruct((rows, width), BF16),
        compiler_params=_params(("parallel",)),
        name="cast_w_qx",
    )(w)


def _norm_project_kernel(x_ref, g_ref, w_ref, cos_ref, sin_ref, h_ref, o_ref, acc_ref):
    tm = x_ref.shape[0]
    for r0 in range(0, tm, NORM_ROWS):
        rows = slice(r0, r0 + NORM_ROWS)
        h_ref[rows, :] = _rms_normalize(x_ref[rows, :], g_ref[...]).astype(h_ref.dtype)
    acc_ref[...] = jnp.dot(h_ref[...], w_ref[...], preferred_element_type=F32)
    for r0 in range(0, tm, EPILOGUE_ROWS):
        rows = slice(r0, r0 + EPILOGUE_ROWS)
        o_ref[rows, :] = _rope_two_tiles(acc_ref[rows, :], cos_ref[0, rows, :],
                                         sin_ref[0, rows, :]).astype(o_ref.dtype)


def _norm_project(x2d, gain, w_bf, cos, sin, seq, tm):
    rows, d = x2d.shape
    width = w_bf.shape[1]
    tiles_per_seq = seq // tm
    table = pl.BlockSpec((1, tm, LANES), lambda i: (0, i % tiles_per_seq, 0))
    return pl.pallas_call(
        _norm_project_kernel,
        grid=(rows // tm,),
        in_specs=[pl.BlockSpec((tm, d), lambda i: (i, 0)),
                  pl.BlockSpec((1, d), lambda i: (0, 0)),
                  pl.BlockSpec((d, width), lambda i: (0, 0)),
                  table, table],
        out_specs=[pl.BlockSpec((tm, d), lambda i: (i, 0)),
                   pl.BlockSpec((tm, width), lambda i: (i, 0))],
        out_shape=[jax.ShapeDtypeStruct((rows, d), BF16),
                   jax.ShapeDtypeStruct((rows, width), BF16)],
        scratch_shapes=[pltpu.VMEM((tm, width), F32)],
        compiler_params=_params(("arbitrary",)),
        name="norm_project_qx",
    )(x2d, gain.reshape(1, d), w_bf, cos, sin)


def _silu(z):
    half = 0.5 * z
    return half + half * jnp.tanh(half)


def _rope_half_lane(acc, cos, sin_signed):
    outs = []
    for h in range(acc.shape[1] // HEAD_DIM):
        xh = acc[:, h * HEAD_DIM:(h + 1) * HEAD_DIM]
        outs.append(xh * cos + pltpu.roll(xh, HEAD_DIM // 2, axis=1) * sin_signed)
    return jnp.concatenate(outs, axis=1)


def _rope_two_tiles(acc, cos, sin):
    outs = []
    for h in range(acc.shape[1] // XATTN_HEAD_DIM):
        x1 = acc[:, h * XATTN_HEAD_DIM:h * XATTN_HEAD_DIM + LANES]
        x2 = acc[:, h * XATTN_HEAD_DIM + LANES:(h + 1) * XATTN_HEAD_DIM]
        outs.append(x1 * cos - x2 * sin)
        outs.append(x2 * cos + x1 * sin)
    return jnp.concatenate(outs, axis=1)


def _elementwise_finish(epilogue):
    def finish(acc_ref, slot, o_ref, aux_refs, scratch_refs, row_tile):
        for r0 in range(0, o_ref.shape[0], EPILOGUE_ROWS):
            rows = slice(r0, r0 + EPILOGUE_ROWS)
            o_ref[rows, :] = epilogue(acc_ref[slot, rows, :],
                                      *[r[0, rows, :] for r in aux_refs]).astype(o_ref.dtype)
    return finish


def _gated_conv_finish(acc_ref, slot, o_ref, aux_refs, scratch_refs, row_tile, *, tiles_per_seq):
    (w_ref,), (carry_ref,) = aux_refs, scratch_refs
    tm = o_ref.shape[0]
    u_cols, b_cols, c_cols, z_cols = (slice(k * LANES, (k + 1) * LANES) for k in range(4))
    w = w_ref[...]
    tail = jnp.where(row_tile % tiles_per_seq == 0, 0.0, carry_ref[...])
    for r0 in range(0, tm, EPILOGUE_ROWS):
        rows = slice(r0, r0 + EPILOGUE_ROWS)
        cu = acc_ref[slot, rows, c_cols] * acc_ref[slot, rows, u_cols]
        window = jnp.concatenate([tail, cu], axis=0)
        conv = cu * w[CONV_K - 1:CONV_K, :]
        for lag in range(1, CONV_K):
            shifted = pltpu.roll(window, lag, axis=0)[CONV_HALO:, :]
            conv = conv + shifted * w[CONV_K - 1 - lag:CONV_K - lag, :]
        o_ref[rows, :] = (acc_ref[slot, rows, b_cols] * conv
                          * _silu(acc_ref[slot, rows, z_cols])).astype(o_ref.dtype)
        tail = cu[EPILOGUE_ROWS - CONV_HALO:, :]
    carry_ref[...] = tail


def _in_proj_kernel(*refs, ni, n_steps, n_w, w_pieces, n_aux, n_scratch, finish_fn, rider_fn):
    h_ref = refs[0]
    w_refs = refs[1:1 + n_w]
    aux_refs = refs[1 + n_w:1 + n_w + n_aux]
    scratch_refs = refs[len(refs) - n_scratch:] if n_scratch else ()
    rest = refs[1 + n_w + n_aux:len(refs) - n_scratch]
    if rider_fn is not None:
        rider_in_ref, *rider_row_refs = rest[:-4]
        o_ref, rider_out_ref, wbf_ref, acc_ref = rest[-4:]
    else:
        o_ref, wbf_ref, acc_ref = rest
    t = pl.program_id(0)
    piece_rows = wbf_ref.shape[0] // w_pieces
    slab_cols = wbf_ref.shape[1] // (n_w // w_pieces)

    @pl.when((t % ni == 0) & (t < n_steps))
    def _():
        for k, w_ref in enumerate(w_refs):
            slab, p = divmod(k, w_pieces)
            wbf_ref[p * piece_rows:(p + 1) * piece_rows,
                    slab * slab_cols:(slab + 1) * slab_cols] = w_ref[...].astype(BF16)

    def ride():
        for r0 in range(0, rider_in_ref.shape[0], RIDER_ROWS):
            rows = slice(r0, r0 + RIDER_ROWS)
            rider_out_ref[rows, :] = rider_fn(
                rider_in_ref[rows, :], *[r[...] for r in rider_row_refs]
            ).astype(rider_out_ref.dtype)

    def multiply(slot):
        acc_ref[slot] = jnp.dot(h_ref[...], wbf_ref[...], preferred_element_type=F32)
        if rider_fn is not None:
            ride()

    def finish(slot):
        finish_fn(acc_ref, slot, o_ref, aux_refs, scratch_refs, ((t - 1) % ni))

    @pl.when(t == 0)
    def _():
        for ref in scratch_refs:
            ref[...] = jnp.zeros_like(ref)
        multiply(0)

    for parity in range(2):
        @pl.when((t > 0) & (t < n_steps) & (t % 2 == parity))
        def _(parity=parity):
            finish(1 - parity)
            multiply(parity)

    @pl.when(t == n_steps)
    def _():
        finish((n_steps - 1) % 2)


def _in_proj(h, w, col_slabs, n_col_tiles, finish_fn, aux, tm, tn, name, out_tn=None,
             scratch=(), rider=None, w_pieces=W_PIECES):
    rows, d = h.shape
    ni = rows // tm
    n_steps = n_col_tiles * ni
    out_tn = tn if out_tn is None else out_tn
    slab_cols = tn // len(col_slabs)
    rider_in_specs, rider_out_specs, rider_out_shapes, rider_operands = [], [], [], []
    rider_fn, rider_steps = None, 0
    if rider is not None:
        rider_fn, rider_matrix, rider_rows = rider
        r_rows, r_cols = rider_matrix.shape
        assert all(r.shape == (1, r_cols) for r in rider_rows)
        rider_steps = max(k for k in range(1, n_steps + 1) if r_rows % k == 0)
        rider_spec = pl.BlockSpec((r_rows // rider_steps, r_cols),
                                  lambda t: (jnp.minimum(t, rider_steps - 1), 0))
        rider_in_specs = [rider_spec] + [pl.BlockSpec((1, r_cols), lambda t: (0, 0))
                                         for _ in rider_rows]
        rider_out_specs = [rider_spec]
        rider_out_shapes = [jax.ShapeDtypeStruct(rider_matrix.shape, BF16)]
        rider_operands = [rider_matrix, *rider_rows]

    def cur(t):
        c = jnp.minimum(t, n_steps - 1)
        return c // ni, c % ni

    def prev(t):
        p = jnp.maximum(t - 1, 0)
        return p // ni, p % ni

    def aux_spec(block, index):
        return pl.BlockSpec(block, lambda t: index(*prev(t)))

    def w_piece_spec(col_slab, p):
        def index(t):
            j, i = cur(t)
            ahead = (i >= ni - w_pieces + p).astype(jnp.int32)
            return p, col_slab(jnp.minimum(j + ahead, n_col_tiles - 1))
        return pl.BlockSpec((d // w_pieces, slab_cols), index)

    assert ni > w_pieces and d % w_pieces == 0
    w_specs = [w_piece_spec(col_slab, p) for col_slab in col_slabs for p in range(w_pieces)]
    results = pl.pallas_call(
        functools.partial(_in_proj_kernel, ni=ni, n_steps=n_steps, n_w=len(w_specs),
                          w_pieces=w_pieces, n_aux=len(aux), n_scratch=len(scratch),
                          finish_fn=finish_fn, rider_fn=rider_fn),
        grid=(n_steps + 1,),
        in_specs=[pl.BlockSpec((tm, d), lambda t: (cur(t)[1], 0))] + w_specs
                 + [aux_spec(block, index) for _, block, index in aux] + rider_in_specs,
        out_specs=[pl.BlockSpec((tm, out_tn), lambda t: (prev(t)[1], prev(t)[0]))]
                  + rider_out_specs,
        out_shape=[jax.ShapeDtypeStruct((rows, n_col_tiles * out_tn), BF16)] + rider_out_shapes,
        scratch_shapes=[pltpu.VMEM((d, tn), BF16), pltpu.VMEM((2, tm, tn), F32), *scratch],
        compiler_params=_params(("arbitrary",)),
        name=name,
    )(h, *([w] * len(w_specs)), *[arr for arr, _, _ in aux], *rider_operands)
    return results if rider is not None else results[0]


def _mem_proj_kernel(h_ref, w_ref, cos_ref, sin_ref, o_ref, *, tn):
    j = pl.program_id(0)
    acc = jnp.dot(h_ref[...], w_ref[...].astype(BF16), preferred_element_type=F32)

    @pl.when(j < XATTN_WIDTH // tn)
    def _():
        o_ref[...] = _rope_two_tiles(acc, cos_ref[...], sin_ref[...]).astype(o_ref.dtype)

    @pl.when(j >= XATTN_WIDTH // tn)
    def _():
        o_ref[...] = acc.astype(o_ref.dtype)


def _mem_proj(hm, w, cos, sin, tn):
    rows, d = hm.shape
    ncols = w.shape[1]
    tab = pl.BlockSpec((rows, LANES), lambda j: (0, 0))
    return pl.pallas_call(
        functools.partial(_mem_proj_kernel, tn=tn),
        grid=(ncols // tn,),
        in_specs=[pl.BlockSpec((rows, d), lambda j: (0, 0)),
                  pl.BlockSpec((d, tn), lambda j: (0, j)),
                  tab, tab],
        out_specs=pl.BlockSpec((rows, tn), lambda j: (0, j)),
        out_shape=jax.ShapeDtypeStruct((rows, ncols), BF16),
        compiler_params=_params(("parallel",)),
        name="mem_proj",
    )(hm, w, cos, sin)


MAX_SINGLE_ACCESS_STRIDE = 4


def _split_stride(dil):
    if dil <= MAX_SINGLE_ACCESS_STRIDE:
        return dil, 1
    assert dil % MAX_SINGLE_ACCESS_STRIDE == 0
    f2 = dil // MAX_SINGLE_ACCESS_STRIDE
    assert f2 <= MAX_SINGLE_ACCESS_STRIDE
    return MAX_SINGLE_ACCESS_STRIDE, f2


def _dilated_attn_kernel(q0, q1, q2, k0, k1, k2, v0, v1, v2, z0, z1, z2, y0, y1, y2,
                         stage_ref, mid_ref, sub_ref, osub_ref, lsub_ref, onat_ref, lnat_ref,
                         *, seq):
    q_refs, k_refs, v_refs = (q0, q1, q2), (k0, k1, k2), (v0, v1, v2)
    z_refs, y_refs = (z0, z1, z2), (y0, y1, y2)
    nblk = seq // BLOCK
    qi = lax.broadcasted_iota(jnp.int32, (BLOCK, 2 * BLOCK), 0)
    kk = lax.broadcasted_iota(jnp.int32, (BLOCK, 2 * BLOCK), 1)
    dist = qi + BLOCK - kk
    band = (dist >= 0) & (dist <= BLOCK)
    causal = (lax.broadcasted_iota(jnp.int32, (BLOCK, BLOCK), 1)
              <= lax.broadcasted_iota(jnp.int32, (BLOCK, BLOCK), 0))

    rows_of = []
    for g, (window, dil) in enumerate(DIL_GROUPS):
        sub_len = seq // dil
        if dil == 1:
            def group_rows(which, start, size, g=g):
                return (q_refs, k_refs, v_refs)[which][g][0, pl.ds(start, size), :]
        else:
            slab = 3 * (g - 1)
            f1, f2 = _split_stride(dil)
            len1 = seq // f1
            for which, refs in enumerate((q_refs, k_refs, v_refs)):
                stage_ref[slab + which] = refs[g][0].astype(F32)
                if f2 == 1:
                    for r in range(dil):
                        sub_ref[slab + which, r * sub_len:(r + 1) * sub_len, :] = (
                            stage_ref[slab + which, pl.ds(r, sub_len, stride=dil), :].astype(BF16))
                    continue
                for r1 in range(f1):
                    mid_ref[which, r1 * len1:(r1 + 1) * len1, :] = (
                        stage_ref[slab + which, pl.ds(r1, len1, stride=f1), :])
                for r1 in range(f1):
                    for r2 in range(f2):
                        r = r2 * f1 + r1
                        sub_ref[slab + which, r * sub_len:(r + 1) * sub_len, :] = (
                            mid_ref[which, pl.ds(r1 * len1 + r2, sub_len, stride=f2), :]
                            .astype(BF16))

            def group_rows(which, start, size, slab=slab):
                return sub_ref[slab + which, pl.ds(start, size), :]
        rows_of.append(group_rows)

    def attend(g, start, first):
        q = rows_of[g](0, start, BLOCK)
        if first:
            k = rows_of[g](1, start, BLOCK)
            v = rows_of[g](2, start, BLOCK)
            mask = causal
        else:
            k = rows_of[g](1, start - BLOCK, 2 * BLOCK)
            v = rows_of[g](2, start - BLOCK, 2 * BLOCK)
            mask = band
        s = lax.dot_general(q, k, (((1,), (1,)), ((), ())), preferred_element_type=F32)
        s = jnp.where(mask, s, NEG_INF)
        m = jnp.max(s, axis=-1, keepdims=True)
        p = jnp.exp(s - m)
        l = jnp.sum(p, axis=-1, keepdims=True)
        o = jnp.dot(p.astype(BF16), v, preferred_element_type=F32) * (1.0 / l)
        lse = jnp.broadcast_to(m + jnp.log(l), (BLOCK, HEAD_DIM))
        if DIL_GROUPS[g][1] == 1:
            onat_ref[g, pl.ds(start, BLOCK), :] = o
            lnat_ref[g, pl.ds(start, BLOCK), :] = lse
        else:
            osub_ref[g - 1, pl.ds(start, BLOCK), :] = o
            lsub_ref[g - 1, pl.ds(start, BLOCK), :] = lse

    for c in range(nblk):
        for g, (window, dil) in enumerate(DIL_GROUPS):
            attend(g, c * BLOCK, c % (seq // dil // BLOCK) == 0)

    for g, (window, dil) in enumerate(DIL_GROUPS):
        sub_len = seq // dil
        if dil == 1:
            continue
        f1, f2 = _split_stride(dil)
        len1 = seq // f1
        for idx, (nat_ref, subseq_ref) in enumerate(((onat_ref, osub_ref), (lnat_ref, lsub_ref))):
            if f2 == 1:
                for r in range(dil):
                    src = slice(r * sub_len, (r + 1) * sub_len)
                    nat_ref[g, pl.ds(r, sub_len, stride=dil), :] = subseq_ref[g - 1, src, :]
                continue
            mid = 3 + idx
            for r1 in range(f1):
                for r2 in range(f2):
                    r = r2 * f1 + r1
                    mid_ref[mid, pl.ds(r1 * len1 + r2, sub_len, stride=f2), :] = (
                        subseq_ref[g - 1, r * sub_len:(r + 1) * sub_len, :])
            for r1 in range(f1):
                nat_ref[g, pl.ds(r1, len1, stride=f1), :] = mid_ref[mid, r1 * len1:(r1 + 1) * len1, :]

    chunk = 2 * BLOCK
    for t in range(seq // chunk):
        rows = slice(t * chunk, (t + 1) * chunk)
        lse = [lnat_ref[g, rows, :] for g in range(len(DIL_GROUPS))]
        mx = jnp.maximum(jnp.maximum(lse[0], lse[1]), lse[2])
        e = [jnp.exp(x - mx) for x in lse]
        inv = 1.0 / (e[0] + e[1] + e[2])
        for g in range(len(DIL_GROUPS)):
            y_refs[g][0, rows, :] = (onat_ref[g, rows, :] * (e[g] * inv)
                                     * z_refs[g][0, rows, :].astype(F32)).astype(y_refs[g].dtype)


def _dilated_attn(q_src, k_src, v_src, z_src):
    sources = (q_src, k_src, v_src, z_src)
    b, s, _ = q_src[0].shape
    n_groups = len(DIL_GROUPS)
    n_regrouped = n_groups - 1
    assert sum(_split_stride(dil)[1] > 1 for _, dil in DIL_GROUPS) <= 1

    def head_spec(off, g):
        first = (off + g * GROUP_WIDTH) // HEAD_DIM
        return pl.BlockSpec((1, s, HEAD_DIM), lambda bi, hg: (bi, 0, first + hg))

    in_specs = [head_spec(off, g) for _, off in sources for g in range(n_groups)]
    operands = [arr for arr, _ in sources for g in range(n_groups)]
    out_spec = pl.BlockSpec((1, s, HEAD_DIM), lambda bi, hg: (bi, 0, hg))
    return pl.pallas_call(
        functools.partial(_dilated_attn_kernel, seq=s),
        grid=(b, HEADS_PER_DIL),
        in_specs=in_specs,
        out_specs=[out_spec] * n_groups,
        out_shape=[jax.ShapeDtypeStruct((b, s, GROUP_WIDTH), BF16)] * n_groups,
        scratch_shapes=[pltpu.VMEM((3 * n_regrouped, s, HEAD_DIM), F32),
                        pltpu.VMEM((5, s, HEAD_DIM), F32),
                        pltpu.VMEM((3 * n_regrouped, s, HEAD_DIM), BF16),
                        pltpu.VMEM((n_regrouped, s, HEAD_DIM), F32),
                        pltpu.VMEM((n_regrouped, s, HEAD_DIM), F32),
                        pltpu.VMEM((n_groups, s, HEAD_DIM), F32),
                        pltpu.VMEM((n_groups, s, HEAD_DIM), F32)],
        compiler_params=_params(("parallel", "parallel")),
        name="dilated_attn",
    )(*operands)


def _xattn_kernel(q_ref, z_ref, mk_ref, mv_ref, y_ref):
    for h in range(N_XATTN_HEADS):
        cols = slice(h * XATTN_HEAD_DIM, (h + 1) * XATTN_HEAD_DIM)
        s = lax.dot_general(q_ref[0, :, cols], mk_ref[0, :, cols],
                            (((1,), (1,)), ((), ())), preferred_element_type=F32)
        m = jnp.max(s, axis=-1, keepdims=True)
        p = jnp.exp(s - m)
        l = jnp.sum(p, axis=-1, keepdims=True)
        o = jnp.dot(p.astype(BF16), mv_ref[0, :, cols], preferred_element_type=F32)
        y_ref[0, :, cols] = (o * (1.0 / l) * z_ref[0, :, cols].astype(F32)).astype(y_ref.dtype)


def _xattn(q_src, z_src, mkv3, tm):
    b, s, _ = q_src[0].shape
    m_len = mkv3.shape[1]
    q_off, z_off = q_src[1], z_src[1]
    assert q_off % XATTN_WIDTH == 0 and z_off % XATTN_WIDTH == 0
    return pl.pallas_call(
        _xattn_kernel,
        grid=(b, s // tm),
        in_specs=[pl.BlockSpec((1, tm, XATTN_WIDTH), lambda bi, i: (bi, i, q_off // XATTN_WIDTH)),
                  pl.BlockSpec((1, tm, XATTN_WIDTH), lambda bi, i: (bi, i, z_off // XATTN_WIDTH)),
                  pl.BlockSpec((1, m_len, XATTN_WIDTH), lambda bi, i: (bi, 0, 0)),
                  pl.BlockSpec((1, m_len, XATTN_WIDTH), lambda bi, i: (bi, 0, 1))],
        out_specs=pl.BlockSpec((1, tm, XATTN_WIDTH), lambda bi, i: (bi, i, 0)),
        out_shape=jax.ShapeDtypeStruct((b, s, XATTN_WIDTH), BF16),
        compiler_params=_params(("parallel", "parallel")),
        name="mem_xattn",
    )(q_src[0], z_src[0], mkv3, mkv3)


def _out_proj_kernel(*refs, ni, d, n_mix):
    y_refs, w_refs = refs[:n_mix], refs[n_mix:2 * n_mix]
    x_ref, g_ref, o_ref, y2_ref, ssq_ref, scale_ref = refs[2 * n_mix:]
    i = pl.program_id(0)
    j = pl.program_id(1)

    @pl.when(j == 0)
    def _():
        @pl.when(i > 0)
        def _():
            scale_ref[...] = lax.rsqrt(ssq_ref[...] * (1.0 / d) + EPS)
        ssq_ref[...] = jnp.zeros_like(ssq_ref)

    def finish_previous_tile():
        o_ref[...] = x_ref[...] + y2_ref[j] * scale_ref[...] * g_ref[...]

    def multiply_this_tile():
        y2 = jnp.dot(y_refs[0][...], w_refs[0][...], preferred_element_type=F32)
        for y_ref, w_ref in zip(y_refs[1:], w_refs[1:]):
            y2 = y2 + jnp.dot(y_ref[...], w_ref[...], preferred_element_type=F32)
        y2_ref[j] = y2
        ssq_ref[...] += jnp.sum(y2 * y2, axis=-1, keepdims=True)

    @pl.when(i == 0)
    def _():
        multiply_this_tile()

    @pl.when((i > 0) & (i < ni))
    def _():
        finish_previous_tile()
        multiply_this_tile()

    @pl.when(i == ni)
    def _():
        finish_previous_tile()


def _out_proj(ys, w_bf, x2d, gain, tm, tn):
    rows, d = x2d.shape
    nj = d // tn
    ni = rows // tm

    def y_spec(width):
        return pl.BlockSpec((tm, width), lambda i, j: (jnp.minimum(i, ni - 1), 0))

    def w_spec(width, row_off):
        assert row_off % width == 0
        return pl.BlockSpec((width, tn),
                            lambda i, j: (row_off // width, jnp.where(i == ni, nj - 1, j)))

    io_spec = pl.BlockSpec((tm, tn),
                           lambda i, j: (jnp.maximum(i - 1, 0), jnp.where(i == 0, 0, j)))
    widths = [y.shape[1] for y in ys]
    offsets = [sum(widths[:k]) for k in range(len(widths))]
    return pl.pallas_call(
        functools.partial(_out_proj_kernel, ni=ni, d=d, n_mix=len(ys)),
        grid=(ni + 1, nj),
        in_specs=([y_spec(w) for w in widths]
                  + [w_spec(w, off) for w, off in zip(widths, offsets)]
                  + [io_spec, pl.BlockSpec((1, tn), lambda i, j: (0, j))]),
        out_specs=io_spec,
        out_shape=jax.ShapeDtypeStruct((rows, d), F32),
        scratch_shapes=[pltpu.VMEM((nj, tm, tn), F32),
                        pltpu.VMEM((tm, 1), F32),
                        pltpu.VMEM((tm, 1), F32)],
        compiler_params=_params(("arbitrary", "arbitrary")),
        name="out_proj",
    )(*ys, *([w_bf] * len(ys)), x2d, gain.reshape(1, d))


def _rope_tables(pos, half):
    inv = 1.0 / (ROPE_THETA ** (np.arange(half, dtype=np.float64) / half))
    ang = np.asarray(pos, dtype=np.float64)[:, None] * inv[None, :]
    return np.cos(ang), np.sin(ang)


def kernel(x, mem, pre_norm, w_in, conv_w, mem_norm, w_mem_kv, w_out, post_norm):
    b, s, d = x.shape
    m_len = mem.shape[1]
    depth = w_in.shape[0]
    assert all(window // dil == BLOCK for window, dil in DIL_GROUPS)
    pos = np.arange(s)
    cos_a, sin_a = _rope_tables(pos, HEAD_DIM // 2)
    cosa = np.concatenate([cos_a, cos_a], axis=-1)
    sina = np.concatenate([-sin_a, sin_a], axis=-1)
    a_scale = HEAD_DIM ** -0.5
    cos_qkv = jnp.asarray(np.stack([cosa * a_scale, cosa, np.ones_like(cosa)]), F32)
    sin_qkv = jnp.asarray(np.stack([sina * a_scale, sina, np.zeros_like(sina)]), F32)
    cosx, sinx = _rope_tables(pos + m_len, XATTN_HEAD_DIM // 2)
    x_scale = XATTN_HEAD_DIM ** -0.5
    cos_qx = jnp.asarray((cosx * x_scale)[None], F32)
    sin_qx = jnp.asarray((sinx * x_scale)[None], F32)
    cos_m, sin_m = _rope_tables(np.arange(m_len), XATTN_HEAD_DIM // 2)
    cosm = jnp.asarray(np.tile(cos_m, (b, 1)), F32)
    sinm = jnp.asarray(np.tile(sin_m, (b, 1)), F32)

    tm, tn = 1024, 512

    def tile_of(off):
        assert off % tn == 0
        return off // tn

    attn_tiles, x_tiles = ATTN_WIDTH // tn, XATTN_WIDTH // tn
    tiles_per_seq = s // tm

    def view(t):
        return t.reshape(b, s, t.shape[-1])

    def rope_tables(cos, sin, kind):
        def index(col_tile, row_tile):
            return kind(col_tile), row_tile % tiles_per_seq, 0
        return [(cos, (1, tm, LANES), index), (sin, (1, tm, LANES), index)]

    def lane_slab(off):
        assert off % LANES == 0
        return lambda j: off // LANES + j

    for layer in range(depth):
        x2d = x.reshape(b * s, d)
        w = w_in[layer]
        common = dict(tm=tm, tn=tn)
        w_qx_bf = _cast_columns_bf16(w, OFF_QX, XATTN_WIDTH, tr=2048)
        h, qx = _norm_project(x2d, pre_norm[layer], w_qx_bf, cos_qx, sin_qx, seq=s, tm=512)
        qx = view(qx)

        qkv = view(_in_proj(h, w, [lambda j: tile_of(OFF_QA) + j], 3 * attn_tiles,
                            _elementwise_finish(_rope_half_lane),
                            rope_tables(cos_qkv, sin_qkv, lambda j: j // attn_tiles),
                            name="in_proj_qkv", **common))
        y_conv, w_out_bf = _in_proj(
            h, w, [lane_slab(OFF_UC), lane_slab(OFF_BC), lane_slab(OFF_CC), lane_slab(OFF_ZC)],
            CONV_WIDTH // LANES,
            functools.partial(_gated_conv_finish, tiles_per_seq=tiles_per_seq),
            [(conv_w[layer], (CONV_K, LANES), lambda col_tile, row_tile: (0, col_tile))],
            out_tn=LANES, scratch=[pltpu.VMEM((CONV_HALO, LANES), F32)],
            rider=(lambda block: block, w_out[layer], []), name="in_proj_conv",
            w_pieces=W_PIECES // 2, **common)
        gates, hm = _in_proj(
            h, w, [lambda j: jnp.where(j < x_tiles, tile_of(OFF_ZX) + j,
                                       tile_of(OFF_ZA) - x_tiles + j)],
            x_tiles + attn_tiles, _elementwise_finish(_silu), [], name="in_proj_gates",
            rider=(_rms_normalize, mem.reshape(b * m_len, d), [mem_norm[layer].reshape(1, d)]),
            **common)
        gates = view(gates)
        mkv = _mem_proj(hm, w_mem_kv[layer], cosm, sinm, tn=512)

        y_groups = _dilated_attn((qkv, 0), (qkv, ATTN_WIDTH), (qkv, 2 * ATTN_WIDTH),
                                 (gates, XATTN_WIDTH))
        y_groups = [y.reshape(b * s, GROUP_WIDTH) for y in y_groups]
        y_x = _xattn((qx, 0), (gates, 0), mkv.reshape(b, m_len, 2 * XATTN_WIDTH), tm=s)
        y_x = y_x.reshape(b * s, XATTN_WIDTH)

        out = _out_proj([*y_groups, y_conv, y_x], w_out_bf, x2d,
                        post_norm[layer], tm=1024, tn=512)
        x = out.reshape(b, s, d)
    return x
```

```python
import functools

import jax
import jax.numpy as jnp
import numpy as np
from jax import lax
from jax.experimental import pallas as pl
from jax.experimental.pallas import tpu as pltpu

D_MODEL = 4096
HEAD_DIM = 128
DIL_GROUPS = ((128, 1), (512, 4), (2048, 16))
ATTN_WIDTH = 3 * D_MODEL // 8
CONV_WIDTH = 3 * D_MODEL // 8
XATTN_WIDTH = D_MODEL // 4
N_XATTN_HEADS = 4
XATTN_HEAD_DIM = XATTN_WIDTH // N_XATTN_HEADS
HEADS_PER_DIL = 4
GROUP_WIDTH = HEADS_PER_DIL * HEAD_DIM
CONV_K = 3
BLOCK = 128
ROPE_THETA = 10000.0
EPS = 1e-6
NEG_INF = -1e30

OFF_QA = 0
OFF_KA = OFF_QA + ATTN_WIDTH
OFF_VA = OFF_KA + ATTN_WIDTH
OFF_ZA = OFF_VA + ATTN_WIDTH
OFF_UC = OFF_ZA + ATTN_WIDTH
OFF_BC = OFF_UC + CONV_WIDTH
OFF_CC = OFF_BC + CONV_WIDTH
OFF_ZC = OFF_CC + CONV_WIDTH
OFF_QX = OFF_ZC + CONV_WIDTH
OFF_ZX = OFF_QX + XATTN_WIDTH

V7X_VMEM_LIMIT_BYTES = 56 * 1024 * 1024
LANES = 128
W_PIECES = 4
EPILOGUE_ROWS = 128
CONV_HALO = 8
NORM_ROWS = 64
RIDER_ROWS = 16

BF16 = jnp.bfloat16
F32 = jnp.float32


def _params(semantics):
    return pltpu.CompilerParams(dimension_semantics=semantics,
                                vmem_limit_bytes=V7X_VMEM_LIMIT_BYTES)


def _rms_normalize(x, gain):
    ms = jnp.mean(x * x, axis=-1, keepdims=True)
    return x * lax.rsqrt(ms + EPS) * gain


def _cast_kernel(x_ref, o_ref):
    o_ref[...] = x_ref[...].astype(o_ref.dtype)


def _cast_columns_bf16(w, col_off, width, tr):
    rows = w.shape[0]
    assert col_off % width == 0
    return pl.pallas_call(
        _cast_kernel,
        grid=(rows // tr,),
        in_specs=[pl.BlockSpec((tr, width), lambda i: (i, col_off // width))],
        out_specs=pl.BlockSpec((tr, width), lambda i: (i, 0)),
        out_shape=jax.ShapeDtypeStruct((rows, width), BF16),
        compiler_params=_params(("parallel",)),
        name="cast_w_qx",
    )(w)


def _norm_project_kernel(x_ref, g_ref, w_ref, cos_ref, sin_ref, h_ref, o_ref, acc_ref):
    tm = x_ref.shape[0]
    for r0 in range(0, tm, NORM_ROWS):
        rows = slice(r0, r0 + NORM_ROWS)
        h_ref[rows, :] = _rms_normalize(x_ref[rows, :], g_ref[...]).astype(h_ref.dtype)
    acc_ref[...] = jnp.dot(h_ref[...], w_ref[...], preferred_element_type=F32)
    for r0 in range(0, tm, EPILOGUE_ROWS):
        rows = slice(r0, r0 + EPILOGUE_ROWS)
        o_ref[rows, :] = _rope_two_tiles(acc_ref[rows, :], cos_ref[0, rows, :],
                                         sin_ref[0, rows, :]).astype(o_ref.dtype)


def _norm_project(x2d, gain, w_bf, cos, sin, seq, tm):
    rows, d = x2d.shape
    width = w_bf.shape[1]
    tiles_per_seq = seq // tm
    table = pl.BlockSpec((1, tm, LANES), lambda i: (0, i % tiles_per_seq, 0))
    return pl.pallas_call(
        _norm_project_kernel,
        grid=(rows // tm,),
        in_specs=[pl.BlockSpec((tm, d), lambda i: (i, 0)),
                  pl.BlockSpec((1, d), lambda i: (0, 0)),
                  pl.BlockSpec((d, width), lambda i: (0, 0)),
                  table, table],
        out_specs=[pl.BlockSpec((tm, d), lambda i: (i, 0)),
                   pl.BlockSpec((tm, width), lambda i: (i, 0))],
        out_shape=[jax.ShapeDtypeStruct((rows, d), BF16),
                   jax.ShapeDtypeStruct((rows, width), BF16)],
        scratch_shapes=[pltpu.VMEM((tm, width), F32)],
        compiler_params=_params(("arbitrary",)),
        name="norm_project_qx",
    )(x2d, gain.reshape(1, d), w_bf, cos, sin)


def _silu(z):
    half = 0.5 * z
    return half + half * jnp.tanh(half)


def _rope_half_lane(acc, cos, sin_signed):
    outs = []
    for h in range(acc.shape[1] // HEAD_DIM):
        xh = acc[:, h * HEAD_DIM:(h + 1) * HEAD_DIM]
        outs.append(xh * cos + pltpu.roll(xh, HEAD_DIM // 2, axis=1) * sin_signed)
    return jnp.concatenate(outs, axis=1)


def _rope_two_tiles(acc, cos, sin):
    outs = []
    for h in range(acc.shape[1] // XATTN_HEAD_DIM):
        x1 = acc[:, h * XATTN_HEAD_DIM:h * XATTN_HEAD_DIM + LANES]
        x2 = acc[:, h * XATTN_HEAD_DIM + LANES:(h + 1) * XATTN_HEAD_DIM]
        outs.append(x1 * cos - x2 * sin)
        outs.append(x2 * cos + x1 * sin)
    return jnp.concatenate(outs, axis=1)


def _elementwise_finish(epilogue):
    def finish(acc_ref, slot, o_ref, aux_refs, scratch_refs, row_tile):
        for r0 in range(0, o_ref.shape[0], EPILOGUE_ROWS):
            rows = slice(r0, r0 + EPILOGUE_ROWS)
            o_ref[rows, :] = epilogue(acc_ref[slot, rows, :],
                                      *[r[0, rows, :] for r in aux_refs]).astype(o_ref.dtype)
    return finish


def _gated_conv_finish(acc_ref, slot, o_ref, aux_refs, scratch_refs, row_tile, *, tiles_per_seq):
    (w_ref,), (carry_ref,) = aux_refs, scratch_refs
    tm = o_ref.shape[0]
    u_cols, b_cols, c_cols, z_cols = (slice(k * LANES, (k + 1) * LANES) for k in range(4))
    w = w_ref[...]
    tail = jnp.where(row_tile % tiles_per_seq == 0, 0.0, carry_ref[...])
    for r0 in range(0, tm, EPILOGUE_ROWS):
        rows = slice(r0, r0 + EPILOGUE_ROWS)
        cu = acc_ref[slot, rows, c_cols] * acc_ref[slot, rows, u_cols]
        window = jnp.concatenate([tail, cu], axis=0)
        conv = cu * w[CONV_K - 1:CONV_K, :]
        for lag in range(1, CONV_K):
            shifted = pltpu.roll(window, lag, axis=0)[CONV_HALO:, :]
            conv = conv + shifted * w[CONV_K - 1 - lag:CONV_K - lag, :]
        o_ref[rows, :] = (acc_ref[slot, rows, b_cols] * conv
                          * _silu(acc_ref[slot, rows, z_cols])).astype(o_ref.dtype)
        tail = cu[EPILOGUE_ROWS - CONV_HALO:, :]
    carry_ref[...] = tail


def _in_proj_kernel(*refs, ni, n_steps, n_w, w_pieces, n_aux, n_scratch, finish_fn, rider_fn):
    h_ref = refs[0]
    w_refs = refs[1:1 + n_w]
    aux_refs = refs[1 + n_w:1 + n_w + n_aux]
    scratch_refs = refs[len(refs) - n_scratch:] if n_scratch else ()
    rest = refs[1 + n_w + n_aux:len(refs) - n_scratch]
    if rider_fn is not None:
        rider_in_ref, *rider_row_refs = rest[:-4]
        o_ref, rider_out_ref, wbf_ref, acc_ref = rest[-4:]
    else:
        o_ref, wbf_ref, acc_ref = rest
    t = pl.program_id(0)
    piece_rows = wbf_ref.shape[0] // w_pieces
    slab_cols = wbf_ref.shape[1] // (n_w // w_pieces)

    @pl.when((t % ni == 0) & (t < n_steps))
    def _():
        for k, w_ref in enumerate(w_refs):
            slab, p = divmod(k, w_pieces)
            wbf_ref[p * piece_rows:(p + 1) * piece_rows,
                    slab * slab_cols:(slab + 1) * slab_cols] = w_ref[...].astype(BF16)

    def ride():
        for r0 in range(0, rider_in_ref.shape[0], RIDER_ROWS):
            rows = slice(r0, r0 + RIDER_ROWS)
            rider_out_ref[rows, :] = rider_fn(
                rider_in_ref[rows, :], *[r[...] for r in rider_row_refs]
            ).astype(rider_out_ref.dtype)

    def multiply(slot):
        acc_ref[slot] = jnp.dot(h_ref[...], wbf_ref[...], preferred_element_type=F32)
        if rider_fn is not None:
            ride()

    def finish(slot):
        finish_fn(acc_ref, slot, o_ref, aux_refs, scratch_refs, ((t - 1) % ni))

    @pl.when(t == 0)
    def _():
        for ref in scratch_refs:
            ref[...] = jnp.zeros_like(ref)
        multiply(0)

    for parity in range(2):
        @pl.when((t > 0) & (t < n_steps) & (t % 2 == parity))
        def _(parity=parity):
            finish(1 - parity)
            multiply(parity)

    @pl.when(t == n_steps)
    def _():
        finish((n_steps - 1) % 2)


def _in_proj(h, w, col_slabs, n_col_tiles, finish_fn, aux, tm, tn, name, out_tn=None,
             scratch=(), rider=None, w_pieces=W_PIECES):
    rows, d = h.shape
    ni = rows // tm
    n_steps = n_col_tiles * ni
    out_tn = tn if out_tn is None else out_tn
    slab_cols = tn // len(col_slabs)
    rider_in_specs, rider_out_specs, rider_out_shapes, rider_operands = [], [], [], []
    rider_fn, rider_steps = None, 0
    if rider is not None:
        rider_fn, rider_matrix, rider_rows = rider
        r_rows, r_cols = rider_matrix.shape
        assert all(r.shape == (1, r_cols) for r in rider_rows)
        rider_steps = max(k for k in range(1, n_steps + 1) if r_rows % k == 0)
        rider_spec = pl.BlockSpec((r_rows // rider_steps, r_cols),
                                  lambda t: (jnp.minimum(t, rider_steps - 1), 0))
        rider_in_specs = [rider_spec] + [pl.BlockSpec((1, r_cols), lambda t: (0, 0))
                                         for _ in rider_rows]
        rider_out_specs = [rider_spec]
        rider_out_shapes = [jax.ShapeDtypeStruct(rider_matrix.shape, BF16)]
        rider_operands = [rider_matrix, *rider_rows]

    def cur(t):
        c = jnp.minimum(t, n_steps - 1)
        return c // ni, c % ni

    def prev(t):
        p = jnp.maximum(t - 1, 0)
        return p // ni, p % ni

    def aux_spec(block, index):
        return pl.BlockSpec(block, lambda t: index(*prev(t)))

    n_w = len(col_slabs) * w_pieces
    stagger = min(W_PIECES, n_w)

    def w_piece_spec(slab, col_slab, p):
        turn = (slab * w_pieces + p) * stagger // n_w

        def index(t):
            j, i = cur(t)
            ahead = (i >= ni - stagger + turn).astype(jnp.int32)
            return p, col_slab(jnp.minimum(j + ahead, n_col_tiles - 1))
        return pl.BlockSpec((d // w_pieces, slab_cols), index)

    assert ni > stagger and d % w_pieces == 0
    w_specs = [w_piece_spec(slab, col_slab, p) for slab, col_slab in enumerate(col_slabs)
               for p in range(w_pieces)]
    assert len(w_specs) == n_w
    results = pl.pallas_call(
        functools.partial(_in_proj_kernel, ni=ni, n_steps=n_steps, n_w=len(w_specs),
                          w_pieces=w_pieces, n_aux=len(aux), n_scratch=len(scratch),
                          finish_fn=finish_fn, rider_fn=rider_fn),
        grid=(n_steps + 1,),
        in_specs=[pl.BlockSpec((tm, d), lambda t: (cur(t)[1], 0))] + w_specs
                 + [aux_spec(block, index) for _, block, index in aux] + rider_in_specs,
        out_specs=[pl.BlockSpec((tm, out_tn), lambda t: (prev(t)[1], prev(t)[0]))]
                  + rider_out_specs,
        out_shape=[jax.ShapeDtypeStruct((rows, n_col_tiles * out_tn), BF16)] + rider_out_shapes,
        scratch_shapes=[pltpu.VMEM((d, tn), BF16), pltpu.VMEM((2, tm, tn), F32), *scratch],
        compiler_params=_params(("arbitrary",)),
        name=name,
    )(h, *([w] * len(w_specs)), *[arr for arr, _, _ in aux], *rider_operands)
    return results if rider is not None else results[0]


def _mem_proj_kernel(h_ref, w_ref, cos_ref, sin_ref, o_ref, *, tn):
    j = pl.program_id(0)
    acc = jnp.dot(h_ref[...], w_ref[...].astype(BF16), preferred_element_type=F32)

    @pl.when(j < XATTN_WIDTH // tn)
    def _():
        o_ref[...] = _rope_two_tiles(acc, cos_ref[...], sin_ref[...]).astype(o_ref.dtype)

    @pl.when(j >= XATTN_WIDTH // tn)
    def _():
        o_ref[...] = acc.astype(o_ref.dtype)


def _mem_proj(hm, w, cos, sin, tn):
    rows, d = hm.shape
    ncols = w.shape[1]
    tab = pl.BlockSpec((rows, LANES), lambda j: (0, 0))
    return pl.pallas_call(
        functools.partial(_mem_proj_kernel, tn=tn),
        grid=(ncols // tn,),
        in_specs=[pl.BlockSpec((rows, d), lambda j: (0, 0)),
                  pl.BlockSpec((d, tn), lambda j: (0, j)),
                  tab, tab],
        out_specs=pl.BlockSpec((rows, tn), lambda j: (0, j)),
        out_shape=jax.ShapeDtypeStruct((rows, ncols), BF16),
        compiler_params=_params(("parallel",)),
        name="mem_proj",
    )(hm, w, cos, sin)


MAX_SINGLE_ACCESS_STRIDE = 4


def _split_stride(dil):
    if dil <= MAX_SINGLE_ACCESS_STRIDE:
        return dil, 1
    assert dil % MAX_SINGLE_ACCESS_STRIDE == 0
    f2 = dil // MAX_SINGLE_ACCESS_STRIDE
    assert f2 <= MAX_SINGLE_ACCESS_STRIDE
    return MAX_SINGLE_ACCESS_STRIDE, f2


def _dilated_attn_kernel(q0, q1, q2, k0, k1, k2, v0, v1, v2, z0, z1, z2, y0, y1, y2,
                         stage_ref, mid_ref, sub_ref, osub_ref, lsub_ref, onat_ref, lnat_ref,
                         *, seq):
    q_refs, k_refs, v_refs = (q0, q1, q2), (k0, k1, k2), (v0, v1, v2)
    z_refs, y_refs = (z0, z1, z2), (y0, y1, y2)
    nblk = seq // BLOCK
    qi = lax.broadcasted_iota(jnp.int32, (BLOCK, 2 * BLOCK), 0)
    kk = lax.broadcasted_iota(jnp.int32, (BLOCK, 2 * BLOCK), 1)
    dist = qi + BLOCK - kk
    band = (dist >= 0) & (dist <= BLOCK)
    causal = (lax.broadcasted_iota(jnp.int32, (BLOCK, BLOCK), 1)
              <= lax.broadcasted_iota(jnp.int32, (BLOCK, BLOCK), 0))

    rows_of = []
    for g, (window, dil) in enumerate(DIL_GROUPS):
        sub_len = seq // dil
        if dil == 1:
            def group_rows(which, start, size, g=g):
                return (q_refs, k_refs, v_refs)[which][g][0, pl.ds(start, size), :]
        else:
            slab = 3 * (g - 1)
            f1, f2 = _split_stride(dil)
            len1 = seq // f1
            for which, refs in enumerate((q_refs, k_refs, v_refs)):
                stage_ref[slab + which] = refs[g][0].astype(F32)
                if f2 == 1:
                    for r in range(dil):
                        sub_ref[slab + which, r * sub_len:(r + 1) * sub_len, :] = (
                            stage_ref[slab + which, pl.ds(r, sub_len, stride=dil), :].astype(BF16))
                    continue
                for r1 in range(f1):
                    mid_ref[which, r1 * len1:(r1 + 1) * len1, :] = (
                        stage_ref[slab + which, pl.ds(r1, len1, stride=f1), :])
                for r1 in range(f1):
                    for r2 in range(f2):
                        r = r2 * f1 + r1
                        sub_ref[slab + which, r * sub_len:(r + 1) * sub_len, :] = (
                            mid_ref[which, pl.ds(r1 * len1 + r2, sub_len, stride=f2), :]
                            .astype(BF16))

            def group_rows(which, start, size, slab=slab):
                return sub_ref[slab + which, pl.ds(start, size), :]
        rows_of.append(group_rows)

    def attend(g, start, first):
        q = rows_of[g](0, start, BLOCK)
        if first:
            k = rows_of[g](1, start, BLOCK)
            v = rows_of[g](2, start, BLOCK)
            mask = causal
        else:
            k = rows_of[g](1, start - BLOCK, 2 * BLOCK)
            v = rows_of[g](2, start - BLOCK, 2 * BLOCK)
            mask = band
        s = lax.dot_general(q, k, (((1,), (1,)), ((), ())), preferred_element_type=F32)
        s = jnp.where(mask, s, NEG_INF)
        m = jnp.max(s, axis=-1, keepdims=True)
        p = jnp.exp(s - m)
        l = jnp.sum(p, axis=-1, keepdims=True)
        o = jnp.dot(p.astype(BF16), v, preferred_element_type=F32) * (1.0 / l)
        lse = jnp.broadcast_to(m + jnp.log(l), (BLOCK, HEAD_DIM))
        if DIL_GROUPS[g][1] == 1:
            onat_ref[g, pl.ds(start, BLOCK), :] = o
            lnat_ref[g, pl.ds(start, BLOCK), :] = lse
        else:
            osub_ref[g - 1, pl.ds(start, BLOCK), :] = o
            lsub_ref[g - 1, pl.ds(start, BLOCK), :] = lse

    for c in range(nblk):
        for g, (window, dil) in enumerate(DIL_GROUPS):
            attend(g, c * BLOCK, c % (seq // dil // BLOCK) == 0)

    for g, (window, dil) in enumerate(DIL_GROUPS):
        sub_len = seq // dil
        if dil == 1:
            continue
        f1, f2 = _split_stride(dil)
        len1 = seq // f1
        for idx, (nat_ref, subseq_ref) in enumerate(((onat_ref, osub_ref), (lnat_ref, lsub_ref))):
            if f2 == 1:
                for r in range(dil):
                    src = slice(r * sub_len, (r + 1) * sub_len)
                    nat_ref[g, pl.ds(r, sub_len, stride=dil), :] = subseq_ref[g - 1, src, :]
                continue
            mid = 3 + idx
            for r1 in range(f1):
                for r2 in range(f2):
                    r = r2 * f1 + r1
                    mid_ref[mid, pl.ds(r1 * len1 + r2, sub_len, stride=f2), :] = (
                        subseq_ref[g - 1, r * sub_len:(r + 1) * sub_len, :])
            for r1 in range(f1):
                nat_ref[g, pl.ds(r1, len1, stride=f1), :] = mid_ref[mid, r1 * len1:(r1 + 1) * len1, :]

    chunk = 2 * BLOCK
    for t in range(seq // chunk):
        rows = slice(t * chunk, (t + 1) * chunk)
        lse = [lnat_ref[g, rows, :] for g in range(len(DIL_GROUPS))]
        mx = jnp.maximum(jnp.maximum(lse[0], lse[1]), lse[2])
        e = [jnp.exp(x - mx) for x in lse]
        inv = 1.0 / (e[0] + e[1] + e[2])
        for g in range(len(DIL_GROUPS)):
            y_refs[g][0, rows, :] = (onat_ref[g, rows, :] * (e[g] * inv)
                                     * z_refs[g][0, rows, :].astype(F32)).astype(y_refs[g].dtype)


def _dilated_attn(q_src, k_src, v_src, z_src):
    sources = (q_src, k_src, v_src, z_src)
    b, s, _ = q_src[0].shape
    n_groups = len(DIL_GROUPS)
    n_regrouped = n_groups - 1
    assert sum(_split_stride(dil)[1] > 1 for _, dil in DIL_GROUPS) <= 1

    def head_spec(off, g):
        first = (off + g * GROUP_WIDTH) // HEAD_DIM
        return pl.BlockSpec((1, s, HEAD_DIM), lambda bi, hg: (bi, 0, first + hg))

    in_specs = [head_spec(off, g) for _, off in sources for g in range(n_groups)]
    operands = [arr for arr, _ in sources for g in range(n_groups)]
    out_spec = pl.BlockSpec((1, s, HEAD_DIM), lambda bi, hg: (bi, 0, hg))
    return pl.pallas_call(
        functools.partial(_dilated_attn_kernel, seq=s),
        grid=(b, HEADS_PER_DIL),
        in_specs=in_specs,
        out_specs=[out_spec] * n_groups,
        out_shape=[jax.ShapeDtypeStruct((b, s, GROUP_WIDTH), BF16)] * n_groups,
        scratch_shapes=[pltpu.VMEM((3 * n_regrouped, s, HEAD_DIM), F32),
                        pltpu.VMEM((5, s, HEAD_DIM), F32),
                        pltpu.VMEM((3 * n_regrouped, s, HEAD_DIM), BF16),
                        pltpu.VMEM((n_regrouped, s, HEAD_DIM), F32),
                        pltpu.VMEM((n_regrouped, s, HEAD_DIM), F32),
                        pltpu.VMEM((n_groups, s, HEAD_DIM), F32),
                        pltpu.VMEM((n_groups, s, HEAD_DIM), F32)],
        compiler_params=_params(("parallel", "parallel")),
        name="dilated_attn",
    )(*operands)


def _xattn_kernel(q_ref, z_ref, mk_ref, mv_ref, y_ref):
    for h in range(N_XATTN_HEADS):
        cols = slice(h * XATTN_HEAD_DIM, (h + 1) * XATTN_HEAD_DIM)
        s = lax.dot_general(q_ref[0, :, cols], mk_ref[0, :, cols],
                            (((1,), (1,)), ((), ())), preferred_element_type=F32)
        m = jnp.max(s, axis=-1, keepdims=True)
        p = jnp.exp(s - m)
        l = jnp.sum(p, axis=-1, keepdims=True)
        o = jnp.dot(p.astype(BF16), mv_ref[0, :, cols], preferred_element_type=F32)
        y_ref[0, :, cols] = (o * (1.0 / l) * z_ref[0, :, cols].astype(F32)).astype(y_ref.dtype)


def _xattn(q_src, z_src, mkv3, tm):
    b, s, _ = q_src[0].shape
    m_len = mkv3.shape[1]
    q_off, z_off = q_src[1], z_src[1]
    assert q_off % XATTN_WIDTH == 0 and z_off % XATTN_WIDTH == 0
    return pl.pallas_call(
        _xattn_kernel,
        grid=(b, s // tm),
        in_specs=[pl.BlockSpec((1, tm, XATTN_WIDTH), lambda bi, i: (bi, i, q_off // XATTN_WIDTH)),
                  pl.BlockSpec((1, tm, XATTN_WIDTH), lambda bi, i: (bi, i, z_off // XATTN_WIDTH)),
                  pl.BlockSpec((1, m_len, XATTN_WIDTH), lambda bi, i: (bi, 0, 0)),
                  pl.BlockSpec((1, m_len, XATTN_WIDTH), lambda bi, i: (bi, 0, 1))],
        out_specs=pl.BlockSpec((1, tm, XATTN_WIDTH), lambda bi, i: (bi, i, 0)),
        out_shape=jax.ShapeDtypeStruct((b, s, XATTN_WIDTH), BF16),
        compiler_params=_params(("parallel", "parallel")),
        name="mem_xattn",
    )(q_src[0], z_src[0], mkv3, mkv3)


def _out_proj_kernel(*refs, ni, d, n_mix):
    y_refs, w_refs = refs[:n_mix], refs[n_mix:2 * n_mix]
    x_ref, g_ref, o_ref, y2_ref, ssq_ref, scale_ref = refs[2 * n_mix:]
    i = pl.program_id(0)
    j = pl.program_id(1)

    @pl.when(j == 0)
    def _():
        @pl.when(i > 0)
        def _():
            scale_ref[...] = lax.rsqrt(ssq_ref[...] * (1.0 / d) + EPS)
        ssq_ref[...] = jnp.zeros_like(ssq_ref)

    def finish_previous_tile():
        o_ref[...] = x_ref[...] + y2_ref[j] * scale_ref[...] * g_ref[...]

    def multiply_this_tile():
        y2 = jnp.dot(y_refs[0][...], w_refs[0][...], preferred_element_type=F32)
        for y_ref, w_ref in zip(y_refs[1:], w_refs[1:]):
            y2 = y2 + jnp.dot(y_ref[...], w_ref[...], preferred_element_type=F32)
        y2_ref[j] = y2
        ssq_ref[...] += jnp.sum(y2 * y2, axis=-1, keepdims=True)

    @pl.when(i == 0)
    def _():
        multiply_this_tile()

    @pl.when((i > 0) & (i < ni))
    def _():
        finish_previous_tile()
        multiply_this_tile()

    @pl.when(i == ni)
    def _():
        finish_previous_tile()


def _out_proj(ys, w_bf, x2d, gain, tm, tn):
    rows, d = x2d.shape
    nj = d // tn
    ni = rows // tm

    def y_spec(width):
        return pl.BlockSpec((tm, width), lambda i, j: (jnp.minimum(i, ni - 1), 0))

    def w_spec(width, row_off):
        assert row_off % width == 0
        return pl.BlockSpec((width, tn),
                            lambda i, j: (row_off // width, jnp.where(i == ni, nj - 1, j)))

    io_spec = pl.BlockSpec((tm, tn),
                           lambda i, j: (jnp.maximum(i - 1, 0), jnp.where(i == 0, 0, j)))
    widths = [y.shape[1] for y in ys]
    offsets = [sum(widths[:k]) for k in range(len(widths))]
    return pl.pallas_call(
        functools.partial(_out_proj_kernel, ni=ni, d=d, n_mix=len(ys)),
        grid=(ni + 1, nj),
        in_specs=([y_spec(w) for w in widths]
                  + [w_spec(w, off) for w, off in zip(widths, offsets)]
                  + [io_spec, pl.BlockSpec((1, tn), lambda i, j: (0, j))]),
        out_specs=io_spec,
        out_shape=jax.ShapeDtypeStruct((rows, d), F32),
        scratch_shapes=[pltpu.VMEM((nj, tm, tn), F32),
                        pltpu.VMEM((tm, 1), F32),
                        pltpu.VMEM((tm, 1), F32)],
        compiler_params=_params(("arbitrary", "arbitrary")),
        name="out_proj",
    )(*ys, *([w_bf] * len(ys)), x2d, gain.reshape(1, d))


def _rope_tables(pos, half):
    inv = 1.0 / (ROPE_THETA ** (np.arange(half, dtype=np.float64) / half))
    ang = np.asarray(pos, dtype=np.float64)[:, None] * inv[None, :]
    return np.cos(ang), np.sin(ang)


def kernel(x, mem, pre_norm, w_in, conv_w, mem_norm, w_mem_kv, w_out, post_norm):
    b, s, d = x.shape
    m_len = mem.shape[1]
    depth = w_in.shape[0]
    assert all(window // dil == BLOCK for window, dil in DIL_GROUPS)
    pos = np.arange(s)
    cos_a, sin_a = _rope_tables(pos, HEAD_DIM // 2)
    cosa = np.concatenate([cos_a, cos_a], axis=-1)
    sina = np.concatenate([-sin_a, sin_a], axis=-1)
    a_scale = HEAD_DIM ** -0.5
    cos_qkv = jnp.asarray(np.stack([cosa * a_scale, cosa, np.ones_like(cosa)]), F32)
    sin_qkv = jnp.asarray(np.stack([sina * a_scale, sina, np.zeros_like(sina)]), F32)
    cosx, sinx = _rope_tables(pos + m_len, XATTN_HEAD_DIM // 2)
    x_scale = XATTN_HEAD_DIM ** -0.5
    cos_qx = jnp.asarray((cosx * x_scale)[None], F32)
    sin_qx = jnp.asarray((sinx * x_scale)[None], F32)
    cos_m, sin_m = _rope_tables(np.arange(m_len), XATTN_HEAD_DIM // 2)
    cosm = jnp.asarray(np.tile(cos_m, (b, 1)), F32)
    sinm = jnp.asarray(np.tile(sin_m, (b, 1)), F32)

    tm, tn = 1024, 512

    def tile_of(off):
        assert off % tn == 0
        return off // tn

    attn_tiles, x_tiles = ATTN_WIDTH // tn, XATTN_WIDTH // tn
    tiles_per_seq = s // tm

    def view(t):
        return t.reshape(b, s, t.shape[-1])

    def rope_tables(cos, sin, kind):
        def index(col_tile, row_tile):
            return kind(col_tile), row_tile % tiles_per_seq, 0
        return [(cos, (1, tm, LANES), index), (sin, (1, tm, LANES), index)]

    def lane_slab(off):
        assert off % LANES == 0
        return lambda j: off // LANES + j

    for layer in range(depth):
        x2d = x.reshape(b * s, d)
        w = w_in[layer]
        common = dict(tm=tm, tn=tn)
        w_qx_bf = _cast_columns_bf16(w, OFF_QX, XATTN_WIDTH, tr=2048)
        h, qx = _norm_project(x2d, pre_norm[layer], w_qx_bf, cos_qx, sin_qx, seq=s, tm=512)
        qx = view(qx)

        qkv = view(_in_proj(h, w, [lambda j: tile_of(OFF_QA) + j], 3 * attn_tiles,
                            _elementwise_finish(_rope_half_lane),
                            rope_tables(cos_qkv, sin_qkv, lambda j: j // attn_tiles),
                            name="in_proj_qkv", **common))
        y_conv, w_out_bf = _in_proj(
            h, w, [lane_slab(OFF_UC), lane_slab(OFF_BC), lane_slab(OFF_CC), lane_slab(OFF_ZC)],
            CONV_WIDTH // LANES,
            functools.partial(_gated_conv_finish, tiles_per_seq=tiles_per_seq),
            [(conv_w[layer], (CONV_K, LANES), lambda col_tile, row_tile: (0, col_tile))],
            out_tn=LANES, scratch=[pltpu.VMEM((CONV_HALO, LANES), F32)],
            rider=(lambda block: block, w_out[layer], []), name="in_proj_conv",
            w_pieces=W_PIECES // 2, **common)
        gates, hm = _in_proj(
            h, w, [lambda j: jnp.where(j < x_tiles, tile_of(OFF_ZX) + j,
                                       tile_of(OFF_ZA) - x_tiles + j)],
            x_tiles + attn_tiles, _elementwise_finish(_silu), [], name="in_proj_gates",
            rider=(_rms_normalize, mem.reshape(b * m_len, d), [mem_norm[layer].reshape(1, d)]),
            **common)
        gates = view(gates)
        mkv = _mem_proj(hm, w_mem_kv[layer], cosm, sinm, tn=512)

        y_groups = _dilated_attn((qkv, 0), (qkv, ATTN_WIDTH), (qkv, 2 * ATTN_WIDTH),
                                 (gates, XATTN_WIDTH))
        y_groups = [y.reshape(b * s, GROUP_WIDTH) for y in y_groups]
        y_x = _xattn((qx, 0), (gates, 0), mkv.reshape(b, m_len, 2 * XATTN_WIDTH), tm=s)
        y_x = y_x.reshape(b * s, XATTN_WIDTH)

        out = _out_proj([*y_groups, y_conv, y_x], w_out_bf, x2d,
                        post_norm[layer], tm=1024, tn=512)
        x = out.reshape(b, s, d)
    return x
```

```python
import functools

import jax
import jax.numpy as jnp
import numpy as np
from jax import lax
from jax.experimental import pallas as pl
from jax.experimental.pallas import tpu as pltpu

D_MODEL = 4096
HEAD_DIM = 128
DIL_GROUPS = ((128, 1), (512, 4), (2048, 16))
ATTN_WIDTH = 3 * D_MODEL // 8
CONV_WIDTH = 3 * D_MODEL // 8
XATTN_WIDTH = D_MODEL // 4
N_XATTN_HEADS = 4
XATTN_HEAD_DIM = XATTN_WIDTH // N_XATTN_HEADS
HEADS_PER_DIL = 4
GROUP_WIDTH = HEADS_PER_DIL * HEAD_DIM
CONV_K = 3
BLOCK = 128
ROPE_THETA = 10000.0
EPS = 1e-6
NEG_INF = -1e30

OFF_QA = 0
OFF_KA = OFF_QA + ATTN_WIDTH
OFF_VA = OFF_KA + ATTN_WIDTH
OFF_ZA = OFF_VA + ATTN_WIDTH
OFF_UC = OFF_ZA + ATTN_WIDTH
OFF_BC = OFF_UC + CONV_WIDTH
OFF_CC = OFF_BC + CONV_WIDTH
OFF_ZC = OFF_CC + CONV_WIDTH
OFF_QX = OFF_ZC + CONV_WIDTH
OFF_ZX = OFF_QX + XATTN_WIDTH

V7X_VMEM_LIMIT_BYTES = 56 * 1024 * 1024
LANES = 128
W_PIECES = 4
EPILOGUE_ROWS = 128
CONV_HALO = 8
NORM_ROWS = 64
RIDER_ROWS = 16

BF16 = jnp.bfloat16
F32 = jnp.float32


def _params(semantics):
    return pltpu.CompilerParams(dimension_semantics=semantics,
                                vmem_limit_bytes=V7X_VMEM_LIMIT_BYTES)


def _rms_normalize(x, gain):
    ms = jnp.mean(x * x, axis=-1, keepdims=True)
    return x * lax.rsqrt(ms + EPS) * gain


def _cast_kernel(x_ref, o_ref):
    o_ref[...] = x_ref[...].astype(o_ref.dtype)


def _cast_columns_bf16(w, col_off, width, tr):
    rows = w.shape[0]
    assert col_off % width == 0
    return pl.pallas_call(
        _cast_kernel,
        grid=(rows // tr,),
        in_specs=[pl.BlockSpec((tr, width), lambda i: (i, col_off // width))],
        out_specs=pl.BlockSpec((tr, width), lambda i: (i, 0)),
        out_shape=jax.ShapeDtypeStruct((rows, width), BF16),
        compiler_params=_params(("parallel",)),
        name="cast_w_qx",
    )(w)


def _norm_project_kernel(x_ref, g_ref, w_ref, cos_ref, sin_ref, h_ref, o_ref, acc_ref):
    tm = x_ref.shape[0]
    for r0 in range(0, tm, NORM_ROWS):
        rows = slice(r0, r0 + NORM_ROWS)
        h_ref[rows, :] = _rms_normalize(x_ref[rows, :], g_ref[...]).astype(h_ref.dtype)
    acc_ref[...] = jnp.dot(h_ref[...], w_ref[...], preferred_element_type=F32)
    for r0 in range(0, tm, EPILOGUE_ROWS):
        rows = slice(r0, r0 + EPILOGUE_ROWS)
        o_ref[rows, :] = _rope_two_tiles(acc_ref[rows, :], cos_ref[0, rows, :],
                                         sin_ref[0, rows, :]).astype(o_ref.dtype)


def _norm_project(x2d, gain, w_bf, cos, sin, seq, tm):
    rows, d = x2d.shape
    width = w_bf.shape[1]
    tiles_per_seq = seq // tm
    table = pl.BlockSpec((1, tm, LANES), lambda i: (0, i % tiles_per_seq, 0))
    return pl.pallas_call(
        _norm_project_kernel,
        grid=(rows // tm,),
        in_specs=[pl.BlockSpec((tm, d), lambda i: (i, 0)),
                  pl.BlockSpec((1, d), lambda i: (0, 0)),
                  pl.BlockSpec((d, width), lambda i: (0, 0)),
                  table, table],
        out_specs=[pl.BlockSpec((tm, d), lambda i: (i, 0)),
                   pl.BlockSpec((tm, width), lambda i: (i, 0))],
        out_shape=[jax.ShapeDtypeStruct((rows, d), BF16),
                   jax.ShapeDtypeStruct((rows, width), BF16)],
        scratch_shapes=[pltpu.VMEM((tm, width), F32)],
        compiler_params=_params(("arbitrary",)),
        name="norm_project_qx",
    )(x2d, gain.reshape(1, d), w_bf, cos, sin)


def _silu(z):
    half = 0.5 * z
    return half + half * jnp.tanh(half)


def _rope_half_lane(acc, cos, sin_signed):
    outs = []
    for h in range(acc.shape[1] // HEAD_DIM):
        xh = acc[:, h * HEAD_DIM:(h + 1) * HEAD_DIM]
        outs.append(xh * cos + pltpu.roll(xh, HEAD_DIM // 2, axis=1) * sin_signed)
    return jnp.concatenate(outs, axis=1)


def _rope_two_tiles(acc, cos, sin):
    outs = []
    for h in range(acc.shape[1] // XATTN_HEAD_DIM):
        x1 = acc[:, h * XATTN_HEAD_DIM:h * XATTN_HEAD_DIM + LANES]
        x2 = acc[:, h * XATTN_HEAD_DIM + LANES:(h + 1) * XATTN_HEAD_DIM]
        outs.append(x1 * cos - x2 * sin)
        outs.append(x2 * cos + x1 * sin)
    return jnp.concatenate(outs, axis=1)


def _elementwise_finish(epilogue):
    def finish(acc_ref, slot, o_ref, aux_refs, scratch_refs, row_tile):
        for r0 in range(0, o_ref.shape[0], EPILOGUE_ROWS):
            rows = slice(r0, r0 + EPILOGUE_ROWS)
            o_ref[rows, :] = epilogue(acc_ref[slot, rows, :],
                                      *[r[0, rows, :] for r in aux_refs]).astype(o_ref.dtype)
    return finish


def _gated_conv_finish(acc_ref, slot, o_ref, aux_refs, scratch_refs, row_tile, *, tiles_per_seq):
    (w_ref,), (carry_ref,) = aux_refs, scratch_refs
    tm = o_ref.shape[0]
    u_cols, b_cols, c_cols, z_cols = (slice(k * LANES, (k + 1) * LANES) for k in range(4))
    w = w_ref[...]
    tail = jnp.where(row_tile % tiles_per_seq == 0, 0.0, carry_ref[...])
    for r0 in range(0, tm, EPILOGUE_ROWS):
        rows = slice(r0, r0 + EPILOGUE_ROWS)
        cu = acc_ref[slot, rows, c_cols] * acc_ref[slot, rows, u_cols]
        window = jnp.concatenate([tail, cu], axis=0)
        conv = cu * w[CONV_K - 1:CONV_K, :]
        for lag in range(1, CONV_K):
            shifted = pltpu.roll(window, lag, axis=0)[CONV_HALO:, :]
            conv = conv + shifted * w[CONV_K - 1 - lag:CONV_K - lag, :]
        o_ref[rows, :] = (acc_ref[slot, rows, b_cols] * conv
                          * _silu(acc_ref[slot, rows, z_cols])).astype(o_ref.dtype)
        tail = cu[EPILOGUE_ROWS - CONV_HALO:, :]
    carry_ref[...] = tail


def _in_proj_kernel(*refs, ni, n_steps, n_w, w_pieces, n_aux, n_scratch, finish_fn, rider_fn):
    h_ref = refs[0]
    w_refs = refs[1:1 + n_w]
    aux_refs = refs[1 + n_w:1 + n_w + n_aux]
    scratch_refs = refs[len(refs) - n_scratch:] if n_scratch else ()
    rest = refs[1 + n_w + n_aux:len(refs) - n_scratch]
    if rider_fn is not None:
        rider_in_ref, *rider_row_refs = rest[:-4]
        o_ref, rider_out_ref, wbf_ref, acc_ref = rest[-4:]
    else:
        o_ref, wbf_ref, acc_ref = rest
    t = pl.program_id(0)
    piece_rows = wbf_ref.shape[0] // w_pieces
    slab_cols = wbf_ref.shape[1] // (n_w // w_pieces)

    @pl.when((t % ni == 0) & (t < n_steps))
    def _():
        for k, w_ref in enumerate(w_refs):
            slab, p = divmod(k, w_pieces)
            wbf_ref[p * piece_rows:(p + 1) * piece_rows,
                    slab * slab_cols:(slab + 1) * slab_cols] = w_ref[...].astype(BF16)

    def ride():
        for r0 in range(0, rider_in_ref.shape[0], RIDER_ROWS):
            rows = slice(r0, r0 + RIDER_ROWS)
            rider_out_ref[rows, :] = rider_fn(
                rider_in_ref[rows, :], *[r[...] for r in rider_row_refs]
            ).astype(rider_out_ref.dtype)

    def multiply(slot):
        acc_ref[slot] = jnp.dot(h_ref[...], wbf_ref[...], preferred_element_type=F32)
        if rider_fn is not None:
            ride()

    def finish(slot):
        finish_fn(acc_ref, slot, o_ref, aux_refs, scratch_refs, ((t - 1) % ni))

    @pl.when(t == 0)
    def _():
        for ref in scratch_refs:
            ref[...] = jnp.zeros_like(ref)
        multiply(0)

    for parity in range(2):
        @pl.when((t > 0) & (t < n_steps) & (t % 2 == parity))
        def _(parity=parity):
            finish(1 - parity)
            multiply(parity)

    @pl.when(t == n_steps)
    def _():
        finish((n_steps - 1) % 2)


def _in_proj(h, w, col_slabs, n_col_tiles, finish_fn, aux, tm, tn, name, out_tn=None,
             scratch=(), rider=None, w_pieces=W_PIECES):
    rows, d = h.shape
    ni = rows // tm
    n_steps = n_col_tiles * ni
    out_tn = tn if out_tn is None else out_tn
    slab_cols = tn // len(col_slabs)
    rider_in_specs, rider_out_specs, rider_out_shapes, rider_operands = [], [], [], []
    rider_fn, rider_steps = None, 0
    if rider is not None:
        rider_fn, rider_matrix, rider_rows = rider
        r_rows, r_cols = rider_matrix.shape
        assert all(r.shape == (1, r_cols) for r in rider_rows)
        rider_steps = max(k for k in range(1, n_steps + 1) if r_rows % k == 0)
        rider_spec = pl.BlockSpec((r_rows // rider_steps, r_cols),
                                  lambda t: (jnp.minimum(t, rider_steps - 1), 0))
        rider_in_specs = [rider_spec] + [pl.BlockSpec((1, r_cols), lambda t: (0, 0))
                                         for _ in rider_rows]
        rider_out_specs = [rider_spec]
        rider_out_shapes = [jax.ShapeDtypeStruct(rider_matrix.shape, BF16)]
        rider_operands = [rider_matrix, *rider_rows]

    def cur(t):
        c = jnp.minimum(t, n_steps - 1)
        return c // ni, c % ni

    def prev(t):
        p = jnp.maximum(t - 1, 0)
        return p // ni, p % ni

    def aux_spec(block, index):
        return pl.BlockSpec(block, lambda t: index(*prev(t)))

    n_w = len(col_slabs) * w_pieces
    stagger = min(W_PIECES, n_w)

    def w_piece_spec(slab, col_slab, p):
        turn = (slab * w_pieces + p) * stagger // n_w

        def index(t):
            j, i = cur(t)
            ahead = (i >= ni - stagger + turn).astype(jnp.int32)
            return p, col_slab(jnp.minimum(j + ahead, n_col_tiles - 1))
        return pl.BlockSpec((d // w_pieces, slab_cols), index)

    assert ni > stagger and d % w_pieces == 0
    w_specs = [w_piece_spec(slab, col_slab, p) for slab, col_slab in enumerate(col_slabs)
               for p in range(w_pieces)]
    assert len(w_specs) == n_w
    results = pl.pallas_call(
        functools.partial(_in_proj_kernel, ni=ni, n_steps=n_steps, n_w=len(w_specs),
                          w_pieces=w_pieces, n_aux=len(aux), n_scratch=len(scratch),
                          finish_fn=finish_fn, rider_fn=rider_fn),
        grid=(n_steps + 1,),
        in_specs=[pl.BlockSpec((tm, d), lambda t: (cur(t)[1], 0))] + w_specs
                 + [aux_spec(block, index) for _, block, index in aux] + rider_in_specs,
        out_specs=[pl.BlockSpec((tm, out_tn), lambda t: (prev(t)[1], prev(t)[0]))]
                  + rider_out_specs,
        out_shape=[jax.ShapeDtypeStruct((rows, n_col_tiles * out_tn), BF16)] + rider_out_shapes,
        scratch_shapes=[pltpu.VMEM((d, tn), BF16), pltpu.VMEM((2, tm, tn), F32), *scratch],
        compiler_params=_params(("arbitrary",)),
        name=name,
    )(h, *([w] * len(w_specs)), *[arr for arr, _, _ in aux], *rider_operands)
    return results if rider is not None else results[0]


def _mem_proj_kernel(h_ref, w_ref, cos_ref, sin_ref, o_ref, *, tn):
    j = pl.program_id(0)
    acc = jnp.dot(h_ref[...], w_ref[...].astype(BF16), preferred_element_type=F32)

    @pl.when(j < XATTN_WIDTH // tn)
    def _():
        o_ref[...] = _rope_two_tiles(acc, cos_ref[...], sin_ref[...]).astype(o_ref.dtype)

    @pl.when(j >= XATTN_WIDTH // tn)
    def _():
        o_ref[...] = acc.astype(o_ref.dtype)


def _mem_proj(hm, w, cos, sin, tn):
    rows, d = hm.shape
    ncols = w.shape[1]
    tab = pl.BlockSpec((rows, LANES), lambda j: (0, 0))
    return pl.pallas_call(
        functools.partial(_mem_proj_kernel, tn=tn),
        grid=(ncols // tn,),
        in_specs=[pl.BlockSpec((rows, d), lambda j: (0, 0)),
                  pl.BlockSpec((d, tn), lambda j: (0, j)),
                  tab, tab],
        out_specs=pl.BlockSpec((rows, tn), lambda j: (0, j)),
        out_shape=jax.ShapeDtypeStruct((rows, ncols), BF16),
        compiler_params=_params(("parallel",)),
        name="mem_proj",
    )(hm, w, cos, sin)


MAX_SINGLE_ACCESS_STRIDE = 4


def _split_stride(dil):
    if dil <= MAX_SINGLE_ACCESS_STRIDE:
        return dil, 1
    assert dil % MAX_SINGLE_ACCESS_STRIDE == 0
    f2 = dil // MAX_SINGLE_ACCESS_STRIDE
    assert f2 <= MAX_SINGLE_ACCESS_STRIDE
    return MAX_SINGLE_ACCESS_STRIDE, f2


def _dilated_attn_kernel(q0, q1, q2, k0, k1, k2, v0, v1, v2, z0, z1, z2, y0, y1, y2,
                         stage_ref, mid_ref, sub_ref, osub_ref, lsub_ref, onat_ref, lnat_ref,
                         *, seq):
    q_refs, k_refs, v_refs = (q0, q1, q2), (k0, k1, k2), (v0, v1, v2)
    z_refs, y_refs = (z0, z1, z2), (y0, y1, y2)
    nblk = seq // BLOCK
    qi = lax.broadcasted_iota(jnp.int32, (BLOCK, 2 * BLOCK), 0)
    kk = lax.broadcasted_iota(jnp.int32, (BLOCK, 2 * BLOCK), 1)
    dist = qi + BLOCK - kk
    band = (dist >= 0) & (dist <= BLOCK)
    causal = (lax.broadcasted_iota(jnp.int32, (BLOCK, BLOCK), 1)
              <= lax.broadcasted_iota(jnp.int32, (BLOCK, BLOCK), 0))

    rows_of = []
    for g, (window, dil) in enumerate(DIL_GROUPS):
        sub_len = seq // dil
        if dil == 1:
            def group_rows(which, start, size, g=g):
                return (q_refs, k_refs, v_refs)[which][g][0, pl.ds(start, size), :]
        else:
            slab = 3 * (g - 1)
            f1, f2 = _split_stride(dil)
            len1 = seq // f1
            for which, refs in enumerate((q_refs, k_refs, v_refs)):
                stage_ref[slab + which] = refs[g][0].astype(F32)
                if f2 == 1:
                    for r in range(dil):
                        sub_ref[slab + which, r * sub_len:(r + 1) * sub_len, :] = (
                            stage_ref[slab + which, pl.ds(r, sub_len, stride=dil), :].astype(BF16))
                    continue
                for r1 in range(f1):
                    mid_ref[which, r1 * len1:(r1 + 1) * len1, :] = (
                        stage_ref[slab + which, pl.ds(r1, len1, stride=f1), :])
                for r1 in range(f1):
                    for r2 in range(f2):
                        r = r2 * f1 + r1
                        sub_ref[slab + which, r * sub_len:(r + 1) * sub_len, :] = (
                            mid_ref[which, pl.ds(r1 * len1 + r2, sub_len, stride=f2), :]
                            .astype(BF16))

            def group_rows(which, start, size, slab=slab):
                return sub_ref[slab + which, pl.ds(start, size), :]
        rows_of.append(group_rows)

    def attend(g, start, first):
        q = rows_of[g](0, start, BLOCK)
        if first:
            k = rows_of[g](1, start, BLOCK)
            v = rows_of[g](2, start, BLOCK)
            mask = causal
        else:
            k = rows_of[g](1, start - BLOCK, 2 * BLOCK)
            v = rows_of[g](2, start - BLOCK, 2 * BLOCK)
            mask = band
        s = lax.dot_general(q, k, (((1,), (1,)), ((), ())), preferred_element_type=F32)
        s = jnp.where(mask, s, NEG_INF)
        m = jnp.max(s, axis=-1, keepdims=True)
        p = jnp.exp(s - m)
        l = jnp.sum(p, axis=-1, keepdims=True)
        o = jnp.dot(p.astype(BF16), v, preferred_element_type=F32) * (1.0 / l)
        lse = jnp.broadcast_to(m + jnp.log(l), (BLOCK, HEAD_DIM))
        if DIL_GROUPS[g][1] == 1:
            onat_ref[g, pl.ds(start, BLOCK), :] = o
            lnat_ref[g, pl.ds(start, BLOCK), :] = lse
        else:
            osub_ref[g - 1, pl.ds(start, BLOCK), :] = o
            lsub_ref[g - 1, pl.ds(start, BLOCK), :] = lse

    for c in range(nblk):
        for g, (window, dil) in enumerate(DIL_GROUPS):
            attend(g, c * BLOCK, c % (seq // dil // BLOCK) == 0)

    for g, (window, dil) in enumerate(DIL_GROUPS):
        sub_len = seq // dil
        if dil == 1:
            continue
        f1, f2 = _split_stride(dil)
        len1 = seq // f1
        for idx, (nat_ref, subseq_ref) in enumerate(((onat_ref, osub_ref), (lnat_ref, lsub_ref))):
            if f2 == 1:
                for r in range(dil):
                    src = slice(r * sub_len, (r + 1) * sub_len)
                    nat_ref[g, pl.ds(r, sub_len, stride=dil), :] = subseq_ref[g - 1, src, :]
                continue
            mid = 3 + idx
            for r1 in range(f1):
                for r2 in range(f2):
                    r = r2 * f1 + r1
                    mid_ref[mid, pl.ds(r1 * len1 + r2, sub_len, stride=f2), :] = (
                        subseq_ref[g - 1, r * sub_len:(r + 1) * sub_len, :])
            for r1 in range(f1):
                nat_ref[g, pl.ds(r1, len1, stride=f1), :] = mid_ref[mid, r1 * len1:(r1 + 1) * len1, :]

    chunk = 2 * BLOCK
    for t in range(seq // chunk):
        rows = slice(t * chunk, (t + 1) * chunk)
        lse = [lnat_ref[g, rows, :] for g in range(len(DIL_GROUPS))]
        mx = jnp.maximum(jnp.maximum(lse[0], lse[1]), lse[2])
        e = [jnp.exp(x - mx) for x in lse]
        inv = 1.0 / (e[0] + e[1] + e[2])
        for g in range(len(DIL_GROUPS)):
            y_refs[g][0, rows, :] = (onat_ref[g, rows, :] * (e[g] * inv)
                                     * z_refs[g][0, rows, :].astype(F32)).astype(y_refs[g].dtype)


def _dilated_attn(q_src, k_src, v_src, z_src):
    sources = (q_src, k_src, v_src, z_src)
    b, s, _ = q_src[0].shape
    n_groups = len(DIL_GROUPS)
    n_regrouped = n_groups - 1
    assert sum(_split_stride(dil)[1] > 1 for _, dil in DIL_GROUPS) <= 1

    def head_spec(off, g):
        first = (off + g * GROUP_WIDTH) // HEAD_DIM
        return pl.BlockSpec((1, s, HEAD_DIM), lambda bi, hg: (bi, 0, first + hg))

    in_specs = [head_spec(off, g) for _, off in sources for g in range(n_groups)]
    operands = [arr for arr, _ in sources for g in range(n_groups)]
    out_spec = pl.BlockSpec((1, s, HEAD_DIM), lambda bi, hg: (bi, 0, hg))
    return pl.pallas_call(
        functools.partial(_dilated_attn_kernel, seq=s),
        grid=(b, HEADS_PER_DIL),
        in_specs=in_specs,
        out_specs=[out_spec] * n_groups,
        out_shape=[jax.ShapeDtypeStruct((b, s, GROUP_WIDTH), BF16)] * n_groups,
        scratch_shapes=[pltpu.VMEM((3 * n_regrouped, s, HEAD_DIM), F32),
                        pltpu.VMEM((5, s, HEAD_DIM), F32),
                        pltpu.VMEM((3 * n_regrouped, s, HEAD_DIM), BF16),
                        pltpu.VMEM((n_regrouped, s, HEAD_DIM), F32),
                        pltpu.VMEM((n_regrouped, s, HEAD_DIM), F32),
                        pltpu.VMEM((n_groups, s, HEAD_DIM), F32),
                        pltpu.VMEM((n_groups, s, HEAD_DIM), F32)],
        compiler_params=_params(("parallel", "parallel")),
        name="dilated_attn",
    )(*operands)


def _xattn_kernel(q_ref, z_ref, mk_ref, mv_ref, y_ref):
    for h in range(N_XATTN_HEADS):
        cols = slice(h * XATTN_HEAD_DIM, (h + 1) * XATTN_HEAD_DIM)
        s = lax.dot_general(q_ref[0, :, cols], mk_ref[0, :, cols],
                            (((1,), (1,)), ((), ())), preferred_element_type=F32)
        m = jnp.max(s, axis=-1, keepdims=True)
        p = jnp.exp(s - m)
        l = jnp.sum(p, axis=-1, keepdims=True)
        o = jnp.dot(p.astype(BF16), mv_ref[0, :, cols], preferred_element_type=F32)
        y_ref[0, :, cols] = (o * (1.0 / l) * z_ref[0, :, cols].astype(F32)).astype(y_ref.dtype)


def _xattn(q_src, z_src, mkv3, tm):
    b, s, _ = q_src[0].shape
    m_len = mkv3.shape[1]
    q_off, z_off = q_src[1], z_src[1]
    assert q_off % XATTN_WIDTH == 0 and z_off % XATTN_WIDTH == 0
    return pl.pallas_call(
        _xattn_kernel,
        grid=(b, s // tm),
        in_specs=[pl.BlockSpec((1, tm, XATTN_WIDTH), lambda bi, i: (bi, i, q_off // XATTN_WIDTH)),
                  pl.BlockSpec((1, tm, XATTN_WIDTH), lambda bi, i: (bi, i, z_off // XATTN_WIDTH)),
                  pl.BlockSpec((1, m_len, XATTN_WIDTH), lambda bi, i: (bi, 0, 0)),
                  pl.BlockSpec((1, m_len, XATTN_WIDTH), lambda bi, i: (bi, 0, 1))],
        out_specs=pl.BlockSpec((1, tm, XATTN_WIDTH), lambda bi, i: (bi, i, 0)),
        out_shape=jax.ShapeDtypeStruct((b, s, XATTN_WIDTH), BF16),
        compiler_params=_params(("parallel", "parallel")),
        name="mem_xattn",
    )(q_src[0], z_src[0], mkv3, mkv3)


def _out_proj_kernel(*refs, ni, d, n_mix):
    y_refs, w_refs = refs[:n_mix], refs[n_mix:2 * n_mix]
    x_ref, g_ref, o_ref, y2_ref, ssq_ref, scale_ref = refs[2 * n_mix:]
    i = pl.program_id(0)
    j = pl.program_id(1)

    @pl.when(j == 0)
    def _():
        @pl.when(i > 0)
        def _():
            scale_ref[...] = lax.rsqrt(ssq_ref[...] * (1.0 / d) + EPS)
        ssq_ref[...] = jnp.zeros_like(ssq_ref)

    def finish_previous_tile():
        o_ref[...] = x_ref[...] + y2_ref[j] * scale_ref[...] * g_ref[...]

    def multiply_this_tile():
        y2 = jnp.dot(y_refs[0][...], w_refs[0][...], preferred_element_type=F32)
        for y_ref, w_ref in zip(y_refs[1:], w_refs[1:]):
            y2 = y2 + jnp.dot(y_ref[...], w_ref[...], preferred_element_type=F32)
        y2_ref[j] = y2
        ssq_ref[...] += jnp.sum(y2 * y2, axis=-1, keepdims=True)

    @pl.when(i == 0)
    def _():
        multiply_this_tile()

    @pl.when((i > 0) & (i < ni))
    def _():
        finish_previous_tile()
        multiply_this_tile()

    @pl.when(i == ni)
    def _():
        finish_previous_tile()


def _out_proj(ys, w_bf, x2d, gain, tm, tn):
    rows, d = x2d.shape
    nj = d // tn
    ni = rows // tm

    def y_spec(width):
        return pl.BlockSpec((tm, width), lambda i, j: (jnp.minimum(i, ni - 1), 0))

    def w_spec(width, row_off):
        assert row_off % width == 0
        return pl.BlockSpec((width, tn),
                            lambda i, j: (row_off // width, jnp.where(i == ni, nj - 1, j)))

    io_spec = pl.BlockSpec((tm, tn),
                           lambda i, j: (jnp.maximum(i - 1, 0), jnp.where(i == 0, 0, j)))
    widths = [y.shape[1] for y in ys]
    offsets = [sum(widths[:k]) for k in range(len(widths))]
    return pl.pallas_call(
        functools.partial(_out_proj_kernel, ni=ni, d=d, n_mix=len(ys)),
        grid=(ni + 1, nj),
        in_specs=([y_spec(w) for w in widths]
                  + [w_spec(w, off) for w, off in zip(widths, offsets)]
                  + [io_spec, pl.BlockSpec((1, tn), lambda i, j: (0, j))]),
        out_specs=io_spec,
        out_shape=jax.ShapeDtypeStruct((rows, d), F32),
        scratch_shapes=[pltpu.VMEM((nj, tm, tn), F32),
                        pltpu.VMEM((tm, 1), F32),
                        pltpu.VMEM((tm, 1), F32)],
        compiler_params=_params(("arbitrary", "arbitrary")),
        name="out_proj",
    )(*ys, *([w_bf] * len(ys)), x2d, gain.reshape(1, d))


def _rope_tables(pos, half):
    inv = 1.0 / (ROPE_THETA ** (np.arange(half, dtype=np.float64) / half))
    ang = np.asarray(pos, dtype=np.float64)[:, None] * inv[None, :]
    return np.cos(ang), np.sin(ang)


def kernel(x, mem, pre_norm, w_in, conv_w, mem_norm, w_mem_kv, w_out, post_norm):
    b, s, d = x.shape
    m_len = mem.shape[1]
    depth = w_in.shape[0]
    assert all(window // dil == BLOCK for window, dil in DIL_GROUPS)
    pos = np.arange(s)
    cos_a, sin_a = _rope_tables(pos, HEAD_DIM // 2)
    cosa = np.concatenate([cos_a, cos_a], axis=-1)
    sina = np.concatenate([-sin_a, sin_a], axis=-1)
    a_scale = HEAD_DIM ** -0.5
    cos_qkv = jnp.asarray(np.stack([cosa * a_scale, cosa, np.ones_like(cosa)]), F32)
    sin_qkv = jnp.asarray(np.stack([sina * a_scale, sina, np.zeros_like(sina)]), F32)
    cosx, sinx = _rope_tables(pos + m_len, XATTN_HEAD_DIM // 2)
    x_scale = XATTN_HEAD_DIM ** -0.5
    cos_qx = jnp.asarray((cosx * x_scale)[None], F32)
    sin_qx = jnp.asarray((sinx * x_scale)[None], F32)
    cos_m, sin_m = _rope_tables(np.arange(m_len), XATTN_HEAD_DIM // 2)
    cosm = jnp.asarray(np.tile(cos_m, (b, 1)), F32)
    sinm = jnp.asarray(np.tile(sin_m, (b, 1)), F32)

    tm, tn = 1024, 512

    def tile_of(off):
        assert off % tn == 0
        return off // tn

    attn_tiles, x_tiles = ATTN_WIDTH // tn, XATTN_WIDTH // tn
    tiles_per_seq = s // tm

    def view(t):
        return t.reshape(b, s, t.shape[-1])

    def rope_tables(cos, sin, kind):
        def index(col_tile, row_tile):
            return kind(col_tile), row_tile % tiles_per_seq, 0
        return [(cos, (1, tm, LANES), index), (sin, (1, tm, LANES), index)]

    def lane_slab(off):
        assert off % LANES == 0
        return lambda j: off // LANES + j

    for layer in range(depth):
        x2d = x.reshape(b * s, d)
        w = w_in[layer]
        common = dict(tm=tm, tn=tn)
        w_qx_bf = _cast_columns_bf16(w, OFF_QX, XATTN_WIDTH, tr=2048)
        h, qx = _norm_project(x2d, pre_norm[layer], w_qx_bf, cos_qx, sin_qx, seq=s, tm=512)
        qx = view(qx)

        qkv = view(_in_proj(h, w, [lambda j: tile_of(OFF_QA) + j], 3 * attn_tiles,
                            _elementwise_finish(_rope_half_lane),
                            rope_tables(cos_qkv, sin_qkv, lambda j: j // attn_tiles),
                            name="in_proj_qkv", **common))
        y_conv, w_out_bf = _in_proj(
            h, w, [lane_slab(OFF_UC), lane_slab(OFF_BC), lane_slab(OFF_CC), lane_slab(OFF_ZC)],
            CONV_WIDTH // LANES,
            functools.partial(_gated_conv_finish, tiles_per_seq=tiles_per_seq),
            [(conv_w[layer], (CONV_K, LANES), lambda col_tile, row_tile: (0, col_tile))],
            out_tn=LANES, scratch=[pltpu.VMEM((CONV_HALO, LANES), F32)],
            rider=(lambda block: block, w_out[layer], []), name="in_proj_conv",
            w_pieces=1, **common)
        gates, hm = _in_proj(
            h, w, [lambda j: jnp.where(j < x_tiles, tile_of(OFF_ZX) + j,
                                       tile_of(OFF_ZA) - x_tiles + j)],
            x_tiles + attn_tiles, _elementwise_finish(_silu), [], name="in_proj_gates",
            rider=(_rms_normalize, mem.reshape(b * m_len, d), [mem_norm[layer].reshape(1, d)]),
            **common)
        gates = view(gates)
        mkv = _mem_proj(hm, w_mem_kv[layer], cosm, sinm, tn=512)

        y_groups = _dilated_attn((qkv, 0), (qkv, ATTN_WIDTH), (qkv, 2 * ATTN_WIDTH),
                                 (gates, XATTN_WIDTH))
        y_groups = [y.reshape(b * s, GROUP_WIDTH) for y in y_groups]
        y_x = _xattn((qx, 0), (gates, 0), mkv.reshape(b, m_len, 2 * XATTN_WIDTH), tm=s)
        y_x = y_x.reshape(b * s, XATTN_WIDTH)

        out = _out_proj([*y_groups, y_conv, y_x], w_out_bf, x2d,
                        post_norm[layer], tm=1024, tn=512)
        x = out.reshape(b, s, d)
    return x
```

```python
import functools

import jax
import jax.numpy as jnp
import numpy as np
from jax import lax
from jax.experimental import pallas as pl
from jax.experimental.pallas import tpu as pltpu

D_MODEL = 4096
HEAD_DIM = 128
DIL_GROUPS = ((128, 1), (512, 4), (2048, 16))
ATTN_WIDTH = 3 * D_MODEL // 8
CONV_WIDTH = 3 * D_MODEL // 8
XATTN_WIDTH = D_MODEL // 4
N_XATTN_HEADS = 4
XATTN_HEAD_DIM = XATTN_WIDTH // N_XATTN_HEADS
HEADS_PER_DIL = 4
GROUP_WIDTH = HEADS_PER_DIL * HEAD_DIM
CONV_K = 3
BLOCK = 128
ROPE_THETA = 10000.0
EPS = 1e-6
NEG_INF = -1e30

OFF_QA = 0
OFF_KA = OFF_QA + ATTN_WIDTH
OFF_VA = OFF_KA + ATTN_WIDTH
OFF_ZA = OFF_VA + ATTN_WIDTH
OFF_UC = OFF_ZA + ATTN_WIDTH
OFF_BC = OFF_UC + CONV_WIDTH
OFF_CC = OFF_BC + CONV_WIDTH
OFF_ZC = OFF_CC + CONV_WIDTH
OFF_QX = OFF_ZC + CONV_WIDTH
OFF_ZX = OFF_QX + XATTN_WIDTH

V7X_VMEM_LIMIT_BYTES = 56 * 1024 * 1024
LANES = 128
W_PIECES = 4
EPILOGUE_ROWS = 128
CONV_HALO = 8
NORM_ROWS = 64
RIDER_ROWS = 16

BF16 = jnp.bfloat16
F32 = jnp.float32


def _params(semantics):
    return pltpu.CompilerParams(dimension_semantics=semantics,
                                vmem_limit_bytes=V7X_VMEM_LIMIT_BYTES)


def _rms_normalize(x, gain):
    ms = jnp.mean(x * x, axis=-1, keepdims=True)
    return x * lax.rsqrt(ms + EPS) * gain


def _cast_kernel(x_ref, o_ref):
    o_ref[...] = x_ref[...].astype(o_ref.dtype)


def _cast_columns_bf16(w, col_off, width, tr):
    rows = w.shape[0]
    assert col_off % width == 0
    return pl.pallas_call(
        _cast_kernel,
        grid=(rows // tr,),
        in_specs=[pl.BlockSpec((tr, width), lambda i: (i, col_off // width))],
        out_specs=pl.BlockSpec((tr, width), lambda i: (i, 0)),
        out_shape=jax.ShapeDtypeStruct((rows, width), BF16),
        compiler_params=_params(("parallel",)),
        name="cast_w_qx",
    )(w)


def _norm_project_kernel(x_ref, g_ref, w_ref, cos_ref, sin_ref, h_ref, o_ref, acc_ref):
    tm = x_ref.shape[0]
    for r0 in range(0, tm, NORM_ROWS):
        rows = slice(r0, r0 + NORM_ROWS)
        h_ref[rows, :] = _rms_normalize(x_ref[rows, :], g_ref[...]).astype(h_ref.dtype)
    acc_ref[...] = jnp.dot(h_ref[...], w_ref[...], preferred_element_type=F32)
    for r0 in range(0, tm, EPILOGUE_ROWS):
        rows = slice(r0, r0 + EPILOGUE_ROWS)
        o_ref[rows, :] = _rope_two_tiles(acc_ref[rows, :], cos_ref[0, rows, :],
                                         sin_ref[0, rows, :]).astype(o_ref.dtype)


def _norm_project(x2d, gain, w_bf, cos, sin, seq, tm):
    rows, d = x2d.shape
    width = w_bf.shape[1]
    tiles_per_seq = seq // tm
    table = pl.BlockSpec((1, tm, LANES), lambda i: (0, i % tiles_per_seq, 0))
    return pl.pallas_call(
        _norm_project_kernel,
        grid=(rows // tm,),
        in_specs=[pl.BlockSpec((tm, d), lambda i: (i, 0)),
                  pl.BlockSpec((1, d), lambda i: (0, 0)),
                  pl.BlockSpec((d, width), lambda i: (0, 0)),
                  table, table],
        out_specs=[pl.BlockSpec((tm, d), lambda i: (i, 0)),
                   pl.BlockSpec((tm, width), lambda i: (i, 0))],
        out_shape=[jax.ShapeDtypeStruct((rows, d), BF16),
                   jax.ShapeDtypeStruct((rows, width), BF16)],
        scratch_shapes=[pltpu.VMEM((tm, width), F32)],
        compiler_params=_params(("arbitrary",)),
        name="norm_project_qx",
    )(x2d, gain.reshape(1, d), w_bf, cos, sin)


def _silu(z):
    half = 0.5 * z
    return half + half * jnp.tanh(half)


def _rope_half_lane(acc, cos, sin_signed):
    outs = []
    for h in range(acc.shape[1] // HEAD_DIM):
        xh = acc[:, h * HEAD_DIM:(h + 1) * HEAD_DIM]
        outs.append(xh * cos + pltpu.roll(xh, HEAD_DIM // 2, axis=1) * sin_signed)
    return jnp.concatenate(outs, axis=1)


def _rope_two_tiles(acc, cos, sin):
    outs = []
    for h in range(acc.shape[1] // XATTN_HEAD_DIM):
        x1 = acc[:, h * XATTN_HEAD_DIM:h * XATTN_HEAD_DIM + LANES]
        x2 = acc[:, h * XATTN_HEAD_DIM + LANES:(h + 1) * XATTN_HEAD_DIM]
        outs.append(x1 * cos - x2 * sin)
        outs.append(x2 * cos + x1 * sin)
    return jnp.concatenate(outs, axis=1)


def _elementwise_finish(epilogue):
    def finish(acc_ref, slot, o_ref, aux_refs, scratch_refs, row_tile):
        for r0 in range(0, o_ref.shape[0], EPILOGUE_ROWS):
            rows = slice(r0, r0 + EPILOGUE_ROWS)
            o_ref[rows, :] = epilogue(acc_ref[slot, rows, :],
                                      *[r[0, rows, :] for r in aux_refs]).astype(o_ref.dtype)
    return finish


def _gated_conv_finish(acc_ref, slot, o_ref, aux_refs, scratch_refs, row_tile, *, tiles_per_seq):
    (w_ref,), (carry_ref,) = aux_refs, scratch_refs
    tm = o_ref.shape[0]
    u_cols, b_cols, c_cols, z_cols = (slice(k * LANES, (k + 1) * LANES) for k in range(4))
    w = w_ref[...]
    tail = jnp.where(row_tile % tiles_per_seq == 0, 0.0, carry_ref[...])
    for r0 in range(0, tm, EPILOGUE_ROWS):
        rows = slice(r0, r0 + EPILOGUE_ROWS)
        cu = acc_ref[slot, rows, c_cols] * acc_ref[slot, rows, u_cols]
        window = jnp.concatenate([tail, cu], axis=0)
        conv = cu * w[CONV_K - 1:CONV_K, :]
        for lag in range(1, CONV_K):
            shifted = pltpu.roll(window, lag, axis=0)[CONV_HALO:, :]
            conv = conv + shifted * w[CONV_K - 1 - lag:CONV_K - lag, :]
        o_ref[rows, :] = (acc_ref[slot, rows, b_cols] * conv
                          * _silu(acc_ref[slot, rows, z_cols])).astype(o_ref.dtype)
        tail = cu[EPILOGUE_ROWS - CONV_HALO:, :]
    carry_ref[...] = tail


def _in_proj_kernel(*refs, ni, n_steps, n_w, w_pieces, n_aux, n_scratch, finish_fn, rider_fn):
    h_ref = refs[0]
    w_refs = refs[1:1 + n_w]
    aux_refs = refs[1 + n_w:1 + n_w + n_aux]
    scratch_refs = refs[len(refs) - n_scratch:] if n_scratch else ()
    rest = refs[1 + n_w + n_aux:len(refs) - n_scratch]
    if rider_fn is not None:
        rider_in_ref, *rider_row_refs = rest[:-4]
        o_ref, rider_out_ref, wbf_ref, acc_ref = rest[-4:]
    else:
        o_ref, wbf_ref, acc_ref = rest
    t = pl.program_id(0)
    piece_rows = wbf_ref.shape[0] // w_pieces
    slab_cols = wbf_ref.shape[1] // (n_w // w_pieces)

    @pl.when((t % ni == 0) & (t < n_steps))
    def _():
        for k, w_ref in enumerate(w_refs):
            slab, p = divmod(k, w_pieces)
            wbf_ref[p * piece_rows:(p + 1) * piece_rows,
                    slab * slab_cols:(slab + 1) * slab_cols] = w_ref[...].astype(BF16)

    def ride():
        for r0 in range(0, rider_in_ref.shape[0], RIDER_ROWS):
            rows = slice(r0, r0 + RIDER_ROWS)
            rider_out_ref[rows, :] = rider_fn(
                rider_in_ref[rows, :], *[r[...] for r in rider_row_refs]
            ).astype(rider_out_ref.dtype)

    def multiply(slot):
        acc_ref[slot] = jnp.dot(h_ref[...], wbf_ref[...], preferred_element_type=F32)
        if rider_fn is not None:
            ride()

    def finish(slot):
        finish_fn(acc_ref, slot, o_ref, aux_refs, scratch_refs, ((t - 1) % ni))

    @pl.when(t == 0)
    def _():
        for ref in scratch_refs:
            ref[...] = jnp.zeros_like(ref)
        multiply(0)

    for parity in range(2):
        @pl.when((t > 0) & (t < n_steps) & (t % 2 == parity))
        def _(parity=parity):
            finish(1 - parity)
            multiply(parity)

    @pl.when(t == n_steps)
    def _():
        finish((n_steps - 1) % 2)


def _in_proj(h, w, col_slabs, n_col_tiles, finish_fn, aux, tm, tn, name, out_tn=None,
             scratch=(), rider=None, w_pieces=W_PIECES):
    rows, d = h.shape
    ni = rows // tm
    n_steps = n_col_tiles * ni
    out_tn = tn if out_tn is None else out_tn
    slab_cols = tn // len(col_slabs)
    rider_in_specs, rider_out_specs, rider_out_shapes, rider_operands = [], [], [], []
    rider_fn, rider_steps = None, 0
    if rider is not None:
        rider_fn, rider_matrix, rider_rows = rider
        r_rows, r_cols = rider_matrix.shape
        assert all(r.shape == (1, r_cols) for r in rider_rows)
        rider_steps = max(k for k in range(1, n_steps + 1) if r_rows % k == 0)
        rider_spec = pl.BlockSpec((r_rows // rider_steps, r_cols),
                                  lambda t: (jnp.minimum(t, rider_steps - 1), 0))
        rider_in_specs = [rider_spec] + [pl.BlockSpec((1, r_cols), lambda t: (0, 0))
                                         for _ in rider_rows]
        rider_out_specs = [rider_spec]
        rider_out_shapes = [jax.ShapeDtypeStruct(rider_matrix.shape, BF16)]
        rider_operands = [rider_matrix, *rider_rows]

    def cur(t):
        c = jnp.minimum(t, n_steps - 1)
        return c // ni, c % ni

    def prev(t):
        p = jnp.maximum(t - 1, 0)
        return p // ni, p % ni

    def aux_spec(block, index):
        return pl.BlockSpec(block, lambda t: index(*prev(t)))

    n_w = len(col_slabs) * w_pieces
    stagger = min(W_PIECES, n_w)

    def w_piece_spec(slab, col_slab, p):
        turn = (slab * w_pieces + p) * stagger // n_w

        def index(t):
            j, i = cur(t)
            ahead = (i >= ni - stagger + turn).astype(jnp.int32)
            return p, col_slab(jnp.minimum(j + ahead, n_col_tiles - 1))
        return pl.BlockSpec((d // w_pieces, slab_cols), index)

    assert ni > stagger and d % w_pieces == 0
    w_specs = [w_piece_spec(slab, col_slab, p) for slab, col_slab in enumerate(col_slabs)
               for p in range(w_pieces)]
    assert len(w_specs) == n_w
    results = pl.pallas_call(
        functools.partial(_in_proj_kernel, ni=ni, n_steps=n_steps, n_w=len(w_specs),
                          w_pieces=w_pieces, n_aux=len(aux), n_scratch=len(scratch),
                          finish_fn=finish_fn, rider_fn=rider_fn),
        grid=(n_steps + 1,),
        in_specs=[pl.BlockSpec((tm, d), lambda t: (cur(t)[1], 0))] + w_specs
                 + [aux_spec(block, index) for _, block, index in aux] + rider_in_specs,
        out_specs=[pl.BlockSpec((tm, out_tn), lambda t: (prev(t)[1], prev(t)[0]))]
                  + rider_out_specs,
        out_shape=[jax.ShapeDtypeStruct((rows, n_col_tiles * out_tn), BF16)] + rider_out_shapes,
        scratch_shapes=[pltpu.VMEM((d, tn), BF16), pltpu.VMEM((2, tm, tn), F32), *scratch],
        compiler_params=_params(("arbitrary",)),
        name=name,
    )(h, *([w] * len(w_specs)), *[arr for arr, _, _ in aux], *rider_operands)
    return results if rider is not None else results[0]


def _mem_proj_kernel(h_ref, w_ref, cos_ref, sin_ref, o_ref, *, tn):
    j = pl.program_id(0)
    acc = jnp.dot(h_ref[...], w_ref[...].astype(BF16), preferred_element_type=F32)

    @pl.when(j < XATTN_WIDTH // tn)
    def _():
        o_ref[...] = _rope_two_tiles(acc, cos_ref[...], sin_ref[...]).astype(o_ref.dtype)

    @pl.when(j >= XATTN_WIDTH // tn)
    def _():
        o_ref[...] = acc.astype(o_ref.dtype)


def _mem_proj(hm, w, cos, sin, tn):
    rows, d = hm.shape
    ncols = w.shape[1]
    tab = pl.BlockSpec((rows, LANES), lambda j: (0, 0))
    return pl.pallas_call(
        functools.partial(_mem_proj_kernel, tn=tn),
        grid=(ncols // tn,),
        in_specs=[pl.BlockSpec((rows, d), lambda j: (0, 0)),
                  pl.BlockSpec((d, tn), lambda j: (0, j)),
                  tab, tab],
        out_specs=pl.BlockSpec((rows, tn), lambda j: (0, j)),
        out_shape=jax.ShapeDtypeStruct((rows, ncols), BF16),
        compiler_params=_params(("parallel",)),
        name="mem_proj",
    )(hm, w, cos, sin)


MAX_SINGLE_ACCESS_STRIDE = 4


def _split_stride(dil):
    if dil <= MAX_SINGLE_ACCESS_STRIDE:
        return dil, 1
    assert dil % MAX_SINGLE_ACCESS_STRIDE == 0
    f2 = dil // MAX_SINGLE_ACCESS_STRIDE
    assert f2 <= MAX_SINGLE_ACCESS_STRIDE
    return MAX_SINGLE_ACCESS_STRIDE, f2


def _dilated_attn_kernel(q0, q1, q2, k0, k1, k2, v0, v1, v2, z0, z1, z2, y0, y1, y2,
                         stage_ref, mid_ref, sub_ref, osub_ref, lsub_ref, onat_ref, lnat_ref,
                         kt_ref, *, seq):
    q_refs, k_refs, v_refs = (q0, q1, q2), (k0, k1, k2), (v0, v1, v2)
    z_refs, y_refs = (z0, z1, z2), (y0, y1, y2)
    nblk = seq // BLOCK
    qi = lax.broadcasted_iota(jnp.int32, (BLOCK, 2 * BLOCK), 0)
    kk = lax.broadcasted_iota(jnp.int32, (BLOCK, 2 * BLOCK), 1)
    dist = qi + BLOCK - kk
    band = (dist >= 0) & (dist <= BLOCK)
    causal = (lax.broadcasted_iota(jnp.int32, (BLOCK, BLOCK), 1)
              <= lax.broadcasted_iota(jnp.int32, (BLOCK, BLOCK), 0))

    rows_of = []
    for g, (window, dil) in enumerate(DIL_GROUPS):
        sub_len = seq // dil
        if dil == 1:
            def group_rows(which, start, size, g=g):
                return (q_refs, k_refs, v_refs)[which][g][0, pl.ds(start, size), :]
        else:
            slab = 3 * (g - 1)
            f1, f2 = _split_stride(dil)
            len1 = seq // f1
            for which, refs in enumerate((q_refs, k_refs, v_refs)):
                stage_ref[slab + which] = refs[g][0].astype(F32)
                if f2 == 1:
                    for r in range(dil):
                        sub_ref[slab + which, r * sub_len:(r + 1) * sub_len, :] = (
                            stage_ref[slab + which, pl.ds(r, sub_len, stride=dil), :].astype(BF16))
                    continue
                for r1 in range(f1):
                    mid_ref[which, r1 * len1:(r1 + 1) * len1, :] = (
                        stage_ref[slab + which, pl.ds(r1, len1, stride=f1), :])
                for r1 in range(f1):
                    for r2 in range(f2):
                        r = r2 * f1 + r1
                        sub_ref[slab + which, r * sub_len:(r + 1) * sub_len, :] = (
                            mid_ref[which, pl.ds(r1 * len1 + r2, sub_len, stride=f2), :]
                            .astype(BF16))

            def group_rows(which, start, size, slab=slab):
                return sub_ref[slab + which, pl.ds(start, size), :]
        rows_of.append(group_rows)
        kt_ref[g] = group_rows(1, 0, seq).astype(F32).T.astype(BF16)

    def attend(g, start, first):
        q = rows_of[g](0, start, BLOCK)
        if first:
            kt = kt_ref[g, :, start:start + BLOCK]
            v = rows_of[g](2, start, BLOCK)
            mask = causal
        else:
            kt = kt_ref[g, :, start - BLOCK:start + BLOCK]
            v = rows_of[g](2, start - BLOCK, 2 * BLOCK)
            mask = band
        s = jnp.dot(q, kt, preferred_element_type=F32)
        s = jnp.where(mask, s, NEG_INF)
        m = jnp.max(s, axis=-1, keepdims=True)
        p = jnp.exp(s - m)
        l = jnp.sum(p, axis=-1, keepdims=True)
        o = jnp.dot(p.astype(BF16), v, preferred_element_type=F32) * (1.0 / l)
        lse = jnp.broadcast_to(m + jnp.log(l), (BLOCK, HEAD_DIM))
        if DIL_GROUPS[g][1] == 1:
            onat_ref[g, pl.ds(start, BLOCK), :] = o
            lnat_ref[g, pl.ds(start, BLOCK), :] = lse
        else:
            osub_ref[g - 1, pl.ds(start, BLOCK), :] = o
            lsub_ref[g - 1, pl.ds(start, BLOCK), :] = lse

    for c in range(nblk):
        for g, (window, dil) in enumerate(DIL_GROUPS):
            attend(g, c * BLOCK, c % (seq // dil // BLOCK) == 0)

    for g, (window, dil) in enumerate(DIL_GROUPS):
        sub_len = seq // dil
        if dil == 1:
            continue
        f1, f2 = _split_stride(dil)
        len1 = seq // f1
        for idx, (nat_ref, subseq_ref) in enumerate(((onat_ref, osub_ref), (lnat_ref, lsub_ref))):
            if f2 == 1:
                for r in range(dil):
                    src = slice(r * sub_len, (r + 1) * sub_len)
                    nat_ref[g, pl.ds(r, sub_len, stride=dil), :] = subseq_ref[g - 1, src, :]
                continue
            mid = 3 + idx
            for r1 in range(f1):
                for r2 in range(f2):
                    r = r2 * f1 + r1
                    mid_ref[mid, pl.ds(r1 * len1 + r2, sub_len, stride=f2), :] = (
                        subseq_ref[g - 1, r * sub_len:(r + 1) * sub_len, :])
            for r1 in range(f1):
                nat_ref[g, pl.ds(r1, len1, stride=f1), :] = mid_ref[mid, r1 * len1:(r1 + 1) * len1, :]

    chunk = 2 * BLOCK
    for t in range(seq // chunk):
        rows = slice(t * chunk, (t + 1) * chunk)
        lse = [lnat_ref[g, rows, :] for g in range(len(DIL_GROUPS))]
        mx = jnp.maximum(jnp.maximum(lse[0], lse[1]), lse[2])
        e = [jnp.exp(x - mx) for x in lse]
        inv = 1.0 / (e[0] + e[1] + e[2])
        for g in range(len(DIL_GROUPS)):
            y_refs[g][0, rows, :] = (onat_ref[g, rows, :] * (e[g] * inv)
                                     * z_refs[g][0, rows, :].astype(F32)).astype(y_refs[g].dtype)


def _dilated_attn(q_src, k_src, v_src, z_src):
    sources = (q_src, k_src, v_src, z_src)
    b, s, _ = q_src[0].shape
    n_groups = len(DIL_GROUPS)
    n_regrouped = n_groups - 1
    assert sum(_split_stride(dil)[1] > 1 for _, dil in DIL_GROUPS) <= 1

    def head_spec(off, g):
        first = (off + g * GROUP_WIDTH) // HEAD_DIM
        return pl.BlockSpec((1, s, HEAD_DIM), lambda bi, hg: (bi, 0, first + hg))

    in_specs = [head_spec(off, g) for _, off in sources for g in range(n_groups)]
    operands = [arr for arr, _ in sources for g in range(n_groups)]
    out_spec = pl.BlockSpec((1, s, HEAD_DIM), lambda bi, hg: (bi, 0, hg))
    return pl.pallas_call(
        functools.partial(_dilated_attn_kernel, seq=s),
        grid=(b, HEADS_PER_DIL),
        in_specs=in_specs,
        out_specs=[out_spec] * n_groups,
        out_shape=[jax.ShapeDtypeStruct((b, s, GROUP_WIDTH), BF16)] * n_groups,
        scratch_shapes=[pltpu.VMEM((3 * n_regrouped, s, HEAD_DIM), F32),
                        pltpu.VMEM((5, s, HEAD_DIM), F32),
                        pltpu.VMEM((3 * n_regrouped, s, HEAD_DIM), BF16),
                        pltpu.VMEM((n_regrouped, s, HEAD_DIM), F32),
                        pltpu.VMEM((n_regrouped, s, HEAD_DIM), F32),
                        pltpu.VMEM((n_groups, s, HEAD_DIM), F32),
                        pltpu.VMEM((n_groups, s, HEAD_DIM), F32),
                        pltpu.VMEM((n_groups, HEAD_DIM, s), BF16)],
        compiler_params=_params(("parallel", "parallel")),
        name="dilated_attn",
    )(*operands)


def _xattn_kernel(q_ref, z_ref, mk_ref, mv_ref, y_ref):
    for h in range(N_XATTN_HEADS):
        cols = slice(h * XATTN_HEAD_DIM, (h + 1) * XATTN_HEAD_DIM)
        s = lax.dot_general(q_ref[0, :, cols], mk_ref[0, :, cols],
                            (((1,), (1,)), ((), ())), preferred_element_type=F32)
        m = jnp.max(s, axis=-1, keepdims=True)
        p = jnp.exp(s - m)
        l = jnp.sum(p, axis=-1, keepdims=True)
        o = jnp.dot(p.astype(BF16), mv_ref[0, :, cols], preferred_element_type=F32)
        y_ref[0, :, cols] = (o * (1.0 / l) * z_ref[0, :, cols].astype(F32)).astype(y_ref.dtype)


def _xattn(q_src, z_src, mkv3, tm):
    b, s, _ = q_src[0].shape
    m_len = mkv3.shape[1]
    q_off, z_off = q_src[1], z_src[1]
    assert q_off % XATTN_WIDTH == 0 and z_off % XATTN_WIDTH == 0
    return pl.pallas_call(
        _xattn_kernel,
        grid=(b, s // tm),
        in_specs=[pl.BlockSpec((1, tm, XATTN_WIDTH), lambda bi, i: (bi, i, q_off // XATTN_WIDTH)),
                  pl.BlockSpec((1, tm, XATTN_WIDTH), lambda bi, i: (bi, i, z_off // XATTN_WIDTH)),
                  pl.BlockSpec((1, m_len, XATTN_WIDTH), lambda bi, i: (bi, 0, 0)),
                  pl.BlockSpec((1, m_len, XATTN_WIDTH), lambda bi, i: (bi, 0, 1))],
        out_specs=pl.BlockSpec((1, tm, XATTN_WIDTH), lambda bi, i: (bi, i, 0)),
        out_shape=jax.ShapeDtypeStruct((b, s, XATTN_WIDTH), BF16),
        compiler_params=_params(("parallel", "parallel")),
        name="mem_xattn",
    )(q_src[0], z_src[0], mkv3, mkv3)


def _out_proj_kernel(*refs, ni, d, n_mix):
    y_refs, w_refs = refs[:n_mix], refs[n_mix:2 * n_mix]
    x_ref, g_ref, o_ref, y2_ref, ssq_ref, scale_ref = refs[2 * n_mix:]
    i = pl.program_id(0)
    j = pl.program_id(1)

    @pl.when(j == 0)
    def _():
        @pl.when(i > 0)
        def _():
            scale_ref[...] = lax.rsqrt(ssq_ref[...] * (1.0 / d) + EPS)
        ssq_ref[...] = jnp.zeros_like(ssq_ref)

    def finish_previous_tile():
        o_ref[...] = x_ref[...] + y2_ref[j] * scale_ref[...] * g_ref[...]

    def multiply_this_tile():
        y2 = jnp.dot(y_refs[0][...], w_refs[0][...], preferred_element_type=F32)
        for y_ref, w_ref in zip(y_refs[1:], w_refs[1:]):
            y2 = y2 + jnp.dot(y_ref[...], w_ref[...], preferred_element_type=F32)
        y2_ref[j] = y2
        ssq_ref[...] += jnp.sum(y2 * y2, axis=-1, keepdims=True)

    @pl.when(i == 0)
    def _():
        multiply_this_tile()

    @pl.when((i > 0) & (i < ni))
    def _():
        finish_previous_tile()
        multiply_this_tile()

    @pl.when(i == ni)
    def _():
        finish_previous_tile()


def _out_proj(ys, w_bf, x2d, gain, tm, tn):
    rows, d = x2d.shape
    nj = d // tn
    ni = rows // tm

    def y_spec(width):
        return pl.BlockSpec((tm, width), lambda i, j: (jnp.minimum(i, ni - 1), 0))

    def w_spec(width, row_off):
        assert row_off % width == 0
        return pl.BlockSpec((width, tn),
                            lambda i, j: (row_off // width, jnp.where(i == ni, nj - 1, j)))

    io_spec = pl.BlockSpec((tm, tn),
                           lambda i, j: (jnp.maximum(i - 1, 0), jnp.where(i == 0, 0, j)))
    widths = [y.shape[1] for y in ys]
    offsets = [sum(widths[:k]) for k in range(len(widths))]
    return pl.pallas_call(
        functools.partial(_out_proj_kernel, ni=ni, d=d, n_mix=len(ys)),
        grid=(ni + 1, nj),
        in_specs=([y_spec(w) for w in widths]
                  + [w_spec(w, off) for w, off in zip(widths, offsets)]
                  + [io_spec, pl.BlockSpec((1, tn), lambda i, j: (0, j))]),
        out_specs=io_spec,
        out_shape=jax.ShapeDtypeStruct((rows, d), F32),
        scratch_shapes=[pltpu.VMEM((nj, tm, tn), F32),
                        pltpu.VMEM((tm, 1), F32),
                        pltpu.VMEM((tm, 1), F32)],
        compiler_params=_params(("arbitrary", "arbitrary")),
        name="out_proj",
    )(*ys, *([w_bf] * len(ys)), x2d, gain.reshape(1, d))


def _rope_tables(pos, half):
    inv = 1.0 / (ROPE_THETA ** (np.arange(half, dtype=np.float64) / half))
    ang = np.asarray(pos, dtype=np.float64)[:, None] * inv[None, :]
    return np.cos(ang), np.sin(ang)


def kernel(x, mem, pre_norm, w_in, conv_w, mem_norm, w_mem_kv, w_out, post_norm):
    b, s, d = x.shape
    m_len = mem.shape[1]
    depth = w_in.shape[0]
    assert all(window // dil == BLOCK for window, dil in DIL_GROUPS)
    pos = np.arange(s)
    cos_a, sin_a = _rope_tables(pos, HEAD_DIM // 2)
    cosa = np.concatenate([cos_a, cos_a], axis=-1)
    sina = np.concatenate([-sin_a, sin_a], axis=-1)
    a_scale = HEAD_DIM ** -0.5
    cos_qkv = jnp.asarray(np.stack([cosa * a_scale, cosa, np.ones_like(cosa)]), F32)
    sin_qkv = jnp.asarray(np.stack([sina * a_scale, sina, np.zeros_like(sina)]), F32)
    cosx, sinx = _rope_tables(pos + m_len, XATTN_HEAD_DIM // 2)
    x_scale = XATTN_HEAD_DIM ** -0.5
    cos_qx = jnp.asarray((cosx * x_scale)[None], F32)
    sin_qx = jnp.asarray((sinx * x_scale)[None], F32)
    cos_m, sin_m = _rope_tables(np.arange(m_len), XATTN_HEAD_DIM // 2)
    cosm = jnp.asarray(np.tile(cos_m, (b, 1)), F32)
    sinm = jnp.asarray(np.tile(sin_m, (b, 1)), F32)

    tm, tn = 1024, 512

    def tile_of(off):
        assert off % tn == 0
        return off // tn

    attn_tiles, x_tiles = ATTN_WIDTH // tn, XATTN_WIDTH // tn
    tiles_per_seq = s // tm

    def view(t):
        return t.reshape(b, s, t.shape[-1])

    def rope_tables(cos, sin, kind):
        def index(col_tile, row_tile):
            return kind(col_tile), row_tile % tiles_per_seq, 0
        return [(cos, (1, tm, LANES), index), (sin, (1, tm, LANES), index)]

    def lane_slab(off):
        assert off % LANES == 0
        return lambda j: off // LANES + j

    for layer in range(depth):
        x2d = x.reshape(b * s, d)
        w = w_in[layer]
        common = dict(tm=tm, tn=tn)
        w_qx_bf = _cast_columns_bf16(w, OFF_QX, XATTN_WIDTH, tr=2048)
        h, qx = _norm_project(x2d, pre_norm[layer], w_qx_bf, cos_qx, sin_qx, seq=s, tm=512)
        qx = view(qx)

        qkv = view(_in_proj(h, w, [lambda j: tile_of(OFF_QA) + j], 3 * attn_tiles,
                            _elementwise_finish(_rope_half_lane),
                            rope_tables(cos_qkv, sin_qkv, lambda j: j // attn_tiles),
                            name="in_proj_qkv", **common))
        y_conv, w_out_bf = _in_proj(
            h, w, [lane_slab(OFF_UC), lane_slab(OFF_BC), lane_slab(OFF_CC), lane_slab(OFF_ZC)],
            CONV_WIDTH // LANES,
            functools.partial(_gated_conv_finish, tiles_per_seq=tiles_per_seq),
            [(conv_w[layer], (CONV_K, LANES), lambda col_tile, row_tile: (0, col_tile))],
            out_tn=LANES, scratch=[pltpu.VMEM((CONV_HALO, LANES), F32)],
            rider=(lambda block: block, w_out[layer], []), name="in_proj_conv",
            w_pieces=1, **common)
        gates, hm = _in_proj(
            h, w, [lambda j: jnp.where(j < x_tiles, tile_of(OFF_ZX) + j,
                                       tile_of(OFF_ZA) - x_tiles + j)],
            x_tiles + attn_tiles, _elementwise_finish(_silu), [], name="in_proj_gates",
            rider=(_rms_normalize, mem.reshape(b * m_len, d), [mem_norm[layer].reshape(1, d)]),
            **common)
        gates = view(gates)
        mkv = _mem_proj(hm, w_mem_kv[layer], cosm, sinm, tn=512)

        y_groups = _dilated_attn((qkv, 0), (qkv, ATTN_WIDTH), (qkv, 2 * ATTN_WIDTH),
                                 (gates, XATTN_WIDTH))
        y_groups = [y.reshape(b * s, GROUP_WIDTH) for y in y_groups]
        y_x = _xattn((qx, 0), (gates, 0), mkv.reshape(b, m_len, 2 * XATTN_WIDTH), tm=s)
        y_x = y_x.reshape(b * s, XATTN_WIDTH)

        out = _out_proj([*y_groups, y_conv, y_x], w_out_bf, x2d,
                        post_norm[layer], tm=1024, tn=512)
        x = out.reshape(b, s, d)
    return x
```
